```python
import jax, jax.numpy as jnp
from jax import lax
import numpy as np

D_MODEL = 1024
BATCH = 16
SEQ = 256
DEPTH = 2
DEC_BATCH = 4
DEC_SEQ = 2048
PAST_LEN = 512

GRID_W = 64
HEAD_DIM = 64
A_HEADS = 8
A_KV = 2
WINDOW = 128
B_HEADS = 4
B_DIM = 128
C_HEADS = 4
C_DK = 128
C_DV = 128
D_HEADS = 8
D_KV = 2
D_FF = 2816
Q_BLOCK = 128
MLSTM_CHUNK = 64
HGRN_CHUNK = 32
ROPE_THETA = 10000.0
N_EVEN = (DEPTH + 1) // 2
N_ODD = DEPTH // 2
ALPHA = (2 * DEPTH) ** 0.25
BETA = (8 * DEPTH) ** -0.25
EVEN_SIZES = (A_HEADS * HEAD_DIM, A_KV * HEAD_DIM, A_KV * HEAD_DIM,
              B_HEADS * B_DIM, B_HEADS * B_DIM, B_HEADS * B_DIM, 4 * B_HEADS, B_HEADS * B_DIM)
ODD_SIZES = (C_HEADS * C_DK, C_HEADS * C_DK, C_HEADS * C_DK, C_HEADS * C_DV, C_HEADS * C_DV,
             D_HEADS * HEAD_DIM, D_KV * HEAD_DIM, D_KV * HEAD_DIM)
EVEN_OUT = A_HEADS * HEAD_DIM + B_HEADS * B_DIM
ODD_OUT = C_HEADS * C_DV + D_HEADS * HEAD_DIM
NEG_INF = -1e30
F32 = jnp.float32

kernel_name = 'hybrid_diffusion_prefix_step'


def _split(p, sizes):
    return jnp.split(p, np.cumsum(sizes)[:-1].tolist(), axis=-1)


def _flip(t):
    return jnp.flip(t, axis=1)


def _layernorm(x, g, b, eps=1e-5):
    xf = x.astype(F32)
    mu = jnp.mean(xf, -1, keepdims=True)
    xc = xf - mu
    var = jnp.mean(xc * xc, -1, keepdims=True)
    return (xc * lax.rsqrt(var + eps) * g.astype(F32) + b.astype(F32)).astype(x.dtype)


def _rms(x, g, eps=1e-6):
    xf = x.astype(F32)
    return (xf * lax.rsqrt(jnp.mean(xf * xf, -1, keepdims=True) + eps) * g.astype(F32)).astype(x.dtype)


def _modulation(cvec, w, b):
    return (jax.nn.silu(cvec) @ w + b).reshape(cvec.shape[0], 9, D_MODEL)


def _mod_parts(mod, j):
    return mod[:, 3 * j][:, None], mod[:, 3 * j + 1][:, None], mod[:, 3 * j + 2][:, None]


def _ffn_sublayer(x, mod, j, g, b, w1, w3, w2):
    shift, scale, gate = _mod_parts(mod, j)
    h = x * (1 + scale) + shift
    y = (jax.nn.silu(h @ w1) * (h @ w3)) @ w2
    return _layernorm(ALPHA * x + 0.5 * gate * y, g, b)


def _axial_rope(x):
    L = x.shape[1]
    rows = L // GRID_W
    row = jnp.repeat(jnp.arange(rows), GRID_W)
    col = jnp.arange(L) % GRID_W
    half = x.shape[-1] // 2
    nf = half // 2
    inv = ROPE_THETA ** (-jnp.arange(nf, dtype=F32) / nf)
    shp = (L,) + (1,) * (x.ndim - 3) + (nf,)

    def rot(xh, pos):
        ang = pos.astype(F32)[:, None] * inv[None]
        cos = jnp.cos(ang).reshape(shp).astype(x.dtype)
        sin = jnp.sin(ang).reshape(shp).astype(x.dtype)
        x1, x2 = xh[..., :nf], xh[..., nf:]
        return jnp.concatenate([x1 * cos - x2 * sin, x1 * sin + x2 * cos], -1)

    return jnp.concatenate([rot(x[..., :half], row), rot(x[..., half:], col)], -1)


def _to_blocks(x, size):
    b, L = x.shape[:2]
    return jnp.moveaxis(x.reshape((b, L // size, size) + x.shape[2:]), 1, 0)


def _from_blocks(x):
    n, b, size = x.shape[:3]
    return jnp.moveaxis(x, 0, 1).reshape((b, n * size) + x.shape[3:])


def _attend(q, k, v, mask, sink):
    s = jnp.einsum('bqhgd,bkhd->bhgqk', q, k).astype(F32) * (q.shape[-1] ** -0.5)
    if mask is not None:
        s = jnp.where(mask, s, NEG_INF)
    if sink is not None:
        sk = jnp.broadcast_to(sink.astype(F32)[None, :, :, None, None], s.shape[:-1] + (1,))
        s = jnp.concatenate([s, sk], axis=-1)
    p = jax.nn.softmax(s, axis=-1)
    if sink is not None:
        p = p[..., :-1]
    return jnp.einsum('bhgqk,bkhd->bqhgd', p.astype(v.dtype), v)


def _dense_attention(q, k, v, sink):
    out = lax.map(lambda qq: _attend(qq, k, v, None, sink), _to_blocks(q, Q_BLOCK))
    return _from_blocks(out)


def _banded_attention(q, k, v, k_ctx, v_ctx, sink):
    L = q.shape[1]
    span = Q_BLOCK + 2 * WINDOW
    pad = ((0, 0), (WINDOW, WINDOW), (0, 0), (0, 0))
    kp, vp = jnp.pad(k, pad), jnp.pad(v, pad)
    rel = jnp.arange(span)[None, :] - WINDOW - jnp.arange(Q_BLOCK)[:, None]
    band = jnp.abs(rel) <= WINDOW
    ctx_mask = jnp.ones((Q_BLOCK, k_ctx.shape[1]), bool)
    kc, vc = k_ctx.astype(k.dtype), v_ctx.astype(v.dtype)

    def block(args):
        qq, j = args
        start = j * Q_BLOCK
        kk = lax.dynamic_slice_in_dim(kp, start, span, axis=1)
        vv = lax.dynamic_slice_in_dim(vp, start, span, axis=1)
        kpos = start - WINDOW + jnp.arange(span)
        valid = band & ((kpos >= 0) & (kpos < L))[None, :]
        mask = jnp.concatenate([valid, ctx_mask], axis=1)
        return _attend(qq, jnp.concatenate([kk, kc], 1), jnp.concatenate([vv, vc], 1), mask, sink)

    out = lax.map(block, (_to_blocks(q, Q_BLOCK), jnp.arange(L // Q_BLOCK)))
    return _from_blocks(out)


def _mlstm_scan(q, k, v, log_i, log_f, state):
    T = MLSTM_CHUNK
    causal = jnp.tril(jnp.ones((T, T), bool))[None, :, :, None]

    def step(carry, xs):
        C, n, m = carry
        qc, kc, vc, ic, fc = xs
        bc = jnp.cumsum(fc, axis=1)
        dmat = jnp.where(causal, bc[:, :, None] - bc[:, None] + ic[:, None], -jnp.inf)
        inter = bc + m[:, None]
        m_t = jnp.maximum(inter, jnp.max(dmat, axis=2))
        w = jnp.exp(dmat - m_t[:, :, None]) * jnp.einsum('bthd,bshd->btsh', qc, kc)
        a = jnp.exp(inter - m_t)
        num = a[..., None] * jnp.einsum('bthd,bhde->bthe', qc, C) + jnp.einsum('btsh,bshe->bthe', w, vc)
        den = a * jnp.einsum('bthd,bhd->bth', qc, n) + jnp.sum(w, axis=2)
        h = num / jnp.maximum(jnp.abs(den), jnp.exp(-m_t))[..., None]
        b_tot = bc[:, -1]
        g = b_tot[:, None] - bc + ic
        m_new = jnp.maximum(b_tot + m, jnp.max(g, axis=1))
        ws = jnp.exp(g - m_new[:, None])
        decay = jnp.exp(b_tot + m - m_new)
        C_new = decay[..., None, None] * C + jnp.einsum('bsh,bshd,bshe->bhde', ws, kc, vc)
        n_new = decay[..., None] * n + jnp.einsum('bsh,bshd->bhd', ws, kc)
        return (C_new, n_new, m_new), h

    xs = tuple(_to_blocks(t.astype(F32), T) for t in (q, k, v, log_i, log_f))
    state = tuple(s.astype(F32) for s in state)
    final, hs = lax.scan(step, state, xs)
    return _from_blocks(hs), final


def _hgrn2_scan(q, k, v, log_f, S):
    T = HGRN_CHUNK
    causal = jnp.tril(jnp.ones((T, T), bool))[None, :, :, None, None]

    def step(S, xs):
        qc, kc, vc, fc = xs
        A = jnp.cumsum(fc, axis=1)
        decay = jnp.exp(jnp.where(causal, A[:, :, None] - A[:, None], -jnp.inf))
        att = jnp.einsum('bthd,btshd,bshd->btsh', qc, decay, kc)
        o = jnp.einsum('bthd,bhde->bthe', qc * jnp.exp(A), S) + jnp.einsum('btsh,bshe->bthe', att, vc)
        A_tot = A[:, -1]
        S_new = jnp.exp(A_tot)[..., None] * S + jnp.einsum('bshd,bshe->bhde', kc * jnp.exp(A_tot[:, None] - A), vc)
        return S_new, o

    xs = tuple(_to_blocks(t.astype(F32), T) for t in (q, k, v, log_f))
    final, os_ = lax.scan(step, S.astype(F32), xs)
    return _from_blocks(os_), final


def _even_mixer(h, w_in, w_out, sink, gate_bias, norm_g, ctx):
    bsz, L, _ = h.shape
    aq, ak, av, bq, bk, bv, bg, bo = _split(h @ w_in, EVEN_SIZES)
    aq = aq.reshape(bsz, L, A_KV, A_HEADS // A_KV, HEAD_DIM)
    ak = ak.reshape(bsz, L, A_KV, HEAD_DIM)
    av = av.reshape(bsz, L, A_KV, HEAD_DIM)
    bq = bq.reshape(bsz, L, B_HEADS, B_DIM)
    bk = bk.reshape(bsz, L, B_HEADS, B_DIM) * (B_DIM ** -0.5)
    bv = bv.reshape(bsz, L, B_HEADS, B_DIM)
    gates = bg.reshape(bsz, L, 4, B_HEADS).astype(F32) + gate_bias.astype(F32)
    if ctx is None:
        zero = (jnp.zeros((bsz, B_HEADS, B_DIM, B_DIM), F32), jnp.zeros((bsz, B_HEADS, B_DIM), F32),
                jnp.zeros((bsz, B_HEADS), F32))
        st_f, st_b = zero, zero
        ya = _dense_attention(aq, ak, av, sink)
    else:
        k_ctx, v_ctx, C0, n0, m0 = ctx
        st_f = (C0[:, 0], n0[:, 0], m0[:, 0])
        st_b = (C0[:, 1], n0[:, 1], m0[:, 1])
        ya = _banded_attention(_axial_rope(aq), _axial_rope(ak), av, k_ctx, v_ctx, sink)
    hf, (Cf, nf, mf) = _mlstm_scan(bq, bk, bv, gates[:, :, 0], jax.nn.log_sigmoid(gates[:, :, 1]), st_f)
    hb, (Cb, nb, mb) = _mlstm_scan(_flip(bq), _flip(bk), _flip(bv), _flip(gates[:, :, 2]),
                                   _flip(jax.nn.log_sigmoid(gates[:, :, 3])), st_b)
    yb = jax.nn.sigmoid(bo.reshape(bsz, L, B_HEADS, B_DIM).astype(F32)) * _rms(hf + _flip(hb), norm_g)
    y = jnp.concatenate([ya.reshape(bsz, L, -1), yb.reshape(bsz, L, -1).astype(ya.dtype)], -1) @ w_out
    if ctx is None:
        return y, (ak, av, jnp.stack([Cf, Cb], 1), jnp.stack([nf, nb], 1), jnp.stack([mf, mb], 1))
    return y, None


def _odd_mixer(h, w_in, w_out, lb, norm_g, q_norm, k_norm, ctx):
    bsz, L, _ = h.shape
    cq, cf_f, cf_b, ci, cg, dq, dk, dv = _split(h @ w_in, ODD_SIZES)
    q = jax.nn.silu(cq.astype(F32)).reshape(bsz, L, C_HEADS, C_DK)
    v = ci.astype(F32).reshape(bsz, L, C_HEADS, C_DV)
    lbh = lb.reshape(C_HEADS, C_DK)

    def hgrn_gates(fp):
        f = lbh + (1.0 - lbh) * jax.nn.sigmoid(fp.astype(F32).reshape(bsz, L, C_HEADS, C_DK))
        return jnp.log(f), 1.0 - f

    lf_f, k_f = hgrn_gates(cf_f)
    lf_b, k_b = hgrn_gates(cf_b)
    dq = _rms(dq.reshape(bsz, L, D_KV, D_HEADS // D_KV, HEAD_DIM), q_norm)
    dk = _rms(dk.reshape(bsz, L, D_KV, HEAD_DIM), k_norm)
    dv = dv.reshape(bsz, L, D_KV, HEAD_DIM)
    if ctx is None:
        s0 = jnp.zeros((bsz, C_HEADS, C_DK, C_DV), F32)
        S_f0, S_b0 = s0, s0
    else:
        k_ctx, v_ctx, S0 = ctx
        S_f0, S_b0 = S0[:, 0], S0[:, 1]
    of, S_f = _hgrn2_scan(q, k_f, v, lf_f, S_f0)
    ob, S_b = _hgrn2_scan(_flip(q), _flip(k_b), _flip(v), _flip(lf_b), S_b0)
    yc = _rms(of + _flip(ob), norm_g) * jax.nn.silu(cg.astype(F32)).reshape(bsz, L, C_HEADS, C_DV)
    if ctx is None:
        yd = _dense_attention(dq, dk, dv, None)
        new = (dk, dv, jnp.stack([S_f, S_b], 1))
    else:
        keys = jnp.concatenate([_axial_rope(dk), k_ctx.astype(dk.dtype)], 1)
        vals = jnp.concatenate([dv, v_ctx.astype(dv.dtype)], 1)
        yd = _dense_attention(_axial_rope(dq), keys, vals, None)
        new = None
    y = jnp.concatenate([yc.reshape(bsz, L, -1).astype(yd.dtype), yd.reshape(bsz, L, -1)], -1) @ w_out
    return y, new


def setup_inputs(seed: int = 0) -> dict:
    key = jax.random.key(seed)
    ks = iter(jax.random.split(key, 32))

    def nrm(shape, scale=1.0):
        return scale * jax.random.normal(next(ks), shape, F32)

    fgate_base = jnp.array([0.0, 1.0, 0.0, 1.0], F32)[:, None] * jnp.linspace(3.0, 6.0, B_HEADS)[None]
    return {
        'x_prompt': nrm((BATCH, SEQ, D_MODEL)),
        'x_sample': nrm((DEC_BATCH, DEC_SEQ, D_MODEL)),
        'c': nrm((DEC_BATCH, D_MODEL)),
        'cache_a_k': nrm((DEC_BATCH, N_EVEN, PAST_LEN, A_KV, HEAD_DIM)),
        'cache_a_v': nrm((DEC_BATCH, N_EVEN, PAST_LEN, A_KV, HEAD_DIM)),
        'state_b_C': nrm((DEC_BATCH, N_EVEN, 2, B_HEADS, B_DIM, B_DIM), 0.1),
        'state_b_n': nrm((DEC_BATCH, N_EVEN, 2, B_HEADS, B_DIM), 0.1),
        'state_b_m': nrm((DEC_BATCH, N_EVEN, 2, B_HEADS)),
        'state_c_S': nrm((DEC_BATCH, N_ODD, 2, C_HEADS, C_DK, C_DV), 0.5),
        'cache_d_k': nrm((DEC_BATCH, N_ODD, PAST_LEN, D_KV, HEAD_DIM)),
        'cache_d_v': nrm((DEC_BATCH, N_ODD, PAST_LEN, D_KV, HEAD_DIM)),
        'c_ctx': nrm((D_MODEL,)),
        'ada_w': nrm((DEPTH, D_MODEL, 9 * D_MODEL), 0.5 * D_MODEL ** -0.5),
        'ada_b': nrm((DEPTH, 9 * D_MODEL), 0.01),
        'ln_g': 1.0 + nrm((DEPTH, 3, D_MODEL), 0.02),
        'ln_b': nrm((DEPTH, 3, D_MODEL), 0.02),
        'ffn_w1': nrm((DEPTH, 2, D_MODEL, D_FF), D_MODEL ** -0.5),
        'ffn_w3': nrm((DEPTH, 2, D_MODEL, D_FF), D_MODEL ** -0.5),
        'ffn_w2': nrm((DEPTH, 2, D_FF, D_MODEL), BETA * D_FF ** -0.5),
        'w_in_even': nrm((N_EVEN, D_MODEL, sum(EVEN_SIZES)), D_MODEL ** -0.5),
        'w_out_even': nrm((N_EVEN, EVEN_OUT, D_MODEL), BETA * EVEN_OUT ** -0.5),
        'a_sink': nrm((N_EVEN, A_KV, A_HEADS // A_KV), 0.5),
        'b_gate_bias': fgate_base + nrm((N_EVEN, 4, B_HEADS), 0.1),
        'b_norm_g': 1.0 + nrm((N_EVEN, B_HEADS, B_DIM), 0.02),
        'w_in_odd': nrm((N_ODD, D_MODEL, sum(ODD_SIZES)), D_MODEL ** -0.5),
        'w_out_odd': nrm((N_ODD, ODD_OUT, D_MODEL), BETA * ODD_OUT ** -0.5),
        'c_lb_logits': nrm((DEPTH, C_HEADS * C_DK)),
        'c_norm_g': 1.0 + nrm((N_ODD, C_HEADS, C_DV), 0.02),
        'd_q_norm': 1.0 + nrm((N_ODD, HEAD_DIM), 0.02),
        'd_k_norm': 1.0 + nrm((N_ODD, HEAD_DIM), 0.02),
    }


def reference(x_prompt, x_sample, c, cache_a_k, cache_a_v, state_b_C, state_b_n, state_b_m, state_c_S,
              cache_d_k, cache_d_v, c_ctx, ada_w, ada_b, ln_g, ln_b, ffn_w1, ffn_w3, ffn_w2,
              w_in_even, w_out_even, a_sink, b_gate_bias, b_norm_g, w_in_odd, w_out_odd,
              c_lb_logits, c_norm_g, d_q_norm, d_k_norm):
    lb_sm = jax.nn.softmax(c_lb_logits.astype(F32), axis=0)
    lower_bounds = jnp.cumsum(lb_sm, axis=0) - lb_sm[0]
    xp, xs = x_prompt, x_sample
    a_k_l, a_v_l, b_C_l, b_n_l, b_m_l, c_S_l, d_k_l, d_v_l = [], [], [], [], [], [], [], []
    for l in range(DEPTH):
        mod_p = _modulation(c_ctx[None], ada_w[l], ada_b[l])
        mod_s = _modulation(c, ada_w[l], ada_b[l])
        xp = _ffn_sublayer(xp, mod_p, 0, ln_g[l, 0], ln_b[l, 0], ffn_w1[l, 0], ffn_w3[l, 0], ffn_w2[l, 0])
        xs = _ffn_sublayer(xs, mod_s, 0, ln_g[l, 0], ln_b[l, 0], ffn_w1[l, 0], ffn_w3[l, 0], ffn_w2[l, 0])
        sh_p, sc_p, gt_p = _mod_parts(mod_p, 1)
        sh_s, sc_s, gt_s = _mod_parts(mod_s, 1)
        hp = xp * (1 + sc_p) + sh_p
        hs = xs * (1 + sc_s) + sh_s
        i = l // 2
        if l % 2 == 0:
            yp, (ak, av, bC, bn, bm) = _even_mixer(hp, w_in_even[i], w_out_even[i], a_sink[i], b_gate_bias[i],
                                                   b_norm_g[i], None)
            ys, _ = _even_mixer(hs, w_in_even[i], w_out_even[i], a_sink[i], b_gate_bias[i], b_norm_g[i],
                                (cache_a_k[:, i], cache_a_v[:, i], state_b_C[:, i], state_b_n[:, i],
                                 state_b_m[:, i]))
            a_k_l.append(ak)
            a_v_l.append(av)
            b_C_l.append(bC)
            b_n_l.append(bn)
            b_m_l.append(bm)
        else:
            yp, (dk, dv, cS) = _odd_mixer(hp, w_in_odd[i], w_out_odd[i], lower_bounds[l], c_norm_g[i],
                                          d_q_norm[i], d_k_norm[i], None)
            ys, _ = _odd_mixer(hs, w_in_odd[i], w_out_odd[i], lower_bounds[l], c_norm_g[i], d_q_norm[i],
                               d_k_norm[i], (cache_d_k[:, i], cache_d_v[:, i], state_c_S[:, i]))
            d_k_l.append(dk)
            d_v_l.append(dv)
            c_S_l.append(cS)
        xp = _layernorm(ALPHA * xp + gt_p * yp, ln_g[l, 1], ln_b[l, 1])
        xs = _layernorm(ALPHA * xs + gt_s * ys, ln_g[l, 1], ln_b[l, 1])
        xp = _ffn_sublayer(xp, mod_p, 2, ln_g[l, 2], ln_b[l, 2], ffn_w1[l, 1], ffn_w3[l, 1], ffn_w2[l, 1])
        xs = _ffn_sublayer(xs, mod_s, 2, ln_g[l, 2], ln_b[l, 2], ffn_w1[l, 1], ffn_w3[l, 1], ffn_w2[l, 1])
    return (xp, xs, jnp.stack(a_k_l, 1), jnp.stack(a_v_l, 1), jnp.stack(b_C_l, 1), jnp.stack(b_n_l, 1),
            jnp.stack(b_m_l, 1), jnp.stack(c_S_l, 1), jnp.stack(d_k_l, 1), jnp.stack(d_v_l, 1))
```

```python
import functools

import jax
import jax.numpy as jnp
from jax import lax
from jax.experimental import pallas as pl
from jax.experimental.pallas import tpu as pltpu

F32 = jnp.float32
BF16 = jnp.bfloat16

D_MODEL = 1024
DEPTH = 2
GRID_W = 64
HEAD_DIM = 64
A_HEADS = 8
A_KV = 2
WINDOW = 128
B_HEADS = 4
B_DIM = 128
C_HEADS = 4
C_DK = 128
C_DV = 128
D_HEADS = 8
D_KV = 2
D_FF = 2816
Q_BLOCK = 128
ROPE_THETA = 10000.0
ALPHA = (2 * DEPTH) ** 0.25
NEG_INF = -1e30

MOD_ROWS = 8
FF_CHUNK = 256
TOKEN_TILE = 512
MLSTM_T = 64
HGRN_T = 32
SUB = 8
ATT_KEY_TILE = 512
VMEM_LIMIT = 56 * 1024 * 1024


def _cparams(sem):
    return pltpu.CompilerParams(dimension_semantics=sem, vmem_limit_bytes=VMEM_LIMIT)


def _dot(a, b):
    return jnp.dot(a, b, preferred_element_type=F32)


def _dot_nt(a, b):
    return lax.dot_general(a, b, (((1,), (1,)), ((), ())), preferred_element_type=F32)


def _dot_tn(a, b):
    return lax.dot_general(a, b, (((0,), (0,)), ((), ())), preferred_element_type=F32)


def _split3(x):
    hi = x.astype(BF16)
    r1 = x - hi.astype(F32)
    mid = r1.astype(BF16)
    lo = (r1 - mid.astype(F32)).astype(BF16)
    return hi, mid, lo


def _dot_sel(sel, x):
    hi, mid, lo = _split3(x)
    return _dot(sel, hi) + _dot(sel, mid) + _dot(sel, lo)


def _silu(x):
    return x * jax.nn.sigmoid(x)


def _layernorm(z, g, b):
    mu = jnp.mean(z, -1, keepdims=True)
    zc = z - mu
    var = jnp.mean(zc * zc, -1, keepdims=True)
    return zc * lax.rsqrt(var + 1e-5) * g + b


def _rms_lastdim(x, g):
    return x * lax.rsqrt(jnp.mean(x * x, -1, keepdims=True) + 1e-6) * g


def _mod_index(group_start, tiles_per_request):
    if tiles_per_request is None:
        return lambda i: (group_start, 0, 0)
    return lambda i: (group_start + i // tiles_per_request, 0, 0)


def _mod_kernel(c_ref, w_ref, b_ref, o_ref):
    s = _silu(c_ref[...]).astype(BF16)
    o_ref[0] = _dot(s, w_ref[0].astype(BF16)) + b_ref[0]


def _modulation(cvec, ada_w, ada_b):
    depth, d, n = ada_w.shape
    tn = 1152
    return pl.pallas_call(
        _mod_kernel,
        grid=(depth, n // tn),
        in_specs=[pl.BlockSpec((MOD_ROWS, d), lambda l, j: (0, 0)),
                  pl.BlockSpec((1, d, tn), lambda l, j: (l, 0, j)),
                  pl.BlockSpec((1, 1, tn), lambda l, j: (l, 0, j))],
        out_specs=pl.BlockSpec((1, MOD_ROWS, tn), lambda l, j: (l, 0, j)),
        out_shape=jax.ShapeDtypeStruct((depth, MOD_ROWS, n), F32),
        compiler_params=_cparams(("parallel", "parallel")),
        name="modulation",
    )(cvec, ada_w, ada_b.reshape(depth, 1, n))


def _ffn_kernel(x_ref, mod_ref, w1_ref, w3_ref, w2_ref, g_ref, b_ref, o_ref, h_ref, acc_ref, *, j, nf):
    m = mod_ref[0]
    shift, scale, gate = m[3 * j:3 * j + 1], m[3 * j + 1:3 * j + 2], m[3 * j + 2:3 * j + 3]
    h_ref[...] = (x_ref[...] * (1.0 + scale) + shift).astype(BF16)
    acc_ref[...] = jnp.zeros_like(acc_ref)

    def body(f, carry):
        h = h_ref[...]
        a = _dot(h, w1_ref[f])
        b = _dot(h, w3_ref[f])
        u = (_silu(a) * b).astype(BF16)
        acc_ref[...] += _dot(u, w2_ref[f])
        return carry

    lax.fori_loop(0, nf, body, 0)
    z = ALPHA * x_ref[...] + 0.5 * gate * acc_ref[...]
    o_ref[...] = _layernorm(z, g_ref[...], b_ref[...])


def _ffn(x, mod, j, g, b, w1, w3, w2, mod_index):
    n, d = x.shape
    nf = w1.shape[0]
    tm = TOKEN_TILE
    whole = lambda a: pl.BlockSpec(a.shape, lambda i: (0,) * a.ndim, pipeline_mode=pl.Buffered(1))
    return pl.pallas_call(
        functools.partial(_ffn_kernel, j=j, nf=nf),
        grid=(n // tm,),
        in_specs=[pl.BlockSpec((tm, d), lambda i: (i, 0)),
                  pl.BlockSpec((1, 9, d), mod_index),
                  whole(w1), whole(w3), whole(w2),
                  pl.BlockSpec((1, d), lambda i: (0, 0)),
                  pl.BlockSpec((1, d), lambda i: (0, 0))],
        out_specs=pl.BlockSpec((tm, d), lambda i: (i, 0)),
        out_shape=jax.ShapeDtypeStruct((n, d), F32),
        scratch_shapes=[pltpu.VMEM((tm, d), BF16), pltpu.VMEM((tm, d), F32)],
        compiler_params=_cparams(("parallel",)),
        name="ffn_sublayer",
    )(x, mod, w1, w3, w2, g.reshape(1, d), b.reshape(1, d))


def _rope_tables(length):
    t = jnp.arange(length)
    nf = HEAD_DIM // 4
    inv = ROPE_THETA ** (-jnp.arange(nf, dtype=F32) / nf)
    ang_r = (t // GRID_W).astype(F32)[:, None] * inv[None]
    ang_c = (t % GRID_W).astype(F32)[:, None] * inv[None]
    cr, sr, cc, sc = jnp.cos(ang_r), jnp.sin(ang_r), jnp.cos(ang_c), jnp.sin(ang_c)
    z = jnp.zeros_like(cr)
    cos = jnp.concatenate([cr, cr, cc, cc], 1)
    sin_up = jnp.concatenate([-sr, z, -sc, z], 1)
    sin_dn = jnp.concatenate([z, sr, z, sc], 1)
    two = lambda a: jnp.concatenate([a, a], 1)
    return two(cos), two(sin_up), two(sin_dn)


def _rope128(x, cos, sin_up, sin_dn):
    nf = HEAD_DIM // 4
    return x * cos + pltpu.roll(x, 128 - nf, 1) * sin_up + pltpu.roll(x, nf, 1) * sin_dn


def _rope(x, cos, sin_up, sin_dn):
    parts = [_rope128(x[:, c:c + 128], cos, sin_up, sin_dn) for c in range(0, x.shape[1], 128)]
    return parts[0] if len(parts) == 1 else jnp.concatenate(parts, 1)


EVEN_COLS = (512, 256, 512, 512, 512, 512, 128)


def _even_in_kernel(*refs, rope, cache):
    it = iter(refs)
    x_ref, mod_ref, w_ref = next(it), next(it), next(it)
    tabs = (next(it), next(it), next(it)) if rope else None
    aq_ref, akv_ref = next(it), next(it)
    cache_ref = next(it) if cache else None
    bq_ref, bk_ref, bv_ref, bo_ref, bg_ref = next(it), next(it), next(it), next(it), next(it)

    m = mod_ref[0]
    h = (x_ref[...] * (1.0 + m[4:5]) + m[3:4]).astype(BF16)
    offs = [0]
    for c in EVEN_COLS:
        offs.append(offs[-1] + c)
    proj = lambda k: _dot(h, w_ref[:, offs[k]:offs[k + 1]])

    aq = proj(0)
    akv = proj(1)
    if cache:
        cache_ref[...] = akv
    if rope:
        cos, s_up, s_dn = (t[...] for t in tabs)
        aq = _rope(aq, cos, s_up, s_dn)
        akv = jnp.concatenate([_rope(akv[:, :128], cos, s_up, s_dn), akv[:, 128:]], 1)
    aq_ref[...] = aq.astype(BF16)
    akv_ref[...] = akv.astype(BF16)
    bq_ref[...] = proj(2).astype(BF16)
    bk_ref[...] = (proj(3) * (B_DIM ** -0.5)).astype(BF16)
    bv_ref[...] = proj(4).astype(BF16)
    bo_ref[...] = proj(5)
    bg_ref[...] = proj(6)


def _even_in(x, mod, w, mod_index, seq_len, rope, cache):
    n, d = x.shape
    tm = TOKEN_TILE
    row = lambda width: pl.BlockSpec((tm, width), lambda i: (i, 0))
    in_specs = [row(d), pl.BlockSpec((1, 9, d), mod_index),
                pl.BlockSpec(w.shape, lambda i: (0, 0), pipeline_mode=pl.Buffered(1))]
    args = [x, mod, w]
    if rope:
        per_seq = seq_len // tm
        tabs = _rope_tables(seq_len)
        in_specs += [pl.BlockSpec((tm, 128), lambda i: (i % per_seq, 0))] * 3
        args += list(tabs)
    outs = [(512, BF16), (256, BF16)] + ([(256, F32)] if cache else []) + \
           [(512, BF16), (512, BF16), (512, BF16), (512, F32), (128, F32)]
    return pl.pallas_call(
        functools.partial(_even_in_kernel, rope=rope, cache=cache),
        grid=(n // tm,),
        in_specs=in_specs,
        out_specs=[row(wd) for wd, _ in outs],
        out_shape=[jax.ShapeDtypeStruct((n, wd), dt) for wd, dt in outs],
        compiler_params=_cparams(("parallel",)),
        name="even_in_proj",
    )(*args)


ODD_COLS = (512, 512, 512, 512, 512, 512, 256)


def _head_rms(x, seg_ref, g):
    x2 = x * x
    hi = x2.astype(BF16)
    lo = (x2 - hi.astype(F32)).astype(BF16)
    ms = _dot(hi, seg_ref[...]) + _dot(lo, seg_ref[...])
    return x * lax.rsqrt(ms + 1e-6) * g


def _odd_in_kernel(*refs, rope, cache):
    it = iter(refs)
    x_ref, mod_ref, w_ref, segq_ref, segk_ref, qn_ref, kn_ref = (next(it) for _ in range(7))
    tabs = (next(it), next(it), next(it)) if rope else None
    q_ref, ff_ref, fb_ref, v_ref, cg_ref, dq_ref, dkv_ref = (next(it) for _ in range(7))
    cache_ref = next(it) if cache else None

    m = mod_ref[0]
    h = (x_ref[...] * (1.0 + m[4:5]) + m[3:4]).astype(BF16)
    offs = [0]
    for c in ODD_COLS:
        offs.append(offs[-1] + c)
    proj = lambda k: _dot(h, w_ref[:, offs[k]:offs[k + 1]])

    q_ref[...] = _silu(proj(0)).astype(BF16)
    ff_ref[...] = proj(1)
    fb_ref[...] = proj(2)
    v_ref[...] = proj(3).astype(BF16)
    cg_ref[...] = proj(4)
    dq = _head_rms(proj(5), segq_ref, qn_ref[...])
    dkv = proj(6)
    dk = _head_rms(dkv[:, :128], segk_ref, kn_ref[...])
    dv = dkv[:, 128:]
    if cache:
        cache_ref[...] = jnp.concatenate([dk, dv], 1)
    if rope:
        cos, s_up, s_dn = (t[...] for t in tabs)
        dq = _rope(dq, cos, s_up, s_dn)
        dk = _rope(dk, cos, s_up, s_dn)
    dq_ref[...] = dq.astype(BF16)
    dkv_ref[...] = jnp.concatenate([dk, dv], 1).astype(BF16)


def _segment_mean_matrix(width):
    r = jnp.arange(width) // HEAD_DIM
    return jnp.where(r[:, None] == r[None, :], 1.0 / HEAD_DIM, 0.0).astype(BF16)


def _odd_in(x, mod, w, q_norm, k_norm, mod_index, seq_len, rope, cache):
    n, d = x.shape
    tm = TOKEN_TILE
    row = lambda width: pl.BlockSpec((tm, width), lambda i: (i, 0))
    const = lambda a: pl.BlockSpec(a.shape, lambda i: (0, 0))
    segq, segk = _segment_mean_matrix(512), _segment_mean_matrix(128)
    qn = jnp.tile(q_norm, D_HEADS).reshape(1, 512)
    kn = jnp.tile(k_norm, D_KV).reshape(1, 128)
    in_specs = [row(d), pl.BlockSpec((1, 9, d), mod_index),
                pl.BlockSpec(w.shape, lambda i: (0, 0), pipeline_mode=pl.Buffered(1)),
                const(segq), const(segk), const(qn), const(kn)]
    args = [x, mod, w, segq, segk, qn, kn]
    if rope:
        per_seq = seq_len // tm
        tabs = _rope_tables(seq_len)
        in_specs += [pl.BlockSpec((tm, 128), lambda i: (i % per_seq, 0))] * 3
        args += list(tabs)
    outs = [(512, BF16), (512, F32), (512, F32), (512, BF16), (512, F32), (512, BF16), (256, BF16)] + \
           ([(256, F32)] if cache else [])
    return pl.pallas_call(
        functools.partial(_odd_in_kernel, rope=rope, cache=cache),
        grid=(n // tm,),
        in_specs=in_specs,
        out_specs=[row(wd) for wd, _ in outs],
        out_shape=[jax.ShapeDtypeStruct((n, wd), dt) for wd, dt in outs],
        compiler_params=_cparams(("parallel",)),
        name="odd_in_proj",
    )(*args)


def _attn_kernel(*refs, seq_len, tq, n_ctx, banded, has_sink):
    it = iter(refs)
    q_ref, kv_ref = next(it), next(it)
    ck_ref, cv_ref = (next(it), next(it)) if n_ctx else (None, None)
    sink_ref = next(it) if has_sink else None
    o_ref = next(it)
    groups = A_HEADS // A_KV
    rows = groups * tq
    j = pl.program_id(1)

    if banded:
        span = tq + 2 * WINDOW
        start = pl.multiple_of(jnp.clip(j * tq - WINDOW, 0, seq_len - span), WINDOW)
        qpos = j * tq + (lax.broadcasted_iota(jnp.int32, (rows, span), 0) & (tq - 1))
        kpos = start + lax.broadcasted_iota(jnp.int32, (rows, span), 1)
        band = jnp.abs(kpos - qpos) <= WINDOW

    for kh in range(A_KV):
        kcol = slice(kh * HEAD_DIM, (kh + 1) * HEAD_DIM)
        vcol = slice(128 + kh * HEAD_DIM, 128 + (kh + 1) * HEAD_DIM)
        qs = jnp.concatenate([q_ref[:, (kh * groups + g) * HEAD_DIM:(kh * groups + g + 1) * HEAD_DIM]
                              for g in range(groups)], axis=0)
        if has_sink:
            m = jnp.concatenate([jnp.full((tq, 1), sink_ref[kh * groups + g], F32) for g in range(groups)], 0)
            l = jnp.ones((rows, 1), F32)
        else:
            m = jnp.full((rows, 1), NEG_INF, F32)
            l = jnp.zeros((rows, 1), F32)
        acc = jnp.zeros((rows, HEAD_DIM), F32)

        tiles = []
        if banded:
            tiles.append((kv_ref[pl.ds(start, span), kcol], kv_ref[pl.ds(start, span), vcol], band))
        else:
            tk = min(ATT_KEY_TILE, seq_len)
            for t in range(seq_len // tk):
                tiles.append((kv_ref[t * tk:(t + 1) * tk, kcol], kv_ref[t * tk:(t + 1) * tk, vcol], None))
        if n_ctx:
            tiles.append((ck_ref[0, :, kcol], cv_ref[0, :, kcol], None))

        for k_t, v_t, mask in tiles:
            s = _dot_nt(qs, k_t) * (HEAD_DIM ** -0.5)
            if mask is not None:
                s = jnp.where(mask, s, NEG_INF)
            m_new = jnp.maximum(m, jnp.max(s, -1, keepdims=True))
            p = jnp.exp(s - m_new)
            alpha = jnp.exp(m - m_new)
            l = alpha * l + jnp.sum(p, -1, keepdims=True)
            acc = alpha * acc + _dot(p.astype(BF16), v_t)
            m = m_new
        o = acc / l
        for g in range(0, groups, 2):
            pair = jnp.concatenate([o[g * tq:(g + 1) * tq], o[(g + 1) * tq:(g + 2) * tq]], 1)
            c0 = (kh * groups + g) * HEAD_DIM
            o_ref[:, c0:c0 + 2 * HEAD_DIM] = pair.astype(BF16)


def _attention(q, kv, ctx_k, ctx_v, sink, batch, seq_len, tq, banded):
    n_ctx = 0 if ctx_k is None else ctx_k.shape[1]
    per_seq = seq_len // tq
    in_specs = [pl.BlockSpec((tq, 512), lambda b, j: (b * per_seq + j, 0)),
                pl.BlockSpec((seq_len, 256), lambda b, j: (b, 0))]
    args = [q, kv]
    if n_ctx:
        in_specs += [pl.BlockSpec((1, n_ctx, 128), lambda b, j: (b, 0, 0))] * 2
        args += [ctx_k, ctx_v]
    if sink is not None:
        in_specs.append(pl.BlockSpec(memory_space=pltpu.SMEM))
        args.append(sink.reshape(-1).astype(F32))
    return pl.pallas_call(
        functools.partial(_attn_kernel, seq_len=seq_len, tq=tq, n_ctx=n_ctx, banded=banded,
                          has_sink=sink is not None),
        grid=(batch, per_seq),
        in_specs=in_specs,
        out_specs=pl.BlockSpec((tq, 512), lambda b, j: (b * per_seq + j, 0)),
        out_shape=jax.ShapeDtypeStruct(q.shape, BF16),
        compiler_params=_cparams(("parallel", "arbitrary")),
        name="gqa_attention",
    )(*args)


def _log_sigmoid(x):
    return jnp.minimum(x, 0.0) - jnp.log1p(jnp.exp(-jnp.abs(x)))


def _mlstm_kernel(*refs, seq_len, has_init, emit_state):
    it = iter(refs)
    q_ref, k_ref, v_ref, g_ref, bo_ref, gb_ref, ng_ref = (next(it) for _ in range(7))
    c0_ref, n0_ref, m0_ref = (next(it), next(it), next(it)) if has_init else (None, None, None)
    y_ref = next(it)
    co_ref, no_ref, mo_ref = (next(it), next(it), next(it)) if emit_state else (None, None, None)
    c_scr, n_scr, m_scr, hf_scr, hb_scr = (next(it) for _ in range(5))

    t_len = MLSTM_T
    n_chunks = seq_len // t_len
    n_streams = 2 * B_HEADS

    for s in range(n_streams):
        if has_init:
            c_scr[s] = c0_ref[0, s // B_HEADS, s % B_HEADS]
        else:
            c_scr[s] = jnp.zeros((B_DIM, B_DIM), F32)
    if has_init:
        n_scr[...] = n0_ref[0]
        m_scr[...] = m0_ref[0]
    else:
        n_scr[...] = jnp.zeros_like(n_scr)
        m_scr[...] = jnp.zeros_like(m_scr)

    ri = lax.broadcasted_iota(jnp.int32, (t_len, t_len), 0)
    ci = lax.broadcasted_iota(jnp.int32, (t_len, t_len), 1)
    eye = ri == ci
    lower = ci <= ri
    upper = ci >= ri
    tri_f = jnp.where(lower, 1.0, 0.0).astype(BF16)
    tri_b = jnp.where(upper, 1.0, 0.0).astype(BF16)
    ones8 = jnp.ones((8, t_len), BF16)
    gate_bias = gb_ref[...]

    def step(c, carry):
        for direction in range(2):
            chunk = c if direction == 0 else n_chunks - 1 - c
            r0 = pl.multiple_of(chunk * t_len, t_len)
            rows = pl.ds(r0, t_len)
            gates = g_ref[rows, :] + gate_bias
            log_f = _log_sigmoid(gates)
            tri = tri_f if direction == 0 else tri_b
            causal = lower if direction == 0 else upper
            key_upto = upper if direction == 0 else lower
            last = t_len - 1 if direction == 0 else 0
            for hd in range(B_HEADS):
                s = direction * B_HEADS + hd
                col = slice(hd * B_DIM, (hd + 1) * B_DIM)
                i_col = 2 * direction * B_HEADS + hd
                f_col = (2 * direction + 1) * B_HEADS + hd
                ig = jnp.broadcast_to(gates[:, i_col:i_col + 1], (t_len, t_len))
                fg = jnp.broadcast_to(log_f[:, f_col:f_col + 1], (t_len, t_len))
                bc = _dot_sel(tri, fg)
                x = jnp.where(eye, ig, 0.0) - jnp.where(key_upto, fg, 0.0)
                rrow = _dot_sel(ones8, x)[0:1]
                dmat = jnp.where(causal, bc + rrow, -jnp.inf)
                m_prev = m_scr[s:s + 1, :t_len]
                inter = bc + m_prev
                m_t = jnp.maximum(inter, jnp.max(dmat, -1, keepdims=True))
                qc, kc, vc = q_ref[rows, col], k_ref[rows, col], v_ref[rows, col]
                w = jnp.exp(dmat - m_t) * _dot_nt(qc, kc)
                a = jnp.exp(inter - m_t)[:, 0:1]
                c_prev = c_scr[s]
                n_prev = n_scr[s:s + 1, :]
                num = a * _dot(qc, c_prev.astype(BF16)) + _dot(w.astype(BF16), vc)
                qn = jnp.sum(qc.astype(F32) * n_prev, -1, keepdims=True)
                den = a * qn + jnp.sum(w, -1, keepdims=True)
                h = num / jnp.maximum(jnp.abs(den), jnp.exp(-m_t[:, 0:1]))
                if direction == 0:
                    hf_scr[rows, col] = h
                else:
                    hb_scr[rows, col] = h
                b_tot = bc[last:last + 1, :]
                g = b_tot - bc + ig
                m_new = jnp.maximum(b_tot + m_prev, jnp.max(g, 0, keepdims=True))
                ws = jnp.exp(g - m_new)
                decay = jnp.exp(b_tot + m_prev - m_new)
                ws2 = jnp.concatenate([ws, ws], 1)
                decay2 = jnp.concatenate([decay, decay], 1)
                kw = kc.astype(F32) * ws2
                c_scr[s] = decay2 * c_prev + _dot_tn(kw.astype(BF16), vc)
                n_scr[s:s + 1, :] = decay2 * n_prev + jnp.sum(kw, 0, keepdims=True)
                m_scr[s:s + 1, :] = jnp.concatenate([m_new, m_new], 1)
        return carry

    lax.fori_loop(0, n_chunks, step, 0)

    blk = min(256, seq_len)
    ng = ng_ref[...]

    def finish(i, carry):
        rows = pl.ds(pl.multiple_of(i * blk, blk), blk)
        hsum = hf_scr[rows, :] + hb_scr[rows, :]
        gate = jax.nn.sigmoid(bo_ref[rows, :])
        parts = [_rms_lastdim(hsum[:, hd * B_DIM:(hd + 1) * B_DIM], ng[:, hd * B_DIM:(hd + 1) * B_DIM])
                 for hd in range(B_HEADS)]
        y_ref[rows, :] = (gate * jnp.concatenate(parts, 1)).astype(BF16)
        return carry

    lax.fori_loop(0, seq_len // blk, finish, 0)

    if emit_state:
        for s in range(n_streams):
            co_ref[0, s // B_HEADS, s % B_HEADS] = c_scr[s]
        no_ref[0] = n_scr[...]
        mo_ref[0] = m_scr[...]


def _mlstm(q, k, v, gates, bo, gate_bias, norm_g, init, batch, seq_len, emit_state):
    n = q.shape[0]
    row = lambda width: pl.BlockSpec((seq_len, width), lambda b: (b, 0))
    const = lambda a: pl.BlockSpec(a.shape, lambda b: (0,) * a.ndim)
    gb = jnp.zeros((1, 128), F32).at[0, :4 * B_HEADS].set(gate_bias.reshape(-1))
    ng = norm_g.reshape(1, B_HEADS * B_DIM)
    in_specs = [row(512), row(512), row(512), row(128), row(512), const(gb), const(ng)]
    args = [q, k, v, gates, bo, gb, ng]
    n_streams = 2 * B_HEADS
    if init is not None:
        c0, n0, m0 = init
        in_specs += [pl.BlockSpec((1, 2, B_HEADS, B_DIM, B_DIM), lambda b: (b, 0, 0, 0, 0)),
                     pl.BlockSpec((1, n_streams, B_DIM), lambda b: (b, 0, 0)),
                     pl.BlockSpec((1, n_streams, B_DIM), lambda b: (b, 0, 0))]
        args += [c0, n0.reshape(batch, n_streams, B_DIM),
                 jnp.broadcast_to(m0.reshape(batch, n_streams, 1), (batch, n_streams, B_DIM))]
    out_specs = [row(512)]
    out_shape = [jax.ShapeDtypeStruct((n, 512), BF16)]
    if emit_state:
        out_specs += [pl.BlockSpec((1, 2, B_HEADS, B_DIM, B_DIM), lambda b: (b, 0, 0, 0, 0)),
                      pl.BlockSpec((1, n_streams, B_DIM), lambda b: (b, 0, 0)),
                      pl.BlockSpec((1, n_streams, B_DIM), lambda b: (b, 0, 0))]
        out_shape += [jax.ShapeDtypeStruct((batch, 2, B_HEADS, B_DIM, B_DIM), F32),
                      jax.ShapeDtypeStruct((batch, n_streams, B_DIM), F32),
                      jax.ShapeDtypeStruct((batch, n_streams, B_DIM), F32)]
    return pl.pallas_call(
        functools.partial(_mlstm_kernel, seq_len=seq_len, has_init=init is not None, emit_state=emit_state),
        grid=(batch,),
        in_specs=in_specs,
        out_specs=out_specs,
        out_shape=out_shape,
        scratch_shapes=[pltpu.VMEM((n_streams, B_DIM, B_DIM), F32),
                        pltpu.VMEM((n_streams, B_DIM), F32),
                        pltpu.VMEM((n_streams, B_DIM), F32),
                        pltpu.VMEM((seq_len, 512), F32),
                        pltpu.VMEM((seq_len, 512), F32)],
        compiler_params=_cparams(("parallel",)),
        name="mlstm_scan",
    )(*args)


def _hgrn_kernel(*refs, seq_len, layer, has_init, emit_state):
    it = iter(refs)
    q_ref, ff_ref, fb_ref, v_ref, cg_ref, lbl_ref, ng_ref = (next(it) for _ in range(7))
    s0_ref = next(it) if has_init else None
    y_ref = next(it)
    so_ref = next(it) if emit_state else None
    st_scr, of_scr, ob_scr = next(it), next(it), next(it)

    t_len = HGRN_T
    n_sub = t_len // SUB
    n_chunks = seq_len // t_len
    n_streams = 2 * C_HEADS

    logits = lbl_ref[...]
    e = jnp.exp(logits - jnp.max(logits, 0, keepdims=True))
    sm = e / jnp.sum(e, 0, keepdims=True)
    lb = jnp.sum(sm[0:layer + 1], 0, keepdims=True) - sm[0:1]

    for s in range(n_streams):
        if has_init:
            st_scr[s] = s0_ref[0, s // C_HEADS, s % C_HEADS].T
        else:
            st_scr[s] = jnp.zeros((C_DV, C_DK), F32)

    ri = lax.broadcasted_iota(jnp.int32, (t_len, t_len), 0)
    ci = lax.broadcasted_iota(jnp.int32, (t_len, t_len), 1)
    tri_f = jnp.where(ci <= ri, 1.0, 0.0).astype(BF16)
    tri_b = jnp.where(ci >= ri, 1.0, 0.0).astype(BF16)
    sub_row = lax.broadcasted_iota(jnp.int32, (SUB, C_DK), 0)
    ones_dk = jnp.ones((C_DK, C_DK), BF16)

    def step(c, carry):
        for direction in range(2):
            chunk = c if direction == 0 else n_chunks - 1 - c
            r0 = pl.multiple_of(chunk * t_len, t_len)
            rows = pl.ds(r0, t_len)
            f_ref = ff_ref if direction == 0 else fb_ref
            tri = tri_f if direction == 0 else tri_b
            last = t_len - 1 if direction == 0 else 0
            for hd in range(C_HEADS):
                s = direction * C_HEADS + hd
                col = slice(hd * C_DK, (hd + 1) * C_DK)
                lbh = lb[:, col]
                f = lbh + (1.0 - lbh) * jax.nn.sigmoid(f_ref[rows, col])
                kk = 1.0 - f
                a_cum = _dot_sel(tri, jnp.log(f))
                a_tot = a_cum[last:last + 1, :]
                qf = q_ref[rows, col].astype(F32)
                vc = v_ref[rows, col]
                vf = vc.astype(F32)
                st = st_scr[s]
                inter = _dot_nt((qf * jnp.exp(a_cum)).astype(BF16), st.astype(BF16))
                kd = (kk * jnp.exp(a_tot - a_cum)).astype(BF16)
                st_scr[s] = jnp.exp(a_tot) * st + _dot_tn(vc, kd)

                outs = []
                for blk in range(n_sub):
                    b0 = blk * SUB
                    a_i, q_i, k_i, v_i = (t[b0:b0 + SUB] for t in (a_cum, qf, kk, vf))
                    ps = []
                    for j in range(SUB):
                        seen = (sub_row >= j) if direction == 0 else (sub_row <= j)
                        dec = jnp.where(seen, jnp.exp(a_i - a_i[j:j + 1]), 0.0)
                        ps.append(dec * q_i * k_i[j:j + 1])
                    att = _dot(jnp.concatenate(ps, 0).astype(BF16), ones_dk)
                    o_i = inter[b0:b0 + SUB]
                    for j in range(SUB):
                        o_i = o_i + att[j * SUB:(j + 1) * SUB] * v_i[j:j + 1]
                    if direction == 0 and blk > 0:
                        a_ref = a_cum[b0 - 1:b0]
                        kr = slice(0, b0)
                    elif direction == 1 and blk < n_sub - 1:
                        a_ref = a_cum[b0 + SUB:b0 + SUB + 1]
                        kr = slice(b0 + SUB, t_len)
                    else:
                        kr = None
                    if kr is not None:
                        qt = (q_i * jnp.exp(a_i - a_ref)).astype(BF16)
                        kt = (kk[kr] * jnp.exp(a_ref - a_cum[kr])).astype(BF16)
                        o_i = o_i + _dot(_dot_nt(qt, kt).astype(BF16), vc[kr])
                    outs.append(o_i)
                o = jnp.concatenate(outs, 0)
                if direction == 0:
                    of_scr[rows, col] = o
                else:
                    ob_scr[rows, col] = o
        return carry

    lax.fori_loop(0, n_chunks, step, 0)

    blk_rows = min(256, seq_len)
    ng = ng_ref[...]

    def finish(i, carry):
        rows = pl.ds(pl.multiple_of(i * blk_rows, blk_rows), blk_rows)
        osum = of_scr[rows, :] + ob_scr[rows, :]
        parts = [_rms_lastdim(osum[:, hd * C_DV:(hd + 1) * C_DV], ng[:, hd * C_DV:(hd + 1) * C_DV])
                 for hd in range(C_HEADS)]
        y_ref[rows, :] = (jnp.concatenate(parts, 1) * _silu(cg_ref[rows, :])).astype(BF16)
        return carry

    lax.fori_loop(0, seq_len // blk_rows, finish, 0)

    if emit_state:
        for s in range(n_streams):
            so_ref[0, s // C_HEADS, s % C_HEADS] = st_scr[s].T


def _hgrn(q, ff, fb, v, cg, lb_logits, norm_g, layer, init, batch, seq_len, emit_state):
    n = q.shape[0]
    row = lambda width: pl.BlockSpec((seq_len, width), lambda b: (b, 0))
    const = lambda a: pl.BlockSpec(a.shape, lambda b: (0,) * a.ndim)
    ng = norm_g.reshape(1, C_HEADS * C_DV)
    state_spec = pl.BlockSpec((1, 2, C_HEADS, C_DK, C_DV), lambda b: (b, 0, 0, 0, 0))
    in_specs = [row(512)] * 5 + [const(lb_logits), const(ng)]
    args = [q, ff, fb, v, cg, lb_logits, ng]
    if init is not None:
        in_specs.append(state_spec)
        args.append(init)
    out_specs = [row(512)]
    out_shape = [jax.ShapeDtypeStruct((n, 512), BF16)]
    if emit_state:
        out_specs.append(state_spec)
        out_shape.append(jax.ShapeDtypeStruct((batch, 2, C_HEADS, C_DK, C_DV), F32))
    return pl.pallas_call(
        functools.partial(_hgrn_kernel, seq_len=seq_len, layer=layer, has_init=init is not None,
                          emit_state=emit_state),
        grid=(batch,),
        in_specs=in_specs,
        out_specs=out_specs,
        out_shape=out_shape,
        scratch_shapes=[pltpu.VMEM((2 * C_HEADS, C_DV, C_DK), F32),
                        pltpu.VMEM((seq_len, 512), F32),
                        pltpu.VMEM((seq_len, 512), F32)],
        compiler_params=_cparams(("parallel",)),
        name="hgrn2_scan",
    )(*args)


def _mix_out_kernel(x_ref, mod_ref, ya_ref, yb_ref, w_ref, g_ref, b_ref, o_ref):
    gate = mod_ref[0][5:6]
    half = ya_ref.shape[1]
    y = _dot(ya_ref[...], w_ref[:half, :]) + _dot(yb_ref[...], w_ref[half:, :])
    o_ref[...] = _layernorm(ALPHA * x_ref[...] + gate * y, g_ref[...], b_ref[...])


def _mix_out(x, mod, ya, yb, w, g, b, mod_index):
    n, d = x.shape
    tm = TOKEN_TILE
    row = lambda width: pl.BlockSpec((tm, width), lambda i: (i, 0))
    return pl.pallas_call(
        _mix_out_kernel,
        grid=(n // tm,),
        in_specs=[row(d), pl.BlockSpec((1, 9, d), mod_index), row(ya.shape[1]), row(yb.shape[1]),
                  pl.BlockSpec(w.shape, lambda i: (0, 0), pipeline_mode=pl.Buffered(1)),
                  pl.BlockSpec((1, d), lambda i: (0, 0)), pl.BlockSpec((1, d), lambda i: (0, 0))],
        out_specs=row(d),
        out_shape=jax.ShapeDtypeStruct((n, d), F32),
        compiler_params=_cparams(("parallel",)),
        name="mixer_out_proj",
    )(x, mod, ya, yb, w, g.reshape(1, d), b.reshape(1, d))


def _prep_ffn(w1, w3, w2):
    d, f = w1.shape
    nf = f // FF_CHUNK
    chunk_major = lambda w: jnp.transpose(w.astype(BF16).reshape(d, nf, FF_CHUNK), (1, 0, 2))
    return chunk_major(w1), chunk_major(w3), w2.astype(BF16).reshape(nf, FF_CHUNK, d)


def _prep_w_in_even(w):
    d = w.shape[0]
    a, bq, bk, bv = w[:, :768], w[:, 768:1280], w[:, 1280:1792], w[:, 1792:2304]
    bg, bo = w[:, 2304:2320], w[:, 2320:2832]
    pad = jnp.zeros((d, 128 - bg.shape[1]), w.dtype)
    return jnp.concatenate([a, bq, bk, bv, bo, bg, pad], 1).astype(BF16)


def kernel(x_prompt, x_sample, c, cache_a_k, cache_a_v, state_b_C, state_b_n, state_b_m, state_c_S, cache_d_k, cache_d_v, c_ctx, ada_w, ada_b, ln_g, ln_b, ffn_w1, ffn_w3, ffn_w2, w_in_even, w_out_even, a_sink, b_gate_bias, b_norm_g, w_in_odd, w_out_odd, c_lb_logits, c_norm_g, d_q_norm, d_k_norm):
    batch_p, len_p, d = x_prompt.shape
    batch_s, len_s, _ = x_sample.shape
    past = cache_a_k.shape[2]

    cvec = jnp.concatenate([c_ctx[None], c, jnp.zeros((MOD_ROWS - 1 - batch_s, d), F32)], 0)
    mod_all = _modulation(cvec, ada_w, ada_b)

    groups = [
        dict(x=x_prompt.reshape(batch_p * len_p, d), batch=batch_p, seq=len_p, prompt=True,
             mod_index=_mod_index(0, None)),
        dict(x=x_sample.reshape(batch_s * len_s, d), batch=batch_s, seq=len_s, prompt=False,
             mod_index=_mod_index(1, len_s // TOKEN_TILE)),
    ]
    new = {}

    for l in range(DEPTH):
        mod = mod_all[l].reshape(MOD_ROWS, 9, d)
        i = l // 2
        ffn_a = _prep_ffn(ffn_w1[l, 0], ffn_w3[l, 0], ffn_w2[l, 0])
        ffn_b = _prep_ffn(ffn_w1[l, 1], ffn_w3[l, 1], ffn_w2[l, 1])
        if l % 2 == 0:
            w_in = _prep_w_in_even(w_in_even[i])
            w_out = w_out_even[i].astype(BF16)
        else:
            w_in = w_in_odd[i].astype(BF16)
            w_out = w_out_odd[i].astype(BF16)

        for grp in groups:
            x, mi, nb, sl, prompt = grp["x"], grp["mod_index"], grp["batch"], grp["seq"], grp["prompt"]
            x = _ffn(x, mod, 0, ln_g[l, 0], ln_b[l, 0], *ffn_a, mi)
            if l % 2 == 0:
                outs = _even_in(x, mod, w_in, mi, sl, rope=not prompt, cache=prompt)
                if prompt:
                    aq, akv, kv_cache, bq, bk, bv, bo, bg = outs
                    new["a_k"] = kv_cache[:, :128].reshape(nb, 1, sl, A_KV, HEAD_DIM)
                    new["a_v"] = kv_cache[:, 128:].reshape(nb, 1, sl, A_KV, HEAD_DIM)
                    ya = _attention(aq, akv, None, None, a_sink[i], nb, sl, tq=sl, banded=False)
                    yb, c_new, n_new, m_new = _mlstm(bq, bk, bv, bg, bo, b_gate_bias[i], b_norm_g[i], None,
                                                     nb, sl, emit_state=True)
                    new["b_C"] = c_new[:, None]
                    new["b_n"] = n_new.reshape(nb, 1, 2, B_HEADS, B_DIM)
                    new["b_m"] = m_new[:, :, 0].reshape(nb, 1, 2, B_HEADS)
                else:
                    aq, akv, bq, bk, bv, bo, bg = outs
                    ctx_k = cache_a_k[:, i].reshape(nb, past, A_KV * HEAD_DIM).astype(BF16)
                    ctx_v = cache_a_v[:, i].reshape(nb, past, A_KV * HEAD_DIM).astype(BF16)
                    ya = _attention(aq, akv, ctx_k, ctx_v, a_sink[i], nb, sl, tq=Q_BLOCK, banded=True)
                    init = (state_b_C[:, i], state_b_n[:, i], state_b_m[:, i])
                    yb, = _mlstm(bq, bk, bv, bg, bo, b_gate_bias[i], b_norm_g[i], init, nb, sl, emit_state=False)
                x = _mix_out(x, mod, ya, yb, w_out, ln_g[l, 1], ln_b[l, 1], mi)
            else:
                outs = _odd_in(x, mod, w_in, d_q_norm[i], d_k_norm[i], mi, sl, rope=not prompt, cache=prompt)
                if prompt:
                    cq, ff, fb, cv, cg, dq, dkv, kv_cache = outs
                    new["d_k"] = kv_cache[:, :128].reshape(nb, 1, sl, D_KV, HEAD_DIM)
                    new["d_v"] = kv_cache[:, 128:].reshape(nb, 1, sl, D_KV, HEAD_DIM)
                    yc, s_new = _hgrn(cq, ff, fb, cv, cg, c_lb_logits, c_norm_g[i], l, None, nb, sl, emit_state=True)
                    new["c_S"] = s_new[:, None]
                    yd = _attention(dq, dkv, None, None, None, nb, sl, tq=sl, banded=False)
                else:
                    cq, ff, fb, cv, cg, dq, dkv = outs
                    yc, = _hgrn(cq, ff, fb, cv, cg, c_lb_logits, c_norm_g[i], l, state_c_S[:, i], nb, sl,
                                emit_state=False)
                    ctx_k = cache_d_k[:, i].reshape(nb, past, D_KV * HEAD_DIM).astype(BF16)
                    ctx_v = cache_d_v[:, i].reshape(nb, past, D_KV * HEAD_DIM).astype(BF16)
                    yd = _attention(dq, dkv, ctx_k, ctx_v, None, nb, sl, tq=Q_BLOCK, banded=False)
                x = _mix_out(x, mod, yc, yd, w_out, ln_g[l, 1], ln_b[l, 1], mi)
            x = _ffn(x, mod, 2, ln_g[l, 2], ln_b[l, 2], *ffn_b, mi)
            grp["x"] = x

    y_prompt = groups[0]["x"].reshape(batch_p, len_p, d)
    y_sample = groups[1]["x"].reshape(batch_s, len_s, d)
    return (y_prompt, y_sample, new["a_k"], new["a_v"], new["b_C"], new["b_n"], new["b_m"], new["c_S"],
            new["d_k"], new["d_v"])
```

```python
import functools

import jax
import jax.numpy as jnp
from jax import lax
from jax.experimental import pallas as pl
from jax.experimental.pallas import tpu as pltpu

F32 = jnp.float32
BF16 = jnp.bfloat16

D_MODEL = 1024
DEPTH = 2
GRID_W = 64
HEAD_DIM = 64
A_HEADS = 8
A_KV = 2
WINDOW = 128
B_HEADS = 4
B_DIM = 128
C_HEADS = 4
C_DK = 128
C_DV = 128
D_HEADS = 8
D_KV = 2
D_FF = 2816
Q_BLOCK = 128
ROPE_THETA = 10000.0
ALPHA = (2 * DEPTH) ** 0.25
NEG_INF = -1e30

MOD_ROWS = 8
FF_CHUNK = 256
TOKEN_TILE = 512
MLSTM_T = 64
HGRN_T = 32
SUB = 8
ATT_KEY_TILE = 512
VMEM_LIMIT = 56 * 1024 * 1024


def _cparams(sem):
    return pltpu.CompilerParams(dimension_semantics=sem, vmem_limit_bytes=VMEM_LIMIT)


def _dot(a, b):
    return jnp.dot(a, b, preferred_element_type=F32)


def _dot_nt(a, b):
    return lax.dot_general(a, b, (((1,), (1,)), ((), ())), preferred_element_type=F32)


def _dot_tn(a, b):
    return lax.dot_general(a, b, (((0,), (0,)), ((), ())), preferred_element_type=F32)


def _split3(x):
    hi = x.astype(BF16)
    r1 = x - hi.astype(F32)
    mid = r1.astype(BF16)
    lo = (r1 - mid.astype(F32)).astype(BF16)
    return hi, mid, lo


def _dot_sel(sel, x):
    hi, mid, lo = _split3(x)
    return _dot(sel, hi) + _dot(sel, mid) + _dot(sel, lo)


def _run_interleaved(gens):
    live = list(gens)
    while live:
        nxt = []
        for g in live:
            try:
                next(g)
                nxt.append(g)
            except StopIteration:
                pass
        live = nxt


def _silu(x):
    return x * jax.nn.sigmoid(x)


def _layernorm(z, g, b):
    mu = jnp.mean(z, -1, keepdims=True)
    zc = z - mu
    var = jnp.mean(zc * zc, -1, keepdims=True)
    return zc * lax.rsqrt(var + 1e-5) * g + b


def _rms_lastdim(x, g):
    return x * lax.rsqrt(jnp.mean(x * x, -1, keepdims=True) + 1e-6) * g


def _mod_index(group_start, tiles_per_request):
    if tiles_per_request is None:
        return lambda i: (group_start, 0, 0)
    return lambda i: (group_start + i // tiles_per_request, 0, 0)


def _mod_kernel(c_ref, w_ref, b_ref, o_ref):
    s = _silu(c_ref[...]).astype(BF16)
    o_ref[0] = _dot(s, w_ref[0].astype(BF16)) + b_ref[0]


def _modulation(cvec, ada_w, ada_b):
    depth, d, n = ada_w.shape
    tn = 1152
    return pl.pallas_call(
        _mod_kernel,
        grid=(depth, n // tn),
        in_specs=[pl.BlockSpec((MOD_ROWS, d), lambda l, j: (0, 0)),
                  pl.BlockSpec((1, d, tn), lambda l, j: (l, 0, j)),
                  pl.BlockSpec((1, 1, tn), lambda l, j: (l, 0, j))],
        out_specs=pl.BlockSpec((1, MOD_ROWS, tn), lambda l, j: (l, 0, j)),
        out_shape=jax.ShapeDtypeStruct((depth, MOD_ROWS, n), F32),
        compiler_params=_cparams(("parallel", "parallel")),
        name="modulation",
    )(cvec, ada_w, ada_b.reshape(depth, 1, n))


def _ffn_kernel(x_ref, mod_ref, w1_ref, w3_ref, w2_ref, g_ref, b_ref, o_ref, h_ref, acc_ref, *, j, nf):
    m = mod_ref[0]
    shift, scale, gate = m[3 * j:3 * j + 1], m[3 * j + 1:3 * j + 2], m[3 * j + 2:3 * j + 3]
    h_ref[...] = (x_ref[...] * (1.0 + scale) + shift).astype(BF16)
    acc_ref[...] = jnp.zeros_like(acc_ref)

    def body(f, carry):
        h = h_ref[...]
        a = _dot(h, w1_ref[f])
        b = _dot(h, w3_ref[f])
        u = (_silu(a) * b).astype(BF16)
        acc_ref[...] += _dot(u, w2_ref[f])
        return carry

    lax.fori_loop(0, nf, body, 0)
    z = ALPHA * x_ref[...] + 0.5 * gate * acc_ref[...]
    o_ref[...] = _layernorm(z, g_ref[...], b_ref[...])


def _ffn(x, mod, j, g, b, w1, w3, w2, mod_index):
    n, d = x.shape
    nf = w1.shape[0]
    tm = TOKEN_TILE
    whole = lambda a: pl.BlockSpec(a.shape, lambda i: (0,) * a.ndim, pipeline_mode=pl.Buffered(1))
    return pl.pallas_call(
        functools.partial(_ffn_kernel, j=j, nf=nf),
        grid=(n // tm,),
        in_specs=[pl.BlockSpec((tm, d), lambda i: (i, 0)),
                  pl.BlockSpec((1, 9, d), mod_index),
                  whole(w1), whole(w3), whole(w2),
                  pl.BlockSpec((1, d), lambda i: (0, 0)),
                  pl.BlockSpec((1, d), lambda i: (0, 0))],
        out_specs=pl.BlockSpec((tm, d), lambda i: (i, 0)),
        out_shape=jax.ShapeDtypeStruct((n, d), F32),
        scratch_shapes=[pltpu.VMEM((tm, d), BF16), pltpu.VMEM((tm, d), F32)],
        compiler_params=_cparams(("parallel",)),
        name="ffn_sublayer",
    )(x, mod, w1, w3, w2, g.reshape(1, d), b.reshape(1, d))


def _rope_tables(length):
    t = jnp.arange(length)
    nf = HEAD_DIM // 4
    inv = ROPE_THETA ** (-jnp.arange(nf, dtype=F32) / nf)
    ang_r = (t // GRID_W).astype(F32)[:, None] * inv[None]
    ang_c = (t % GRID_W).astype(F32)[:, None] * inv[None]
    cr, sr, cc, sc = jnp.cos(ang_r), jnp.sin(ang_r), jnp.cos(ang_c), jnp.sin(ang_c)
    z = jnp.zeros_like(cr)
    cos = jnp.concatenate([cr, cr, cc, cc], 1)
    sin_up = jnp.concatenate([-sr, z, -sc, z], 1)
    sin_dn = jnp.concatenate([z, sr, z, sc], 1)
    two = lambda a: jnp.concatenate([a, a], 1)
    return two(cos), two(sin_up), two(sin_dn)


def _rope128(x, cos, sin_up, sin_dn):
    nf = HEAD_DIM // 4
    return x * cos + pltpu.roll(x, 128 - nf, 1) * sin_up + pltpu.roll(x, nf, 1) * sin_dn


def _rope(x, cos, sin_up, sin_dn):
    parts = [_rope128(x[:, c:c + 128], cos, sin_up, sin_dn) for c in range(0, x.shape[1], 128)]
    return parts[0] if len(parts) == 1 else jnp.concatenate(parts, 1)


EVEN_COLS = (512, 256, 512, 512, 512, 512, 128)


def _even_in_kernel(*refs, rope, cache):
    it = iter(refs)
    x_ref, mod_ref, w_ref = next(it), next(it), next(it)
    tabs = (next(it), next(it), next(it)) if rope else None
    aq_ref, akv_ref = next(it), next(it)
    cache_ref = next(it) if cache else None
    bq_ref, bk_ref, bv_ref, bo_ref, bg_ref = next(it), next(it), next(it), next(it), next(it)

    m = mod_ref[0]
    h = (x_ref[...] * (1.0 + m[4:5]) + m[3:4]).astype(BF16)
    offs = [0]
    for c in EVEN_COLS:
        offs.append(offs[-1] + c)
    proj = lambda k: _dot(h, w_ref[:, offs[k]:offs[k + 1]])

    aq = proj(0)
    akv = proj(1)
    if cache:
        cache_ref[...] = akv
    if rope:
        cos, s_up, s_dn = (t[...] for t in tabs)
        aq = _rope(aq, cos, s_up, s_dn)
        akv = jnp.concatenate([_rope(akv[:, :128], cos, s_up, s_dn), akv[:, 128:]], 1)
    aq_ref[...] = aq.astype(BF16)
    akv_ref[...] = akv.astype(BF16)
    bq_ref[...] = proj(2).astype(BF16)
    bk_ref[...] = (proj(3) * (B_DIM ** -0.5)).astype(BF16)
    bv_ref[...] = proj(4).astype(BF16)
    bo_ref[...] = proj(5)
    bg_ref[...] = proj(6)


def _even_in(x, mod, w, mod_index, seq_len, rope, cache):
    n, d = x.shape
    tm = TOKEN_TILE
    row = lambda width: pl.BlockSpec((tm, width), lambda i: (i, 0))
    in_specs = [row(d), pl.BlockSpec((1, 9, d), mod_index),
                pl.BlockSpec(w.shape, lambda i: (0, 0), pipeline_mode=pl.Buffered(1))]
    args = [x, mod, w]
    if rope:
        per_seq = seq_len // tm
        tabs = _rope_tables(seq_len)
        in_specs += [pl.BlockSpec((tm, 128), lambda i: (i % per_seq, 0))] * 3
        args += list(tabs)
    outs = [(512, BF16), (256, BF16)] + ([(256, F32)] if cache else []) + \
           [(512, BF16), (512, BF16), (512, BF16), (512, F32), (128, F32)]
    return pl.pallas_call(
        functools.partial(_even_in_kernel, rope=rope, cache=cache),
        grid=(n // tm,),
        in_specs=in_specs,
        out_specs=[row(wd) for wd, _ in outs],
        out_shape=[jax.ShapeDtypeStruct((n, wd), dt) for wd, dt in outs],
        compiler_params=_cparams(("parallel",)),
        name="even_in_proj",
    )(*args)


ODD_COLS = (512, 512, 512, 512, 512, 512, 256)


def _head_rms(x, seg_ref, g):
    x2 = x * x
    hi = x2.astype(BF16)
    lo = (x2 - hi.astype(F32)).astype(BF16)
    ms = _dot(hi, seg_ref[...]) + _dot(lo, seg_ref[...])
    return x * lax.rsqrt(ms + 1e-6) * g


def _odd_in_kernel(*refs, rope, cache):
    it = iter(refs)
    x_ref, mod_ref, w_ref, segq_ref, segk_ref, qn_ref, kn_ref = (next(it) for _ in range(7))
    tabs = (next(it), next(it), next(it)) if rope else None
    q_ref, ff_ref, fb_ref, v_ref, cg_ref, dq_ref, dkv_ref = (next(it) for _ in range(7))
    cache_ref = next(it) if cache else None

    m = mod_ref[0]
    h = (x_ref[...] * (1.0 + m[4:5]) + m[3:4]).astype(BF16)
    offs = [0]
    for c in ODD_COLS:
        offs.append(offs[-1] + c)
    proj = lambda k: _dot(h, w_ref[:, offs[k]:offs[k + 1]])

    q_ref[...] = _silu(proj(0)).astype(BF16)
    ff_ref[...] = proj(1)
    fb_ref[...] = proj(2)
    v_ref[...] = proj(3).astype(BF16)
    cg_ref[...] = proj(4)
    dq = _head_rms(proj(5), segq_ref, qn_ref[...])
    dkv = proj(6)
    dk = _head_rms(dkv[:, :128], segk_ref, kn_ref[...])
    dv = dkv[:, 128:]
    if cache:
        cache_ref[...] = jnp.concatenate([dk, dv], 1)
    if rope:
        cos, s_up, s_dn = (t[...] for t in tabs)
        dq = _rope(dq, cos, s_up, s_dn)
        dk = _rope(dk, cos, s_up, s_dn)
    dq_ref[...] = dq.astype(BF16)
    dkv_ref[...] = jnp.concatenate([dk, dv], 1).astype(BF16)


def _segment_mean_matrix(width):
    r = jnp.arange(width) // HEAD_DIM
    return jnp.where(r[:, None] == r[None, :], 1.0 / HEAD_DIM, 0.0).astype(BF16)


def _odd_in(x, mod, w, q_norm, k_norm, mod_index, seq_len, rope, cache):
    n, d = x.shape
    tm = TOKEN_TILE
    row = lambda width: pl.BlockSpec((tm, width), lambda i: (i, 0))
    const = lambda a: pl.BlockSpec(a.shape, lambda i: (0, 0))
    segq, segk = _segment_mean_matrix(512), _segment_mean_matrix(128)
    qn = jnp.tile(q_norm, D_HEADS).reshape(1, 512)
    kn = jnp.tile(k_norm, D_KV).reshape(1, 128)
    in_specs = [row(d), pl.BlockSpec((1, 9, d), mod_index),
                pl.BlockSpec(w.shape, lambda i: (0, 0), pipeline_mode=pl.Buffered(1)),
                const(segq), const(segk), const(qn), const(kn)]
    args = [x, mod, w, segq, segk, qn, kn]
    if rope:
        per_seq = seq_len // tm
        tabs = _rope_tables(seq_len)
        in_specs += [pl.BlockSpec((tm, 128), lambda i: (i % per_seq, 0))] * 3
        args += list(tabs)
    outs = [(512, BF16), (512, F32), (512, F32), (512, BF16), (512, F32), (512, BF16), (256, BF16)] + \
           ([(256, F32)] if cache else [])
    return pl.pallas_call(
        functools.partial(_odd_in_kernel, rope=rope, cache=cache),
        grid=(n // tm,),
        in_specs=in_specs,
        out_specs=[row(wd) for wd, _ in outs],
        out_shape=[jax.ShapeDtypeStruct((n, wd), dt) for wd, dt in outs],
        compiler_params=_cparams(("parallel",)),
        name="odd_in_proj",
    )(*args)


def _attn_kernel(*refs, seq_len, tq, n_ctx, banded, has_sink):
    it = iter(refs)
    q_ref, kv_ref = next(it), next(it)
    ck_ref, cv_ref = (next(it), next(it)) if n_ctx else (None, None)
    sink_ref = next(it) if has_sink else None
    o_ref = next(it)
    groups = A_HEADS // A_KV
    rows = groups * tq
    j = pl.program_id(1)

    if banded:
        span = tq + 2 * WINDOW
        start = pl.multiple_of(jnp.clip(j * tq - WINDOW, 0, seq_len - span), WINDOW)
        qpos = j * tq + (lax.broadcasted_iota(jnp.int32, (rows, span), 0) & (tq - 1))
        kpos = start + lax.broadcasted_iota(jnp.int32, (rows, span), 1)
        band = jnp.abs(kpos - qpos) <= WINDOW

    for kh in range(A_KV):
        kcol = slice(kh * HEAD_DIM, (kh + 1) * HEAD_DIM)
        vcol = slice(128 + kh * HEAD_DIM, 128 + (kh + 1) * HEAD_DIM)
        qs = jnp.concatenate([q_ref[:, (kh * groups + g) * HEAD_DIM:(kh * groups + g + 1) * HEAD_DIM]
                              for g in range(groups)], axis=0)
        if has_sink:
            m = jnp.concatenate([jnp.full((tq, 1), sink_ref[kh * groups + g], F32) for g in range(groups)], 0)
            l = jnp.ones((rows, 1), F32)
        else:
            m = jnp.full((rows, 1), NEG_INF, F32)
            l = jnp.zeros((rows, 1), F32)
        acc = jnp.zeros((rows, HEAD_DIM), F32)

        tiles = []
        if banded:
            tiles.append((kv_ref[pl.ds(start, span), kcol], kv_ref[pl.ds(start, span), vcol], band))
        else:
            tk = min(ATT_KEY_TILE, seq_len)
            for t in range(seq_len // tk):
                tiles.append((kv_ref[t * tk:(t + 1) * tk, kcol], kv_ref[t * tk:(t + 1) * tk, vcol], None))
        if n_ctx:
            tiles.append((ck_ref[0, :, kcol], cv_ref[0, :, kcol], None))

        for k_t, v_t, mask in tiles:
            s = _dot_nt(qs, k_t) * (HEAD_DIM ** -0.5)
            if mask is not None:
                s = jnp.where(mask, s, NEG_INF)
            m_new = jnp.maximum(m, jnp.max(s, -1, keepdims=True))
            p = jnp.exp(s - m_new)
            alpha = jnp.exp(m - m_new)
            l = alpha * l + jnp.sum(p, -1, keepdims=True)
            acc = alpha * acc + _dot(p.astype(BF16), v_t)
            m = m_new
        o = acc / l
        for g in range(0, groups, 2):
            pair = jnp.concatenate([o[g * tq:(g + 1) * tq], o[(g + 1) * tq:(g + 2) * tq]], 1)
            c0 = (kh * groups + g) * HEAD_DIM
            o_ref[:, c0:c0 + 2 * HEAD_DIM] = pair.astype(BF16)


def _attention(q, kv, ctx_k, ctx_v, sink, batch, seq_len, tq, banded):
    n_ctx = 0 if ctx_k is None else ctx_k.shape[1]
    per_seq = seq_len // tq
    in_specs = [pl.BlockSpec((tq, 512), lambda b, j: (b * per_seq + j, 0)),
                pl.BlockSpec((seq_len, 256), lambda b, j: (b, 0))]
    args = [q, kv]
    if n_ctx:
        in_specs += [pl.BlockSpec((1, n_ctx, 128), lambda b, j: (b, 0, 0))] * 2
        args += [ctx_k, ctx_v]
    if sink is not None:
        in_specs.append(pl.BlockSpec(memory_space=pltpu.SMEM))
        args.append(sink.reshape(-1).astype(F32))
    return pl.pallas_call(
        functools.partial(_attn_kernel, seq_len=seq_len, tq=tq, n_ctx=n_ctx, banded=banded,
                          has_sink=sink is not None),
        grid=(batch, per_seq),
        in_specs=in_specs,
        out_specs=pl.BlockSpec((tq, 512), lambda b, j: (b * per_seq + j, 0)),
        out_shape=jax.ShapeDtypeStruct(q.shape, BF16),
        compiler_params=_cparams(("parallel", "arbitrary")),
        name="gqa_attention",
    )(*args)


def _log_sigmoid(x):
    return jnp.minimum(x, 0.0) - jnp.log1p(jnp.exp(-jnp.abs(x)))


def _mlstm_kernel(*refs, seq_len, has_init, emit_state):
    it = iter(refs)
    q_ref, k_ref, v_ref, g_ref, bo_ref, gb_ref, ng_ref = (next(it) for _ in range(7))
    c0_ref, n0_ref, m0_ref = (next(it), next(it), next(it)) if has_init else (None, None, None)
    y_ref = next(it)
    co_ref, no_ref, mo_ref = (next(it), next(it), next(it)) if emit_state else (None, None, None)
    c_scr, n_scr, m_scr, hf_scr, hb_scr = (next(it) for _ in range(5))

    t_len = MLSTM_T
    n_chunks = seq_len // t_len
    n_streams = 2 * B_HEADS

    for s in range(n_streams):
        if has_init:
            c_scr[s] = c0_ref[0, s // B_HEADS, s % B_HEADS]
        else:
            c_scr[s] = jnp.zeros((B_DIM, B_DIM), F32)
    if has_init:
        n_scr[...] = n0_ref[0]
        m_scr[...] = m0_ref[0]
    else:
        n_scr[...] = jnp.zeros_like(n_scr)
        m_scr[...] = jnp.zeros_like(m_scr)

    ri = lax.broadcasted_iota(jnp.int32, (t_len, t_len), 0)
    ci = lax.broadcasted_iota(jnp.int32, (t_len, t_len), 1)
    lower = ci <= ri
    upper = ci >= ri
    tri_f = jnp.where(lower, 1.0, 0.0).astype(BF16)
    tri_b = jnp.where(upper, 1.0, 0.0).astype(BF16)
    gate_bias = gb_ref[...]

    def run_direction(c, direction):
        chunk = c if direction == 0 else n_chunks - 1 - c
        rows = pl.ds(pl.multiple_of(chunk * t_len, t_len), t_len)
        tri = tri_f if direction == 0 else tri_b
        causal = lower if direction == 0 else upper
        last = t_len - 1 if direction == 0 else 0
        h_out = hf_scr if direction == 0 else hb_scr
        gates = g_ref[rows, :] + gate_bias
        hi, mid, lo = _split3(_log_sigmoid(gates))
        b3 = _dot(tri, jnp.concatenate([hi, mid, lo], 1))
        yield
        bc_all = b3[:, :128] + b3[:, 128:256] + b3[:, 256:]
        u_t = (gates - pltpu.roll(bc_all, 128 - B_HEADS, 1)).T
        heads = []
        for hd in range(B_HEADS):
            s = direction * B_HEADS + hd
            col = slice(hd * B_DIM, (hd + 1) * B_DIM)
            i_col = 2 * direction * B_HEADS + hd
            f_col = i_col + B_HEADS
            ig = gates[:, i_col:i_col + 1]
            bc = bc_all[:, f_col:f_col + 1]
            dmat = jnp.where(causal, bc + u_t[i_col:i_col + 1, :], -jnp.inf)
            m_prev = m_scr[s:s + 1, 0:1]
            inter = bc + m_prev
            m_t = jnp.maximum(inter, jnp.max(dmat, -1, keepdims=True))
            qc, kc, vc = q_ref[rows, col], k_ref[rows, col], v_ref[rows, col]
            c_prev = c_scr[s]
            qk = _dot_nt(qc, kc)
            q_state = _dot(qc, c_prev.astype(BF16))
            heads.append((s, col, ig, bc, dmat, m_prev, inter, m_t, qc, kc, vc, c_prev, qk, q_state))
        yield
        staged = []
        for (s, col, ig, bc, dmat, m_prev, inter, m_t, qc, kc, vc, c_prev, qk, q_state) in heads:
            w = jnp.exp(dmat - m_t) * qk
            wv = _dot(w.astype(BF16), vc)
            b_tot = bc[last:last + 1, :]
            g = b_tot - bc + ig
            m_new = jnp.maximum(b_tot + m_prev, jnp.max(g, 0, keepdims=True))
            kw = kc.astype(F32) * jnp.exp(g - m_new)
            upd = _dot_tn(kw.astype(BF16), vc)
            decay = jnp.exp(b_tot + m_prev - m_new)
            staged.append((s, col, inter, m_t, qc, c_prev, q_state, w, wv, m_new, kw, upd, decay))
        yield
        for (s, col, inter, m_t, qc, c_prev, q_state, w, wv, m_new, kw, upd, decay) in staged:
            a = jnp.exp(inter - m_t)
            n_prev = n_scr[s:s + 1, :]
            num = a * q_state + wv
            qn = jnp.sum(qc.astype(F32) * n_prev, -1, keepdims=True)
            den = a * qn + jnp.sum(w, -1, keepdims=True)
            h_out[rows, col] = num / jnp.maximum(jnp.abs(den), jnp.exp(-m_t))
            c_scr[s] = decay * c_prev + upd
            n_scr[s:s + 1, :] = decay * n_prev + jnp.sum(kw, 0, keepdims=True)
            m_scr[s:s + 1, :] = jnp.broadcast_to(m_new, (1, B_DIM))

    def step(c, carry):
        _run_interleaved([run_direction(c, 0), run_direction(c, 1)])
        return carry

    lax.fori_loop(0, n_chunks, step, 0)

    blk = min(256, seq_len)
    ng = ng_ref[...]

    def finish(i, carry):
        rows = pl.ds(pl.multiple_of(i * blk, blk), blk)
        hsum = hf_scr[rows, :] + hb_scr[rows, :]
        gate = jax.nn.sigmoid(bo_ref[rows, :])
        parts = [_rms_lastdim(hsum[:, hd * B_DIM:(hd + 1) * B_DIM], ng[:, hd * B_DIM:(hd + 1) * B_DIM])
                 for hd in range(B_HEADS)]
        y_ref[rows, :] = (gate * jnp.concatenate(parts, 1)).astype(BF16)
        return carry

    lax.fori_loop(0, seq_len // blk, finish, 0)

    if emit_state:
        for s in range(n_streams):
            co_ref[0, s // B_HEADS, s % B_HEADS] = c_scr[s]
        no_ref[0] = n_scr[...]
        mo_ref[0] = m_scr[...]


def _mlstm(q, k, v, gates, bo, gate_bias, norm_g, init, batch, seq_len, emit_state):
    n = q.shape[0]
    row = lambda width: pl.BlockSpec((seq_len, width), lambda b: (b, 0))
    const = lambda a: pl.BlockSpec(a.shape, lambda b: (0,) * a.ndim)
    gb = jnp.zeros((1, 128), F32).at[0, :4 * B_HEADS].set(gate_bias.reshape(-1))
    ng = norm_g.reshape(1, B_HEADS * B_DIM)
    in_specs = [row(512), row(512), row(512), row(128), row(512), const(gb), const(ng)]
    args = [q, k, v, gates, bo, gb, ng]
    n_streams = 2 * B_HEADS
    if init is not None:
        c0, n0, m0 = init
        in_specs += [pl.BlockSpec((1, 2, B_HEADS, B_DIM, B_DIM), lambda b: (b, 0, 0, 0, 0)),
                     pl.BlockSpec((1, n_streams, B_DIM), lambda b: (b, 0, 0)),
                     pl.BlockSpec((1, n_streams, B_DIM), lambda b: (b, 0, 0))]
        args += [c0, n0.reshape(batch, n_streams, B_DIM),
                 jnp.broadcast_to(m0.reshape(batch, n_streams, 1), (batch, n_streams, B_DIM))]
    out_specs = [row(512)]
    out_shape = [jax.ShapeDtypeStruct((n, 512), BF16)]
    if emit_state:
        out_specs += [pl.BlockSpec((1, 2, B_HEADS, B_DIM, B_DIM), lambda b: (b, 0, 0, 0, 0)),
                      pl.BlockSpec((1, n_streams, B_DIM), lambda b: (b, 0, 0)),
                      pl.BlockSpec((1, n_streams, B_DIM), lambda b: (b, 0, 0))]
        out_shape += [jax.ShapeDtypeStruct((batch, 2, B_HEADS, B_DIM, B_DIM), F32),
                      jax.ShapeDtypeStruct((batch, n_streams, B_DIM), F32),
                      jax.ShapeDtypeStruct((batch, n_streams, B_DIM), F32)]
    return pl.pallas_call(
        functools.partial(_mlstm_kernel, seq_len=seq_len, has_init=init is not None, emit_state=emit_state),
        grid=(batch,),
        in_specs=in_specs,
        out_specs=out_specs,
        out_shape=out_shape,
        scratch_shapes=[pltpu.VMEM((n_streams, B_DIM, B_DIM), F32),
                        pltpu.VMEM((n_streams, B_DIM), F32),
                        pltpu.VMEM((n_streams, B_DIM), F32),
                        pltpu.VMEM((seq_len, 512), F32),
                        pltpu.VMEM((seq_len, 512), F32)],
        compiler_params=_cparams(("parallel",)),
        name="mlstm_scan",
    )(*args)


def _hgrn_kernel(*refs, seq_len, layer, has_init, emit_state):
    it = iter(refs)
    q_ref, ff_ref, fb_ref, v_ref, cg_ref, lbl_ref, ng_ref = (next(it) for _ in range(7))
    s0_ref = next(it) if has_init else None
    y_ref = next(it)
    so_ref = next(it) if emit_state else None
    st_scr, of_scr, ob_scr = next(it), next(it), next(it)

    t_len = HGRN_T
    n_sub = t_len // SUB
    n_chunks = seq_len // t_len
    n_streams = 2 * C_HEADS

    logits = lbl_ref[...]
    e = jnp.exp(logits - jnp.max(logits, 0, keepdims=True))
    sm = e / jnp.sum(e, 0, keepdims=True)
    lb = jnp.sum(sm[0:layer + 1], 0, keepdims=True) - sm[0:1]

    for s in range(n_streams):
        if has_init:
            st_scr[s] = s0_ref[0, s // C_HEADS, s % C_HEADS].T
        else:
            st_scr[s] = jnp.zeros((C_DV, C_DK), F32)

    ri = lax.broadcasted_iota(jnp.int32, (t_len, t_len), 0)
    ci = lax.broadcasted_iota(jnp.int32, (t_len, t_len), 1)
    tri_f = jnp.where(ci <= ri, 1.0, 0.0).astype(BF16)
    tri_b = jnp.where(ci >= ri, 1.0, 0.0).astype(BF16)
    sub_row = lax.broadcasted_iota(jnp.int32, (SUB, C_DK), 0)
    ones_dk = jnp.ones((C_DK, C_DK), BF16)

    def run_stream(c, direction, hd):
        chunk = c if direction == 0 else n_chunks - 1 - c
        rows = pl.ds(pl.multiple_of(chunk * t_len, t_len), t_len)
        f_ref = ff_ref if direction == 0 else fb_ref
        tri = tri_f if direction == 0 else tri_b
        last = t_len - 1 if direction == 0 else 0
        o_out = of_scr if direction == 0 else ob_scr
        s = direction * C_HEADS + hd
        col = slice(hd * C_DK, (hd + 1) * C_DK)
        lbh = lb[:, col]
        f = lbh + (1.0 - lbh) * jax.nn.sigmoid(f_ref[rows, col])
        kk = 1.0 - f
        hi, mid, lo = _split3(jnp.log(f))
        a3 = _dot(tri, jnp.concatenate([hi, mid, lo], 1))
        yield
        a_cum = a3[:, :C_DK] + a3[:, C_DK:2 * C_DK] + a3[:, 2 * C_DK:]
        a_tot = a_cum[last:last + 1, :]
        qf = q_ref[rows, col].astype(F32)
        vc = v_ref[rows, col]
        vf = vc.astype(F32)
        st = st_scr[s]
        inter = _dot_nt((qf * jnp.exp(a_cum)).astype(BF16), st.astype(BF16))
        upd = _dot_tn(vc, (kk * jnp.exp(a_tot - a_cum)).astype(BF16))
        ps = []
        for blk in range(n_sub):
            b0 = blk * SUB
            a_i, q_i, k_i = (t[b0:b0 + SUB] for t in (a_cum, qf, kk))
            for j in range(SUB):
                seen = (sub_row >= j) if direction == 0 else (sub_row <= j)
                dec = jnp.where(seen, jnp.exp(a_i - a_i[j:j + 1]), 0.0)
                ps.append(dec * q_i * k_i[j:j + 1])
        att = _dot(jnp.concatenate(ps, 0).astype(BF16), ones_dk)
        off = []
        for blk in range(n_sub):
            b0 = blk * SUB
            if direction == 0 and blk > 0:
                a_ref, kr = a_cum[b0 - 1:b0], slice(0, b0)
            elif direction == 1 and blk < n_sub - 1:
                a_ref, kr = a_cum[b0 + SUB:b0 + SUB + 1], slice(b0 + SUB, t_len)
            else:
                off.append(None)
                continue
            qt = (qf[b0:b0 + SUB] * jnp.exp(a_cum[b0:b0 + SUB] - a_ref)).astype(BF16)
            kt = (kk[kr] * jnp.exp(a_ref - a_cum[kr])).astype(BF16)
            off.append((_dot_nt(qt, kt), kr))
        yield
        st_scr[s] = jnp.exp(a_tot) * st + upd
        outs = []
        for blk in range(n_sub):
            b0 = blk * SUB
            o_i = inter[b0:b0 + SUB]
            for j in range(SUB):
                r = (blk * SUB + j) * SUB
                o_i = o_i + att[r:r + SUB] * vf[b0 + j:b0 + j + 1]
            if off[blk] is not None:
                att_off, kr = off[blk]
                outs.append((o_i, _dot(att_off.astype(BF16), vc[kr])))
            else:
                outs.append((o_i, None))
        yield
        o_out[rows, col] = jnp.concatenate([o_i if o_off is None else o_i + o_off for o_i, o_off in outs], 0)

    def step(c, carry):
        _run_interleaved([run_stream(c, direction, hd) for direction in range(2) for hd in range(C_HEADS)])
        return carry

    lax.fori_loop(0, n_chunks, step, 0)

    blk_rows = min(256, seq_len)
    ng = ng_ref[...]

    def finish(i, carry):
        rows = pl.ds(pl.multiple_of(i * blk_rows, blk_rows), blk_rows)
        osum = of_scr[rows, :] + ob_scr[rows, :]
        parts = [_rms_lastdim(osum[:, hd * C_DV:(hd + 1) * C_DV], ng[:, hd * C_DV:(hd + 1) * C_DV])
                 for hd in range(C_HEADS)]
        y_ref[rows, :] = (jnp.concatenate(parts, 1) * _silu(cg_ref[rows, :])).astype(BF16)
        return carry

    lax.fori_loop(0, seq_len // blk_rows, finish, 0)

    if emit_state:
        for s in range(n_streams):
            so_ref[0, s // C_HEADS, s % C_HEADS] = st_scr[s].T


def _hgrn(q, ff, fb, v, cg, lb_logits, norm_g, layer, init, batch, seq_len, emit_state):
    n = q.shape[0]
    row = lambda width: pl.BlockSpec((seq_len, width), lambda b: (b, 0))
    const = lambda a: pl.BlockSpec(a.shape, lambda b: (0,) * a.ndim)
    ng = norm_g.reshape(1, C_HEADS * C_DV)
    state_spec = pl.BlockSpec((1, 2, C_HEADS, C_DK, C_DV), lambda b: (b, 0, 0, 0, 0))
    in_specs = [row(512)] * 5 + [const(lb_logits), const(ng)]
    args = [q, ff, fb, v, cg, lb_logits, ng]
    if init is not None:
        in_specs.append(state_spec)
        args.append(init)
    out_specs = [row(512)]
    out_shape = [jax.ShapeDtypeStruct((n, 512), BF16)]
    if emit_state:
        out_specs.append(state_spec)
        out_shape.append(jax.ShapeDtypeStruct((batch, 2, C_HEADS, C_DK, C_DV), F32))
    return pl.pallas_call(
        functools.partial(_hgrn_kernel, seq_len=seq_len, layer=layer, has_init=init is not None,
                          emit_state=emit_state),
        grid=(batch,),
        in_specs=in_specs,
        out_specs=out_specs,
        out_shape=out_shape,
        scratch_shapes=[pltpu.VMEM((2 * C_HEADS, C_DV, C_DK), F32),
                        pltpu.VMEM((seq_len, 512), F32),
                        pltpu.VMEM((seq_len, 512), F32)],
        compiler_params=_cparams(("parallel",)),
        name="hgrn2_scan",
    )(*args)


def _mix_out_kernel(x_ref, mod_ref, ya_ref, yb_ref, w_ref, g_ref, b_ref, o_ref):
    gate = mod_ref[0][5:6]
    half = ya_ref.shape[1]
    y = _dot(ya_ref[...], w_ref[:half, :]) + _dot(yb_ref[...], w_ref[half:, :])
    o_ref[...] = _layernorm(ALPHA * x_ref[...] + gate * y, g_ref[...], b_ref[...])


def _mix_out(x, mod, ya, yb, w, g, b, mod_index):
    n, d = x.shape
    tm = TOKEN_TILE
    row = lambda width: pl.BlockSpec((tm, width), lambda i: (i, 0))
    return pl.pallas_call(
        _mix_out_kernel,
        grid=(n // tm,),
        in_specs=[row(d), pl.BlockSpec((1, 9, d), mod_index), row(ya.shape[1]), row(yb.shape[1]),
                  pl.BlockSpec(w.shape, lambda i: (0, 0), pipeline_mode=pl.Buffered(1)),
                  pl.BlockSpec((1, d), lambda i: (0, 0)), pl.BlockSpec((1, d), lambda i: (0, 0))],
        out_specs=row(d),
        out_shape=jax.ShapeDtypeStruct((n, d), F32),
        compiler_params=_cparams(("parallel",)),
        name="mixer_out_proj",
    )(x, mod, ya, yb, w, g.reshape(1, d), b.reshape(1, d))


def _prep_ffn(w1, w3, w2):
    d, f = w1.shape
    nf = f // FF_CHUNK
    chunk_major = lambda w: jnp.transpose(w.astype(BF16).reshape(d, nf, FF_CHUNK), (1, 0, 2))
    return chunk_major(w1), chunk_major(w3), w2.astype(BF16).reshape(nf, FF_CHUNK, d)


def _prep_w_in_even(w):
    d = w.shape[0]
    a, bq, bk, bv = w[:, :768], w[:, 768:1280], w[:, 1280:1792], w[:, 1792:2304]
    bg, bo = w[:, 2304:2320], w[:, 2320:2832]
    pad = jnp.zeros((d, 128 - bg.shape[1]), w.dtype)
    return jnp.concatenate([a, bq, bk, bv, bo, bg, pad], 1).astype(BF16)


def kernel(x_prompt, x_sample, c, cache_a_k, cache_a_v, state_b_C, state_b_n, state_b_m, state_c_S, cache_d_k, cache_d_v, c_ctx, ada_w, ada_b, ln_g, ln_b, ffn_w1, ffn_w3, ffn_w2, w_in_even, w_out_even, a_sink, b_gate_bias, b_norm_g, w_in_odd, w_out_odd, c_lb_logits, c_norm_g, d_q_norm, d_k_norm):
    batch_p, len_p, d = x_prompt.shape
    batch_s, len_s, _ = x_sample.shape
    past = cache_a_k.shape[2]

    cvec = jnp.concatenate([c_ctx[None], c, jnp.zeros((MOD_ROWS - 1 - batch_s, d), F32)], 0)
    mod_all = _modulation(cvec, ada_w, ada_b)

    groups = [
        dict(x=x_prompt.reshape(batch_p * len_p, d), batch=batch_p, seq=len_p, prompt=True,
             mod_index=_mod_index(0, None)),
        dict(x=x_sample.reshape(batch_s * len_s, d), batch=batch_s, seq=len_s, prompt=False,
             mod_index=_mod_index(1, len_s // TOKEN_TILE)),
    ]
    new = {}

    for l in range(DEPTH):
        mod = mod_all[l].reshape(MOD_ROWS, 9, d)
        i = l // 2
        ffn_a = _prep_ffn(ffn_w1[l, 0], ffn_w3[l, 0], ffn_w2[l, 0])
        ffn_b = _prep_ffn(ffn_w1[l, 1], ffn_w3[l, 1], ffn_w2[l, 1])
        if l % 2 == 0:
            w_in = _prep_w_in_even(w_in_even[i])
            w_out = w_out_even[i].astype(BF16)
        else:
            w_in = w_in_odd[i].astype(BF16)
            w_out = w_out_odd[i].astype(BF16)

        for grp in groups:
            x, mi, nb, sl, prompt = grp["x"], grp["mod_index"], grp["batch"], grp["seq"], grp["prompt"]
            x = _ffn(x, mod, 0, ln_g[l, 0], ln_b[l, 0], *ffn_a, mi)
            if l % 2 == 0:
                outs = _even_in(x, mod, w_in, mi, sl, rope=not prompt, cache=prompt)
                if prompt:
                    aq, akv, kv_cache, bq, bk, bv, bo, bg = outs
                    new["a_k"] = kv_cache[:, :128].reshape(nb, 1, sl, A_KV, HEAD_DIM)
                    new["a_v"] = kv_cache[:, 128:].reshape(nb, 1, sl, A_KV, HEAD_DIM)
                    ya = _attention(aq, akv, None, None, a_sink[i], nb, sl, tq=sl, banded=False)
                    yb, c_new, n_new, m_new = _mlstm(bq, bk, bv, bg, bo, b_gate_bias[i], b_norm_g[i], None,
                                                     nb, sl, emit_state=True)
                    new["b_C"] = c_new[:, None]
                    new["b_n"] = n_new.reshape(nb, 1, 2, B_HEADS, B_DIM)
                    new["b_m"] = m_new[:, :, 0].reshape(nb, 1, 2, B_HEADS)
                else:
                    aq, akv, bq, bk, bv, bo, bg = outs
                    ctx_k = cache_a_k[:, i].reshape(nb, past, A_KV * HEAD_DIM).astype(BF16)
                    ctx_v = cache_a_v[:, i].reshape(nb, past, A_KV * HEAD_DIM).astype(BF16)
                    ya = _attention(aq, akv, ctx_k, ctx_v, a_sink[i], nb, sl, tq=Q_BLOCK, banded=True)
                    init = (state_b_C[:, i], state_b_n[:, i], state_b_m[:, i])
                    yb, = _mlstm(bq, bk, bv, bg, bo, b_gate_bias[i], b_norm_g[i], init, nb, sl, emit_state=False)
                x = _mix_out(x, mod, ya, yb, w_out, ln_g[l, 1], ln_b[l, 1], mi)
            else:
                outs = _odd_in(x, mod, w_in, d_q_norm[i], d_k_norm[i], mi, sl, rope=not prompt, cache=prompt)
                if prompt:
                    cq, ff, fb, cv, cg, dq, dkv, kv_cache = outs
                    new["d_k"] = kv_cache[:, :128].reshape(nb, 1, sl, D_KV, HEAD_DIM)
                    new["d_v"] = kv_cache[:, 128:].reshape(nb, 1, sl, D_KV, HEAD_DIM)
                    yc, s_new = _hgrn(cq, ff, fb, cv, cg, c_lb_logits, c_norm_g[i], l, None, nb, sl, emit_state=True)
                    new["c_S"] = s_new[:, None]
                    yd = _attention(dq, dkv, None, None, None, nb, sl, tq=sl, banded=False)
                else:
                    cq, ff, fb, cv, cg, dq, dkv = outs
                    yc, = _hgrn(cq, ff, fb, cv, cg, c_lb_logits, c_norm_g[i], l, state_c_S[:, i], nb, sl,
                                emit_state=False)
                    ctx_k = cache_d_k[:, i].reshape(nb, past, D_KV * HEAD_DIM).astype(BF16)
                    ctx_v = cache_d_v[:, i].reshape(nb, past, D_KV * HEAD_DIM).astype(BF16)
                    yd = _attention(dq, dkv, ctx_k, ctx_v, None, nb, sl, tq=Q_BLOCK, banded=False)
                x = _mix_out(x, mod, yc, yd, w_out, ln_g[l, 1], ln_b[l, 1], mi)
            x = _ffn(x, mod, 2, ln_g[l, 2], ln_b[l, 2], *ffn_b, mi)
            grp["x"] = x

    y_prompt = groups[0]["x"].reshape(batch_p, len_p, d)
    y_sample = groups[1]["x"].reshape(batch_s, len_s, d)
    return (y_prompt, y_sample, new["a_k"], new["a_v"], new["b_C"], new["b_n"], new["b_m"], new["c_S"],
            new["d_k"], new["d_v"])
```

```python
import functools

import jax
import jax.numpy as jnp
from jax import lax
from jax.experimental import pallas as pl
from jax.experimental.pallas import tpu as pltpu

F32 = jnp.float32
BF16 = jnp.bfloat16

D_MODEL = 1024
DEPTH = 2
GRID_W = 64
HEAD_DIM = 64
A_HEADS = 8
A_KV = 2
WINDOW = 128
B_HEADS = 4
B_DIM = 128
C_HEADS = 4
C_DK = 128
C_DV = 128
D_HEADS = 8
D_KV = 2
D_FF = 2816
Q_BLOCK = 128
ROPE_THETA = 10000.0
ALPHA = (2 * DEPTH) ** 0.25
NEG_INF = -1e30
LOG2_E = 1.4426950408889634
QK_SCALE = HEAD_DIM ** -0.5 * LOG2_E

MOD_ROWS = 8
FF_CHUNK = 256
TOKEN_TILE = 512
MLSTM_T = 64
HGRN_T = 32
SUB = 8
ATT_KEY_TILE = 512
VMEM_LIMIT = 56 * 1024 * 1024


def _cparams(sem):
    return pltpu.CompilerParams(dimension_semantics=sem, vmem_limit_bytes=VMEM_LIMIT)


def _dot(a, b):
    return jnp.dot(a, b, preferred_element_type=F32)


def _dot_nt(a, b):
    return lax.dot_general(a, b, (((1,), (1,)), ((), ())), preferred_element_type=F32)


def _dot_tn(a, b):
    return lax.dot_general(a, b, (((0,), (0,)), ((), ())), preferred_element_type=F32)


def _split3(x):
    hi = x.astype(BF16)
    r1 = x - hi.astype(F32)
    mid = r1.astype(BF16)
    lo = (r1 - mid.astype(F32)).astype(BF16)
    return hi, mid, lo


def _dot_sel(sel, x):
    hi, mid, lo = _split3(x)
    return _dot(sel, hi) + _dot(sel, mid) + _dot(sel, lo)


def _dot_sel_rhs(x, sel):
    hi, mid, lo = _split3(x)
    return _dot(hi, sel) + _dot(mid, sel) + _dot(lo, sel)


def _run_interleaved(gens):
    live = list(gens)
    while live:
        nxt = []
        for g in live:
            try:
                next(g)
                nxt.append(g)
            except StopIteration:
                pass
        live = nxt


def _silu(x):
    return x * jax.nn.sigmoid(x)


def _layernorm(z, g, b):
    mu = jnp.mean(z, -1, keepdims=True)
    zc = z - mu
    var = jnp.mean(zc * zc, -1, keepdims=True)
    return zc * lax.rsqrt(var + 1e-5) * g + b


def _rms_lastdim(x, g):
    return x * lax.rsqrt(jnp.mean(x * x, -1, keepdims=True) + 1e-6) * g


def _mod_index(group_start, tiles_per_request):
    if tiles_per_request is None:
        return lambda i: (group_start, 0, 0)
    return lambda i: (group_start + i // tiles_per_request, 0, 0)


def _mod_kernel(c_ref, w_ref, b_ref, o_ref):
    s = _silu(c_ref[...]).astype(BF16)
    o_ref[0] = _dot(s, w_ref[0].astype(BF16)) + b_ref[0]


def _modulation(cvec, ada_w, ada_b):
    depth, d, n = ada_w.shape
    tn = 1152
    return pl.pallas_call(
        _mod_kernel,
        grid=(depth, n // tn),
        in_specs=[pl.BlockSpec((MOD_ROWS, d), lambda l, j: (0, 0)),
                  pl.BlockSpec((1, d, tn), lambda l, j: (l, 0, j)),
                  pl.BlockSpec((1, 1, tn), lambda l, j: (l, 0, j))],
        out_specs=pl.BlockSpec((1, MOD_ROWS, tn), lambda l, j: (l, 0, j)),
        out_shape=jax.ShapeDtypeStruct((depth, MOD_ROWS, n), F32),
        compiler_params=_cparams(("parallel", "parallel")),
        name="modulation",
    )(cvec, ada_w, ada_b.reshape(depth, 1, n))


def _ffn_kernel(x_ref, mod_ref, w1_ref, w3_ref, w2_ref, g_ref, b_ref, o_ref, h_ref, acc_ref, *, j, nf):
    m = mod_ref[0]
    shift, scale, gate = m[3 * j:3 * j + 1], m[3 * j + 1:3 * j + 2], m[3 * j + 2:3 * j + 3]
    h_ref[...] = (x_ref[...] * (1.0 + scale) + shift).astype(BF16)
    acc_ref[...] = jnp.zeros_like(acc_ref)

    def body(f, carry):
        h = h_ref[...]
        a = _dot(h, w1_ref[f])
        b = _dot(h, w3_ref[f])
        u = (_silu(a) * b).astype(BF16)
        acc_ref[...] += _dot(u, w2_ref[f])
        return carry

    lax.fori_loop(0, nf, body, 0)
    z = ALPHA * x_ref[...] + 0.5 * gate * acc_ref[...]
    o_ref[...] = _layernorm(z, g_ref[...], b_ref[...])


def _ffn(x, mod, j, g, b, w1, w3, w2, mod_index):
    n, d = x.shape
    nf = w1.shape[0]
    tm = TOKEN_TILE
    whole = lambda a: pl.BlockSpec(a.shape, lambda i: (0,) * a.ndim, pipeline_mode=pl.Buffered(1))
    return pl.pallas_call(
        functools.partial(_ffn_kernel, j=j, nf=nf),
        grid=(n // tm,),
        in_specs=[pl.BlockSpec((tm, d), lambda i: (i, 0)),
                  pl.BlockSpec((1, 9, d), mod_index),
                  whole(w1), whole(w3), whole(w2),
                  pl.BlockSpec((1, d), lambda i: (0, 0)),
                  pl.BlockSpec((1, d), lambda i: (0, 0))],
        out_specs=pl.BlockSpec((tm, d), lambda i: (i, 0)),
        out_shape=jax.ShapeDtypeStruct((n, d), F32),
        scratch_shapes=[pltpu.VMEM((tm, d), BF16), pltpu.VMEM((tm, d), F32)],
        compiler_params=_cparams(("parallel",)),
        name="ffn_sublayer",
    )(x, mod, w1, w3, w2, g.reshape(1, d), b.reshape(1, d))


def _rope_tables(length):
    t = jnp.arange(length)
    nf = HEAD_DIM // 4
    inv = ROPE_THETA ** (-jnp.arange(nf, dtype=F32) / nf)
    ang_r = (t // GRID_W).astype(F32)[:, None] * inv[None]
    ang_c = (t % GRID_W).astype(F32)[:, None] * inv[None]
    cr, sr, cc, sc = jnp.cos(ang_r), jnp.sin(ang_r), jnp.cos(ang_c), jnp.sin(ang_c)
    z = jnp.zeros_like(cr)
    cos = jnp.concatenate([cr, cr, cc, cc], 1)
    sin_up = jnp.concatenate([-sr, z, -sc, z], 1)
    sin_dn = jnp.concatenate([z, sr, z, sc], 1)
    two = lambda a: jnp.concatenate([a, a], 1)
    return two(cos), two(sin_up), two(sin_dn)


def _rope128(x, cos, sin_up, sin_dn):
    nf = HEAD_DIM // 4
    return x * cos + pltpu.roll(x, 128 - nf, 1) * sin_up + pltpu.roll(x, nf, 1) * sin_dn


def _rope(x, cos, sin_up, sin_dn):
    parts = [_rope128(x[:, c:c + 128], cos, sin_up, sin_dn) for c in range(0, x.shape[1], 128)]
    return parts[0] if len(parts) == 1 else jnp.concatenate(parts, 1)


EVEN_COLS = (512, 256, 512, 512, 512, 512, 128)


def _kv_with_ones(kv):
    ones = jnp.ones((kv.shape[0], HEAD_DIM), kv.dtype)
    return jnp.concatenate([kv[:, :192], ones, kv[:, 192:], ones], 1)


def _even_in_kernel(*refs, rope, cache):
    it = iter(refs)
    x_ref, mod_ref, w_ref = next(it), next(it), next(it)
    tabs = (next(it), next(it), next(it)) if rope else None
    aq_ref, akv_ref = next(it), next(it)
    cache_ref = next(it) if cache else None
    bq_ref, bk_ref, bv_ref, bo_ref, bg_ref = next(it), next(it), next(it), next(it), next(it)

    m = mod_ref[0]
    h = (x_ref[...] * (1.0 + m[4:5]) + m[3:4]).astype(BF16)
    offs = [0]
    for c in EVEN_COLS:
        offs.append(offs[-1] + c)
    proj = lambda k: _dot(h, w_ref[:, offs[k]:offs[k + 1]])

    aq = proj(0)
    akv = proj(1)
    if cache:
        cache_ref[...] = akv
    if rope:
        cos, s_up, s_dn = (t[...] for t in tabs)
        aq = _rope(aq, cos, s_up, s_dn)
        akv = jnp.concatenate([_rope(akv[:, :128], cos, s_up, s_dn), akv[:, 128:]], 1)
    aq_ref[...] = (aq * QK_SCALE).astype(BF16)
    akv_ref[...] = _kv_with_ones(akv).astype(BF16)
    bq_ref[...] = proj(2).astype(BF16)
    bk_ref[...] = (proj(3) * (B_DIM ** -0.5)).astype(BF16)
    bv_ref[...] = proj(4).astype(BF16)
    bo_ref[...] = proj(5)
    bg_ref[...] = proj(6)


def _even_in(x, mod, w, mod_index, seq_len, rope, cache):
    n, d = x.shape
    tm = TOKEN_TILE
    row = lambda width: pl.BlockSpec((tm, width), lambda i: (i, 0))
    in_specs = [row(d), pl.BlockSpec((1, 9, d), mod_index),
                pl.BlockSpec(w.shape, lambda i: (0, 0), pipeline_mode=pl.Buffered(1))]
    args = [x, mod, w]
    if rope:
        per_seq = seq_len // tm
        tabs = _rope_tables(seq_len)
        in_specs += [pl.BlockSpec((tm, 128), lambda i: (i % per_seq, 0))] * 3
        args += list(tabs)
    outs = [(512, BF16), (384, BF16)] + ([(256, F32)] if cache else []) + \
           [(512, BF16), (512, BF16), (512, BF16), (512, F32), (128, F32)]
    return pl.pallas_call(
        functools.partial(_even_in_kernel, rope=rope, cache=cache),
        grid=(n // tm,),
        in_specs=in_specs,
        out_specs=[row(wd) for wd, _ in outs],
        out_shape=[jax.ShapeDtypeStruct((n, wd), dt) for wd, dt in outs],
        compiler_params=_cparams(("parallel",)),
        name="even_in_proj",
    )(*args)


ODD_COLS = (512, 512, 512, 512, 512, 512, 256)


def _head_rms(x, seg_ref, g):
    x2 = x * x
    hi = x2.astype(BF16)
    lo = (x2 - hi.astype(F32)).astype(BF16)
    ms = _dot(hi, seg_ref[...]) + _dot(lo, seg_ref[...])
    return x * lax.rsqrt(ms + 1e-6) * g


def _odd_in_kernel(*refs, rope, cache):
    it = iter(refs)
    x_ref, mod_ref, w_ref, segq_ref, segk_ref, qn_ref, kn_ref = (next(it) for _ in range(7))
    tabs = (next(it), next(it), next(it)) if rope else None
    q_ref, ff_ref, fb_ref, v_ref, cg_ref, dq_ref, dkv_ref = (next(it) for _ in range(7))
    cache_ref = next(it) if cache else None

    m = mod_ref[0]
    h = (x_ref[...] * (1.0 + m[4:5]) + m[3:4]).astype(BF16)
    offs = [0]
    for c in ODD_COLS:
        offs.append(offs[-1] + c)
    proj = lambda k: _dot(h, w_ref[:, offs[k]:offs[k + 1]])

    q_ref[...] = _silu(proj(0)).astype(BF16)
    ff_ref[...] = proj(1)
    fb_ref[...] = proj(2)
    v_ref[...] = proj(3).astype(BF16)
    cg_ref[...] = proj(4)
    dq = _head_rms(proj(5), segq_ref, qn_ref[...])
    dkv = proj(6)
    dk = _head_rms(dkv[:, :128], segk_ref, kn_ref[...])
    dv = dkv[:, 128:]
    if cache:
        cache_ref[...] = jnp.concatenate([dk, dv], 1)
    if rope:
        cos, s_up, s_dn = (t[...] for t in tabs)
        dq = _rope(dq, cos, s_up, s_dn)
        dk = _rope(dk, cos, s_up, s_dn)
    dq_ref[...] = (dq * QK_SCALE).astype(BF16)
    dkv_ref[...] = _kv_with_ones(jnp.concatenate([dk, dv], 1)).astype(BF16)


def _segment_mean_matrix(width):
    r = jnp.arange(width) // HEAD_DIM
    return jnp.where(r[:, None] == r[None, :], 1.0 / HEAD_DIM, 0.0).astype(BF16)


def _odd_in(x, mod, w, q_norm, k_norm, mod_index, seq_len, rope, cache):
    n, d = x.shape
    tm = TOKEN_TILE
    row = lambda width: pl.BlockSpec((tm, width), lambda i: (i, 0))
    const = lambda a: pl.BlockSpec(a.shape, lambda i: (0, 0))
    segq, segk = _segment_mean_matrix(512), _segment_mean_matrix(128)
    qn = jnp.tile(q_norm, D_HEADS).reshape(1, 512)
    kn = jnp.tile(k_norm, D_KV).reshape(1, 128)
    in_specs = [row(d), pl.BlockSpec((1, 9, d), mod_index),
                pl.BlockSpec(w.shape, lambda i: (0, 0), pipeline_mode=pl.Buffered(1)),
                const(segq), const(segk), const(qn), const(kn)]
    args = [x, mod, w, segq, segk, qn, kn]
    if rope:
        per_seq = seq_len // tm
        tabs = _rope_tables(seq_len)
        in_specs += [pl.BlockSpec((tm, 128), lambda i: (i % per_seq, 0))] * 3
        args += list(tabs)
    outs = [(512, BF16), (512, F32), (512, F32), (512, BF16), (512, F32), (512, BF16), (384, BF16)] + \
           ([(256, F32)] if cache else [])
    return pl.pallas_call(
        functools.partial(_odd_in_kernel, rope=rope, cache=cache),
        grid=(n // tm,),
        in_specs=in_specs,
        out_specs=[row(wd) for wd, _ in outs],
        out_shape=[jax.ShapeDtypeStruct((n, wd), dt) for wd, dt in outs],
        compiler_params=_cparams(("parallel",)),
        name="odd_in_proj",
    )(*args)


def _attn_kernel(*refs, seq_len, tq, n_ctx, banded, has_sink):
    it = iter(refs)
    q_ref, kv_ref = next(it), next(it)
    ckv_ref = next(it) if n_ctx else None
    sink_ref = next(it) if has_sink else None
    o_ref = next(it)
    groups = A_HEADS // A_KV
    rows = groups * tq
    j = pl.program_id(1)

    if banded:
        span = tq + 2 * WINDOW
        start = pl.multiple_of(jnp.clip(j * tq - WINDOW, 0, seq_len - span), WINDOW)
        qpos = j * tq + (lax.broadcasted_iota(jnp.int32, (rows, span), 0) & (tq - 1))
        kpos = start + lax.broadcasted_iota(jnp.int32, (rows, span), 1)
        band = jnp.abs(kpos - qpos) <= WINDOW

    def run_kv_head(kh):
        kcol = slice(kh * HEAD_DIM, (kh + 1) * HEAD_DIM)
        vcol = slice(128 + kh * 128, 128 + (kh + 1) * 128)
        qs = jnp.concatenate([q_ref[:, (kh * groups + g) * HEAD_DIM:(kh * groups + g + 1) * HEAD_DIM]
                              for g in range(groups)], axis=0)
        tiles = []
        if banded:
            tiles.append((kv_ref[pl.ds(start, span), kcol], kv_ref[pl.ds(start, span), vcol], band))
        else:
            tk = min(ATT_KEY_TILE, seq_len)
            for t in range(seq_len // tk):
                tiles.append((kv_ref[t * tk:(t + 1) * tk, kcol], kv_ref[t * tk:(t + 1) * tk, vcol], None))
        if n_ctx:
            tiles.append((ckv_ref[0, :, kcol], ckv_ref[0, :, vcol], None))
        scores = [_dot_nt(qs, k_t) for k_t, _, _ in tiles]
        yield
        scores = [s if mask is None else jnp.where(mask, s, NEG_INF) for s, (_, _, mask) in zip(scores, tiles)]
        m = functools.reduce(jnp.maximum, [jnp.max(s, -1, keepdims=True) for s in scores])
        if has_sink:
            sink = jnp.concatenate([jnp.full((tq, 1), sink_ref[kh * groups + g] * LOG2_E, F32)
                                    for g in range(groups)], 0)
            m = jnp.maximum(m, sink)
        pv = functools.reduce(lambda a, b: a + b, [_dot(jnp.exp2(s - m).astype(BF16), v_t)
                                                   for s, (_, v_t, _) in zip(scores, tiles)])
        yield
        total = pv[:, HEAD_DIM:HEAD_DIM + 1]
        if has_sink:
            total = total + jnp.exp2(sink - m)
        o = pv[:, :HEAD_DIM] / total
        for g in range(0, groups, 2):
            pair = jnp.concatenate([o[g * tq:(g + 1) * tq], o[(g + 1) * tq:(g + 2) * tq]], 1)
            c0 = (kh * groups + g) * HEAD_DIM
            o_ref[:, c0:c0 + 2 * HEAD_DIM] = pair.astype(BF16)

    _run_interleaved([run_kv_head(kh) for kh in range(A_KV)])


def _attention(q, kv, ctx_kv, sink, batch, seq_len, tq, banded):
    n_ctx = 0 if ctx_kv is None else ctx_kv.shape[1]
    per_seq = seq_len // tq
    in_specs = [pl.BlockSpec((tq, 512), lambda b, j: (b * per_seq + j, 0)),
                pl.BlockSpec((seq_len, 384), lambda b, j: (b, 0))]
    args = [q, kv]
    if n_ctx:
        in_specs.append(pl.BlockSpec((1, n_ctx, 384), lambda b, j: (b, 0, 0)))
        args.append(ctx_kv)
    if sink is not None:
        in_specs.append(pl.BlockSpec(memory_space=pltpu.SMEM))
        args.append(sink.reshape(-1).astype(F32))
    return pl.pallas_call(
        functools.partial(_attn_kernel, seq_len=seq_len, tq=tq, n_ctx=n_ctx, banded=banded,
                          has_sink=sink is not None),
        grid=(batch, per_seq),
        in_specs=in_specs,
        out_specs=pl.BlockSpec((tq, 512), lambda b, j: (b * per_seq + j, 0)),
        out_shape=jax.ShapeDtypeStruct(q.shape, BF16),
        compiler_params=_cparams(("parallel", "arbitrary")),
        name="gqa_attention",
    )(*args)


def _log_sigmoid(x):
    return jnp.minimum(x, 0.0) - jnp.log1p(jnp.exp(-jnp.abs(x)))


def _mlstm_kernel(*refs, seq_len, has_init, emit_state):
    it = iter(refs)
    q_ref, k_ref, v_ref, g_ref, bo_ref, gb_ref, ng_ref = (next(it) for _ in range(7))
    c0_ref, n0_ref, m0_ref = (next(it), next(it), next(it)) if has_init else (None, None, None)
    y_ref = next(it)
    co_ref, no_ref, mo_ref = (next(it), next(it), next(it)) if emit_state else (None, None, None)
    c_scr, n_scr, m_scr, hf_scr, hb_scr = (next(it) for _ in range(5))

    t_len = MLSTM_T
    n_chunks = seq_len // t_len
    n_streams = 2 * B_HEADS

    for s in range(n_streams):
        if has_init:
            c_scr[s] = c0_ref[0, s // B_HEADS, s % B_HEADS]
        else:
            c_scr[s] = jnp.zeros((B_DIM, B_DIM), F32)
    if has_init:
        n_scr[...] = n0_ref[0]
        m_scr[...] = m0_ref[0]
    else:
        n_scr[...] = jnp.zeros_like(n_scr)
        m_scr[...] = jnp.zeros_like(m_scr)

    ri = lax.broadcasted_iota(jnp.int32, (t_len, t_len), 0)
    ci = lax.broadcasted_iota(jnp.int32, (t_len, t_len), 1)
    lower = ci <= ri
    upper = ci >= ri
    tri_f = jnp.where(lower, 1.0, 0.0).astype(BF16)
    tri_b = jnp.where(upper, 1.0, 0.0).astype(BF16)
    gate_bias = gb_ref[...]
    sel_row = lax.broadcasted_iota(jnp.int32, (128, B_HEADS * B_DIM), 0)
    sel_head = lax.broadcasted_iota(jnp.int32, (128, B_HEADS * B_DIM), 1) // B_DIM
    gate_select = []
    for direction in range(2):
        i_base = 2 * direction * B_HEADS
        gate_select.append((jnp.where(sel_row == i_base + sel_head, 1.0, 0.0).astype(BF16),
                            jnp.where(sel_row == i_base + B_HEADS + sel_head, 1.0, 0.0).astype(BF16)))

    def run_direction(c, direction):
        chunk = c if direction == 0 else n_chunks - 1 - c
        rows = pl.ds(pl.multiple_of(chunk * t_len, t_len), t_len)
        tri = tri_f if direction == 0 else tri_b
        causal = lower if direction == 0 else upper
        last = t_len - 1 if direction == 0 else 0
        h_out = hf_scr if direction == 0 else hb_scr
        sel_i, sel_f = gate_select[direction]
        gates = g_ref[rows, :] + gate_bias
        hi, mid, lo = _split3(_log_sigmoid(gates))
        b3 = _dot(tri, jnp.concatenate([hi, mid, lo], 1))
        ig_rep = _dot_sel_rhs(gates, sel_i)
        yield
        bc_all = b3[:, :128] + b3[:, 128:256] + b3[:, 256:]
        bc_rep = _dot_sel_rhs(bc_all, sel_f)
        u_t = (gates - pltpu.roll(bc_all, 128 - B_HEADS, 1)).T
        heads = []
        for hd in range(B_HEADS):
            s = direction * B_HEADS + hd
            col = slice(hd * B_DIM, (hd + 1) * B_DIM)
            qc, kc, vc = q_ref[rows, col], k_ref[rows, col], v_ref[rows, col]
            c_prev = c_scr[s]
            n_prev = n_scr[s:s + 1, :]
            qk = _dot_nt(qc, kc)
            q_state = _dot(qc, c_prev.astype(BF16))
            qn = _dot_nt(qc, jnp.broadcast_to(n_prev, (B_DIM, B_DIM)).astype(BF16))
            heads.append((s, col, qc, kc, vc, c_prev, n_prev, qk, q_state, qn))
        yield
        staged = []
        for (s, col, qc, kc, vc, c_prev, n_prev, qk, q_state, qn) in heads:
            i_col = 2 * direction * B_HEADS + (s % B_HEADS)
            ig, bc = ig_rep[:, col], bc_rep[:, col]
            dmat = jnp.where(causal, bc[:, :t_len] + u_t[i_col:i_col + 1, :], -jnp.inf)
            m_prev = m_scr[s:s + 1, :]
            inter = bc + m_prev
            m_t = jnp.maximum(inter, jnp.max(dmat, -1, keepdims=True))
            w = (jnp.exp(dmat - m_t[:, :t_len]) * qk).astype(BF16)
            wv = _dot(w, jnp.concatenate([vc, jnp.ones_like(vc)], 1))
            b_tot = bc[last:last + 1, :]
            g = b_tot - bc + ig
            m_new = jnp.maximum(b_tot + m_prev, jnp.max(g, 0, keepdims=True))
            kw = kc.astype(F32) * jnp.exp(g - m_new)
            upd = _dot_tn(kw.astype(BF16), vc)
            decay = jnp.exp(b_tot + m_prev - m_new)
            staged.append((s, col, inter, m_t, c_prev, n_prev, q_state, qn, wv, m_new, kw, upd, decay))
        yield
        for (s, col, inter, m_t, c_prev, n_prev, q_state, qn, wv, m_new, kw, upd, decay) in staged:
            a = jnp.exp(inter - m_t)
            num = a * q_state + wv[:, :B_DIM]
            den = a * qn + wv[:, B_DIM:]
            h_out[rows, col] = num / jnp.maximum(jnp.abs(den), jnp.exp(-m_t))
            c_scr[s] = decay * c_prev + upd
            n_scr[s:s + 1, :] = decay * n_prev + jnp.sum(kw, 0, keepdims=True)
            m_scr[s:s + 1, :] = m_new

    def step(c, carry):
        _run_interleaved([run_direction(c, 0), run_direction(c, 1)])
        return carry

    lax.fori_loop(0, n_chunks, step, 0)

    blk = min(256, seq_len)
    ng = ng_ref[...]

    def finish(i, carry):
        rows = pl.ds(pl.multiple_of(i * blk, blk), blk)
        hsum = hf_scr[rows, :] + hb_scr[rows, :]
        gate = jax.nn.sigmoid(bo_ref[rows, :])
        parts = [_rms_lastdim(hsum[:, hd * B_DIM:(hd + 1) * B_DIM], ng[:, hd * B_DIM:(hd + 1) * B_DIM])
                 for hd in range(B_HEADS)]
        y_ref[rows, :] = (gate * jnp.concatenate(parts, 1)).astype(BF16)
        return carry

    lax.fori_loop(0, seq_len // blk, finish, 0)

    if emit_state:
        for s in range(n_streams):
            co_ref[0, s // B_HEADS, s % B_HEADS] = c_scr[s]
        no_ref[0] = n_scr[...]
        mo_ref[0] = m_scr[...]


def _mlstm(q, k, v, gates, bo, gate_bias, norm_g, init, batch, seq_len, emit_state):
    n = q.shape[0]
    row = lambda width: pl.BlockSpec((seq_len, width), lambda b: (b, 0))
    const = lambda a: pl.BlockSpec(a.shape, lambda b: (0,) * a.ndim)
    gb = jnp.zeros((1, 128), F32).at[0, :4 * B_HEADS].set(gate_bias.reshape(-1))
    ng = norm_g.reshape(1, B_HEADS * B_DIM)
    in_specs = [row(512), row(512), row(512), row(128), row(512), const(gb), const(ng)]
    args = [q, k, v, gates, bo, gb, ng]
    n_streams = 2 * B_HEADS
    if init is not None:
        c0, n0, m0 = init
        in_specs += [pl.BlockSpec((1, 2, B_HEADS, B_DIM, B_DIM), lambda b: (b, 0, 0, 0, 0)),
                     pl.BlockSpec((1, n_streams, B_DIM), lambda b: (b, 0, 0)),
                     pl.BlockSpec((1, n_streams, B_DIM), lambda b: (b, 0, 0))]
        args += [c0, n0.reshape(batch, n_streams, B_DIM),
                 jnp.broadcast_to(m0.reshape(batch, n_streams, 1), (batch, n_streams, B_DIM))]
    out_specs = [row(512)]
    out_shape = [jax.ShapeDtypeStruct((n, 512), BF16)]
    if emit_state:
        out_specs += [pl.BlockSpec((1, 2, B_HEADS, B_DIM, B_DIM), lambda b: (b, 0, 0, 0, 0)),
                      pl.BlockSpec((1, n_streams, B_DIM), lambda b: (b, 0, 0)),
                      pl.BlockSpec((1, n_streams, B_DIM), lambda b: (b, 0, 0))]
        out_shape += [jax.ShapeDtypeStruct((batch, 2, B_HEADS, B_DIM, B_DIM), F32),
                      jax.ShapeDtypeStruct((batch, n_streams, B_DIM), F32),
                      jax.ShapeDtypeStruct((batch, n_streams, B_DIM), F32)]
    return pl.pallas_call(
        functools.partial(_mlstm_kernel, seq_len=seq_len, has_init=init is not None, emit_state=emit_state),
        grid=(batch,),
        in_specs=in_specs,
        out_specs=out_specs,
        out_shape=out_shape,
        scratch_shapes=[pltpu.VMEM((n_streams, B_DIM, B_DIM), F32),
                        pltpu.VMEM((n_streams, B_DIM), F32),
                        pltpu.VMEM((n_streams, B_DIM), F32),
                        pltpu.VMEM((seq_len, 512), F32),
                        pltpu.VMEM((seq_len, 512), F32)],
        compiler_params=_cparams(("parallel",)),
        name="mlstm_scan",
    )(*args)


def _hgrn_kernel(*refs, seq_len, layer, has_init, emit_state):
    it = iter(refs)
    q_ref, ff_ref, fb_ref, v_ref, cg_ref, lbl_ref, ng_ref = (next(it) for _ in range(7))
    s0_ref = next(it) if has_init else None
    y_ref = next(it)
    so_ref = next(it) if emit_state else None
    st_scr, of_scr, ob_scr = next(it), next(it), next(it)

    t_len = HGRN_T
    n_sub = t_len // SUB
    n_chunks = seq_len // t_len
    n_streams = 2 * C_HEADS

    logits = lbl_ref[...]
    e = jnp.exp(logits - jnp.max(logits, 0, keepdims=True))
    sm = e / jnp.sum(e, 0, keepdims=True)
    lb = jnp.sum(sm[0:layer + 1], 0, keepdims=True) - sm[0:1]

    for s in range(n_streams):
        if has_init:
            st_scr[s] = s0_ref[0, s // C_HEADS, s % C_HEADS].T
        else:
            st_scr[s] = jnp.zeros((C_DV, C_DK), F32)

    ri = lax.broadcasted_iota(jnp.int32, (t_len, t_len), 0)
    ci = lax.broadcasted_iota(jnp.int32, (t_len, t_len), 1)
    tri_f = jnp.where(ci <= ri, 1.0, 0.0).astype(BF16)
    tri_b = jnp.where(ci >= ri, 1.0, 0.0).astype(BF16)
    sub_row = lax.broadcasted_iota(jnp.int32, (SUB, C_DK), 0)
    ones_dk = jnp.ones((C_DK, C_DK), BF16)

    def run_stream(c, direction, hd):
        chunk = c if direction == 0 else n_chunks - 1 - c
        rows = pl.ds(pl.multiple_of(chunk * t_len, t_len), t_len)
        f_ref = ff_ref if direction == 0 else fb_ref
        tri = tri_f if direction == 0 else tri_b
        last = t_len - 1 if direction == 0 else 0
        o_out = of_scr if direction == 0 else ob_scr
        s = direction * C_HEADS + hd
        col = slice(hd * C_DK, (hd + 1) * C_DK)
        lbh = lb[:, col]
        f = lbh + (1.0 - lbh) * jax.nn.sigmoid(f_ref[rows, col])
        kk = 1.0 - f
        hi, mid, lo = _split3(jnp.log(f))
        a3 = _dot(tri, jnp.concatenate([hi, mid, lo], 1))
        yield
        a_cum = a3[:, :C_DK] + a3[:, C_DK:2 * C_DK] + a3[:, 2 * C_DK:]
        a_tot = a_cum[last:last + 1, :]
        qf = q_ref[rows, col].astype(F32)
        vc = v_ref[rows, col]
        vf = vc.astype(F32)
        st = st_scr[s]
        inter = _dot_nt((qf * jnp.exp(a_cum)).astype(BF16), st.astype(BF16))
        upd = _dot_tn(vc, (kk * jnp.exp(a_tot - a_cum)).astype(BF16))
        ps = []
        for blk in range(n_sub):
            b0 = blk * SUB
            a_i, q_i, k_i = (t[b0:b0 + SUB] for t in (a_cum, qf, kk))
            for j in range(SUB):
                seen = (sub_row >= j) if direction == 0 else (sub_row <= j)
                dec = jnp.where(seen, jnp.exp(a_i - a_i[j:j + 1]), 0.0)
                ps.append(dec * q_i * k_i[j:j + 1])
        att = _dot(jnp.concatenate(ps, 0).astype(BF16), ones_dk)
        off = []
        for blk in range(n_sub):
            b0 = blk * SUB
            if direction == 0 and blk > 0:
                a_ref, kr = a_cum[b0 - 1:b0], slice(0, b0)
            elif direction == 1 and blk < n_sub - 1:
                a_ref, kr = a_cum[b0 + SUB:b0 + SUB + 1], slice(b0 + SUB, t_len)
            else:
                off.append(None)
                continue
            qt = (qf[b0:b0 + SUB] * jnp.exp(a_cum[b0:b0 + SUB] - a_ref)).astype(BF16)
            kt = (kk[kr] * jnp.exp(a_ref - a_cum[kr])).astype(BF16)
            off.append((_dot_nt(qt, kt), kr))
        yield
        st_scr[s] = jnp.exp(a_tot) * st + upd
        outs = []
        for blk in range(n_sub):
            b0 = blk * SUB
            o_i = inter[b0:b0 + SUB]
            for j in range(SUB):
                r = (blk * SUB + j) * SUB
                o_i = o_i + att[r:r + SUB] * vf[b0 + j:b0 + j + 1]
            if off[blk] is not None:
                att_off, kr = off[blk]
                outs.append((o_i, _dot(att_off.astype(BF16), vc[kr])))
            else:
                outs.append((o_i, None))
        yield
        o_out[rows, col] = jnp.concatenate([o_i if o_off is None else o_i + o_off for o_i, o_off in outs], 0)

    def step(c, carry):
        _run_interleaved([run_stream(c, direction, hd) for direction in range(2) for hd in range(C_HEADS)])
        return carry

    lax.fori_loop(0, n_chunks, step, 0)

    blk_rows = min(256, seq_len)
    ng = ng_ref[...]

    def finish(i, carry):
        rows = pl.ds(pl.multiple_of(i * blk_rows, blk_rows), blk_rows)
        osum = of_scr[rows, :] + ob_scr[rows, :]
        parts = [_rms_lastdim(osum[:, hd * C_DV:(hd + 1) * C_DV], ng[:, hd * C_DV:(hd + 1) * C_DV])
                 for hd in range(C_HEADS)]
        y_ref[rows, :] = (jnp.concatenate(parts, 1) * _silu(cg_ref[rows, :])).astype(BF16)
        return carry

    lax.fori_loop(0, seq_len // blk_rows, finish, 0)

    if emit_state:
        for s in range(n_streams):
            so_ref[0, s // C_HEADS, s % C_HEADS] = st_scr[s].T


def _hgrn(q, ff, fb, v, cg, lb_logits, norm_g, layer, init, batch, seq_len, emit_state):
    n = q.shape[0]
    row = lambda width: pl.BlockSpec((seq_len, width), lambda b: (b, 0))
    const = lambda a: pl.BlockSpec(a.shape, lambda b: (0,) * a.ndim)
    ng = norm_g.reshape(1, C_HEADS * C_DV)
    state_spec = pl.BlockSpec((1, 2, C_HEADS, C_DK, C_DV), lambda b: (b, 0, 0, 0, 0))
    in_specs = [row(512)] * 5 + [const(lb_logits), const(ng)]
    args = [q, ff, fb, v, cg, lb_logits, ng]
    if init is not None:
        in_specs.append(state_spec)
        args.append(init)
    out_specs = [row(512)]
    out_shape = [jax.ShapeDtypeStruct((n, 512), BF16)]
    if emit_state:
        out_specs.append(state_spec)
        out_shape.append(jax.ShapeDtypeStruct((batch, 2, C_HEADS, C_DK, C_DV), F32))
    return pl.pallas_call(
        functools.partial(_hgrn_kernel, seq_len=seq_len, layer=layer, has_init=init is not None,
                          emit_state=emit_state),
        grid=(batch,),
        in_specs=in_specs,
        out_specs=out_specs,
        out_shape=out_shape,
        scratch_shapes=[pltpu.VMEM((2 * C_HEADS, C_DV, C_DK), F32),
                        pltpu.VMEM((seq_len, 512), F32),
                        pltpu.VMEM((seq_len, 512), F32)],
        compiler_params=_cparams(("parallel",)),
        name="hgrn2_scan",
    )(*args)


def _mix_out_kernel(x_ref, mod_ref, ya_ref, yb_ref, w_ref, g_ref, b_ref, o_ref):
    gate = mod_ref[0][5:6]
    half = ya_ref.shape[1]
    y = _dot(ya_ref[...], w_ref[:half, :]) + _dot(yb_ref[...], w_ref[half:, :])
    o_ref[...] = _layernorm(ALPHA * x_ref[...] + gate * y, g_ref[...], b_ref[...])


def _mix_out(x, mod, ya, yb, w, g, b, mod_index):
    n, d = x.shape
    tm = TOKEN_TILE
    row = lambda width: pl.BlockSpec((tm, width), lambda i: (i, 0))
    return pl.pallas_call(
        _mix_out_kernel,
        grid=(n // tm,),
        in_specs=[row(d), pl.BlockSpec((1, 9, d), mod_index), row(ya.shape[1]), row(yb.shape[1]),
                  pl.BlockSpec(w.shape, lambda i: (0, 0), pipeline_mode=pl.Buffered(1)),
                  pl.BlockSpec((1, d), lambda i: (0, 0)), pl.BlockSpec((1, d), lambda i: (0, 0))],
        out_specs=row(d),
        out_shape=jax.ShapeDtypeStruct((n, d), F32),
        compiler_params=_cparams(("parallel",)),
        name="mixer_out_proj",
    )(x, mod, ya, yb, w, g.reshape(1, d), b.reshape(1, d))


def _prep_ffn(w1, w3, w2):
    d, f = w1.shape
    nf = f // FF_CHUNK
    chunk_major = lambda w: jnp.transpose(w.astype(BF16).reshape(d, nf, FF_CHUNK), (1, 0, 2))
    return chunk_major(w1), chunk_major(w3), w2.astype(BF16).reshape(nf, FF_CHUNK, d)


def _prep_w_in_even(w):
    d = w.shape[0]
    a, bq, bk, bv = w[:, :768], w[:, 768:1280], w[:, 1280:1792], w[:, 1792:2304]
    bg, bo = w[:, 2304:2320], w[:, 2320:2832]
    pad = jnp.zeros((d, 128 - bg.shape[1]), w.dtype)
    return jnp.concatenate([a, bq, bk, bv, bo, bg, pad], 1).astype(BF16)


def _prep_ctx_kv(k_ctx, v_ctx):
    nb, past = k_ctx.shape[:2]
    ones = jnp.ones((nb, past, HEAD_DIM), k_ctx.dtype)
    return jnp.concatenate([k_ctx.reshape(nb, past, -1), v_ctx[:, :, 0], ones, v_ctx[:, :, 1], ones], -1).astype(BF16)


def kernel(x_prompt, x_sample, c, cache_a_k, cache_a_v, state_b_C, state_b_n, state_b_m, state_c_S, cache_d_k, cache_d_v, c_ctx, ada_w, ada_b, ln_g, ln_b, ffn_w1, ffn_w3, ffn_w2, w_in_even, w_out_even, a_sink, b_gate_bias, b_norm_g, w_in_odd, w_out_odd, c_lb_logits, c_norm_g, d_q_norm, d_k_norm):
    batch_p, len_p, d = x_prompt.shape
    batch_s, len_s, _ = x_sample.shape

    cvec = jnp.concatenate([c_ctx[None], c, jnp.zeros((MOD_ROWS - 1 - batch_s, d), F32)], 0)
    mod_all = _modulation(cvec, ada_w, ada_b)

    groups = [
        dict(x=x_prompt.reshape(batch_p * len_p, d), batch=batch_p, seq=len_p, prompt=True,
             mod_index=_mod_index(0, None)),
        dict(x=x_sample.reshape(batch_s * len_s, d), batch=batch_s, seq=len_s, prompt=False,
             mod_index=_mod_index(1, len_s // TOKEN_TILE)),
    ]
    new = {}

    for l in range(DEPTH):
        mod = mod_all[l].reshape(MOD_ROWS, 9, d)
        i = l // 2
        ffn_a = _prep_ffn(ffn_w1[l, 0], ffn_w3[l, 0], ffn_w2[l, 0])
        ffn_b = _prep_ffn(ffn_w1[l, 1], ffn_w3[l, 1], ffn_w2[l, 1])
        if l % 2 == 0:
            w_in = _prep_w_in_even(w_in_even[i])
            w_out = w_out_even[i].astype(BF16)
        else:
            w_in = w_in_odd[i].astype(BF16)
            w_out = w_out_odd[i].astype(BF16)

        for grp in groups:
            x, mi, nb, sl, prompt = grp["x"], grp["mod_index"], grp["batch"], grp["seq"], grp["prompt"]
            x = _ffn(x, mod, 0, ln_g[l, 0], ln_b[l, 0], *ffn_a, mi)
            if l % 2 == 0:
                outs = _even_in(x, mod, w_in, mi, sl, rope=not prompt, cache=prompt)
                if prompt:
                    aq, akv, kv_cache, bq, bk, bv, bo, bg = outs
                    new["a_k"] = kv_cache[:, :128].reshape(nb, 1, sl, A_KV, HEAD_DIM)
                    new["a_v"] = kv_cache[:, 128:].reshape(nb, 1, sl, A_KV, HEAD_DIM)
                    ya = _attention(aq, akv, None, a_sink[i], nb, sl, tq=sl, banded=False)
                    yb, c_new, n_new, m_new = _mlstm(bq, bk, bv, bg, bo, b_gate_bias[i], b_norm_g[i], None,
                                                     nb, sl, emit_state=True)
                    new["b_C"] = c_new[:, None]
                    new["b_n"] = n_new.reshape(nb, 1, 2, B_HEADS, B_DIM)
                    new["b_m"] = m_new[:, :, 0].reshape(nb, 1, 2, B_HEADS)
                else:
                    aq, akv, bq, bk, bv, bo, bg = outs
                    ya = _attention(aq, akv, _prep_ctx_kv(cache_a_k[:, i], cache_a_v[:, i]), a_sink[i], nb, sl,
                                    tq=Q_BLOCK, banded=True)
                    init = (state_b_C[:, i], state_b_n[:, i], state_b_m[:, i])
                    yb, = _mlstm(bq, bk, bv, bg, bo, b_gate_bias[i], b_norm_g[i], init, nb, sl, emit_state=False)
                x = _mix_out(x, mod, ya, yb, w_out, ln_g[l, 1], ln_b[l, 1], mi)
            else:
                outs = _odd_in(x, mod, w_in, d_q_norm[i], d_k_norm[i], mi, sl, rope=not prompt, cache=prompt)
                if prompt:
                    cq, ff, fb, cv, cg, dq, dkv, kv_cache = outs
                    new["d_k"] = kv_cache[:, :128].reshape(nb, 1, sl, D_KV, HEAD_DIM)
                    new["d_v"] = kv_cache[:, 128:].reshape(nb, 1, sl, D_KV, HEAD_DIM)
                    yc, s_new = _hgrn(cq, ff, fb, cv, cg, c_lb_logits, c_norm_g[i], l, None, nb, sl, emit_state=True)
                    new["c_S"] = s_new[:, None]
                    yd = _attention(dq, dkv, None, None, nb, sl, tq=sl, banded=False)
                else:
                    cq, ff, fb, cv, cg, dq, dkv = outs
                    yc, = _hgrn(cq, ff, fb, cv, cg, c_lb_logits, c_norm_g[i], l, state_c_S[:, i], nb, sl,
                                emit_state=False)
                    yd = _attention(dq, dkv, _prep_ctx_kv(cache_d_k[:, i], cache_d_v[:, i]), None, nb, sl,
                                    tq=Q_BLOCK, banded=False)
                x = _mix_out(x, mod, yc, yd, w_out, ln_g[l, 1], ln_b[l, 1], mi)
            x = _ffn(x, mod, 2, ln_g[l, 2], ln_b[l, 2], *ffn_b, mi)
            grp["x"] = x

    y_prompt = groups[0]["x"].reshape(batch_p, len_p, d)
    y_sample = groups[1]["x"].reshape(batch_s, len_s, d)
    return (y_prompt, y_sample, new["a_k"], new["a_v"], new["b_C"], new["b_n"], new["b_m"], new["c_S"],
            new["d_k"], new["d_v"])
```

```python
import functools

import jax
import jax.numpy as jnp
from jax import lax
from jax.experimental import pallas as pl
from jax.experimental.pallas import tpu as pltpu

F32 = jnp.float32
BF16 = jnp.bfloat16

D_MODEL = 1024
DEPTH = 2
GRID_W = 64
HEAD_DIM = 64
A_HEADS = 8
A_KV = 2
WINDOW = 128
B_HEADS = 4
B_DIM = 128
C_HEADS = 4
C_DK = 128
C_DV = 128
D_HEADS = 8
D_KV = 2
D_FF = 2816
Q_BLOCK = 128
ROPE_THETA = 10000.0
ALPHA = (2 * DEPTH) ** 0.25
NEG_INF = -1e30
LOG2_E = 1.4426950408889634
QK_SCALE = HEAD_DIM ** -0.5 * LOG2_E

MOD_ROWS = 8
FF_CHUNK = 256
TOKEN_TILE = 512
MLSTM_T = 64
HGRN_T = 32
SUB = 8
ATT_KEY_TILE = 512
VMEM_LIMIT = 56 * 1024 * 1024


def _cparams(sem):
    return pltpu.CompilerParams(dimension_semantics=sem, vmem_limit_bytes=VMEM_LIMIT)


def _dot(a, b):
    return jnp.dot(a, b, preferred_element_type=F32)


def _dot_nt(a, b):
    return lax.dot_general(a, b, (((1,), (1,)), ((), ())), preferred_element_type=F32)


def _dot_tn(a, b):
    return lax.dot_general(a, b, (((0,), (0,)), ((), ())), preferred_element_type=F32)


def _split3(x):
    hi = x.astype(BF16)
    r1 = x - hi.astype(F32)
    mid = r1.astype(BF16)
    lo = (r1 - mid.astype(F32)).astype(BF16)
    return hi, mid, lo


def _dot_sel(sel, x):
    hi, mid, lo = _split3(x)
    return _dot(sel, hi) + _dot(sel, mid) + _dot(sel, lo)


def _dot_sel_rhs(x, sel):
    hi, mid, lo = _split3(x)
    return _dot(hi, sel) + _dot(mid, sel) + _dot(lo, sel)


def _run_interleaved(gens):
    live = list(gens)
    while live:
        nxt = []
        for g in live:
            try:
                next(g)
                nxt.append(g)
            except StopIteration:
                pass
        live = nxt


def _run_staggered(gens):
    pending, live = list(gens), []
    while pending or live:
        if pending:
            live.insert(0, pending.pop(0))
        nxt = []
        for g in live:
            try:
                next(g)
                nxt.append(g)
            except StopIteration:
                pass
        live = nxt


def _silu(x):
    return x * jax.nn.sigmoid(x)


def _layernorm(z, g, b):
    mu = jnp.mean(z, -1, keepdims=True)
    zc = z - mu
    var = jnp.mean(zc * zc, -1, keepdims=True)
    return zc * lax.rsqrt(var + 1e-5) * g + b


def _rms_lastdim(x, g):
    return x * lax.rsqrt(jnp.mean(x * x, -1, keepdims=True) + 1e-6) * g


def _mod_index(group_start, tiles_per_request):
    if tiles_per_request is None:
        return lambda i: (group_start, 0, 0)
    return lambda i: (group_start + i // tiles_per_request, 0, 0)


def _mod_kernel(c_ref, w_ref, b_ref, o_ref):
    s = _silu(c_ref[...]).astype(BF16)
    o_ref[0] = _dot(s, w_ref[0].astype(BF16)) + b_ref[0]


def _modulation(cvec, ada_w, ada_b):
    depth, d, n = ada_w.shape
    tn = 1152
    return pl.pallas_call(
        _mod_kernel,
        grid=(depth, n // tn),
        in_specs=[pl.BlockSpec((MOD_ROWS, d), lambda l, j: (0, 0)),
                  pl.BlockSpec((1, d, tn), lambda l, j: (l, 0, j)),
                  pl.BlockSpec((1, 1, tn), lambda l, j: (l, 0, j))],
        out_specs=pl.BlockSpec((1, MOD_ROWS, tn), lambda l, j: (l, 0, j)),
        out_shape=jax.ShapeDtypeStruct((depth, MOD_ROWS, n), F32),
        compiler_params=_cparams(("parallel", "parallel")),
        name="modulation",
    )(cvec, ada_w, ada_b.reshape(depth, 1, n))


def _ffn_kernel(x_ref, mod_ref, w1_ref, w3_ref, w2_ref, g_ref, b_ref, o_ref, h_ref, acc_ref, *, j, nf):
    m = mod_ref[0]
    shift, scale, gate = m[3 * j:3 * j + 1], m[3 * j + 1:3 * j + 2], m[3 * j + 2:3 * j + 3]
    h_ref[...] = (x_ref[...] * (1.0 + scale) + shift).astype(BF16)
    for f in range(nf):
        cols = slice(f * FF_CHUNK, (f + 1) * FF_CHUNK)
        h = h_ref[...]
        u = (_silu(_dot(h, w1_ref[:, cols])) * _dot(h, w3_ref[:, cols])).astype(BF16)
        y = _dot(u, w2_ref[cols, :])
        if f == 0:
            acc_ref[...] = y
        else:
            acc_ref[...] += y
    z = ALPHA * x_ref[...] + 0.5 * gate * acc_ref[...]
    o_ref[...] = _layernorm(z, g_ref[...], b_ref[...])


def _ffn(x, mod, j, g, b, w1, w3, w2, mod_index):
    n, d = x.shape
    nf = w1.shape[1] // FF_CHUNK
    tm = TOKEN_TILE
    whole = lambda a: pl.BlockSpec(a.shape, lambda i: (0,) * a.ndim, pipeline_mode=pl.Buffered(1))
    return pl.pallas_call(
        functools.partial(_ffn_kernel, j=j, nf=nf),
        grid=(n // tm,),
        in_specs=[pl.BlockSpec((tm, d), lambda i: (i, 0)),
                  pl.BlockSpec((1, 9, d), mod_index),
                  whole(w1), whole(w3), whole(w2),
                  pl.BlockSpec((1, d), lambda i: (0, 0)),
                  pl.BlockSpec((1, d), lambda i: (0, 0))],
        out_specs=pl.BlockSpec((tm, d), lambda i: (i, 0)),
        out_shape=jax.ShapeDtypeStruct((n, d), F32),
        scratch_shapes=[pltpu.VMEM((tm, d), BF16), pltpu.VMEM((tm, d), F32)],
        compiler_params=_cparams(("parallel",)),
        name="ffn_sublayer",
    )(x, mod, w1, w3, w2, g.reshape(1, d), b.reshape(1, d))


def _rope_tables(length):
    t = jnp.arange(length)
    nf = HEAD_DIM // 4
    inv = ROPE_THETA ** (-jnp.arange(nf, dtype=F32) / nf)
    ang_r = (t // GRID_W).astype(F32)[:, None] * inv[None]
    ang_c = (t % GRID_W).astype(F32)[:, None] * inv[None]
    cr, sr, cc, sc = jnp.cos(ang_r), jnp.sin(ang_r), jnp.cos(ang_c), jnp.sin(ang_c)
    z = jnp.zeros_like(cr)
    cos = jnp.concatenate([cr, cr, cc, cc], 1)
    sin_up = jnp.concatenate([-sr, z, -sc, z], 1)
    sin_dn = jnp.concatenate([z, sr, z, sc], 1)
    two = lambda a: jnp.concatenate([a, a], 1)
    return two(cos), two(sin_up), two(sin_dn)


def _rope128(x, cos, sin_up, sin_dn):
    nf = HEAD_DIM // 4
    return x * cos + pltpu.roll(x, 128 - nf, 1) * sin_up + pltpu.roll(x, nf, 1) * sin_dn


def _rope(x, cos, sin_up, sin_dn):
    parts = [_rope128(x[:, c:c + 128], cos, sin_up, sin_dn) for c in range(0, x.shape[1], 128)]
    return parts[0] if len(parts) == 1 else jnp.concatenate(parts, 1)


EVEN_COLS = (512, 256, 512, 512, 512, 512, 128)


KV_WIDTH = 128 + A_KV * 256


def _kv_with_ones(kv):
    ones = jnp.ones((kv.shape[0], 128), kv.dtype)
    v0, v1 = kv[:, 128:192], kv[:, 192:256]
    return jnp.concatenate([kv[:, :128], v0, v0, ones, v1, v1, ones], 1)


def _even_in_kernel(*refs, rope, cache):
    it = iter(refs)
    x_ref, mod_ref, w_ref = next(it), next(it), next(it)
    tabs = (next(it), next(it), next(it)) if rope else None
    aq_ref, akv_ref = next(it), next(it)
    cache_ref = next(it) if cache else None
    bq_ref, bk_ref, bv_ref, bo_ref, bg_ref = next(it), next(it), next(it), next(it), next(it)

    m = mod_ref[0]
    h = (x_ref[...] * (1.0 + m[4:5]) + m[3:4]).astype(BF16)
    offs = [0]
    for c in EVEN_COLS:
        offs.append(offs[-1] + c)
    proj = lambda k: _dot(h, w_ref[:, offs[k]:offs[k + 1]])

    aq = proj(0)
    akv = proj(1)
    if cache:
        cache_ref[...] = akv
    if rope:
        cos, s_up, s_dn = (t[...] for t in tabs)
        aq = _rope(aq, cos, s_up, s_dn)
        akv = jnp.concatenate([_rope(akv[:, :128], cos, s_up, s_dn), akv[:, 128:]], 1)
    aq_ref[...] = (aq * QK_SCALE).astype(BF16)
    akv_ref[...] = _kv_with_ones(akv).astype(BF16)
    bq_ref[...] = proj(2).astype(BF16)
    bk_ref[...] = (proj(3) * (B_DIM ** -0.5)).astype(BF16)
    bv_ref[...] = proj(4).astype(BF16)
    bo_ref[...] = proj(5)
    bg_ref[...] = proj(6)


def _even_in(x, mod, w, mod_index, seq_len, rope, cache):
    n, d = x.shape
    tm = TOKEN_TILE
    row = lambda width: pl.BlockSpec((tm, width), lambda i: (i, 0))
    in_specs = [row(d), pl.BlockSpec((1, 9, d), mod_index),
                pl.BlockSpec(w.shape, lambda i: (0, 0), pipeline_mode=pl.Buffered(1))]
    args = [x, mod, w]
    if rope:
        per_seq = seq_len // tm
        tabs = _rope_tables(seq_len)
        in_specs += [pl.BlockSpec((tm, 128), lambda i: (i % per_seq, 0))] * 3
        args += list(tabs)
    outs = [(512, BF16), (KV_WIDTH, BF16)] + ([(256, F32)] if cache else []) + \
           [(512, BF16), (512, BF16), (512, BF16), (512, F32), (128, F32)]
    return pl.pallas_call(
        functools.partial(_even_in_kernel, rope=rope, cache=cache),
        grid=(n // tm,),
        in_specs=in_specs,
        out_specs=[row(wd) for wd, _ in outs],
        out_shape=[jax.ShapeDtypeStruct((n, wd), dt) for wd, dt in outs],
        compiler_params=_cparams(("parallel",)),
        name="even_in_proj",
    )(*args)


ODD_COLS = (512, 512, 512, 512, 512, 512, 256)


def _head_rms(x, seg_ref, g):
    x2 = x * x
    hi = x2.astype(BF16)
    lo = (x2 - hi.astype(F32)).astype(BF16)
    ms = _dot(hi, seg_ref[...]) + _dot(lo, seg_ref[...])
    return x * lax.rsqrt(ms + 1e-6) * g


def _odd_in_kernel(*refs, rope, cache):
    it = iter(refs)
    x_ref, mod_ref, w_ref, segq_ref, segk_ref, qn_ref, kn_ref = (next(it) for _ in range(7))
    tabs = (next(it), next(it), next(it)) if rope else None
    q_ref, ff_ref, fb_ref, v_ref, cg_ref, dq_ref, dkv_ref = (next(it) for _ in range(7))
    cache_ref = next(it) if cache else None

    m = mod_ref[0]
    h = (x_ref[...] * (1.0 + m[4:5]) + m[3:4]).astype(BF16)
    offs = [0]
    for c in ODD_COLS:
        offs.append(offs[-1] + c)
    proj = lambda k: _dot(h, w_ref[:, offs[k]:offs[k + 1]])

    q_ref[...] = _silu(proj(0)).astype(BF16)
    ff_ref[...] = proj(1)
    fb_ref[...] = proj(2)
    v_ref[...] = proj(3).astype(BF16)
    cg_ref[...] = proj(4)
    dq = _head_rms(proj(5), segq_ref, qn_ref[...])
    dkv = proj(6)
    dk = _head_rms(dkv[:, :128], segk_ref, kn_ref[...])
    dv = dkv[:, 128:]
    if cache:
        cache_ref[...] = jnp.concatenate([dk, dv], 1)
    if rope:
        cos, s_up, s_dn = (t[...] for t in tabs)
        dq = _rope(dq, cos, s_up, s_dn)
        dk = _rope(dk, cos, s_up, s_dn)
    dq_ref[...] = (dq * QK_SCALE).astype(BF16)
    dkv_ref[...] = _kv_with_ones(jnp.concatenate([dk, dv], 1)).astype(BF16)


def _segment_mean_matrix(width):
    r = jnp.arange(width) // HEAD_DIM
    return jnp.where(r[:, None] == r[None, :], 1.0 / HEAD_DIM, 0.0).astype(BF16)


def _odd_in(x, mod, w, q_norm, k_norm, mod_index, seq_len, rope, cache):
    n, d = x.shape
    tm = TOKEN_TILE
    row = lambda width: pl.BlockSpec((tm, width), lambda i: (i, 0))
    const = lambda a: pl.BlockSpec(a.shape, lambda i: (0, 0))
    segq, segk = _segment_mean_matrix(512), _segment_mean_matrix(128)
    qn = jnp.tile(q_norm, D_HEADS).reshape(1, 512)
    kn = jnp.tile(k_norm, D_KV).reshape(1, 128)
    in_specs = [row(d), pl.BlockSpec((1, 9, d), mod_index),
                pl.BlockSpec(w.shape, lambda i: (0, 0), pipeline_mode=pl.Buffered(1)),
                const(segq), const(segk), const(qn), const(kn)]
    args = [x, mod, w, segq, segk, qn, kn]
    if rope:
        per_seq = seq_len // tm
        tabs = _rope_tables(seq_len)
        in_specs += [pl.BlockSpec((tm, 128), lambda i: (i % per_seq, 0))] * 3
        args += list(tabs)
    outs = [(512, BF16), (512, F32), (512, F32), (512, BF16), (512, F32), (512, BF16), (KV_WIDTH, BF16)] + \
           ([(256, F32)] if cache else [])
    return pl.pallas_call(
        functools.partial(_odd_in_kernel, rope=rope, cache=cache),
        grid=(n // tm,),
        in_specs=in_specs,
        out_specs=[row(wd) for wd, _ in outs],
        out_shape=[jax.ShapeDtypeStruct((n, wd), dt) for wd, dt in outs],
        compiler_params=_cparams(("parallel",)),
        name="odd_in_proj",
    )(*args)


def _attn_kernel(*refs, seq_len, tq, q_blocks, n_ctx, banded, has_sink):
    it = iter(refs)
    q_ref, kv_ref = next(it), next(it)
    ckv_ref = next(it) if n_ctx else None
    sink_ref = next(it) if has_sink else None
    o_ref = next(it)
    groups = A_HEADS // A_KV
    rows = groups * tq
    low_half = lax.broadcasted_iota(jnp.int32, (tq, 2 * HEAD_DIM), 1) < HEAD_DIM

    def run_kv_head(blk, kh):
        r0 = blk * tq
        j = pl.program_id(1) * q_blocks + blk
        kcol = slice(kh * HEAD_DIM, (kh + 1) * HEAD_DIM)
        vcol = slice(128 + kh * 256, 128 + (kh + 1) * 256)
        qs = jnp.concatenate([q_ref[r0:r0 + tq, (kh * groups + g) * HEAD_DIM:(kh * groups + g + 1) * HEAD_DIM]
                              for g in range(groups)], axis=0)
        tiles = []
        if banded:
            span = tq + 2 * WINDOW
            start = pl.multiple_of(jnp.clip(j * tq - WINDOW, 0, seq_len - span), WINDOW)
            qpos = j * tq + (lax.broadcasted_iota(jnp.int32, (rows, span), 0) & (tq - 1))
            kpos = start + lax.broadcasted_iota(jnp.int32, (rows, span), 1)
            band = jnp.abs(kpos - qpos) <= WINDOW
            tiles.append((kv_ref[pl.ds(start, span), kcol], kv_ref[pl.ds(start, span), vcol], band))
        else:
            tk = min(ATT_KEY_TILE, seq_len)
            for t in range(seq_len // tk):
                tiles.append((kv_ref[t * tk:(t + 1) * tk, kcol], kv_ref[t * tk:(t + 1) * tk, vcol], None))
        if n_ctx:
            tiles.append((ckv_ref[0, :, kcol], ckv_ref[0, :, vcol], None))
        scores = [_dot_nt(qs, k_t) for k_t, _, _ in tiles]
        yield
        scores = [s if mask is None else jnp.where(mask, s, NEG_INF) for s, (_, _, mask) in zip(scores, tiles)]
        m = functools.reduce(jnp.maximum, [jnp.max(s, -1, keepdims=True) for s in scores])
        if has_sink:
            sink = jnp.concatenate([jnp.full((tq, 1), sink_ref[kh * groups + g] * LOG2_E, F32)
                                    for g in range(groups)], 0)
            m = jnp.maximum(m, sink)
        pv = functools.reduce(lambda a, b: a + b, [_dot(jnp.exp2(s - m).astype(BF16), v_t)
                                                   for s, (_, v_t, _) in zip(scores, tiles)])
        yield
        total = pv[:, 2 * HEAD_DIM:]
        if has_sink:
            total = total + jnp.exp2(sink - m)
        o = pv[:, :2 * HEAD_DIM] / total
        for g in range(0, groups, 2):
            pair = jnp.where(low_half, o[g * tq:(g + 1) * tq], o[(g + 1) * tq:(g + 2) * tq])
            c0 = (kh * groups + g) * HEAD_DIM
            o_ref[r0:r0 + tq, c0:c0 + 2 * HEAD_DIM] = pair.astype(BF16)

    run = _run_staggered if seq_len + n_ctx > 2 * ATT_KEY_TILE else _run_interleaved
    run([run_kv_head(blk, kh) for blk in range(q_blocks) for kh in range(A_KV)])


def _attention(q, kv, ctx_kv, sink, batch, seq_len, tq, q_blocks, banded):
    n_ctx = 0 if ctx_kv is None else ctx_kv.shape[1]
    per_seq = seq_len // (tq * q_blocks)
    in_specs = [pl.BlockSpec((q_blocks * tq, 512), lambda b, j: (b * per_seq + j, 0)),
                pl.BlockSpec((seq_len, KV_WIDTH), lambda b, j: (b, 0))]
    args = [q, kv]
    if n_ctx:
        in_specs.append(pl.BlockSpec((1, n_ctx, KV_WIDTH), lambda b, j: (b, 0, 0)))
        args.append(ctx_kv)
    if sink is not None:
        in_specs.append(pl.BlockSpec(memory_space=pltpu.SMEM))
        args.append(sink.reshape(-1).astype(F32))
    return pl.pallas_call(
        functools.partial(_attn_kernel, seq_len=seq_len, tq=tq, q_blocks=q_blocks, n_ctx=n_ctx, banded=banded,
                          has_sink=sink is not None),
        grid=(batch, per_seq),
        in_specs=in_specs,
        out_specs=pl.BlockSpec((q_blocks * tq, 512), lambda b, j: (b * per_seq + j, 0)),
        out_shape=jax.ShapeDtypeStruct(q.shape, BF16),
        compiler_params=_cparams(("parallel", "arbitrary")),
        name="gqa_attention",
    )(*args)


def _log_sigmoid(x):
    return jnp.minimum(x, 0.0) - jnp.log1p(jnp.exp(-jnp.abs(x)))


def _mlstm_kernel(*refs, seq_len, has_init, emit_state):
    it = iter(refs)
    q_ref, k_ref, v_ref, g_ref, bo_ref, gb_ref, ng_ref = (next(it) for _ in range(7))
    c0_ref, n0_ref, m0_ref = (next(it), next(it), next(it)) if has_init else (None, None, None)
    y_ref = next(it)
    co_ref, no_ref, mo_ref = (next(it), next(it), next(it)) if emit_state else (None, None, None)
    c_scr, n_scr, m_scr, hf_scr, hb_scr = (next(it) for _ in range(5))

    t_len = MLSTM_T
    n_chunks = seq_len // t_len
    n_streams = 2 * B_HEADS

    for s in range(n_streams):
        if has_init:
            c_scr[s] = c0_ref[0, s // B_HEADS, s % B_HEADS]
        else:
            c_scr[s] = jnp.zeros((B_DIM, B_DIM), F32)
    if has_init:
        n_scr[...] = n0_ref[0]
        m_scr[...] = m0_ref[0]
    else:
        n_scr[...] = jnp.zeros_like(n_scr)
        m_scr[...] = jnp.zeros_like(m_scr)

    ri = lax.broadcasted_iota(jnp.int32, (t_len, t_len), 0)
    ci = lax.broadcasted_iota(jnp.int32, (t_len, t_len), 1)
    lower = ci <= ri
    upper = ci >= ri
    tri_f = jnp.where(lower, 1.0, 0.0).astype(BF16)
    tri_b = jnp.where(upper, 1.0, 0.0).astype(BF16)
    gate_bias = gb_ref[...]
    sel_row = lax.broadcasted_iota(jnp.int32, (128, B_HEADS * B_DIM), 0)
    sel_head = lax.broadcasted_iota(jnp.int32, (128, B_HEADS * B_DIM), 1) // B_DIM
    gate_select = []
    for direction in range(2):
        i_base = 2 * direction * B_HEADS
        gate_select.append((jnp.where(sel_row == i_base + sel_head, 1.0, 0.0).astype(BF16),
                            jnp.where(sel_row == i_base + B_HEADS + sel_head, 1.0, 0.0).astype(BF16)))

    def run_direction(c, direction):
        chunk = c if direction == 0 else n_chunks - 1 - c
        rows = pl.ds(pl.multiple_of(chunk * t_len, t_len), t_len)
        tri = tri_f if direction == 0 else tri_b
        causal = lower if direction == 0 else upper
        last = t_len - 1 if direction == 0 else 0
        h_out = hf_scr if direction == 0 else hb_scr
        sel_i, sel_f = gate_select[direction]
        gates = g_ref[rows, :] + gate_bias
        hi, mid, lo = _split3(_log_sigmoid(gates))
        b3 = _dot(tri, jnp.concatenate([hi, mid, lo], 1))
        ig_rep = _dot_sel_rhs(gates, sel_i)
        yield
        bc_all = b3[:, :128] + b3[:, 128:256] + b3[:, 256:]
        bc_rep = _dot_sel_rhs(bc_all, sel_f)
        u_t = (gates - pltpu.roll(bc_all, 128 - B_HEADS, 1)).T
        heads = []
        for hd in range(B_HEADS):
            s = direction * B_HEADS + hd
            col = slice(hd * B_DIM, (hd + 1) * B_DIM)
            qc, kc, vc = q_ref[rows, col], k_ref[rows, col], v_ref[rows, col]
            c_prev = c_scr[s]
            n_prev = n_scr[s:s + 1, :]
            qk = _dot_nt(qc, kc)
            q_state = _dot(qc, c_prev.astype(BF16))
            qn = _dot_nt(qc, jnp.broadcast_to(n_prev, (B_DIM, B_DIM)).astype(BF16))
            heads.append((s, col, qc, kc, vc, c_prev, n_prev, qk, q_state, qn))
        yield
        staged = []
        for (s, col, qc, kc, vc, c_prev, n_prev, qk, q_state, qn) in heads:
            i_col = 2 * direction * B_HEADS + (s % B_HEADS)
            ig, bc = ig_rep[:, col], bc_rep[:, col]
            dmat = jnp.where(causal, bc[:, :t_len] + u_t[i_col:i_col + 1, :], -jnp.inf)
            m_prev = m_scr[s:s + 1, :]
            inter = bc + m_prev
            m_t = jnp.maximum(inter, jnp.max(dmat, -1, keepdims=True))
            w = (jnp.exp(dmat - m_t[:, :t_len]) * qk).astype(BF16)
            wv = _dot(w, jnp.concatenate([vc, jnp.ones_like(vc)], 1))
            b_tot = bc[last:last + 1, :]
            g = b_tot - bc + ig
            m_new = jnp.maximum(b_tot + m_prev, jnp.max(g, 0, keepdims=True))
            kw = kc.astype(F32) * jnp.exp(g - m_new)
            upd = _dot_tn(kw.astype(BF16), vc)
            decay = jnp.exp(b_tot + m_prev - m_new)
            staged.append((s, col, inter, m_t, c_prev, n_prev, q_state, qn, wv, m_new, kw, upd, decay))
        yield
        for (s, col, inter, m_t, c_prev, n_prev, q_state, qn, wv, m_new, kw, upd, decay) in staged:
            a = jnp.exp(inter - m_t)
            num = a * q_state + wv[:, :B_DIM]
            den = a * qn + wv[:, B_DIM:]
            h_out[rows, col] = num / jnp.maximum(jnp.abs(den), jnp.exp(-m_t))
            c_scr[s] = decay * c_prev + upd
            n_scr[s:s + 1, :] = decay * n_prev + jnp.sum(kw, 0, keepdims=True)
            m_scr[s:s + 1, :] = m_new

    def step(c, carry):
        _run_interleaved([run_direction(c, 0), run_direction(c, 1)])
        return carry

    lax.fori_loop(0, n_chunks, step, 0)

    blk = min(256, seq_len)
    ng = ng_ref[...]

    def finish(i, carry):
        rows = pl.ds(pl.multiple_of(i * blk, blk), blk)
        hsum = hf_scr[rows, :] + hb_scr[rows, :]
        gate = jax.nn.sigmoid(bo_ref[rows, :])
        parts = [_rms_lastdim(hsum[:, hd * B_DIM:(hd + 1) * B_DIM], ng[:, hd * B_DIM:(hd + 1) * B_DIM])
                 for hd in range(B_HEADS)]
        y_ref[rows, :] = (gate * jnp.concatenate(parts, 1)).astype(BF16)
        return carry

    lax.fori_loop(0, seq_len // blk, finish, 0)

    if emit_state:
        for s in range(n_streams):
            co_ref[0, s // B_HEADS, s % B_HEADS] = c_scr[s]
        no_ref[0] = n_scr[...]
        mo_ref[0] = m_scr[...]


def _mlstm(q, k, v, gates, bo, gate_bias, norm_g, init, batch, seq_len, emit_state):
    n = q.shape[0]
    row = lambda width: pl.BlockSpec((seq_len, width), lambda b: (b, 0))
    const = lambda a: pl.BlockSpec(a.shape, lambda b: (0,) * a.ndim)
    gb = jnp.zeros((1, 128), F32).at[0, :4 * B_HEADS].set(gate_bias.reshape(-1))
    ng = norm_g.reshape(1, B_HEADS * B_DIM)
    in_specs = [row(512), row(512), row(512), row(128), row(512), const(gb), const(ng)]
    args = [q, k, v, gates, bo, gb, ng]
    n_streams = 2 * B_HEADS
    if init is not None:
        c0, n0, m0 = init
        in_specs += [pl.BlockSpec((1, 2, B_HEADS, B_DIM, B_DIM), lambda b: (b, 0, 0, 0, 0)),
                     pl.BlockSpec((1, n_streams, B_DIM), lambda b: (b, 0, 0)),
                     pl.BlockSpec((1, n_streams, B_DIM), lambda b: (b, 0, 0))]
        args += [c0, n0.reshape(batch, n_streams, B_DIM),
                 jnp.broadcast_to(m0.reshape(batch, n_streams, 1), (batch, n_streams, B_DIM))]
    out_specs = [row(512)]
    out_shape = [jax.ShapeDtypeStruct((n, 512), BF16)]
    if emit_state:
        out_specs += [pl.BlockSpec((1, 2, B_HEADS, B_DIM, B_DIM), lambda b: (b, 0, 0, 0, 0)),
                      pl.BlockSpec((1, n_streams, B_DIM), lambda b: (b, 0, 0)),
                      pl.BlockSpec((1, n_streams, B_DIM), lambda b: (b, 0, 0))]
        out_shape += [jax.ShapeDtypeStruct((batch, 2, B_HEADS, B_DIM, B_DIM), F32),
                      jax.ShapeDtypeStruct((batch, n_streams, B_DIM), F32),
                      jax.ShapeDtypeStruct((batch, n_streams, B_DIM), F32)]
    return pl.pallas_call(
        functools.partial(_mlstm_kernel, seq_len=seq_len, has_init=init is not None, emit_state=emit_state),
        grid=(batch,),
        in_specs=in_specs,
        out_specs=out_specs,
        out_shape=out_shape,
        scratch_shapes=[pltpu.VMEM((n_streams, B_DIM, B_DIM), F32),
                        pltpu.VMEM((n_streams, B_DIM), F32),
                        pltpu.VMEM((n_streams, B_DIM), F32),
                        pltpu.VMEM((seq_len, 512), F32),
                        pltpu.VMEM((seq_len, 512), F32)],
        compiler_params=_cparams(("parallel",)),
        name="mlstm_scan",
    )(*args)


def _hgrn_kernel(*refs, seq_len, layer, has_init, emit_state):
    it = iter(refs)
    q_ref, ff_ref, fb_ref, v_ref, cg_ref, lbl_ref, ng_ref = (next(it) for _ in range(7))
    s0_ref = next(it) if has_init else None
    y_ref = next(it)
    so_ref = next(it) if emit_state else None
    st_scr, of_scr, ob_scr = next(it), next(it), next(it)

    t_len = HGRN_T
    n_sub = t_len // SUB
    n_chunks = seq_len // t_len
    n_streams = 2 * C_HEADS

    logits = lbl_ref[...]
    e = jnp.exp(logits - jnp.max(logits, 0, keepdims=True))
    sm = e / jnp.sum(e, 0, keepdims=True)
    lb = jnp.sum(sm[0:layer + 1], 0, keepdims=True) - sm[0:1]

    for s in range(n_streams):
        if has_init:
            st_scr[s] = s0_ref[0, s // C_HEADS, s % C_HEADS].T
        else:
            st_scr[s] = jnp.zeros((C_DV, C_DK), F32)

    ri = lax.broadcasted_iota(jnp.int32, (t_len, t_len), 0)
    ci = lax.broadcasted_iota(jnp.int32, (t_len, t_len), 1)
    tri_f = jnp.where(ci <= ri, 1.0, 0.0).astype(BF16)
    tri_b = jnp.where(ci >= ri, 1.0, 0.0).astype(BF16)
    sub_row = lax.broadcasted_iota(jnp.int32, (SUB, C_DK), 0)
    ones_dk = jnp.ones((C_DK, C_DK), BF16)

    def run_stream(c, direction, hd):
        chunk = c if direction == 0 else n_chunks - 1 - c
        rows = pl.ds(pl.multiple_of(chunk * t_len, t_len), t_len)
        f_ref = ff_ref if direction == 0 else fb_ref
        tri = tri_f if direction == 0 else tri_b
        last = t_len - 1 if direction == 0 else 0
        o_out = of_scr if direction == 0 else ob_scr
        s = direction * C_HEADS + hd
        col = slice(hd * C_DK, (hd + 1) * C_DK)
        lbh = lb[:, col]
        f = lbh + (1.0 - lbh) * jax.nn.sigmoid(f_ref[rows, col])
        kk = 1.0 - f
        hi, mid, lo = _split3(jnp.log(f))
        a3 = _dot(tri, jnp.concatenate([hi, mid, lo], 1))
        yield
        a_cum = a3[:, :C_DK] + a3[:, C_DK:2 * C_DK] + a3[:, 2 * C_DK:]
        a_tot = a_cum[last:last + 1, :]
        qf = q_ref[rows, col].astype(F32)
        vc = v_ref[rows, col]
        vf = vc.astype(F32)
        st = st_scr[s]
        inter = _dot_nt((qf * jnp.exp(a_cum)).astype(BF16), st.astype(BF16))
        upd = _dot_tn(vc, (kk * jnp.exp(a_tot - a_cum)).astype(BF16))
        ps = []
        for blk in range(n_sub):
            b0 = blk * SUB
            a_i, q_i, k_i = (t[b0:b0 + SUB] for t in (a_cum, qf, kk))
            for j in range(SUB):
                seen = (sub_row >= j) if direction == 0 else (sub_row <= j)
                dec = jnp.where(seen, jnp.exp(a_i - a_i[j:j + 1]), 0.0)
                ps.append(dec * q_i * k_i[j:j + 1])
        att = _dot(jnp.concatenate(ps, 0).astype(BF16), ones_dk)
        off = []
        for blk in range(n_sub):
            b0 = blk * SUB
            if direction == 0 and blk > 0:
                a_ref, kr = a_cum[b0 - 1:b0], slice(0, b0)
            elif direction == 1 and blk < n_sub - 1:
                a_ref, kr = a_cum[b0 + SUB:b0 + SUB + 1], slice(b0 + SUB, t_len)
            else:
                off.append(None)
                continue
            qt = (qf[b0:b0 + SUB] * jnp.exp(a_cum[b0:b0 + SUB] - a_ref)).astype(BF16)
            kt = (kk[kr] * jnp.exp(a_ref - a_cum[kr])).astype(BF16)
            off.append((_dot_nt(qt, kt), kr))
        yield
        st_scr[s] = jnp.exp(a_tot) * st + upd
        outs = []
        for blk in range(n_sub):
            b0 = blk * SUB
            o_i = inter[b0:b0 + SUB]
            for j in range(SUB):
                r = (blk * SUB + j) * SUB
                o_i = o_i + att[r:r + SUB] * vf[b0 + j:b0 + j + 1]
            if off[blk] is not None:
                att_off, kr = off[blk]
                outs.append((o_i, _dot(att_off.astype(BF16), vc[kr])))
            else:
                outs.append((o_i, None))
        yield
        o_out[rows, col] = jnp.concatenate([o_i if o_off is None else o_i + o_off for o_i, o_off in outs], 0)

    def step(c, carry):
        _run_interleaved([run_stream(c, direction, hd) for direction in range(2) for hd in range(C_HEADS)])
        return carry

    lax.fori_loop(0, n_chunks, step, 0)

    blk_rows = min(256, seq_len)
    ng = ng_ref[...]

    def finish(i, carry):
        rows = pl.ds(pl.multiple_of(i * blk_rows, blk_rows), blk_rows)
        osum = of_scr[rows, :] + ob_scr[rows, :]
        parts = [_rms_lastdim(osum[:, hd * C_DV:(hd + 1) * C_DV], ng[:, hd * C_DV:(hd + 1) * C_DV])
                 for hd in range(C_HEADS)]
        y_ref[rows, :] = (jnp.concatenate(parts, 1) * _silu(cg_ref[rows, :])).astype(BF16)
        return carry

    lax.fori_loop(0, seq_len // blk_rows, finish, 0)

    if emit_state:
        for s in range(n_streams):
            so_ref[0, s // C_HEADS, s % C_HEADS] = st_scr[s].T


def _hgrn(q, ff, fb, v, cg, lb_logits, norm_g, layer, init, batch, seq_len, emit_state):
    n = q.shape[0]
    row = lambda width: pl.BlockSpec((seq_len, width), lambda b: (b, 0))
    const = lambda a: pl.BlockSpec(a.shape, lambda b: (0,) * a.ndim)
    ng = norm_g.reshape(1, C_HEADS * C_DV)
    state_spec = pl.BlockSpec((1, 2, C_HEADS, C_DK, C_DV), lambda b: (b, 0, 0, 0, 0))
    in_specs = [row(512)] * 5 + [const(lb_logits), const(ng)]
    args = [q, ff, fb, v, cg, lb_logits, ng]
    if init is not None:
        in_specs.append(state_spec)
        args.append(init)
    out_specs = [row(512)]
    out_shape = [jax.ShapeDtypeStruct((n, 512), BF16)]
    if emit_state:
        out_specs.append(state_spec)
        out_shape.append(jax.ShapeDtypeStruct((batch, 2, C_HEADS, C_DK, C_DV), F32))
    return pl.pallas_call(
        functools.partial(_hgrn_kernel, seq_len=seq_len, layer=layer, has_init=init is not None,
                          emit_state=emit_state),
        grid=(batch,),
        in_specs=in_specs,
        out_specs=out_specs,
        out_shape=out_shape,
        scratch_shapes=[pltpu.VMEM((2 * C_HEADS, C_DV, C_DK), F32),
                        pltpu.VMEM((seq_len, 512), F32),
                        pltpu.VMEM((seq_len, 512), F32)],
        compiler_params=_cparams(("parallel",)),
        name="hgrn2_scan",
    )(*args)


def _mix_out_kernel(x_ref, mod_ref, ya_ref, yb_ref, w_ref, g_ref, b_ref, o_ref):
    gate = mod_ref[0][5:6]
    half = ya_ref.shape[1]
    y = _dot(ya_ref[...], w_ref[:half, :]) + _dot(yb_ref[...], w_ref[half:, :])
    o_ref[...] = _layernorm(ALPHA * x_ref[...] + gate * y, g_ref[...], b_ref[...])


def _mix_out(x, mod, ya, yb, w, g, b, mod_index):
    n, d = x.shape
    tm = TOKEN_TILE
    row = lambda width: pl.BlockSpec((tm, width), lambda i: (i, 0))
    return pl.pallas_call(
        _mix_out_kernel,
        grid=(n // tm,),
        in_specs=[row(d), pl.BlockSpec((1, 9, d), mod_index), row(ya.shape[1]), row(yb.shape[1]),
                  pl.BlockSpec(w.shape, lambda i: (0, 0), pipeline_mode=pl.Buffered(1)),
                  pl.BlockSpec((1, d), lambda i: (0, 0)), pl.BlockSpec((1, d), lambda i: (0, 0))],
        out_specs=row(d),
        out_shape=jax.ShapeDtypeStruct((n, d), F32),
        compiler_params=_cparams(("parallel",)),
        name="mixer_out_proj",
    )(x, mod, ya, yb, w, g.reshape(1, d), b.reshape(1, d))


def _prep_ffn(w1, w3, w2):
    return w1.astype(BF16), w3.astype(BF16), w2.astype(BF16)


def _prep_w_in_even(w):
    d = w.shape[0]
    a, bq, bk, bv = w[:, :768], w[:, 768:1280], w[:, 1280:1792], w[:, 1792:2304]
    bg, bo = w[:, 2304:2320], w[:, 2320:2832]
    pad = jnp.zeros((d, 128 - bg.shape[1]), w.dtype)
    return jnp.concatenate([a, bq, bk, bv, bo, bg, pad], 1).astype(BF16)


def _prep_ctx_kv(k_ctx, v_ctx):
    nb, past = k_ctx.shape[:2]
    ones = jnp.ones((nb, past, 128), k_ctx.dtype)
    v0, v1 = v_ctx[:, :, 0], v_ctx[:, :, 1]
    return jnp.concatenate([k_ctx.reshape(nb, past, -1), v0, v0, ones, v1, v1, ones], -1).astype(BF16)


def kernel(x_prompt, x_sample, c, cache_a_k, cache_a_v, state_b_C, state_b_n, state_b_m, state_c_S, cache_d_k, cache_d_v, c_ctx, ada_w, ada_b, ln_g, ln_b, ffn_w1, ffn_w3, ffn_w2, w_in_even, w_out_even, a_sink, b_gate_bias, b_norm_g, w_in_odd, w_out_odd, c_lb_logits, c_norm_g, d_q_norm, d_k_norm):
    batch_p, len_p, d = x_prompt.shape
    batch_s, len_s, _ = x_sample.shape

    cvec = jnp.concatenate([c_ctx[None], c, jnp.zeros((MOD_ROWS - 1 - batch_s, d), F32)], 0)
    mod_all = _modulation(cvec, ada_w, ada_b)

    groups = [
        dict(x=x_prompt.reshape(batch_p * len_p, d), batch=batch_p, seq=len_p, prompt=True,
             mod_index=_mod_index(0, None)),
        dict(x=x_sample.reshape(batch_s * len_s, d), batch=batch_s, seq=len_s, prompt=False,
             mod_index=_mod_index(1, len_s // TOKEN_TILE)),
    ]
    new = {}

    for l in range(DEPTH):
        mod = mod_all[l].reshape(MOD_ROWS, 9, d)
        i = l // 2
        ffn_a = _prep_ffn(ffn_w1[l, 0], ffn_w3[l, 0], ffn_w2[l, 0])
        ffn_b = _prep_ffn(ffn_w1[l, 1], ffn_w3[l, 1], ffn_w2[l, 1])
        if l % 2 == 0:
            w_in = _prep_w_in_even(w_in_even[i])
            w_out = w_out_even[i].astype(BF16)
        else:
            w_in = w_in_odd[i].astype(BF16)
            w_out = w_out_odd[i].astype(BF16)

        for grp in groups:
            x, mi, nb, sl, prompt = grp["x"], grp["mod_index"], grp["batch"], grp["seq"], grp["prompt"]
            x = _ffn(x, mod, 0, ln_g[l, 0], ln_b[l, 0], *ffn_a, mi)
            if l % 2 == 0:
                outs = _even_in(x, mod, w_in, mi, sl, rope=not prompt, cache=prompt)
                if prompt:
                    aq, akv, kv_cache, bq, bk, bv, bo, bg = outs
                    new["a_k"] = kv_cache[:, :128].reshape(nb, 1, sl, A_KV, HEAD_DIM)
                    new["a_v"] = kv_cache[:, 128:].reshape(nb, 1, sl, A_KV, HEAD_DIM)
                    ya = _attention(aq, akv, None, a_sink[i], nb, sl, tq=Q_BLOCK, q_blocks=sl // Q_BLOCK, banded=False)
                    yb, c_new, n_new, m_new = _mlstm(bq, bk, bv, bg, bo, b_gate_bias[i], b_norm_g[i], None,
                                                     nb, sl, emit_state=True)
                    new["b_C"] = c_new[:, None]
                    new["b_n"] = n_new.reshape(nb, 1, 2, B_HEADS, B_DIM)
                    new["b_m"] = m_new[:, :, 0].reshape(nb, 1, 2, B_HEADS)
                else:
                    aq, akv, bq, bk, bv, bo, bg = outs
                    ya = _attention(aq, akv, _prep_ctx_kv(cache_a_k[:, i], cache_a_v[:, i]), a_sink[i], nb, sl,
                                    tq=Q_BLOCK, q_blocks=4, banded=True)
                    init = (state_b_C[:, i], state_b_n[:, i], state_b_m[:, i])
                    yb, = _mlstm(bq, bk, bv, bg, bo, b_gate_bias[i], b_norm_g[i], init, nb, sl, emit_state=False)
                x = _mix_out(x, mod, ya, yb, w_out, ln_g[l, 1], ln_b[l, 1], mi)
            else:
                outs = _odd_in(x, mod, w_in, d_q_norm[i], d_k_norm[i], mi, sl, rope=not prompt, cache=prompt)
                if prompt:
                    cq, ff, fb, cv, cg, dq, dkv, kv_cache = outs
                    new["d_k"] = kv_cache[:, :128].reshape(nb, 1, sl, D_KV, HEAD_DIM)
                    new["d_v"] = kv_cache[:, 128:].reshape(nb, 1, sl, D_KV, HEAD_DIM)
                    yc, s_new = _hgrn(cq, ff, fb, cv, cg, c_lb_logits, c_norm_g[i], l, None, nb, sl, emit_state=True)
                    new["c_S"] = s_new[:, None]
                    yd = _attention(dq, dkv, None, None, nb, sl, tq=Q_BLOCK, q_blocks=sl // Q_BLOCK, banded=False)
                else:
                    cq, ff, fb, cv, cg, dq, dkv = outs
                    yc, = _hgrn(cq, ff, fb, cv, cg, c_lb_logits, c_norm_g[i], l, state_c_S[:, i], nb, sl,
                                emit_state=False)
                    yd = _attention(dq, dkv, _prep_ctx_kv(cache_d_k[:, i], cache_d_v[:, i]), None, nb, sl,
                                    tq=Q_BLOCK, q_blocks=1, banded=False)
                x = _mix_out(x, mod, yc, yd, w_out, ln_g[l, 1], ln_b[l, 1], mi)
            x = _ffn(x, mod, 2, ln_g[l, 2], ln_b[l, 2], *ffn_b, mi)
            grp["x"] = x

    y_prompt = groups[0]["x"].reshape(batch_p, len_p, d)
    y_sample = groups[1]["x"].reshape(batch_s, len_s, d)
    return (y_prompt, y_sample, new["a_k"], new["a_v"], new["b_C"], new["b_n"], new["b_m"], new["c_S"],
            new["d_k"], new["d_v"])
```

```python
import functools

import jax
import jax.numpy as jnp
from jax import lax
from jax.experimental import pallas as pl
from jax.experimental.pallas import tpu as pltpu

F32 = jnp.float32
BF16 = jnp.bfloat16

D_MODEL = 1024
DEPTH = 2
GRID_W = 64
HEAD_DIM = 64
A_HEADS = 8
A_KV = 2
WINDOW = 128
B_HEADS = 4
B_DIM = 128
C_HEADS = 4
C_DK = 128
C_DV = 128
D_HEADS = 8
D_KV = 2
D_FF = 2816
Q_BLOCK = 128
ROPE_THETA = 10000.0
ALPHA = (2 * DEPTH) ** 0.25
NEG_INF = -1e30
LOG2_E = 1.4426950408889634
QK_SCALE = HEAD_DIM ** -0.5 * LOG2_E

MOD_ROWS = 8
FF_CHUNK = 256
TOKEN_TILE = 512
MLSTM_T = 64
HGRN_T = 64
SUB = 8
ATT_KEY_TILE = 512
VMEM_LIMIT = 56 * 1024 * 1024


def _cparams(sem):
    return pltpu.CompilerParams(dimension_semantics=sem, vmem_limit_bytes=VMEM_LIMIT)


def _dot(a, b):
    return jnp.dot(a, b, preferred_element_type=F32)


def _dot_nt(a, b):
    return lax.dot_general(a, b, (((1,), (1,)), ((), ())), preferred_element_type=F32)


def _dot_tn(a, b):
    return lax.dot_general(a, b, (((0,), (0,)), ((), ())), preferred_element_type=F32)


def _split3(x):
    hi = x.astype(BF16)
    r1 = x - hi.astype(F32)
    mid = r1.astype(BF16)
    lo = (r1 - mid.astype(F32)).astype(BF16)
    return hi, mid, lo


def _dot_sel(sel, x):
    hi, mid, lo = _split3(x)
    return _dot(sel, hi) + _dot(sel, mid) + _dot(sel, lo)


def _dot_sel_rhs(x, sel):
    hi, mid, lo = _split3(x)
    return _dot(hi, sel) + _dot(mid, sel) + _dot(lo, sel)


def _run_interleaved(gens):
    live = list(gens)
    while live:
        nxt = []
        for g in live:
            try:
                next(g)
                nxt.append(g)
            except StopIteration:
                pass
        live = nxt


def _run_staggered(gens):
    pending, live = list(gens), []
    while pending or live:
        if pending:
            live.insert(0, pending.pop(0))
        nxt = []
        for g in live:
            try:
                next(g)
                nxt.append(g)
            except StopIteration:
                pass
        live = nxt


def _silu(x):
    return x * jax.nn.sigmoid(x)


def _layernorm(z, g, b):
    mu = jnp.mean(z, -1, keepdims=True)
    zc = z - mu
    var = jnp.mean(zc * zc, -1, keepdims=True)
    return zc * lax.rsqrt(var + 1e-5) * g + b


def _rms_lastdim(x, g):
    return x * lax.rsqrt(jnp.mean(x * x, -1, keepdims=True) + 1e-6) * g


def _mod_index(group_start, tiles_per_request):
    if tiles_per_request is None:
        return lambda i: (group_start, 0, 0)
    return lambda i: (group_start + i // tiles_per_request, 0, 0)


def _mod_kernel(c_ref, w_ref, b_ref, o_ref):
    s = _silu(c_ref[...]).astype(BF16)
    o_ref[0] = _dot(s, w_ref[0].astype(BF16)) + b_ref[0]


def _modulation(cvec, ada_w, ada_b):
    depth, d, n = ada_w.shape
    tn = 1152
    return pl.pallas_call(
        _mod_kernel,
        grid=(depth, n // tn),
        in_specs=[pl.BlockSpec((MOD_ROWS, d), lambda l, j: (0, 0)),
                  pl.BlockSpec((1, d, tn), lambda l, j: (l, 0, j)),
                  pl.BlockSpec((1, 1, tn), lambda l, j: (l, 0, j))],
        out_specs=pl.BlockSpec((1, MOD_ROWS, tn), lambda l, j: (l, 0, j)),
        out_shape=jax.ShapeDtypeStruct((depth, MOD_ROWS, n), F32),
        compiler_params=_cparams(("parallel", "parallel")),
        name="modulation",
    )(cvec, ada_w, ada_b.reshape(depth, 1, n))


def _ffn_kernel(x_ref, mod_ref, w1_ref, w3_ref, w2_ref, g_ref, b_ref, o_ref, h_ref, acc_ref, *, j, nf):
    m = mod_ref[0]
    shift, scale, gate = m[3 * j:3 * j + 1], m[3 * j + 1:3 * j + 2], m[3 * j + 2:3 * j + 3]
    h_ref[...] = (x_ref[...] * (1.0 + scale) + shift).astype(BF16)
    for f in range(nf):
        cols = slice(f * FF_CHUNK, (f + 1) * FF_CHUNK)
        h = h_ref[...]
        u = (_silu(_dot(h, w1_ref[:, cols])) * _dot(h, w3_ref[:, cols])).astype(BF16)
        y = _dot(u, w2_ref[cols, :])
        if f == 0:
            acc_ref[...] = y
        else:
            acc_ref[...] += y
    z = ALPHA * x_ref[...] + 0.5 * gate * acc_ref[...]
    o_ref[...] = _layernorm(z, g_ref[...], b_ref[...])


def _ffn(x, mod, j, g, b, w1, w3, w2, mod_index):
    n, d = x.shape
    nf = w1.shape[1] // FF_CHUNK
    tm = TOKEN_TILE
    whole = lambda a: pl.BlockSpec(a.shape, lambda i: (0,) * a.ndim, pipeline_mode=pl.Buffered(1))
    return pl.pallas_call(
        functools.partial(_ffn_kernel, j=j, nf=nf),
        grid=(n // tm,),
        in_specs=[pl.BlockSpec((tm, d), lambda i: (i, 0)),
                  pl.BlockSpec((1, 9, d), mod_index),
                  whole(w1), whole(w3), whole(w2),
                  pl.BlockSpec((1, d), lambda i: (0, 0)),
                  pl.BlockSpec((1, d), lambda i: (0, 0))],
        out_specs=pl.BlockSpec((tm, d), lambda i: (i, 0)),
        out_shape=jax.ShapeDtypeStruct((n, d), F32),
        scratch_shapes=[pltpu.VMEM((tm, d), BF16), pltpu.VMEM((tm, d), F32)],
        compiler_params=_cparams(("parallel",)),
        name="ffn_sublayer",
    )(x, mod, w1, w3, w2, g.reshape(1, d), b.reshape(1, d))


def _rope_tables(length):
    t = jnp.arange(length)
    nf = HEAD_DIM // 4
    inv = ROPE_THETA ** (-jnp.arange(nf, dtype=F32) / nf)
    ang_r = (t // GRID_W).astype(F32)[:, None] * inv[None]
    ang_c = (t % GRID_W).astype(F32)[:, None] * inv[None]
    cr, sr, cc, sc = jnp.cos(ang_r), jnp.sin(ang_r), jnp.cos(ang_c), jnp.sin(ang_c)
    z = jnp.zeros_like(cr)
    cos = jnp.concatenate([cr, cr, cc, cc], 1)
    sin_up = jnp.concatenate([-sr, z, -sc, z], 1)
    sin_dn = jnp.concatenate([z, sr, z, sc], 1)
    two = lambda a: jnp.concatenate([a, a], 1)
    return two(cos), two(sin_up), two(sin_dn)


def _rope128(x, cos, sin_up, sin_dn):
    nf = HEAD_DIM // 4
    return x * cos + pltpu.roll(x, 128 - nf, 1) * sin_up + pltpu.roll(x, nf, 1) * sin_dn


def _rope(x, cos, sin_up, sin_dn):
    parts = [_rope128(x[:, c:c + 128], cos, sin_up, sin_dn) for c in range(0, x.shape[1], 128)]
    return parts[0] if len(parts) == 1 else jnp.concatenate(parts, 1)


EVEN_COLS = (512, 256, 512, 512, 512, 512, 128)


KV_WIDTH = 128 + A_KV * 256


def _kv_with_ones(kv):
    ones = jnp.ones((kv.shape[0], 128), kv.dtype)
    v0, v1 = kv[:, 128:192], kv[:, 192:256]
    return jnp.concatenate([kv[:, :128], v0, v0, ones, v1, v1, ones], 1)


def _even_in_kernel(*refs, rope, cache):
    it = iter(refs)
    x_ref, mod_ref, w_ref = next(it), next(it), next(it)
    tabs = (next(it), next(it), next(it)) if rope else None
    aq_ref, akv_ref = next(it), next(it)
    cache_ref = next(it) if cache else None
    bq_ref, bk_ref, bv_ref, bo_ref, bg_ref = next(it), next(it), next(it), next(it), next(it)

    m = mod_ref[0]
    h = (x_ref[...] * (1.0 + m[4:5]) + m[3:4]).astype(BF16)
    offs = [0]
    for c in EVEN_COLS:
        offs.append(offs[-1] + c)
    proj = lambda k: _dot(h, w_ref[:, offs[k]:offs[k + 1]])

    aq = proj(0)
    akv = proj(1)
    if cache:
        cache_ref[...] = akv
    if rope:
        cos, s_up, s_dn = (t[...] for t in tabs)
        aq = _rope(aq, cos, s_up, s_dn)
        akv = jnp.concatenate([_rope(akv[:, :128], cos, s_up, s_dn), akv[:, 128:]], 1)
    aq_ref[...] = (aq * QK_SCALE).astype(BF16)
    akv_ref[...] = _kv_with_ones(akv).astype(BF16)
    bq_ref[...] = proj(2).astype(BF16)
    bk_ref[...] = (proj(3) * (B_DIM ** -0.5)).astype(BF16)
    bv_ref[...] = proj(4).astype(BF16)
    bo_ref[...] = proj(5)
    bg_ref[...] = proj(6)


def _even_in(x, mod, w, mod_index, seq_len, rope, cache):
    n, d = x.shape
    tm = TOKEN_TILE
    row = lambda width: pl.BlockSpec((tm, width), lambda i: (i, 0))
    in_specs = [row(d), pl.BlockSpec((1, 9, d), mod_index),
                pl.BlockSpec(w.shape, lambda i: (0, 0), pipeline_mode=pl.Buffered(1))]
    args = [x, mod, w]
    if rope:
        per_seq = seq_len // tm
        tabs = _rope_tables(seq_len)
        in_specs += [pl.BlockSpec((tm, 128), lambda i: (i % per_seq, 0))] * 3
        args += list(tabs)
    outs = [(512, BF16), (KV_WIDTH, BF16)] + ([(256, F32)] if cache else []) + \
           [(512, BF16), (512, BF16), (512, BF16), (512, F32), (128, F32)]
    return pl.pallas_call(
        functools.partial(_even_in_kernel, rope=rope, cache=cache),
        grid=(n // tm,),
        in_specs=in_specs,
        out_specs=[row(wd) for wd, _ in outs],
        out_shape=[jax.ShapeDtypeStruct((n, wd), dt) for wd, dt in outs],
        compiler_params=_cparams(("parallel",)),
        name="even_in_proj",
    )(*args)


ODD_COLS = (512, 512, 512, 512, 512, 512, 256)


def _head_rms(x, seg_ref, g):
    x2 = x * x
    hi = x2.astype(BF16)
    lo = (x2 - hi.astype(F32)).astype(BF16)
    ms = _dot(hi, seg_ref[...]) + _dot(lo, seg_ref[...])
    return x * lax.rsqrt(ms + 1e-6) * g


def _odd_in_kernel(*refs, rope, cache):
    it = iter(refs)
    x_ref, mod_ref, w_ref, segq_ref, segk_ref, qn_ref, kn_ref = (next(it) for _ in range(7))
    tabs = (next(it), next(it), next(it)) if rope else None
    q_ref, ff_ref, fb_ref, v_ref, cg_ref, dq_ref, dkv_ref = (next(it) for _ in range(7))
    cache_ref = next(it) if cache else None

    m = mod_ref[0]
    h = (x_ref[...] * (1.0 + m[4:5]) + m[3:4]).astype(BF16)
    offs = [0]
    for c in ODD_COLS:
        offs.append(offs[-1] + c)
    proj = lambda k: _dot(h, w_ref[:, offs[k]:offs[k + 1]])

    q_ref[...] = _silu(proj(0)).astype(BF16)
    ff_ref[...] = proj(1)
    fb_ref[...] = proj(2)
    v_ref[...] = proj(3).astype(BF16)
    cg_ref[...] = proj(4)
    dq = _head_rms(proj(5), segq_ref, qn_ref[...])
    dkv = proj(6)
    dk = _head_rms(dkv[:, :128], segk_ref, kn_ref[...])
    dv = dkv[:, 128:]
    if cache:
        cache_ref[...] = jnp.concatenate([dk, dv], 1)
    if rope:
        cos, s_up, s_dn = (t[...] for t in tabs)
        dq = _rope(dq, cos, s_up, s_dn)
        dk = _rope(dk, cos, s_up, s_dn)
    dq_ref[...] = (dq * QK_SCALE).astype(BF16)
    dkv_ref[...] = _kv_with_ones(jnp.concatenate([dk, dv], 1)).astype(BF16)


def _segment_mean_matrix(width):
    r = jnp.arange(width) // HEAD_DIM
    return jnp.where(r[:, None] == r[None, :], 1.0 / HEAD_DIM, 0.0).astype(BF16)


def _odd_in(x, mod, w, q_norm, k_norm, mod_index, seq_len, rope, cache):
    n, d = x.shape
    tm = TOKEN_TILE
    row = lambda width: pl.BlockSpec((tm, width), lambda i: (i, 0))
    const = lambda a: pl.BlockSpec(a.shape, lambda i: (0, 0))
    segq, segk = _segment_mean_matrix(512), _segment_mean_matrix(128)
    qn = jnp.tile(q_norm, D_HEADS).reshape(1, 512)
    kn = jnp.tile(k_norm, D_KV).reshape(1, 128)
    in_specs = [row(d), pl.BlockSpec((1, 9, d), mod_index),
                pl.BlockSpec(w.shape, lambda i: (0, 0), pipeline_mode=pl.Buffered(1)),
                const(segq), const(segk), const(qn), const(kn)]
    args = [x, mod, w, segq, segk, qn, kn]
    if rope:
        per_seq = seq_len // tm
        tabs = _rope_tables(seq_len)
        in_specs += [pl.BlockSpec((tm, 128), lambda i: (i % per_seq, 0))] * 3
        args += list(tabs)
    outs = [(512, BF16), (512, F32), (512, F32), (512, BF16), (512, F32), (512, BF16), (KV_WIDTH, BF16)] + \
           ([(256, F32)] if cache else [])
    return pl.pallas_call(
        functools.partial(_odd_in_kernel, rope=rope, cache=cache),
        grid=(n // tm,),
        in_specs=in_specs,
        out_specs=[row(wd) for wd, _ in outs],
        out_shape=[jax.ShapeDtypeStruct((n, wd), dt) for wd, dt in outs],
        compiler_params=_cparams(("parallel",)),
        name="odd_in_proj",
    )(*args)


def _attn_kernel(*refs, seq_len, tq, q_blocks, n_ctx, banded, has_sink):
    it = iter(refs)
    q_ref, kv_ref = next(it), next(it)
    ckv_ref = next(it) if n_ctx else None
    sink_ref = next(it) if has_sink else None
    o_ref = next(it)
    groups = A_HEADS // A_KV
    rows = groups * tq
    low_half = lax.broadcasted_iota(jnp.int32, (tq, 2 * HEAD_DIM), 1) < HEAD_DIM

    def run_kv_head(blk, kh):
        r0 = blk * tq
        j = pl.program_id(1) * q_blocks + blk
        kcol = slice(kh * HEAD_DIM, (kh + 1) * HEAD_DIM)
        vcol = slice(128 + kh * 256, 128 + (kh + 1) * 256)
        qs = jnp.concatenate([q_ref[r0:r0 + tq, (kh * groups + g) * HEAD_DIM:(kh * groups + g + 1) * HEAD_DIM]
                              for g in range(groups)], axis=0)
        tiles = []
        if banded:
            span = tq + 2 * WINDOW
            start = pl.multiple_of(jnp.clip(j * tq - WINDOW, 0, seq_len - span), WINDOW)
            qpos = j * tq + (lax.broadcasted_iota(jnp.int32, (rows, span), 0) & (tq - 1))
            kpos = start + lax.broadcasted_iota(jnp.int32, (rows, span), 1)
            band = jnp.abs(kpos - qpos) <= WINDOW
            tiles.append((kv_ref[pl.ds(start, span), kcol], kv_ref[pl.ds(start, span), vcol], band))
        else:
            tk = min(ATT_KEY_TILE, seq_len)
            for t in range(seq_len // tk):
                tiles.append((kv_ref[t * tk:(t + 1) * tk, kcol], kv_ref[t * tk:(t + 1) * tk, vcol], None))
        if n_ctx:
            tiles.append((ckv_ref[0, :, kcol], ckv_ref[0, :, vcol], None))
        scores = [_dot_nt(qs, k_t) for k_t, _, _ in tiles]
        yield
        scores = [s if mask is None else jnp.where(mask, s, NEG_INF) for s, (_, _, mask) in zip(scores, tiles)]
        m = functools.reduce(jnp.maximum, [jnp.max(s, -1, keepdims=True) for s in scores])
        if has_sink:
            sink = jnp.concatenate([jnp.full((tq, 1), sink_ref[kh * groups + g] * LOG2_E, F32)
                                    for g in range(groups)], 0)
            m = jnp.maximum(m, sink)
        pv = functools.reduce(lambda a, b: a + b, [_dot(jnp.exp2(s - m).astype(BF16), v_t)
                                                   for s, (_, v_t, _) in zip(scores, tiles)])
        yield
        total = pv[:, 2 * HEAD_DIM:]
        if has_sink:
            total = total + jnp.exp2(sink - m)
        o = pv[:, :2 * HEAD_DIM] / total
        for g in range(0, groups, 2):
            pair = jnp.where(low_half, o[g * tq:(g + 1) * tq], o[(g + 1) * tq:(g + 2) * tq])
            c0 = (kh * groups + g) * HEAD_DIM
            o_ref[r0:r0 + tq, c0:c0 + 2 * HEAD_DIM] = pair.astype(BF16)

    run = _run_staggered if seq_len + n_ctx > 2 * ATT_KEY_TILE else _run_interleaved
    run([run_kv_head(blk, kh) for blk in range(q_blocks) for kh in range(A_KV)])


def _attention(q, kv, ctx_kv, sink, batch, seq_len, tq, q_blocks, banded):
    n_ctx = 0 if ctx_kv is None else ctx_kv.shape[1]
    per_seq = seq_len // (tq * q_blocks)
    in_specs = [pl.BlockSpec((q_blocks * tq, 512), lambda b, j: (b * per_seq + j, 0)),
                pl.BlockSpec((seq_len, KV_WIDTH), lambda b, j: (b, 0))]
    args = [q, kv]
    if n_ctx:
        in_specs.append(pl.BlockSpec((1, n_ctx, KV_WIDTH), lambda b, j: (b, 0, 0)))
        args.append(ctx_kv)
    if sink is not None:
        in_specs.append(pl.BlockSpec(memory_space=pltpu.SMEM))
        args.append(sink.reshape(-1).astype(F32))
    return pl.pallas_call(
        functools.partial(_attn_kernel, seq_len=seq_len, tq=tq, q_blocks=q_blocks, n_ctx=n_ctx, banded=banded,
                          has_sink=sink is not None),
        grid=(batch, per_seq),
        in_specs=in_specs,
        out_specs=pl.BlockSpec((q_blocks * tq, 512), lambda b, j: (b * per_seq + j, 0)),
        out_shape=jax.ShapeDtypeStruct(q.shape, BF16),
        compiler_params=_cparams(("parallel", "arbitrary")),
        name="gqa_attention",
    )(*args)


def _log_sigmoid(x):
    return jnp.minimum(x, 0.0) - jnp.log1p(jnp.exp(-jnp.abs(x)))


def _mlstm_kernel(*refs, seq_len, has_init, emit_state):
    it = iter(refs)
    q_ref, k_ref, v_ref, g_ref, bo_ref, gb_ref, ng_ref = (next(it) for _ in range(7))
    c0_ref, n0_ref, m0_ref = (next(it), next(it), next(it)) if has_init else (None, None, None)
    y_ref = next(it)
    co_ref, no_ref, mo_ref = (next(it), next(it), next(it)) if emit_state else (None, None, None)
    c_scr, n_scr, m_scr, hf_scr, hb_scr = (next(it) for _ in range(5))

    t_len = MLSTM_T
    n_chunks = seq_len // t_len
    n_streams = 2 * B_HEADS

    for s in range(n_streams):
        if has_init:
            c_scr[s] = c0_ref[0, s // B_HEADS, s % B_HEADS]
        else:
            c_scr[s] = jnp.zeros((B_DIM, B_DIM), F32)
    if has_init:
        n_scr[...] = n0_ref[0]
        m_scr[...] = m0_ref[0]
    else:
        n_scr[...] = jnp.zeros_like(n_scr)
        m_scr[...] = jnp.zeros_like(m_scr)

    ri = lax.broadcasted_iota(jnp.int32, (t_len, t_len), 0)
    ci = lax.broadcasted_iota(jnp.int32, (t_len, t_len), 1)
    lower = ci <= ri
    upper = ci >= ri
    tri_f = jnp.where(lower, 1.0, 0.0).astype(BF16)
    tri_b = jnp.where(upper, 1.0, 0.0).astype(BF16)
    gate_bias = gb_ref[...]
    sel_row = lax.broadcasted_iota(jnp.int32, (128, B_HEADS * B_DIM), 0)
    sel_head = lax.broadcasted_iota(jnp.int32, (128, B_HEADS * B_DIM), 1) // B_DIM
    gate_select = []
    for direction in range(2):
        i_base = 2 * direction * B_HEADS
        gate_select.append((jnp.where(sel_row == i_base + sel_head, 1.0, 0.0).astype(BF16),
                            jnp.where(sel_row == i_base + B_HEADS + sel_head, 1.0, 0.0).astype(BF16)))

    def run_direction(c, direction):
        chunk = c if direction == 0 else n_chunks - 1 - c
        rows = pl.ds(pl.multiple_of(chunk * t_len, t_len), t_len)
        tri = tri_f if direction == 0 else tri_b
        causal = lower if direction == 0 else upper
        last = t_len - 1 if direction == 0 else 0
        h_out = hf_scr if direction == 0 else hb_scr
        sel_i, sel_f = gate_select[direction]
        gates = g_ref[rows, :] + gate_bias
        hi, mid, lo = _split3(_log_sigmoid(gates))
        b3 = _dot(tri, jnp.concatenate([hi, mid, lo], 1))
        ig_rep = _dot_sel_rhs(gates, sel_i)
        yield
        bc_all = b3[:, :128] + b3[:, 128:256] + b3[:, 256:]
        bc_rep = _dot_sel_rhs(bc_all, sel_f)
        u_t = (gates - pltpu.roll(bc_all, 128 - B_HEADS, 1)).T
        heads = []
        for hd in range(B_HEADS):
            s = direction * B_HEADS + hd
            col = slice(hd * B_DIM, (hd + 1) * B_DIM)
            qc, kc, vc = q_ref[rows, col], k_ref[rows, col], v_ref[rows, col]
            c_prev = c_scr[s]
            n_prev = n_scr[s:s + 1, :]
            qk = _dot_nt(qc, kc)
            q_state = _dot(qc, c_prev.astype(BF16))
            qn = _dot_nt(qc, jnp.broadcast_to(n_prev, (B_DIM, B_DIM)).astype(BF16))
            heads.append((s, col, qc, kc, vc, c_prev, n_prev, qk, q_state, qn))
        yield
        staged = []
        for (s, col, qc, kc, vc, c_prev, n_prev, qk, q_state, qn) in heads:
            i_col = 2 * direction * B_HEADS + (s % B_HEADS)
            ig, bc = ig_rep[:, col], bc_rep[:, col]
            dmat = jnp.where(causal, bc[:, :t_len] + u_t[i_col:i_col + 1, :], -jnp.inf)
            m_prev = m_scr[s:s + 1, :]
            inter = bc + m_prev
            m_t = jnp.maximum(inter, jnp.max(dmat, -1, keepdims=True))
            w = (jnp.exp(dmat - m_t[:, :t_len]) * qk).astype(BF16)
            wv = _dot(w, jnp.concatenate([vc, jnp.ones_like(vc)], 1))
            b_tot = bc[last:last + 1, :]
            g = b_tot - bc + ig
            m_new = jnp.maximum(b_tot + m_prev, jnp.max(g, 0, keepdims=True))
            kw = kc.astype(F32) * jnp.exp(g - m_new)
            upd = _dot_tn(kw.astype(BF16), vc)
            decay = jnp.exp(b_tot + m_prev - m_new)
            staged.append((s, col, inter, m_t, c_prev, n_prev, q_state, qn, wv, m_new, kw, upd, decay))
        yield
        for (s, col, inter, m_t, c_prev, n_prev, q_state, qn, wv, m_new, kw, upd, decay) in staged:
            a = jnp.exp(inter - m_t)
            num = a * q_state + wv[:, :B_DIM]
            den = a * qn + wv[:, B_DIM:]
            h_out[rows, col] = num / jnp.maximum(jnp.abs(den), jnp.exp(-m_t))
            c_scr[s] = decay * c_prev + upd
            n_scr[s:s + 1, :] = decay * n_prev + jnp.sum(kw, 0, keepdims=True)
            m_scr[s:s + 1, :] = m_new

    def step(c, carry):
        _run_interleaved([run_direction(c, 0), run_direction(c, 1)])
        return carry

    lax.fori_loop(0, n_chunks, step, 0)

    blk = min(256, seq_len)
    ng = ng_ref[...]

    def finish(i, carry):
        rows = pl.ds(pl.multiple_of(i * blk, blk), blk)
        hsum = hf_scr[rows, :] + hb_scr[rows, :]
        gate = jax.nn.sigmoid(bo_ref[rows, :])
        parts = [_rms_lastdim(hsum[:, hd * B_DIM:(hd + 1) * B_DIM], ng[:, hd * B_DIM:(hd + 1) * B_DIM])
                 for hd in range(B_HEADS)]
        y_ref[rows, :] = (gate * jnp.concatenate(parts, 1)).astype(BF16)
        return carry

    lax.fori_loop(0, seq_len // blk, finish, 0)

    if emit_state:
        for s in range(n_streams):
            co_ref[0, s // B_HEADS, s % B_HEADS] = c_scr[s]
        no_ref[0] = n_scr[...]
        mo_ref[0] = m_scr[...]


def _mlstm(q, k, v, gates, bo, gate_bias, norm_g, init, batch, seq_len, emit_state):
    n = q.shape[0]
    row = lambda width: pl.BlockSpec((seq_len, width), lambda b: (b, 0))
    const = lambda a: pl.BlockSpec(a.shape, lambda b: (0,) * a.ndim)
    gb = jnp.zeros((1, 128), F32).at[0, :4 * B_HEADS].set(gate_bias.reshape(-1))
    ng = norm_g.reshape(1, B_HEADS * B_DIM)
    in_specs = [row(512), row(512), row(512), row(128), row(512), const(gb), const(ng)]
    args = [q, k, v, gates, bo, gb, ng]
    n_streams = 2 * B_HEADS
    if init is not None:
        c0, n0, m0 = init
        in_specs += [pl.BlockSpec((1, 2, B_HEADS, B_DIM, B_DIM), lambda b: (b, 0, 0, 0, 0)),
                     pl.BlockSpec((1, n_streams, B_DIM), lambda b: (b, 0, 0)),
                     pl.BlockSpec((1, n_streams, B_DIM), lambda b: (b, 0, 0))]
        args += [c0, n0.reshape(batch, n_streams, B_DIM),
                 jnp.broadcast_to(m0.reshape(batch, n_streams, 1), (batch, n_streams, B_DIM))]
    out_specs = [row(512)]
    out_shape = [jax.ShapeDtypeStruct((n, 512), BF16)]
    if emit_state:
        out_specs += [pl.BlockSpec((1, 2, B_HEADS, B_DIM, B_DIM), lambda b: (b, 0, 0, 0, 0)),
                      pl.BlockSpec((1, n_streams, B_DIM), lambda b: (b, 0, 0)),
                      pl.BlockSpec((1, n_streams, B_DIM), lambda b: (b, 0, 0))]
        out_shape += [jax.ShapeDtypeStruct((batch, 2, B_HEADS, B_DIM, B_DIM), F32),
                      jax.ShapeDtypeStruct((batch, n_streams, B_DIM), F32),
                      jax.ShapeDtypeStruct((batch, n_streams, B_DIM), F32)]
    return pl.pallas_call(
        functools.partial(_mlstm_kernel, seq_len=seq_len, has_init=init is not None, emit_state=emit_state),
        grid=(batch,),
        in_specs=in_specs,
        out_specs=out_specs,
        out_shape=out_shape,
        scratch_shapes=[pltpu.VMEM((n_streams, B_DIM, B_DIM), F32),
                        pltpu.VMEM((n_streams, B_DIM), F32),
                        pltpu.VMEM((n_streams, B_DIM), F32),
                        pltpu.VMEM((seq_len, 512), F32),
                        pltpu.VMEM((seq_len, 512), F32)],
        compiler_params=_cparams(("parallel",)),
        name="mlstm_scan",
    )(*args)


def _hgrn_kernel(*refs, seq_len, layer, has_init, emit_state):
    it = iter(refs)
    q_ref, ff_ref, fb_ref, v_ref, cg_ref, lbl_ref, ng_ref = (next(it) for _ in range(7))
    s0_ref = next(it) if has_init else None
    y_ref = next(it)
    so_ref = next(it) if emit_state else None
    st_scr, of_scr, ob_scr = next(it), next(it), next(it)

    t_len = HGRN_T
    n_sub = t_len // SUB
    n_chunks = seq_len // t_len
    n_streams = 2 * C_HEADS

    logits = lbl_ref[...]
    e = jnp.exp(logits - jnp.max(logits, 0, keepdims=True))
    sm = e / jnp.sum(e, 0, keepdims=True)
    lb = jnp.sum(sm[0:layer + 1], 0, keepdims=True) - sm[0:1]

    for s in range(n_streams):
        if has_init:
            st_scr[s] = s0_ref[0, s // C_HEADS, s % C_HEADS].T
        else:
            st_scr[s] = jnp.zeros((C_DV, C_DK), F32)

    ri = lax.broadcasted_iota(jnp.int32, (t_len, t_len), 0)
    ci = lax.broadcasted_iota(jnp.int32, (t_len, t_len), 1)
    tri_f = jnp.where(ci <= ri, 1.0, 0.0).astype(BF16)
    tri_b = jnp.where(ci >= ri, 1.0, 0.0).astype(BF16)
    sub_row = lax.broadcasted_iota(jnp.int32, (SUB, C_DK), 0)
    ones_dk = jnp.ones((C_DK, C_DK), BF16)

    def run_stream(c, direction, hd):
        chunk = c if direction == 0 else n_chunks - 1 - c
        rows = pl.ds(pl.multiple_of(chunk * t_len, t_len), t_len)
        f_ref = ff_ref if direction == 0 else fb_ref
        tri = tri_f if direction == 0 else tri_b
        last = t_len - 1 if direction == 0 else 0
        o_out = of_scr if direction == 0 else ob_scr
        s = direction * C_HEADS + hd
        col = slice(hd * C_DK, (hd + 1) * C_DK)
        lbh = lb[:, col]
        f = lbh + (1.0 - lbh) * jax.nn.sigmoid(f_ref[rows, col])
        log_k = jnp.log2(1.0 - f)
        hi, mid, lo = _split3(jnp.log2(f))
        a3 = _dot(tri, jnp.concatenate([hi, mid, lo], 1))
        yield
        a_cum = a3[:, :C_DK] + a3[:, C_DK:2 * C_DK] + a3[:, 2 * C_DK:]
        a_key = a_cum - log_k
        a_tot = a_cum[last:last + 1, :]
        qf = q_ref[rows, col].astype(F32)
        vc = v_ref[rows, col]
        vf = vc.astype(F32)
        st = st_scr[s]
        inter = _dot_nt((qf * jnp.exp2(a_cum)).astype(BF16), st.astype(BF16))
        upd = _dot_tn(vc, jnp.exp2(a_tot - a_key).astype(BF16))
        ps = []
        for blk in range(n_sub):
            b0 = blk * SUB
            a_i, q_i, ak_i = (t[b0:b0 + SUB] for t in (a_cum, qf, a_key))
            for j in range(SUB):
                seen = (sub_row >= j) if direction == 0 else (sub_row <= j)
                ps.append(jnp.where(seen, jnp.exp2(a_i - ak_i[j:j + 1]), 0.0) * q_i)
        att = _dot(jnp.concatenate(ps, 0).astype(BF16), ones_dk)
        off = []
        for blk in range(n_sub):
            b0 = blk * SUB
            if direction == 0 and blk > 0:
                a_ref, kr = a_cum[b0 - 1:b0], slice(0, b0)
            elif direction == 1 and blk < n_sub - 1:
                a_ref, kr = a_cum[b0 + SUB:b0 + SUB + 1], slice(b0 + SUB, t_len)
            else:
                off.append(None)
                continue
            qt = (qf[b0:b0 + SUB] * jnp.exp2(a_cum[b0:b0 + SUB] - a_ref)).astype(BF16)
            kt = jnp.exp2(a_ref - a_key[kr]).astype(BF16)
            off.append((_dot_nt(qt, kt), kr))
        yield
        st_scr[s] = jnp.exp2(a_tot) * st + upd
        outs = []
        for blk in range(n_sub):
            b0 = blk * SUB
            o_i = inter[b0:b0 + SUB]
            for j in range(SUB):
                r = (blk * SUB + j) * SUB
                o_i = o_i + att[r:r + SUB] * vf[b0 + j:b0 + j + 1]
            if off[blk] is not None:
                att_off, kr = off[blk]
                outs.append((o_i, _dot(att_off.astype(BF16), vc[kr])))
            else:
                outs.append((o_i, None))
        yield
        o_out[rows, col] = jnp.concatenate([o_i if o_off is None else o_i + o_off for o_i, o_off in outs], 0)

    def step(c, carry):
        _run_interleaved([run_stream(c, direction, hd) for direction in range(2) for hd in range(C_HEADS)])
        return carry

    lax.fori_loop(0, n_chunks, step, 0)

    blk_rows = min(256, seq_len)
    ng = ng_ref[...]

    def finish(i, carry):
        rows = pl.ds(pl.multiple_of(i * blk_rows, blk_rows), blk_rows)
        osum = of_scr[rows, :] + ob_scr[rows, :]
        parts = [_rms_lastdim(osum[:, hd * C_DV:(hd + 1) * C_DV], ng[:, hd * C_DV:(hd + 1) * C_DV])
                 for hd in range(C_HEADS)]
        y_ref[rows, :] = (jnp.concatenate(parts, 1) * _silu(cg_ref[rows, :])).astype(BF16)
        return carry

    lax.fori_loop(0, seq_len // blk_rows, finish, 0)

    if emit_state:
        for s in range(n_streams):
            so_ref[0, s // C_HEADS, s % C_HEADS] = st_scr[s].T


def _hgrn(q, ff, fb, v, cg, lb_logits, norm_g, layer, init, batch, seq_len, emit_state):
    n = q.shape[0]
    row = lambda width: pl.BlockSpec((seq_len, width), lambda b: (b, 0))
    const = lambda a: pl.BlockSpec(a.shape, lambda b: (0,) * a.ndim)
    ng = norm_g.reshape(1, C_HEADS * C_DV)
    state_spec = pl.BlockSpec((1, 2, C_HEADS, C_DK, C_DV), lambda b: (b, 0, 0, 0, 0))
    in_specs = [row(512)] * 5 + [const(lb_logits), const(ng)]
    args = [q, ff, fb, v, cg, lb_logits, ng]
    if init is not None:
        in_specs.append(state_spec)
        args.append(init)
    out_specs = [row(512)]
    out_shape = [jax.ShapeDtypeStruct((n, 512), BF16)]
    if emit_state:
        out_specs.append(state_spec)
        out_shape.append(jax.ShapeDtypeStruct((batch, 2, C_HEADS, C_DK, C_DV), F32))
    return pl.pallas_call(
        functools.partial(_hgrn_kernel, seq_len=seq_len, layer=layer, has_init=init is not None,
                          emit_state=emit_state),
        grid=(batch,),
        in_specs=in_specs,
        out_specs=out_specs,
        out_shape=out_shape,
        scratch_shapes=[pltpu.VMEM((2 * C_HEADS, C_DV, C_DK), F32),
                        pltpu.VMEM((seq_len, 512), F32),
                        pltpu.VMEM((seq_len, 512), F32)],
        compiler_params=_cparams(("parallel",)),
        name="hgrn2_scan",
    )(*args)


def _mix_out_kernel(x_ref, mod_ref, ya_ref, yb_ref, w_ref, g_ref, b_ref, o_ref):
    gate = mod_ref[0][5:6]
    half = ya_ref.shape[1]
    y = _dot(ya_ref[...], w_ref[:half, :]) + _dot(yb_ref[...], w_ref[half:, :])
    o_ref[...] = _layernorm(ALPHA * x_ref[...] + gate * y, g_ref[...], b_ref[...])


def _mix_out(x, mod, ya, yb, w, g, b, mod_index):
    n, d = x.shape
    tm = TOKEN_TILE
    row = lambda width: pl.BlockSpec((tm, width), lambda i: (i, 0))
    return pl.pallas_call(
        _mix_out_kernel,
        grid=(n // tm,),
        in_specs=[row(d), pl.BlockSpec((1, 9, d), mod_index), row(ya.shape[1]), row(yb.shape[1]),
                  pl.BlockSpec(w.shape, lambda i: (0, 0), pipeline_mode=pl.Buffered(1)),
                  pl.BlockSpec((1, d), lambda i: (0, 0)), pl.BlockSpec((1, d), lambda i: (0, 0))],
        out_specs=row(d),
        out_shape=jax.ShapeDtypeStruct((n, d), F32),
        compiler_params=_cparams(("parallel",)),
        name="mixer_out_proj",
    )(x, mod, ya, yb, w, g.reshape(1, d), b.reshape(1, d))


def _prep_ffn(w1, w3, w2):
    return w1.astype(BF16), w3.astype(BF16), w2.astype(BF16)


def _prep_w_in_even(w):
    d = w.shape[0]
    a, bq, bk, bv = w[:, :768], w[:, 768:1280], w[:, 1280:1792], w[:, 1792:2304]
    bg, bo = w[:, 2304:2320], w[:, 2320:2832]
    pad = jnp.zeros((d, 128 - bg.shape[1]), w.dtype)
    return jnp.concatenate([a, bq, bk, bv, bo, bg, pad], 1).astype(BF16)


def _prep_ctx_kv(k_ctx, v_ctx):
    nb, past = k_ctx.shape[:2]
    ones = jnp.ones((nb, past, 128), k_ctx.dtype)
    v0, v1 = v_ctx[:, :, 0], v_ctx[:, :, 1]
    return jnp.concatenate([k_ctx.reshape(nb, past, -1), v0, v0, ones, v1, v1, ones], -1).astype(BF16)


def kernel(x_prompt, x_sample, c, cache_a_k, cache_a_v, state_b_C, state_b_n, state_b_m, state_c_S, cache_d_k, cache_d_v, c_ctx, ada_w, ada_b, ln_g, ln_b, ffn_w1, ffn_w3, ffn_w2, w_in_even, w_out_even, a_sink, b_gate_bias, b_norm_g, w_in_odd, w_out_odd, c_lb_logits, c_norm_g, d_q_norm, d_k_norm):
    batch_p, len_p, d = x_prompt.shape
    batch_s, len_s, _ = x_sample.shape

    cvec = jnp.concatenate([c_ctx[None], c, jnp.zeros((MOD_ROWS - 1 - batch_s, d), F32)], 0)
    mod_all = _modulation(cvec, ada_w, ada_b)

    groups = [
        dict(x=x_prompt.reshape(batch_p * len_p, d), batch=batch_p, seq=len_p, prompt=True,
             mod_index=_mod_index(0, None)),
        dict(x=x_sample.reshape(batch_s * len_s, d), batch=batch_s, seq=len_s, prompt=False,
             mod_index=_mod_index(1, len_s // TOKEN_TILE)),
    ]
    new = {}

    for l in range(DEPTH):
        mod = mod_all[l].reshape(MOD_ROWS, 9, d)
        i = l // 2
        ffn_a = _prep_ffn(ffn_w1[l, 0], ffn_w3[l, 0], ffn_w2[l, 0])
        ffn_b = _prep_ffn(ffn_w1[l, 1], ffn_w3[l, 1], ffn_w2[l, 1])
        if l % 2 == 0:
            w_in = _prep_w_in_even(w_in_even[i])
            w_out = w_out_even[i].astype(BF16)
        else:
            w_in = w_in_odd[i].astype(BF16)
            w_out = w_out_odd[i].astype(BF16)

        for grp in groups:
            x, mi, nb, sl, prompt = grp["x"], grp["mod_index"], grp["batch"], grp["seq"], grp["prompt"]
            x = _ffn(x, mod, 0, ln_g[l, 0], ln_b[l, 0], *ffn_a, mi)
            if l % 2 == 0:
                outs = _even_in(x, mod, w_in, mi, sl, rope=not prompt, cache=prompt)
                if prompt:
                    aq, akv, kv_cache, bq, bk, bv, bo, bg = outs
                    new["a_k"] = kv_cache[:, :128].reshape(nb, 1, sl, A_KV, HEAD_DIM)
                    new["a_v"] = kv_cache[:, 128:].reshape(nb, 1, sl, A_KV, HEAD_DIM)
                    ya = _attention(aq, akv, None, a_sink[i], nb, sl, tq=Q_BLOCK, q_blocks=sl // Q_BLOCK, banded=False)
                    yb, c_new, n_new, m_new = _mlstm(bq, bk, bv, bg, bo, b_gate_bias[i], b_norm_g[i], None,
                                                     nb, sl, emit_state=True)
                    new["b_C"] = c_new[:, None]
                    new["b_n"] = n_new.reshape(nb, 1, 2, B_HEADS, B_DIM)
                    new["b_m"] = m_new[:, :, 0].reshape(nb, 1, 2, B_HEADS)
                else:
                    aq, akv, bq, bk, bv, bo, bg = outs
                    ya = _attention(aq, akv, _prep_ctx_kv(cache_a_k[:, i], cache_a_v[:, i]), a_sink[i], nb, sl,
                                    tq=Q_BLOCK, q_blocks=4, banded=True)
                    init = (state_b_C[:, i], state_b_n[:, i], state_b_m[:, i])
                    yb, = _mlstm(bq, bk, bv, bg, bo, b_gate_bias[i], b_norm_g[i], init, nb, sl, emit_state=False)
                x = _mix_out(x, mod, ya, yb, w_out, ln_g[l, 1], ln_b[l, 1], mi)
            else:
                outs = _odd_in(x, mod, w_in, d_q_norm[i], d_k_norm[i], mi, sl, rope=not prompt, cache=prompt)
                if prompt:
                    cq, ff, fb, cv, cg, dq, dkv, kv_cache = outs
                    new["d_k"] = kv_cache[:, :128].reshape(nb, 1, sl, D_KV, HEAD_DIM)
                    new["d_v"] = kv_cache[:, 128:].reshape(nb, 1, sl, D_KV, HEAD_DIM)
                    yc, s_new = _hgrn(cq, ff, fb, cv, cg, c_lb_logits, c_norm_g[i], l, None, nb, sl, emit_state=True)
                    new["c_S"] = s_new[:, None]
                    yd = _attention(dq, dkv, None, None, nb, sl, tq=Q_BLOCK, q_blocks=sl // Q_BLOCK, banded=False)
                else:
                    cq, ff, fb, cv, cg, dq, dkv = outs
                    yc, = _hgrn(cq, ff, fb, cv, cg, c_lb_logits, c_norm_g[i], l, state_c_S[:, i], nb, sl,
                                emit_state=False)
                    yd = _attention(dq, dkv, _prep_ctx_kv(cache_d_k[:, i], cache_d_v[:, i]), None, nb, sl,
                                    tq=Q_BLOCK, q_blocks=1, banded=False)
                x = _mix_out(x, mod, yc, yd, w_out, ln_g[l, 1], ln_b[l, 1], mi)
            x = _ffn(x, mod, 2, ln_g[l, 2], ln_b[l, 2], *ffn_b, mi)
            grp["x"] = x

    y_prompt = groups[0]["x"].reshape(batch_p, len_p, d)
    y_sample = groups[1]["x"].reshape(batch_s, len_s, d)
    return (y_prompt, y_sample, new["a_k"], new["a_v"], new["b_C"], new["b_n"], new["b_m"], new["c_S"],
            new["d_k"], new["d_v"])
```

```python
import functools

import jax
import jax.numpy as jnp
import numpy as np
from jax import lax
from jax.experimental import pallas as pl
from jax.experimental.pallas import tpu as pltpu

F32 = jnp.float32
BF16 = jnp.bfloat16

D_MODEL = 1024
DEPTH = 2
GRID_W = 64
HEAD_DIM = 64
A_HEADS = 8
A_KV = 2
WINDOW = 128
B_HEADS = 4
B_DIM = 128
C_HEADS = 4
C_DK = 128
C_DV = 128
D_HEADS = 8
D_KV = 2
D_FF = 2816
Q_BLOCK = 128
ROPE_THETA = 10000.0
ALPHA = (2 * DEPTH) ** 0.25
NEG_INF = -1e30
LOG2_E = 1.4426950408889634
QK_SCALE = HEAD_DIM ** -0.5 * LOG2_E

MOD_ROWS = 8
FF_CHUNK = 256
TOKEN_TILE = 512
MLSTM_T = 64
HGRN_T = 64
SUB = 8
ATT_KEY_TILE = 512
VMEM_LIMIT = 56 * 1024 * 1024


def _cparams(sem):
    return pltpu.CompilerParams(dimension_semantics=sem, vmem_limit_bytes=VMEM_LIMIT)


def _dot(a, b):
    return jnp.dot(a, b, preferred_element_type=F32)


def _dot_nt(a, b):
    return lax.dot_general(a, b, (((1,), (1,)), ((), ())), preferred_element_type=F32)


def _dot_tn(a, b):
    return lax.dot_general(a, b, (((0,), (0,)), ((), ())), preferred_element_type=F32)


def _split3(x):
    hi = x.astype(BF16)
    r1 = x - hi.astype(F32)
    mid = r1.astype(BF16)
    lo = (r1 - mid.astype(F32)).astype(BF16)
    return hi, mid, lo


def _dot_sel(sel, x):
    hi, mid, lo = _split3(x)
    return _dot(sel, hi) + _dot(sel, mid) + _dot(sel, lo)


def _dot_sel_rhs(x, sel):
    hi, mid, lo = _split3(x)
    return _dot(hi, sel) + _dot(mid, sel) + _dot(lo, sel)


def _run_interleaved(gens):
    live = list(gens)
    while live:
        nxt = []
        for g in live:
            try:
                next(g)
                nxt.append(g)
            except StopIteration:
                pass
        live = nxt


def _run_staggered(gens):
    pending, live = list(gens), []
    while pending or live:
        if pending:
            live.insert(0, pending.pop(0))
        nxt = []
        for g in live:
            try:
                next(g)
                nxt.append(g)
            except StopIteration:
                pass
        live = nxt


def _silu(x):
    return x * jax.nn.sigmoid(x)


def _layernorm(z, g, b):
    mu = jnp.mean(z, -1, keepdims=True)
    zc = z - mu
    var = jnp.mean(zc * zc, -1, keepdims=True)
    return zc * lax.rsqrt(var + 1e-5) * g + b


def _rms_lastdim(x, g):
    return x * lax.rsqrt(jnp.mean(x * x, -1, keepdims=True) + 1e-6) * g


def _mod_index(group_start, tiles_per_request):
    if tiles_per_request is None:
        return lambda i: (group_start, 0, 0)
    return lambda i: (group_start + i // tiles_per_request, 0, 0)


def _mod_kernel(c_ref, w_ref, b_ref, o_ref):
    s = _silu(c_ref[...]).astype(BF16)
    o_ref[0] = _dot(s, w_ref[0].astype(BF16)) + b_ref[0]


def _modulation(cvec, ada_w, ada_b):
    depth, d, n = ada_w.shape
    tn = 1152
    return pl.pallas_call(
        _mod_kernel,
        grid=(depth, n // tn),
        in_specs=[pl.BlockSpec((MOD_ROWS, d), lambda l, j: (0, 0)),
                  pl.BlockSpec((1, d, tn), lambda l, j: (l, 0, j)),
                  pl.BlockSpec((1, 1, tn), lambda l, j: (l, 0, j))],
        out_specs=pl.BlockSpec((1, MOD_ROWS, tn), lambda l, j: (l, 0, j)),
        out_shape=jax.ShapeDtypeStruct((depth, MOD_ROWS, n), F32),
        compiler_params=_cparams(("parallel", "parallel")),
        name="modulation",
    )(cvec, ada_w, ada_b.reshape(depth, 1, n))


def _ffn_kernel(*refs, j, nf, mixer):
    it = iter(refs)
    x_ref, mod_ref = next(it), next(it)
    if mixer:
        ya_ref, yb_ref, wo_ref, g1_ref, b1_ref = (next(it) for _ in range(5))
    w1_ref, w3_ref, w2_ref, g_ref, b_ref, o_ref, h_ref, acc_ref = (next(it) for _ in range(8))
    m = mod_ref[0]
    if mixer:
        res_ref = next(it)
        half = ya_ref.shape[1]
        y = _dot(ya_ref[...], wo_ref[:half, :]) + _dot(yb_ref[...], wo_ref[half:, :])
        res_ref[...] = _layernorm(ALPHA * x_ref[...] + m[5:6] * y, g1_ref[...], b1_ref[...])
        x_ref = res_ref
    shift, scale, gate = m[3 * j:3 * j + 1], m[3 * j + 1:3 * j + 2], m[3 * j + 2:3 * j + 3]
    h_ref[...] = (x_ref[...] * (1.0 + scale) + shift).astype(BF16)
    for f in range(nf):
        cols = slice(f * FF_CHUNK, (f + 1) * FF_CHUNK)
        h = h_ref[...]
        u = (_silu(_dot(h, w1_ref[0, 0, :, cols])) * _dot(h, w3_ref[0, 0, :, cols])).astype(BF16)
        y = _dot(u, w2_ref[0, 0, cols, :])
        if f == 0:
            acc_ref[...] = y
        else:
            acc_ref[...] += y
    z = ALPHA * x_ref[...] + 0.5 * gate * acc_ref[...]
    o_ref[...] = _layernorm(z, g_ref[...], b_ref[...])


def _ffn(x, mod, j, g, b, w1, w3, w2, layer, which, mod_index, mixer=None):
    n, d = x.shape
    nf = w1.shape[-1] // FF_CHUNK
    tm = TOKEN_TILE
    row = lambda width: pl.BlockSpec((tm, width), lambda i: (i, 0))
    vec = pl.BlockSpec((1, d), lambda i: (0, 0))
    whole = lambda a: pl.BlockSpec((1, 1) + a.shape[2:], lambda i: (layer, which, 0, 0),
                                   pipeline_mode=pl.Buffered(1))
    in_specs = [row(d), pl.BlockSpec((1, 9, d), mod_index)]
    args = [x, mod]
    scratch = [pltpu.VMEM((tm, d), BF16), pltpu.VMEM((tm, d), F32)]
    if mixer is not None:
        ya, yb, w_out, g1, b1 = mixer
        in_specs += [row(ya.shape[1]), row(yb.shape[1]),
                     pl.BlockSpec(w_out.shape, lambda i: (0, 0), pipeline_mode=pl.Buffered(1)), vec, vec]
        args += [ya, yb, w_out, g1.reshape(1, d), b1.reshape(1, d)]
        scratch.append(pltpu.VMEM((tm, d), F32))
    in_specs += [whole(w1), whole(w3), whole(w2), vec, vec]
    args += [w1, w3, w2, g.reshape(1, d), b.reshape(1, d)]
    return pl.pallas_call(
        functools.partial(_ffn_kernel, j=j, nf=nf, mixer=mixer is not None),
        grid=(n // tm,),
        in_specs=in_specs,
        out_specs=row(d),
        out_shape=jax.ShapeDtypeStruct((n, d), F32),
        scratch_shapes=scratch,
        compiler_params=_cparams(("parallel",)),
        name="ffn_sublayer",
    )(*args)


def _rope_tables(length):
    t = np.arange(length)
    nf = HEAD_DIM // 4
    inv = ROPE_THETA ** (-np.arange(nf, dtype=np.float64) / nf)
    ang_r = (t // GRID_W)[:, None] * inv[None]
    ang_c = (t % GRID_W)[:, None] * inv[None]
    cr, sr, cc, sc = np.cos(ang_r), np.sin(ang_r), np.cos(ang_c), np.sin(ang_c)
    z = np.zeros_like(cr)
    cos = np.concatenate([cr, cr, cc, cc], 1)
    sin_up = np.concatenate([-sr, z, -sc, z], 1)
    sin_dn = np.concatenate([z, sr, z, sc], 1)
    two = lambda a: jnp.asarray(np.concatenate([a, a], 1), F32)
    return two(cos), two(sin_up), two(sin_dn)


def _rope128(x, cos, sin_up, sin_dn):
    nf = HEAD_DIM // 4
    return x * cos + pltpu.roll(x, 128 - nf, 1) * sin_up + pltpu.roll(x, nf, 1) * sin_dn


def _rope(x, cos, sin_up, sin_dn):
    parts = [_rope128(x[:, c:c + 128], cos, sin_up, sin_dn) for c in range(0, x.shape[1], 128)]
    return parts[0] if len(parts) == 1 else jnp.concatenate(parts, 1)


EVEN_COLS = (512, 256, 512, 512, 512, 512, 128)


KV_WIDTH = 128 + A_KV * 256


def _kv_with_ones(kv):
    ones = jnp.ones((kv.shape[0], 128), kv.dtype)
    v0, v1 = kv[:, 128:192], kv[:, 192:256]
    return jnp.concatenate([kv[:, :128], v0, v0, ones, v1, v1, ones], 1)


def _even_in_kernel(*refs, rope, cache):
    it = iter(refs)
    x_ref, mod_ref, w_ref = next(it), next(it), next(it)
    tabs = (next(it), next(it), next(it)) if rope else None
    aq_ref, akv_ref = next(it), next(it)
    cache_ref = next(it) if cache else None
    bq_ref, bk_ref, bv_ref, bo_ref, bg_ref = next(it), next(it), next(it), next(it), next(it)

    m = mod_ref[0]
    h = (x_ref[...] * (1.0 + m[4:5]) + m[3:4]).astype(BF16)
    offs = [0]
    for c in EVEN_COLS:
        offs.append(offs[-1] + c)
    proj = lambda k: _dot(h, w_ref[:, offs[k]:offs[k + 1]])

    aq = proj(0)
    akv = proj(1)
    if cache:
        cache_ref[...] = akv
    if rope:
        cos, s_up, s_dn = (t[...] for t in tabs)
        aq = _rope(aq, cos, s_up, s_dn)
        akv = jnp.concatenate([_rope(akv[:, :128], cos, s_up, s_dn), akv[:, 128:]], 1)
    aq_ref[...] = (aq * QK_SCALE).astype(BF16)
    akv_ref[...] = _kv_with_ones(akv).astype(BF16)
    bq_ref[...] = proj(2).astype(BF16)
    bk_ref[...] = (proj(3) * (B_DIM ** -0.5)).astype(BF16)
    bv_ref[...] = proj(4).astype(BF16)
    bo_ref[...] = proj(5)
    bg_ref[...] = proj(6)


def _even_in(x, mod, w, mod_index, seq_len, rope, cache):
    n, d = x.shape
    tm = TOKEN_TILE
    row = lambda width: pl.BlockSpec((tm, width), lambda i: (i, 0))
    in_specs = [row(d), pl.BlockSpec((1, 9, d), mod_index),
                pl.BlockSpec(w.shape, lambda i: (0, 0), pipeline_mode=pl.Buffered(1))]
    args = [x, mod, w]
    if rope:
        per_seq = seq_len // tm
        tabs = _rope_tables(seq_len)
        in_specs += [pl.BlockSpec((tm, 128), lambda i: (i % per_seq, 0))] * 3
        args += list(tabs)
    outs = [(512, BF16), (KV_WIDTH, BF16)] + ([(256, F32)] if cache else []) + \
           [(512, BF16), (512, BF16), (512, BF16), (512, F32), (128, F32)]
    return pl.pallas_call(
        functools.partial(_even_in_kernel, rope=rope, cache=cache),
        grid=(n // tm,),
        in_specs=in_specs,
        out_specs=[row(wd) for wd, _ in outs],
        out_shape=[jax.ShapeDtypeStruct((n, wd), dt) for wd, dt in outs],
        compiler_params=_cparams(("parallel",)),
        name="even_in_proj",
    )(*args)


ODD_COLS = (512, 512, 512, 512, 512, 512, 256)


def _head_rms(x, seg_ref, g):
    x2 = x * x
    hi = x2.astype(BF16)
    lo = (x2 - hi.astype(F32)).astype(BF16)
    ms = _dot(hi, seg_ref[...]) + _dot(lo, seg_ref[...])
    return x * lax.rsqrt(ms + 1e-6) * g


def _odd_in_kernel(*refs, rope, cache):
    it = iter(refs)
    x_ref, mod_ref, w_ref, segq_ref, segk_ref, qn_ref, kn_ref = (next(it) for _ in range(7))
    tabs = (next(it), next(it), next(it)) if rope else None
    q_ref, ff_ref, fb_ref, v_ref, cg_ref, dq_ref, dkv_ref = (next(it) for _ in range(7))
    cache_ref = next(it) if cache else None

    m = mod_ref[0]
    h = (x_ref[...] * (1.0 + m[4:5]) + m[3:4]).astype(BF16)
    offs = [0]
    for c in ODD_COLS:
        offs.append(offs[-1] + c)
    proj = lambda k: _dot(h, w_ref[:, offs[k]:offs[k + 1]])

    q_ref[...] = _silu(proj(0)).astype(BF16)
    ff_ref[...] = proj(1)
    fb_ref[...] = proj(2)
    v_ref[...] = proj(3).astype(BF16)
    cg_ref[...] = proj(4)
    dq = _head_rms(proj(5), segq_ref, qn_ref[...])
    dkv = proj(6)
    dk = _head_rms(dkv[:, :128], segk_ref, kn_ref[...])
    dv = dkv[:, 128:]
    if cache:
        cache_ref[...] = jnp.concatenate([dk, dv], 1)
    if rope:
        cos, s_up, s_dn = (t[...] for t in tabs)
        dq = _rope(dq, cos, s_up, s_dn)
        dk = _rope(dk, cos, s_up, s_dn)
    dq_ref[...] = (dq * QK_SCALE).astype(BF16)
    dkv_ref[...] = _kv_with_ones(jnp.concatenate([dk, dv], 1)).astype(BF16)


def _segment_mean_matrix(width):
    r = jnp.arange(width) // HEAD_DIM
    return jnp.where(r[:, None] == r[None, :], 1.0 / HEAD_DIM, 0.0).astype(BF16)


def _odd_in(x, mod, w, q_norm, k_norm, mod_index, seq_len, rope, cache):
    n, d = x.shape
    tm = TOKEN_TILE
    row = lambda width: pl.BlockSpec((tm, width), lambda i: (i, 0))
    const = lambda a: pl.BlockSpec(a.shape, lambda i: (0, 0))
    segq, segk = _segment_mean_matrix(512), _segment_mean_matrix(128)
    qn = jnp.tile(q_norm, D_HEADS).reshape(1, 512)
    kn = jnp.tile(k_norm, D_KV).reshape(1, 128)
    in_specs = [row(d), pl.BlockSpec((1, 9, d), mod_index),
                pl.BlockSpec(w.shape, lambda i: (0, 0), pipeline_mode=pl.Buffered(1)),
                const(segq), const(segk), const(qn), const(kn)]
    args = [x, mod, w, segq, segk, qn, kn]
    if rope:
        per_seq = seq_len // tm
        tabs = _rope_tables(seq_len)
        in_specs += [pl.BlockSpec((tm, 128), lambda i: (i % per_seq, 0))] * 3
        args += list(tabs)
    outs = [(512, BF16), (512, F32), (512, F32), (512, BF16), (512, F32), (512, BF16), (KV_WIDTH, BF16)] + \
           ([(256, F32)] if cache else [])
    return pl.pallas_call(
        functools.partial(_odd_in_kernel, rope=rope, cache=cache),
        grid=(n // tm,),
        in_specs=in_specs,
        out_specs=[row(wd) for wd, _ in outs],
        out_shape=[jax.ShapeDtypeStruct((n, wd), dt) for wd, dt in outs],
        compiler_params=_cparams(("parallel",)),
        name="odd_in_proj",
    )(*args)


def _attn_kernel(*refs, seq_len, tq, q_blocks, n_ctx, banded, has_sink):
    it = iter(refs)
    q_ref, kv_ref = next(it), next(it)
    ckv_ref = next(it) if n_ctx else None
    sink_ref = next(it) if has_sink else None
    o_ref = next(it)
    groups = A_HEADS // A_KV
    rows = groups * tq
    low_half = lax.broadcasted_iota(jnp.int32, (tq, 2 * HEAD_DIM), 1) < HEAD_DIM

    def run_kv_head(blk, kh):
        r0 = blk * tq
        j = pl.program_id(1) * q_blocks + blk
        kcol = slice(kh * HEAD_DIM, (kh + 1) * HEAD_DIM)
        vcol = slice(128 + kh * 256, 128 + (kh + 1) * 256)
        qs = jnp.concatenate([q_ref[r0:r0 + tq, (kh * groups + g) * HEAD_DIM:(kh * groups + g + 1) * HEAD_DIM]
                              for g in range(groups)], axis=0)
        tiles = []
        if banded:
            span = tq + 2 * WINDOW
            start = pl.multiple_of(jnp.clip(j * tq - WINDOW, 0, seq_len - span), WINDOW)
            qpos = j * tq + (lax.broadcasted_iota(jnp.int32, (rows, span), 0) & (tq - 1))
            kpos = start + lax.broadcasted_iota(jnp.int32, (rows, span), 1)
            band = jnp.abs(kpos - qpos) <= WINDOW
            tiles.append((kv_ref[pl.ds(start, span), kcol], kv_ref[pl.ds(start, span), vcol], band))
        else:
            tk = min(ATT_KEY_TILE, seq_len)
            for t in range(seq_len // tk):
                tiles.append((kv_ref[t * tk:(t + 1) * tk, kcol], kv_ref[t * tk:(t + 1) * tk, vcol], None))
        if n_ctx:
            tiles.append((ckv_ref[0, :, kcol], ckv_ref[0, :, vcol], None))
        scores = [_dot_nt(qs, k_t) for k_t, _, _ in tiles]
        yield
        scores = [s if mask is None else jnp.where(mask, s, NEG_INF) for s, (_, _, mask) in zip(scores, tiles)]
        m = functools.reduce(jnp.maximum, [jnp.max(s, -1, keepdims=True) for s in scores])
        if has_sink:
            sink = jnp.concatenate([jnp.full((tq, 1), sink_ref[kh * groups + g] * LOG2_E, F32)
                                    for g in range(groups)], 0)
            m = jnp.maximum(m, sink)
        pv = functools.reduce(lambda a, b: a + b, [_dot(jnp.exp2(s - m).astype(BF16), v_t)
                                                   for s, (_, v_t, _) in zip(scores, tiles)])
        yield
        total = pv[:, 2 * HEAD_DIM:]
        if has_sink:
            total = total + jnp.exp2(sink - m)
        o = pv[:, :2 * HEAD_DIM] / total
        for g in range(0, groups, 2):
            pair = jnp.where(low_half, o[g * tq:(g + 1) * tq], o[(g + 1) * tq:(g + 2) * tq])
            c0 = (kh * groups + g) * HEAD_DIM
            o_ref[r0:r0 + tq, c0:c0 + 2 * HEAD_DIM] = pair.astype(BF16)

    run = _run_staggered if seq_len + n_ctx > 2 * ATT_KEY_TILE else _run_interleaved
    run([run_kv_head(blk, kh) for blk in range(q_blocks) for kh in range(A_KV)])


def _attention(q, kv, ctx_kv, sink, batch, seq_len, tq, q_blocks, banded):
    n_ctx = 0 if ctx_kv is None else ctx_kv.shape[1]
    per_seq = seq_len // (tq * q_blocks)
    in_specs = [pl.BlockSpec((q_blocks * tq, 512), lambda b, j: (b * per_seq + j, 0)),
                pl.BlockSpec((seq_len, KV_WIDTH), lambda b, j: (b, 0))]
    args = [q, kv]
    if n_ctx:
        in_specs.append(pl.BlockSpec((1, n_ctx, KV_WIDTH), lambda b, j: (b, 0, 0)))
        args.append(ctx_kv)
    if sink is not None:
        in_specs.append(pl.BlockSpec(memory_space=pltpu.SMEM))
        args.append(sink.reshape(-1).astype(F32))
    return pl.pallas_call(
        functools.partial(_attn_kernel, seq_len=seq_len, tq=tq, q_blocks=q_blocks, n_ctx=n_ctx, banded=banded,
                          has_sink=sink is not None),
        grid=(batch, per_seq),
        in_specs=in_specs,
        out_specs=pl.BlockSpec((q_blocks * tq, 512), lambda b, j: (b * per_seq + j, 0)),
        out_shape=jax.ShapeDtypeStruct(q.shape, BF16),
        compiler_params=_cparams(("parallel", "arbitrary")),
        name="gqa_attention",
    )(*args)


def _log_sigmoid(x):
    return jnp.minimum(x, 0.0) - jnp.log1p(jnp.exp(-jnp.abs(x)))


def _mlstm_kernel(*refs, seq_len, has_init, emit_state):
    it = iter(refs)
    q_ref, k_ref, v_ref, g_ref, bo_ref, gb_ref, ng_ref = (next(it) for _ in range(7))
    c0_ref, n0_ref, m0_ref = (next(it), next(it), next(it)) if has_init else (None, None, None)
    y_ref = next(it)
    co_ref, no_ref, mo_ref = (next(it), next(it), next(it)) if emit_state else (None, None, None)
    c_scr, n_scr, m_scr, hf_scr, hb_scr = (next(it) for _ in range(5))

    t_len = MLSTM_T
    n_chunks = seq_len // t_len
    n_streams = 2 * B_HEADS

    for s in range(n_streams):
        if has_init:
            c_scr[s] = c0_ref[0, s // B_HEADS, s % B_HEADS]
        else:
            c_scr[s] = jnp.zeros((B_DIM, B_DIM), F32)
    if has_init:
        n_scr[...] = n0_ref[0]
        m_scr[...] = m0_ref[0]
    else:
        n_scr[...] = jnp.zeros_like(n_scr)
        m_scr[...] = jnp.zeros_like(m_scr)

    ri = lax.broadcasted_iota(jnp.int32, (t_len, t_len), 0)
    ci = lax.broadcasted_iota(jnp.int32, (t_len, t_len), 1)
    lower = ci <= ri
    upper = ci >= ri
    tri_f = jnp.where(lower, 1.0, 0.0).astype(BF16)
    tri_b = jnp.where(upper, 1.0, 0.0).astype(BF16)
    gate_bias = gb_ref[...]
    sel_row = lax.broadcasted_iota(jnp.int32, (128, B_HEADS * B_DIM), 0)
    sel_head = lax.broadcasted_iota(jnp.int32, (128, B_HEADS * B_DIM), 1) // B_DIM
    gate_select = []
    for direction in range(2):
        i_base = 2 * direction * B_HEADS
        gate_select.append((jnp.where(sel_row == i_base + sel_head, 1.0, 0.0).astype(BF16),
                            jnp.where(sel_row == i_base + B_HEADS + sel_head, 1.0, 0.0).astype(BF16)))

    def run_direction(c, direction):
        chunk = c if direction == 0 else n_chunks - 1 - c
        rows = pl.ds(pl.multiple_of(chunk * t_len, t_len), t_len)
        tri = tri_f if direction == 0 else tri_b
        causal = lower if direction == 0 else upper
        last = t_len - 1 if direction == 0 else 0
        h_out = hf_scr if direction == 0 else hb_scr
        sel_i, sel_f = gate_select[direction]
        gates = g_ref[rows, :] + gate_bias
        hi, mid, lo = _split3(_log_sigmoid(gates))
        b3 = _dot(tri, jnp.concatenate([hi, mid, lo], 1))
        ig_rep = _dot_sel_rhs(gates, sel_i)
        yield
        bc_all = b3[:, :128] + b3[:, 128:256] + b3[:, 256:]
        bc_rep = _dot_sel_rhs(bc_all, sel_f)
        u_t = (gates - pltpu.roll(bc_all, 128 - B_HEADS, 1)).T
        heads = []
        for hd in range(B_HEADS):
            s = direction * B_HEADS + hd
            col = slice(hd * B_DIM, (hd + 1) * B_DIM)
            qc, kc, vc = q_ref[rows, col], k_ref[rows, col], v_ref[rows, col]
            c_prev = c_scr[s]
            n_prev = n_scr[s:s + 1, :]
            qk = _dot_nt(qc, kc)
            q_state = _dot(qc, c_prev.astype(BF16))
            qn = _dot_nt(qc, jnp.broadcast_to(n_prev, (B_DIM, B_DIM)).astype(BF16))
            heads.append((s, col, qc, kc, vc, c_prev, n_prev, qk, q_state, qn))
        yield
        staged = []
        for (s, col, qc, kc, vc, c_prev, n_prev, qk, q_state, qn) in heads:
            i_col = 2 * direction * B_HEADS + (s % B_HEADS)
            ig, bc = ig_rep[:, col], bc_rep[:, col]
            dmat = jnp.where(causal, bc[:, :t_len] + u_t[i_col:i_col + 1, :], -jnp.inf)
            m_prev = m_scr[s:s + 1, :]
            inter = bc + m_prev
            m_t = jnp.maximum(inter, jnp.max(dmat, -1, keepdims=True))
            w = (jnp.exp(dmat - m_t[:, :t_len]) * qk).astype(BF16)
            wv = _dot(w, jnp.concatenate([vc, jnp.ones_like(vc)], 1))
            b_tot = bc[last:last + 1, :]
            g = b_tot - bc + ig
            m_new = jnp.maximum(b_tot + m_prev, jnp.max(g, 0, keepdims=True))
            kw = kc.astype(F32) * jnp.exp(g - m_new)
            upd = _dot_tn(kw.astype(BF16), vc)
            decay = jnp.exp(b_tot + m_prev - m_new)
            staged.append((s, col, inter, m_t, c_prev, n_prev, q_state, qn, wv, m_new, kw, upd, decay))
        yield
        for (s, col, inter, m_t, c_prev, n_prev, q_state, qn, wv, m_new, kw, upd, decay) in staged:
            a = jnp.exp(inter - m_t)
            num = a * q_state + wv[:, :B_DIM]
            den = a * qn + wv[:, B_DIM:]
            h_out[rows, col] = num / jnp.maximum(jnp.abs(den), jnp.exp(-m_t))
            c_scr[s] = decay * c_prev + upd
            n_scr[s:s + 1, :] = decay * n_prev + jnp.sum(kw, 0, keepdims=True)
            m_scr[s:s + 1, :] = m_new

    def step(c, carry):
        _run_interleaved([run_direction(c, 0), run_direction(c, 1)])
        return carry

    lax.fori_loop(0, n_chunks, step, 0)

    blk = min(256, seq_len)
    ng = ng_ref[...]

    def finish(i, carry):
        rows = pl.ds(pl.multiple_of(i * blk, blk), blk)
        hsum = hf_scr[rows, :] + hb_scr[rows, :]
        gate = jax.nn.sigmoid(bo_ref[rows, :])
        parts = [_rms_lastdim(hsum[:, hd * B_DIM:(hd + 1) * B_DIM], ng[:, hd * B_DIM:(hd + 1) * B_DIM])
                 for hd in range(B_HEADS)]
        y_ref[rows, :] = (gate * jnp.concatenate(parts, 1)).astype(BF16)
        return carry

    lax.fori_loop(0, seq_len // blk, finish, 0)

    if emit_state:
        for s in range(n_streams):
            co_ref[0, s // B_HEADS, s % B_HEADS] = c_scr[s]
        no_ref[0] = n_scr[...]
        mo_ref[0] = m_scr[...]


def _mlstm(q, k, v, gates, bo, gate_bias, norm_g, init, batch, seq_len, emit_state):
    n = q.shape[0]
    row = lambda width: pl.BlockSpec((seq_len, width), lambda b: (b, 0))
    const = lambda a: pl.BlockSpec(a.shape, lambda b: (0,) * a.ndim)
    gb = jnp.zeros((1, 128), F32).at[0, :4 * B_HEADS].set(gate_bias.reshape(-1))
    ng = norm_g.reshape(1, B_HEADS * B_DIM)
    in_specs = [row(512), row(512), row(512), row(128), row(512), const(gb), const(ng)]
    args = [q, k, v, gates, bo, gb, ng]
    n_streams = 2 * B_HEADS
    if init is not None:
        c0, n0, m0 = init
        in_specs += [pl.BlockSpec((1, 2, B_HEADS, B_DIM, B_DIM), lambda b: (b, 0, 0, 0, 0)),
                     pl.BlockSpec((1, n_streams, B_DIM), lambda b: (b, 0, 0)),
                     pl.BlockSpec((1, n_streams, B_DIM), lambda b: (b, 0, 0))]
        args += [c0, n0.reshape(batch, n_streams, B_DIM),
                 jnp.broadcast_to(m0.reshape(batch, n_streams, 1), (batch, n_streams, B_DIM))]
    out_specs = [row(512)]
    out_shape = [jax.ShapeDtypeStruct((n, 512), BF16)]
    if emit_state:
        out_specs += [pl.BlockSpec((1, 2, B_HEADS, B_DIM, B_DIM), lambda b: (b, 0, 0, 0, 0)),
                      pl.BlockSpec((1, n_streams, B_DIM), lambda b: (b, 0, 0)),
                      pl.BlockSpec((1, n_streams, B_DIM), lambda b: (b, 0, 0))]
        out_shape += [jax.ShapeDtypeStruct((batch, 2, B_HEADS, B_DIM, B_DIM), F32),
                      jax.ShapeDtypeStruct((batch, n_streams, B_DIM), F32),
                      jax.ShapeDtypeStruct((batch, n_streams, B_DIM), F32)]
    return pl.pallas_call(
        functools.partial(_mlstm_kernel, seq_len=seq_len, has_init=init is not None, emit_state=emit_state),
        grid=(batch,),
        in_specs=in_specs,
        out_specs=out_specs,
        out_shape=out_shape,
        scratch_shapes=[pltpu.VMEM((n_streams, B_DIM, B_DIM), F32),
                        pltpu.VMEM((n_streams, B_DIM), F32),
                        pltpu.VMEM((n_streams, B_DIM), F32),
                        pltpu.VMEM((seq_len, 512), F32),
                        pltpu.VMEM((seq_len, 512), F32)],
        compiler_params=_cparams(("parallel",)),
        name="mlstm_scan",
    )(*args)


def _hgrn_kernel(*refs, seq_len, layer, has_init, emit_state):
    it = iter(refs)
    q_ref, ff_ref, fb_ref, v_ref, cg_ref, lbl_ref, ng_ref = (next(it) for _ in range(7))
    s0_ref = next(it) if has_init else None
    y_ref = next(it)
    so_ref = next(it) if emit_state else None
    st_scr, of_scr, ob_scr = next(it), next(it), next(it)

    t_len = HGRN_T
    n_sub = t_len // SUB
    n_chunks = seq_len // t_len
    n_streams = 2 * C_HEADS

    logits = lbl_ref[...]
    e = jnp.exp(logits - jnp.max(logits, 0, keepdims=True))
    sm = e / jnp.sum(e, 0, keepdims=True)
    lb = jnp.sum(sm[0:layer + 1], 0, keepdims=True) - sm[0:1]

    for s in range(n_streams):
        if has_init:
            st_scr[s] = s0_ref[0, s // C_HEADS, s % C_HEADS].T
        else:
            st_scr[s] = jnp.zeros((C_DV, C_DK), F32)

    ri = lax.broadcasted_iota(jnp.int32, (t_len, t_len), 0)
    ci = lax.broadcasted_iota(jnp.int32, (t_len, t_len), 1)
    tri_f = jnp.where(ci <= ri, 1.0, 0.0).astype(BF16)
    tri_b = jnp.where(ci >= ri, 1.0, 0.0).astype(BF16)
    sub_row = lax.broadcasted_iota(jnp.int32, (SUB, C_DK), 0)
    ones_dk = jnp.ones((C_DK, C_DK), BF16)

    def run_stream(c, direction, hd):
        chunk = c if direction == 0 else n_chunks - 1 - c
        rows = pl.ds(pl.multiple_of(chunk * t_len, t_len), t_len)
        f_ref = ff_ref if direction == 0 else fb_ref
        tri = tri_f if direction == 0 else tri_b
        last = t_len - 1 if direction == 0 else 0
        o_out = of_scr if direction == 0 else ob_scr
        s = direction * C_HEADS + hd
        col = slice(hd * C_DK, (hd + 1) * C_DK)
        lbh = lb[:, col]
        f = lbh + (1.0 - lbh) * jax.nn.sigmoid(f_ref[rows, col])
        log_k = jnp.log2(1.0 - f)
        hi, mid, lo = _split3(jnp.log2(f))
        a3 = _dot(tri, jnp.concatenate([hi, mid, lo], 1))
        yield
        a_cum = a3[:, :C_DK] + a3[:, C_DK:2 * C_DK] + a3[:, 2 * C_DK:]
        a_key = a_cum - log_k
        a_tot = a_cum[last:last + 1, :]
        qf = q_ref[rows, col].astype(F32)
        vc = v_ref[rows, col]
        vf = vc.astype(F32)
        st = st_scr[s]
        inter = _dot_nt((qf * jnp.exp2(a_cum)).astype(BF16), st.astype(BF16))
        upd = _dot_tn(vc, jnp.exp2(a_tot - a_key).astype(BF16))
        ps = []
        for blk in range(n_sub):
            b0 = blk * SUB
            a_i, q_i, ak_i = (t[b0:b0 + SUB] for t in (a_cum, qf, a_key))
            for j in range(SUB):
                seen = (sub_row >= j) if direction == 0 else (sub_row <= j)
                ps.append(jnp.where(seen, jnp.exp2(a_i - ak_i[j:j + 1]), 0.0) * q_i)
        att = _dot(jnp.concatenate(ps, 0).astype(BF16), ones_dk)
        off = []
        for blk in range(n_sub):
            b0 = blk * SUB
            if direction == 0 and blk > 0:
                a_ref, kr = a_cum[b0 - 1:b0], slice(0, b0)
            elif direction == 1 and blk < n_sub - 1:
                a_ref, kr = a_cum[b0 + SUB:b0 + SUB + 1], slice(b0 + SUB, t_len)
            else:
                off.append(None)
                continue
            qt = (qf[b0:b0 + SUB] * jnp.exp2(a_cum[b0:b0 + SUB] - a_ref)).astype(BF16)
            kt = jnp.exp2(a_ref - a_key[kr]).astype(BF16)
            off.append((_dot_nt(qt, kt), kr))
        yield
        st_scr[s] = jnp.exp2(a_tot) * st + upd
        outs = []
        for blk in range(n_sub):
            b0 = blk * SUB
            o_i = inter[b0:b0 + SUB]
            for j in range(SUB):
                r = (blk * SUB + j) * SUB
                o_i = o_i + att[r:r + SUB] * vf[b0 + j:b0 + j + 1]
            if off[blk] is not None:
                att_off, kr = off[blk]
                outs.append((o_i, _dot(att_off.astype(BF16), vc[kr])))
            else:
                outs.append((o_i, None))
        yield
        o_out[rows, col] = jnp.concatenate([o_i if o_off is None else o_i + o_off for o_i, o_off in outs], 0)

    def step(c, carry):
        _run_interleaved([run_stream(c, direction, hd) for direction in range(2) for hd in range(C_HEADS)])
        return carry

    lax.fori_loop(0, n_chunks, step, 0)

    blk_rows = min(256, seq_len)
    ng = ng_ref[...]

    def finish(i, carry):
        rows = pl.ds(pl.multiple_of(i * blk_rows, blk_rows), blk_rows)
        osum = of_scr[rows, :] + ob_scr[rows, :]
        parts = [_rms_lastdim(osum[:, hd * C_DV:(hd + 1) * C_DV], ng[:, hd * C_DV:(hd + 1) * C_DV])
                 for hd in range(C_HEADS)]
        y_ref[rows, :] = (jnp.concatenate(parts, 1) * _silu(cg_ref[rows, :])).astype(BF16)
        return carry

    lax.fori_loop(0, seq_len // blk_rows, finish, 0)

    if emit_state:
        for s in range(n_streams):
            so_ref[0, s // C_HEADS, s % C_HEADS] = st_scr[s].T


def _hgrn(q, ff, fb, v, cg, lb_logits, norm_g, layer, init, batch, seq_len, emit_state):
    n = q.shape[0]
    row = lambda width: pl.BlockSpec((seq_len, width), lambda b: (b, 0))
    const = lambda a: pl.BlockSpec(a.shape, lambda b: (0,) * a.ndim)
    ng = norm_g.reshape(1, C_HEADS * C_DV)
    state_spec = pl.BlockSpec((1, 2, C_HEADS, C_DK, C_DV), lambda b: (b, 0, 0, 0, 0))
    in_specs = [row(512)] * 5 + [const(lb_logits), const(ng)]
    args = [q, ff, fb, v, cg, lb_logits, ng]
    if init is not None:
        in_specs.append(state_spec)
        args.append(init)
    out_specs = [row(512)]
    out_shape = [jax.ShapeDtypeStruct((n, 512), BF16)]
    if emit_state:
        out_specs.append(state_spec)
        out_shape.append(jax.ShapeDtypeStruct((batch, 2, C_HEADS, C_DK, C_DV), F32))
    return pl.pallas_call(
        functools.partial(_hgrn_kernel, seq_len=seq_len, layer=layer, has_init=init is not None,
                          emit_state=emit_state),
        grid=(batch,),
        in_specs=in_specs,
        out_specs=out_specs,
        out_shape=out_shape,
        scratch_shapes=[pltpu.VMEM((2 * C_HEADS, C_DV, C_DK), F32),
                        pltpu.VMEM((seq_len, 512), F32),
                        pltpu.VMEM((seq_len, 512), F32)],
        compiler_params=_cparams(("parallel",)),
        name="hgrn2_scan",
    )(*args)


def _prep_w_in_even(w):
    d = w.shape[0]
    a, bq, bk, bv = w[:, :768], w[:, 768:1280], w[:, 1280:1792], w[:, 1792:2304]
    bg, bo = w[:, 2304:2320], w[:, 2320:2832]
    pad = jnp.zeros((d, 128 - bg.shape[1]), BF16)
    return jnp.concatenate([t.astype(BF16) for t in (a, bq, bk, bv, bo, bg)] + [pad], 1)


def _prep_ctx_kv(k_ctx, v_ctx):
    nb, past = k_ctx.shape[:2]
    ones = jnp.ones((nb, past, 128), k_ctx.dtype)
    v0, v1 = v_ctx[:, :, 0], v_ctx[:, :, 1]
    return jnp.concatenate([k_ctx.reshape(nb, past, -1), v0, v0, ones, v1, v1, ones], -1).astype(BF16)


def kernel(x_prompt, x_sample, c, cache_a_k, cache_a_v, state_b_C, state_b_n, state_b_m, state_c_S, cache_d_k, cache_d_v, c_ctx, ada_w, ada_b, ln_g, ln_b, ffn_w1, ffn_w3, ffn_w2, w_in_even, w_out_even, a_sink, b_gate_bias, b_norm_g, w_in_odd, w_out_odd, c_lb_logits, c_norm_g, d_q_norm, d_k_norm):
    batch_p, len_p, d = x_prompt.shape
    batch_s, len_s, _ = x_sample.shape

    cvec = jnp.concatenate([c_ctx[None], c, jnp.zeros((MOD_ROWS - 1 - batch_s, d), F32)], 0)
    mod_all = _modulation(cvec, ada_w, ada_b)

    groups = [
        dict(x=x_prompt.reshape(batch_p * len_p, d), batch=batch_p, seq=len_p, prompt=True,
             mod_index=_mod_index(0, None)),
        dict(x=x_sample.reshape(batch_s * len_s, d), batch=batch_s, seq=len_s, prompt=False,
             mod_index=_mod_index(1, len_s // TOKEN_TILE)),
    ]
    new = {}
    ffn_w = (ffn_w1.astype(BF16), ffn_w3.astype(BF16), ffn_w2.astype(BF16))

    for l in range(DEPTH):
        mod = mod_all[l].reshape(MOD_ROWS, 9, d)
        i = l // 2
        if l % 2 == 0:
            w_in = _prep_w_in_even(w_in_even[i])
            w_out = w_out_even[i].astype(BF16)
        else:
            w_in = w_in_odd[i].astype(BF16)
            w_out = w_out_odd[i].astype(BF16)

        for grp in groups:
            x, mi, nb, sl, prompt = grp["x"], grp["mod_index"], grp["batch"], grp["seq"], grp["prompt"]
            x = _ffn(x, mod, 0, ln_g[l, 0], ln_b[l, 0], *ffn_w, l, 0, mi)
            if l % 2 == 0:
                outs = _even_in(x, mod, w_in, mi, sl, rope=not prompt, cache=prompt)
                if prompt:
                    aq, akv, kv_cache, bq, bk, bv, bo, bg = outs
                    new["a_k"] = kv_cache[:, :128].reshape(nb, 1, sl, A_KV, HEAD_DIM)
                    new["a_v"] = kv_cache[:, 128:].reshape(nb, 1, sl, A_KV, HEAD_DIM)
                    ya = _attention(aq, akv, None, a_sink[i], nb, sl, tq=Q_BLOCK, q_blocks=sl // Q_BLOCK, banded=False)
                    yb, c_new, n_new, m_new = _mlstm(bq, bk, bv, bg, bo, b_gate_bias[i], b_norm_g[i], None,
                                                     nb, sl, emit_state=True)
                    new["b_C"] = c_new[:, None]
                    new["b_n"] = n_new.reshape(nb, 1, 2, B_HEADS, B_DIM)
                    new["b_m"] = m_new[:, :, 0].reshape(nb, 1, 2, B_HEADS)
                else:
                    aq, akv, bq, bk, bv, bo, bg = outs
                    ya = _attention(aq, akv, _prep_ctx_kv(cache_a_k[:, i], cache_a_v[:, i]), a_sink[i], nb, sl,
                                    tq=Q_BLOCK, q_blocks=4, banded=True)
                    init = (state_b_C[:, i], state_b_n[:, i], state_b_m[:, i])
                    yb, = _mlstm(bq, bk, bv, bg, bo, b_gate_bias[i], b_norm_g[i], init, nb, sl, emit_state=False)
                mixed = (ya, yb)
            else:
                outs = _odd_in(x, mod, w_in, d_q_norm[i], d_k_norm[i], mi, sl, rope=not prompt, cache=prompt)
                if prompt:
                    cq, ff, fb, cv, cg, dq, dkv, kv_cache = outs
                    new["d_k"] = kv_cache[:, :128].reshape(nb, 1, sl, D_KV, HEAD_DIM)
                    new["d_v"] = kv_cache[:, 128:].reshape(nb, 1, sl, D_KV, HEAD_DIM)
                    yc, s_new = _hgrn(cq, ff, fb, cv, cg, c_lb_logits, c_norm_g[i], l, None, nb, sl, emit_state=True)
                    new["c_S"] = s_new[:, None]
                    yd = _attention(dq, dkv, None, None, nb, sl, tq=Q_BLOCK, q_blocks=sl // Q_BLOCK, banded=False)
                else:
                    cq, ff, fb, cv, cg, dq, dkv = outs
                    yc, = _hgrn(cq, ff, fb, cv, cg, c_lb_logits, c_norm_g[i], l, state_c_S[:, i], nb, sl,
                                emit_state=False)
                    yd = _attention(dq, dkv, _prep_ctx_kv(cache_d_k[:, i], cache_d_v[:, i]), None, nb, sl,
                                    tq=Q_BLOCK, q_blocks=1, banded=False)
                mixed = (yc, yd)
            x = _ffn(x, mod, 2, ln_g[l, 2], ln_b[l, 2], *ffn_w, l, 1, mi,
                     mixer=mixed + (w_out, ln_g[l, 1], ln_b[l, 1]))
            grp["x"] = x

    y_prompt = groups[0]["x"].reshape(batch_p, len_p, d)
    y_sample = groups[1]["x"].reshape(batch_s, len_s, d)
    return (y_prompt, y_sample, new["a_k"], new["a_v"], new["b_C"], new["b_n"], new["b_m"], new["c_S"],
            new["d_k"], new["d_v"])
```

```python
import functools

import jax
import jax.numpy as jnp
import numpy as np
from jax import lax
from jax.experimental import pallas as pl
from jax.experimental.pallas import tpu as pltpu

F32 = jnp.float32
BF16 = jnp.bfloat16

D_MODEL = 1024
DEPTH = 2
GRID_W = 64
HEAD_DIM = 64
A_HEADS = 8
A_KV = 2
WINDOW = 128
B_HEADS = 4
B_DIM = 128
C_HEADS = 4
C_DK = 128
C_DV = 128
D_HEADS = 8
D_KV = 2
D_FF = 2816
Q_BLOCK = 128
ROPE_THETA = 10000.0
ALPHA = (2 * DEPTH) ** 0.25
NEG_INF = -1e30
LOG2_E = 1.4426950408889634
QK_SCALE = HEAD_DIM ** -0.5 * LOG2_E

MOD_ROWS = 8
FF_CHUNK = 256
TOKEN_TILE = 512
MLSTM_T = 64
HGRN_T = 64
SUB = 8
ATT_KEY_TILE = 512
VMEM_LIMIT = 56 * 1024 * 1024


def _cparams(sem):
    return pltpu.CompilerParams(dimension_semantics=sem, vmem_limit_bytes=VMEM_LIMIT)


def _dot(a, b):
    return jnp.dot(a, b, preferred_element_type=F32)


def _dot_nt(a, b):
    return lax.dot_general(a, b, (((1,), (1,)), ((), ())), preferred_element_type=F32)


def _dot_tn(a, b):
    return lax.dot_general(a, b, (((0,), (0,)), ((), ())), preferred_element_type=F32)


def _split3(x):
    hi = x.astype(BF16)
    r1 = x - hi.astype(F32)
    mid = r1.astype(BF16)
    lo = (r1 - mid.astype(F32)).astype(BF16)
    return hi, mid, lo


def _dot_sel(sel, x):
    hi, mid, lo = _split3(x)
    return _dot(sel, hi) + _dot(sel, mid) + _dot(sel, lo)


def _dot_sel_rhs(x, sel):
    hi = x.astype(BF16)
    lo = (x - hi.astype(F32)).astype(BF16)
    r = _dot(jnp.concatenate([hi, lo], 0), sel)
    return r[:x.shape[0]] + r[x.shape[0]:]


def _run_interleaved(gens):
    live = list(gens)
    while live:
        nxt = []
        for g in live:
            try:
                next(g)
                nxt.append(g)
            except StopIteration:
                pass
        live = nxt


def _run_staggered(gens):
    pending, live = list(gens), []
    while pending or live:
        if pending:
            live.insert(0, pending.pop(0))
        nxt = []
        for g in live:
            try:
                next(g)
                nxt.append(g)
            except StopIteration:
                pass
        live = nxt


def _silu(x):
    return x * jax.nn.sigmoid(x)


def _layernorm(z, g, b):
    mu = jnp.mean(z, -1, keepdims=True)
    zc = z - mu
    var = jnp.mean(zc * zc, -1, keepdims=True)
    return zc * lax.rsqrt(var + 1e-5) * g + b


def _rms_lastdim(x, g):
    return x * lax.rsqrt(jnp.mean(x * x, -1, keepdims=True) + 1e-6) * g


def _mod_index(group_start, tiles_per_request):
    if tiles_per_request is None:
        return lambda i: (group_start, 0, 0)
    return lambda i: (group_start + i // tiles_per_request, 0, 0)


def _mod_kernel(c_ref, w_ref, b_ref, o_ref):
    s = _silu(c_ref[...]).astype(BF16)
    o_ref[0] = _dot(s, w_ref[0].astype(BF16)) + b_ref[0]


def _modulation(cvec, ada_w, ada_b):
    depth, d, n = ada_w.shape
    tn = 1152
    return pl.pallas_call(
        _mod_kernel,
        grid=(depth, n // tn),
        in_specs=[pl.BlockSpec((MOD_ROWS, d), lambda l, j: (0, 0)),
                  pl.BlockSpec((1, d, tn), lambda l, j: (l, 0, j)),
                  pl.BlockSpec((1, 1, tn), lambda l, j: (l, 0, j))],
        out_specs=pl.BlockSpec((1, MOD_ROWS, tn), lambda l, j: (l, 0, j)),
        out_shape=jax.ShapeDtypeStruct((depth, MOD_ROWS, n), F32),
        compiler_params=_cparams(("parallel", "parallel")),
        name="modulation",
    )(cvec, ada_w, ada_b.reshape(depth, 1, n))


def _ffn_kernel(*refs, j, nf, mixer):
    it = iter(refs)
    x_ref, mod_ref = next(it), next(it)
    if mixer:
        ya_ref, yb_ref, wo_ref, g1_ref, b1_ref = (next(it) for _ in range(5))
    w1_ref, w3_ref, w2_ref, g_ref, b_ref, o_ref, h_ref, acc_ref = (next(it) for _ in range(8))
    m = mod_ref[0]
    if mixer:
        res_ref = next(it)
        half = ya_ref.shape[1]
        y = _dot(ya_ref[...], wo_ref[:half, :]) + _dot(yb_ref[...], wo_ref[half:, :])
        res_ref[...] = _layernorm(ALPHA * x_ref[...] + m[5:6] * y, g1_ref[...], b1_ref[...])
        x_ref = res_ref
    shift, scale, gate = m[3 * j:3 * j + 1], m[3 * j + 1:3 * j + 2], m[3 * j + 2:3 * j + 3]
    h_ref[...] = (x_ref[...] * (1.0 + scale) + shift).astype(BF16)
    for f in range(nf):
        cols = slice(f * FF_CHUNK, (f + 1) * FF_CHUNK)
        h = h_ref[...]
        u = (_silu(_dot(h, w1_ref[0, 0, :, cols])) * _dot(h, w3_ref[0, 0, :, cols])).astype(BF16)
        y = _dot(u, w2_ref[0, 0, cols, :])
        if f == 0:
            acc_ref[...] = y
        else:
            acc_ref[...] += y
    z = ALPHA * x_ref[...] + 0.5 * gate * acc_ref[...]
    o_ref[...] = _layernorm(z, g_ref[...], b_ref[...])


def _ffn(x, mod, j, g, b, w1, w3, w2, layer, which, mod_index, mixer=None):
    n, d = x.shape
    nf = w1.shape[-1] // FF_CHUNK
    tm = TOKEN_TILE
    row = lambda width: pl.BlockSpec((tm, width), lambda i: (i, 0))
    vec = pl.BlockSpec((1, d), lambda i: (0, 0))
    whole = lambda a: pl.BlockSpec((1, 1) + a.shape[2:], lambda i: (layer, which, 0, 0),
                                   pipeline_mode=pl.Buffered(1))
    in_specs = [row(d), pl.BlockSpec((1, 9, d), mod_index)]
    args = [x, mod]
    scratch = [pltpu.VMEM((tm, d), BF16), pltpu.VMEM((tm, d), F32)]
    if mixer is not None:
        ya, yb, w_out, g1, b1 = mixer
        in_specs += [row(ya.shape[1]), row(yb.shape[1]),
                     pl.BlockSpec(w_out.shape, lambda i: (0, 0), pipeline_mode=pl.Buffered(1)), vec, vec]
        args += [ya, yb, w_out, g1.reshape(1, d), b1.reshape(1, d)]
        scratch.append(pltpu.VMEM((tm, d), F32))
    in_specs += [whole(w1), whole(w3), whole(w2), vec, vec]
    args += [w1, w3, w2, g.reshape(1, d), b.reshape(1, d)]
    return pl.pallas_call(
        functools.partial(_ffn_kernel, j=j, nf=nf, mixer=mixer is not None),
        grid=(n // tm,),
        in_specs=in_specs,
        out_specs=row(d),
        out_shape=jax.ShapeDtypeStruct((n, d), F32),
        scratch_shapes=scratch,
        compiler_params=_cparams(("parallel",)),
        name="ffn_sublayer",
    )(*args)


def _rope_tables(length):
    t = np.arange(length)
    nf = HEAD_DIM // 4
    inv = ROPE_THETA ** (-np.arange(nf, dtype=np.float64) / nf)
    ang_r = (t // GRID_W)[:, None] * inv[None]
    ang_c = (t % GRID_W)[:, None] * inv[None]
    cr, sr, cc, sc = np.cos(ang_r), np.sin(ang_r), np.cos(ang_c), np.sin(ang_c)
    z = np.zeros_like(cr)
    cos = np.concatenate([cr, cr, cc, cc], 1)
    sin_up = np.concatenate([-sr, z, -sc, z], 1)
    sin_dn = np.concatenate([z, sr, z, sc], 1)
    two = lambda a: jnp.asarray(np.concatenate([a, a], 1), F32)
    return two(cos), two(sin_up), two(sin_dn)


def _rope128(x, cos, sin_up, sin_dn):
    nf = HEAD_DIM // 4
    return x * cos + pltpu.roll(x, 128 - nf, 1) * sin_up + pltpu.roll(x, nf, 1) * sin_dn


def _rope(x, cos, sin_up, sin_dn):
    parts = [_rope128(x[:, c:c + 128], cos, sin_up, sin_dn) for c in range(0, x.shape[1], 128)]
    return parts[0] if len(parts) == 1 else jnp.concatenate(parts, 1)


EVEN_COLS = (512, 256, 512, 512, 512, 512, 128)


KV_WIDTH = 128 + A_KV * 256


def _store_cache(ck_ref, cv_ref, kv):
    seq_len = ck_ref.shape[2]
    for s in range(ck_ref.shape[0]):
        rows = slice(s * seq_len, (s + 1) * seq_len)
        ck_ref[s] = kv[rows, :128].T
        cv_ref[s] = kv[rows, 128:].T


def _in_proj_outputs(outs, n, tm, seq_len):
    specs, shapes = [], []
    for o in outs:
        if o == "cache":
            specs.append(pl.BlockSpec((tm // seq_len, 128, seq_len), lambda i: (i, 0, 0)))
            shapes.append(jax.ShapeDtypeStruct((n // seq_len, 128, seq_len), F32))
        else:
            specs.append(pl.BlockSpec((tm, o[0]), lambda i: (i, 0)))
            shapes.append(jax.ShapeDtypeStruct((n, o[0]), o[1]))
    return specs, shapes


def _cache_layout(c):
    nb, _, sl = c.shape
    return jnp.transpose(c.reshape(nb, 1, A_KV, HEAD_DIM, sl), (0, 1, 4, 2, 3))


def _kv_with_ones(kv):
    ones = jnp.ones((kv.shape[0], 128), kv.dtype)
    v0, v1 = kv[:, 128:192], kv[:, 192:256]
    return jnp.concatenate([kv[:, :128], v0, v0, ones, v1, v1, ones], 1)


def _even_in_kernel(*refs, rope, cache):
    it = iter(refs)
    x_ref, mod_ref, w_ref = next(it), next(it), next(it)
    tabs = (next(it), next(it), next(it)) if rope else None
    aq_ref, akv_ref = next(it), next(it)
    cache_refs = (next(it), next(it)) if cache else None
    bq_ref, bk_ref, bv_ref, bo_ref, bg_ref = next(it), next(it), next(it), next(it), next(it)

    m = mod_ref[0]
    h = (x_ref[...] * (1.0 + m[4:5]) + m[3:4]).astype(BF16)
    offs = [0]
    for c in EVEN_COLS:
        offs.append(offs[-1] + c)
    proj = lambda k: _dot(h, w_ref[:, offs[k]:offs[k + 1]])

    aq = proj(0)
    akv = proj(1)
    if cache:
        _store_cache(*cache_refs, akv)
    if rope:
        cos, s_up, s_dn = (t[...] for t in tabs)
        aq = _rope(aq, cos, s_up, s_dn)
        akv = jnp.concatenate([_rope(akv[:, :128], cos, s_up, s_dn), akv[:, 128:]], 1)
    aq_ref[...] = (aq * QK_SCALE).astype(BF16)
    akv_ref[...] = _kv_with_ones(akv).astype(BF16)
    bq_ref[...] = proj(2).astype(BF16)
    bk_ref[...] = (proj(3) * (B_DIM ** -0.5)).astype(BF16)
    bv_ref[...] = proj(4).astype(BF16)
    bo_ref[...] = proj(5)
    bg_ref[...] = proj(6)


def _even_in(x, mod, w, mod_index, seq_len, rope, cache):
    n, d = x.shape
    tm = TOKEN_TILE
    row = lambda width: pl.BlockSpec((tm, width), lambda i: (i, 0))
    in_specs = [row(d), pl.BlockSpec((1, 9, d), mod_index),
                pl.BlockSpec(w.shape, lambda i: (0, 0), pipeline_mode=pl.Buffered(1))]
    args = [x, mod, w]
    if rope:
        per_seq = seq_len // tm
        tabs = _rope_tables(seq_len)
        in_specs += [pl.BlockSpec((tm, 128), lambda i: (i % per_seq, 0))] * 3
        args += list(tabs)
    outs = [(512, BF16), (KV_WIDTH, BF16)] + (["cache", "cache"] if cache else []) + \
           [(512, BF16), (512, BF16), (512, BF16), (512, F32), (128, F32)]
    out_specs, out_shape = _in_proj_outputs(outs, n, tm, seq_len)
    return pl.pallas_call(
        functools.partial(_even_in_kernel, rope=rope, cache=cache),
        grid=(n // tm,),
        in_specs=in_specs,
        out_specs=out_specs,
        out_shape=out_shape,
        compiler_params=_cparams(("parallel",)),
        name="even_in_proj",
    )(*args)


ODD_COLS = (512, 512, 512, 512, 512, 512, 256)


def _head_rms(x, seg_ref, g):
    x2 = x * x
    hi = x2.astype(BF16)
    lo = (x2 - hi.astype(F32)).astype(BF16)
    w = seg_ref.shape[0]
    ms = [_dot(hi[:, c:c + w], seg_ref[...]) + _dot(lo[:, c:c + w], seg_ref[...]) for c in range(0, x.shape[1], w)]
    ms = ms[0] if len(ms) == 1 else jnp.concatenate(ms, 1)
    return x * lax.rsqrt(ms + 1e-6) * g


def _odd_in_kernel(*refs, rope, cache):
    it = iter(refs)
    x_ref, mod_ref, w_ref, segq_ref, segk_ref, qn_ref, kn_ref = (next(it) for _ in range(7))
    tabs = (next(it), next(it), next(it)) if rope else None
    q_ref, ff_ref, fb_ref, v_ref, cg_ref, dq_ref, dkv_ref = (next(it) for _ in range(7))
    cache_refs = (next(it), next(it)) if cache else None

    m = mod_ref[0]
    h = (x_ref[...] * (1.0 + m[4:5]) + m[3:4]).astype(BF16)
    offs = [0]
    for c in ODD_COLS:
        offs.append(offs[-1] + c)
    proj = lambda k: _dot(h, w_ref[:, offs[k]:offs[k + 1]])

    q_ref[...] = _silu(proj(0)).astype(BF16)
    ff_ref[...] = proj(1)
    fb_ref[...] = proj(2)
    v_ref[...] = proj(3).astype(BF16)
    cg_ref[...] = proj(4)
    dq = _head_rms(proj(5), segq_ref, qn_ref[...])
    dkv = proj(6)
    dk = _head_rms(dkv[:, :128], segk_ref, kn_ref[...])
    dv = dkv[:, 128:]
    if cache:
        _store_cache(*cache_refs, jnp.concatenate([dk, dv], 1))
    if rope:
        cos, s_up, s_dn = (t[...] for t in tabs)
        dq = _rope(dq, cos, s_up, s_dn)
        dk = _rope(dk, cos, s_up, s_dn)
    dq_ref[...] = (dq * QK_SCALE).astype(BF16)
    dkv_ref[...] = _kv_with_ones(jnp.concatenate([dk, dv], 1)).astype(BF16)


def _segment_mean_matrix(width):
    r = jnp.arange(width) // HEAD_DIM
    return jnp.where(r[:, None] == r[None, :], 1.0 / HEAD_DIM, 0.0).astype(BF16)


def _odd_in(x, mod, w, q_norm, k_norm, mod_index, seq_len, rope, cache):
    n, d = x.shape
    tm = TOKEN_TILE
    row = lambda width: pl.BlockSpec((tm, width), lambda i: (i, 0))
    const = lambda a: pl.BlockSpec(a.shape, lambda i: (0, 0))
    segq, segk = _segment_mean_matrix(256), _segment_mean_matrix(128)
    qn = jnp.tile(q_norm, D_HEADS).reshape(1, 512)
    kn = jnp.tile(k_norm, D_KV).reshape(1, 128)
    in_specs = [row(d), pl.BlockSpec((1, 9, d), mod_index),
                pl.BlockSpec(w.shape, lambda i: (0, 0), pipeline_mode=pl.Buffered(1)),
                const(segq), const(segk), const(qn), const(kn)]
    args = [x, mod, w, segq, segk, qn, kn]
    if rope:
        per_seq = seq_len // tm
        tabs = _rope_tables(seq_len)
        in_specs += [pl.BlockSpec((tm, 128), lambda i: (i % per_seq, 0))] * 3
        args += list(tabs)
    outs = [(512, BF16), (512, F32), (512, F32), (512, BF16), (512, F32), (512, BF16), (KV_WIDTH, BF16)] + \
           (["cache", "cache"] if cache else [])
    out_specs, out_shape = _in_proj_outputs(outs, n, tm, seq_len)
    return pl.pallas_call(
        functools.partial(_odd_in_kernel, rope=rope, cache=cache),
        grid=(n // tm,),
        in_specs=in_specs,
        out_specs=out_specs,
        out_shape=out_shape,
        compiler_params=_cparams(("parallel",)),
        name="odd_in_proj",
    )(*args)


def _attn_kernel(*refs, seq_len, tq, q_blocks, n_ctx, banded, has_sink):
    it = iter(refs)
    q_ref, kv_ref = next(it), next(it)
    ckv_ref = next(it) if n_ctx else None
    sink_ref = next(it) if has_sink else None
    o_ref = next(it)
    groups = A_HEADS // A_KV
    rows = groups * tq
    low_half = lax.broadcasted_iota(jnp.int32, (tq, 2 * HEAD_DIM), 1) < HEAD_DIM

    def run_kv_head(blk, kh):
        r0 = blk * tq
        j = pl.program_id(1) * q_blocks + blk
        kcol = slice(kh * HEAD_DIM, (kh + 1) * HEAD_DIM)
        vcol = slice(128 + kh * 256, 128 + (kh + 1) * 256)
        qs = jnp.concatenate([q_ref[r0:r0 + tq, (kh * groups + g) * HEAD_DIM:(kh * groups + g + 1) * HEAD_DIM]
                              for g in range(groups)], axis=0)
        tiles = []
        if banded:
            span = tq + 2 * WINDOW
            start = pl.multiple_of(jnp.clip(j * tq - WINDOW, 0, seq_len - span), WINDOW)
            qpos = j * tq + (lax.broadcasted_iota(jnp.int32, (rows, span), 0) & (tq - 1))
            kpos = start + lax.broadcasted_iota(jnp.int32, (rows, span), 1)
            band = jnp.abs(kpos - qpos) <= WINDOW
            tiles.append((kv_ref[pl.ds(start, span), kcol], kv_ref[pl.ds(start, span), vcol], band))
        else:
            tk = min(ATT_KEY_TILE, seq_len)
            for t in range(seq_len // tk):
                tiles.append((kv_ref[t * tk:(t + 1) * tk, kcol], kv_ref[t * tk:(t + 1) * tk, vcol], None))
        scores = [_dot_nt(qs, k_t) for k_t, _, _ in tiles]
        if n_ctx:
            tiles.append((None, ckv_ref[0, kh, HEAD_DIM:, :], None))
            scores.append(_dot(qs, ckv_ref[0, kh, :HEAD_DIM, :]))
        yield
        scores = [s if mask is None else jnp.where(mask, s, NEG_INF) for s, (_, _, mask) in zip(scores, tiles)]
        m = functools.reduce(jnp.maximum, [jnp.max(s, -1, keepdims=True) for s in scores])
        if has_sink:
            sink = jnp.concatenate([jnp.full((tq, 1), sink_ref[kh * groups + g] * LOG2_E, F32)
                                    for g in range(groups)], 0)
            m = jnp.maximum(m, sink)
        pv = functools.reduce(lambda a, b: a + b,
                              [(_dot_nt if k_t is None else _dot)(jnp.exp2(s - m).astype(BF16), v_t)
                               for s, (k_t, v_t, _) in zip(scores, tiles)])
        yield
        total = pv[:, 2 * HEAD_DIM:]
        if has_sink:
            total = total + jnp.exp2(sink - m)
        o = pv[:, :2 * HEAD_DIM] / total
        for g in range(0, groups, 2):
            pair = jnp.where(low_half, o[g * tq:(g + 1) * tq], o[(g + 1) * tq:(g + 2) * tq])
            c0 = (kh * groups + g) * HEAD_DIM
            o_ref[r0:r0 + tq, c0:c0 + 2 * HEAD_DIM] = pair.astype(BF16)

    run = _run_staggered if seq_len + n_ctx > 2 * ATT_KEY_TILE else _run_interleaved
    run([run_kv_head(blk, kh) for blk in range(q_blocks) for kh in range(A_KV)])


def _attention(q, kv, ctx_kv, sink, batch, seq_len, tq, q_blocks, banded):
    n_ctx = 0 if ctx_kv is None else ctx_kv.shape[-1]
    per_seq = seq_len // (tq * q_blocks)
    in_specs = [pl.BlockSpec((q_blocks * tq, 512), lambda b, j: (b * per_seq + j, 0)),
                pl.BlockSpec((seq_len, KV_WIDTH), lambda b, j: (b, 0))]
    args = [q, kv]
    if n_ctx:
        in_specs.append(pl.BlockSpec((1,) + ctx_kv.shape[1:], lambda b, j: (b, 0, 0, 0)))
        args.append(ctx_kv)
    if sink is not None:
        in_specs.append(pl.BlockSpec(memory_space=pltpu.SMEM))
        args.append(sink.reshape(-1).astype(F32))
    return pl.pallas_call(
        functools.partial(_attn_kernel, seq_len=seq_len, tq=tq, q_blocks=q_blocks, n_ctx=n_ctx, banded=banded,
                          has_sink=sink is not None),
        grid=(batch, per_seq),
        in_specs=in_specs,
        out_specs=pl.BlockSpec((q_blocks * tq, 512), lambda b, j: (b * per_seq + j, 0)),
        out_shape=jax.ShapeDtypeStruct(q.shape, BF16),
        compiler_params=_cparams(("parallel", "arbitrary")),
        name="gqa_attention",
    )(*args)


def _log_sigmoid(x):
    return jnp.minimum(x, 0.0) - jnp.log1p(jnp.exp(-jnp.abs(x)))


def _mlstm_kernel(*refs, seq_len, has_init, emit_state):
    it = iter(refs)
    q_ref, k_ref, v_ref, g_ref, bo_ref, gb_ref, ng_ref = (next(it) for _ in range(7))
    c0_ref, n0_ref, m0_ref = (next(it), next(it), next(it)) if has_init else (None, None, None)
    y_ref = next(it)
    co_ref, no_ref, mo_ref = (next(it), next(it), next(it)) if emit_state else (None, None, None)
    c_scr, n_scr, m_scr, hf_scr, hb_scr = (next(it) for _ in range(5))

    t_len = MLSTM_T
    n_chunks = seq_len // t_len
    n_streams = 2 * B_HEADS

    for s in range(n_streams):
        if has_init:
            c_scr[s] = c0_ref[0, s // B_HEADS, s % B_HEADS]
        else:
            c_scr[s] = jnp.zeros((B_DIM, B_DIM), F32)
    if has_init:
        n_scr[...] = n0_ref[0]
        m_scr[...] = m0_ref[0]
    else:
        n_scr[...] = jnp.zeros_like(n_scr)
        m_scr[...] = jnp.zeros_like(m_scr)

    ri = lax.broadcasted_iota(jnp.int32, (t_len, t_len), 0)
    ci = lax.broadcasted_iota(jnp.int32, (t_len, t_len), 1)
    lower = ci <= ri
    upper = ci >= ri
    tri_f = jnp.where(lower, 1.0, 0.0).astype(BF16)
    tri_b = jnp.where(upper, 1.0, 0.0).astype(BF16)
    gate_bias = gb_ref[...]
    sel_row = lax.broadcasted_iota(jnp.int32, (128, B_HEADS * B_DIM), 0)
    sel_head = lax.broadcasted_iota(jnp.int32, (128, B_HEADS * B_DIM), 1) // B_DIM
    gate_select = []
    for direction in range(2):
        i_base = 2 * direction * B_HEADS
        gate_select.append((jnp.where(sel_row == i_base + sel_head, 1.0, 0.0).astype(BF16),
                            jnp.where(sel_row == i_base + B_HEADS + sel_head, 1.0, 0.0).astype(BF16)))

    def run_direction(c, direction):
        chunk = c if direction == 0 else n_chunks - 1 - c
        rows = pl.ds(pl.multiple_of(chunk * t_len, t_len), t_len)
        tri = tri_f if direction == 0 else tri_b
        causal = lower if direction == 0 else upper
        last = t_len - 1 if direction == 0 else 0
        h_out = hf_scr if direction == 0 else hb_scr
        sel_i, sel_f = gate_select[direction]
        gates = g_ref[rows, :] + gate_bias
        hi, mid, lo = _split3(_log_sigmoid(gates))
        b3 = _dot(tri, jnp.concatenate([hi, mid, lo], 1))
        ig_rep = _dot_sel_rhs(gates, sel_i)
        yield
        bc_all = b3[:, :128] + b3[:, 128:256] + b3[:, 256:]
        bc_rep = _dot_sel_rhs(bc_all, sel_f)
        u_t = (gates - pltpu.roll(bc_all, 128 - B_HEADS, 1)).T
        heads = []
        for hd in range(B_HEADS):
            s = direction * B_HEADS + hd
            col = slice(hd * B_DIM, (hd + 1) * B_DIM)
            qc, kc, vc = q_ref[rows, col], k_ref[rows, col], v_ref[rows, col]
            c_prev = c_scr[s]
            n_prev = n_scr[s:s + 1, :]
            qk = _dot_nt(qc, kc)
            q_state = _dot(qc, c_prev.astype(BF16))
            qn = _dot_nt(qc, jnp.broadcast_to(n_prev, (B_DIM, B_DIM)).astype(BF16))
            heads.append((s, col, qc, kc, vc, c_prev, n_prev, qk, q_state, qn))
        yield
        staged = []
        for (s, col, qc, kc, vc, c_prev, n_prev, qk, q_state, qn) in heads:
            i_col = 2 * direction * B_HEADS + (s % B_HEADS)
            ig, bc = ig_rep[:, col], bc_rep[:, col]
            dmat = jnp.where(causal, bc[:, :t_len] + u_t[i_col:i_col + 1, :], -jnp.inf)
            m_prev = m_scr[s:s + 1, :]
            inter = bc + m_prev
            m_t = jnp.maximum(inter, jnp.max(dmat, -1, keepdims=True))
            w = (jnp.exp(dmat - m_t[:, :t_len]) * qk).astype(BF16)
            wv = _dot(w, jnp.concatenate([vc, jnp.ones_like(vc)], 1))
            b_tot = bc[last:last + 1, :]
            g = b_tot - bc + ig
            m_new = jnp.maximum(b_tot + m_prev, jnp.max(g, 0, keepdims=True))
            kw = kc.astype(F32) * jnp.exp(g - m_new)
            upd = _dot_tn(kw.astype(BF16), vc)
            decay = jnp.exp(b_tot + m_prev - m_new)
            staged.append((s, col, inter, m_t, c_prev, n_prev, q_state, qn, wv, m_new, kw, upd, decay))
        yield
        for (s, col, inter, m_t, c_prev, n_prev, q_state, qn, wv, m_new, kw, upd, decay) in staged:
            a = jnp.exp(inter - m_t)
            num = a * q_state + wv[:, :B_DIM]
            den = a * qn + wv[:, B_DIM:]
            h_out[rows, col] = num / jnp.maximum(jnp.abs(den), jnp.exp(-m_t))
            c_scr[s] = decay * c_prev + upd
            n_scr[s:s + 1, :] = decay * n_prev + jnp.sum(kw, 0, keepdims=True)
            m_scr[s:s + 1, :] = m_new

    def step(c, carry):
        _run_interleaved([run_direction(c, 0), run_direction(c, 1)])
        return carry

    lax.fori_loop(0, n_chunks, step, 0)

    blk = min(256, seq_len)
    ng = ng_ref[...]

    def finish(i, carry):
        rows = pl.ds(pl.multiple_of(i * blk, blk), blk)
        hsum = hf_scr[rows, :] + hb_scr[rows, :]
        gate = jax.nn.sigmoid(bo_ref[rows, :])
        parts = [_rms_lastdim(hsum[:, hd * B_DIM:(hd + 1) * B_DIM], ng[:, hd * B_DIM:(hd + 1) * B_DIM])
                 for hd in range(B_HEADS)]
        y_ref[rows, :] = (gate * jnp.concatenate(parts, 1)).astype(BF16)
        return carry

    lax.fori_loop(0, seq_len // blk, finish, 0)

    if emit_state:
        for s in range(n_streams):
            co_ref[0, s // B_HEADS, s % B_HEADS] = c_scr[s]
        no_ref[0] = n_scr[...]
        mo_ref[0] = m_scr[...]


def _mlstm(q, k, v, gates, bo, gate_bias, norm_g, init, batch, seq_len, emit_state):
    n = q.shape[0]
    row = lambda width: pl.BlockSpec((seq_len, width), lambda b: (b, 0))
    const = lambda a: pl.BlockSpec(a.shape, lambda b: (0,) * a.ndim)
    gb = jnp.zeros((1, 128), F32).at[0, :4 * B_HEADS].set(gate_bias.reshape(-1))
    ng = norm_g.reshape(1, B_HEADS * B_DIM)
    in_specs = [row(512), row(512), row(512), row(128), row(512), const(gb), const(ng)]
    args = [q, k, v, gates, bo, gb, ng]
    n_streams = 2 * B_HEADS
    if init is not None:
        c0, n0, m0 = init
        in_specs += [pl.BlockSpec((1, 2, B_HEADS, B_DIM, B_DIM), lambda b: (b, 0, 0, 0, 0)),
                     pl.BlockSpec((1, n_streams, B_DIM), lambda b: (b, 0, 0)),
                     pl.BlockSpec((1, n_streams, B_DIM), lambda b: (b, 0, 0))]
        args += [c0, n0.reshape(batch, n_streams, B_DIM),
                 jnp.broadcast_to(m0.reshape(batch, n_streams, 1), (batch, n_streams, B_DIM))]
    out_specs = [row(512)]
    out_shape = [jax.ShapeDtypeStruct((n, 512), BF16)]
    if emit_state:
        out_specs += [pl.BlockSpec((1, 2, B_HEADS, B_DIM, B_DIM), lambda b: (b, 0, 0, 0, 0)),
                      pl.BlockSpec((1, n_streams, B_DIM), lambda b: (b, 0, 0)),
                      pl.BlockSpec((1, n_streams, B_DIM), lambda b: (b, 0, 0))]
        out_shape += [jax.ShapeDtypeStruct((batch, 2, B_HEADS, B_DIM, B_DIM), F32),
                      jax.ShapeDtypeStruct((batch, n_streams, B_DIM), F32),
                      jax.ShapeDtypeStruct((batch, n_streams, B_DIM), F32)]
    return pl.pallas_call(
        functools.partial(_mlstm_kernel, seq_len=seq_len, has_init=init is not None, emit_state=emit_state),
        grid=(batch,),
        in_specs=in_specs,
        out_specs=out_specs,
        out_shape=out_shape,
        scratch_shapes=[pltpu.VMEM((n_streams, B_DIM, B_DIM), F32),
                        pltpu.VMEM((n_streams, B_DIM), F32),
                        pltpu.VMEM((n_streams, B_DIM), F32),
                        pltpu.VMEM((seq_len, 512), F32),
                        pltpu.VMEM((seq_len, 512), F32)],
        compiler_params=_cparams(("parallel",)),
        name="mlstm_scan",
    )(*args)


def _hgrn_kernel(*refs, seq_len, layer, has_init, emit_state):
    it = iter(refs)
    q_ref, ff_ref, fb_ref, v_ref, cg_ref, lbl_ref, ng_ref = (next(it) for _ in range(7))
    s0_ref = next(it) if has_init else None
    y_ref = next(it)
    so_ref = next(it) if emit_state else None
    st_scr, of_scr, ob_scr = next(it), next(it), next(it)

    t_len = HGRN_T
    n_sub = t_len // SUB
    n_chunks = seq_len // t_len
    n_streams = 2 * C_HEADS

    logits = lbl_ref[...]
    e = jnp.exp(logits - jnp.max(logits, 0, keepdims=True))
    sm = e / jnp.sum(e, 0, keepdims=True)
    lb = jnp.sum(sm[0:layer + 1], 0, keepdims=True) - sm[0:1]

    for s in range(n_streams):
        if has_init:
            st_scr[s] = s0_ref[0, s // C_HEADS, s % C_HEADS].T
        else:
            st_scr[s] = jnp.zeros((C_DV, C_DK), F32)

    ri = lax.broadcasted_iota(jnp.int32, (t_len, t_len), 0)
    ci = lax.broadcasted_iota(jnp.int32, (t_len, t_len), 1)
    tri_f = jnp.where(ci <= ri, 1.0, 0.0).astype(BF16)
    tri_b = jnp.where(ci >= ri, 1.0, 0.0).astype(BF16)
    sub_row = lax.broadcasted_iota(jnp.int32, (SUB, C_DK), 0)
    ones_dk = jnp.ones((C_DK, C_DK), BF16)

    def run_stream(c, direction, hd):
        chunk = c if direction == 0 else n_chunks - 1 - c
        rows = pl.ds(pl.multiple_of(chunk * t_len, t_len), t_len)
        f_ref = ff_ref if direction == 0 else fb_ref
        tri = tri_f if direction == 0 else tri_b
        last = t_len - 1 if direction == 0 else 0
        o_out = of_scr if direction == 0 else ob_scr
        s = direction * C_HEADS + hd
        col = slice(hd * C_DK, (hd + 1) * C_DK)
        lbh = lb[:, col]
        f = lbh + (1.0 - lbh) * jax.nn.sigmoid(f_ref[rows, col])
        log_k = jnp.log2(1.0 - f)
        hi, mid, lo = _split3(jnp.log2(f))
        a3 = _dot(tri, jnp.concatenate([hi, mid, lo], 1))
        yield
        a_cum = a3[:, :C_DK] + a3[:, C_DK:2 * C_DK] + a3[:, 2 * C_DK:]
        a_key = a_cum - log_k
        a_tot = a_cum[last:last + 1, :]
        qf = q_ref[rows, col].astype(F32)
        vc = v_ref[rows, col]
        vf = vc.astype(F32)
        st = st_scr[s]
        inter = _dot_nt((qf * jnp.exp2(a_cum)).astype(BF16), st.astype(BF16))
        upd = _dot_tn(vc, jnp.exp2(a_tot - a_key).astype(BF16))
        ps = []
        for blk in range(n_sub):
            b0 = blk * SUB
            a_i, q_i, ak_i = (t[b0:b0 + SUB] for t in (a_cum, qf, a_key))
            for j in range(SUB):
                seen = (sub_row >= j) if direction == 0 else (sub_row <= j)
                ps.append(jnp.where(seen, jnp.exp2(a_i - ak_i[j:j + 1]), 0.0) * q_i)
        att = _dot(jnp.concatenate(ps, 0).astype(BF16), ones_dk)
        off = []
        for blk in range(n_sub):
            b0 = blk * SUB
            if direction == 0 and blk > 0:
                a_ref, kr = a_cum[b0 - 1:b0], slice(0, b0)
            elif direction == 1 and blk < n_sub - 1:
                a_ref, kr = a_cum[b0 + SUB:b0 + SUB + 1], slice(b0 + SUB, t_len)
            else:
                off.append(None)
                continue
            qt = (qf[b0:b0 + SUB] * jnp.exp2(a_cum[b0:b0 + SUB] - a_ref)).astype(BF16)
            kt = jnp.exp2(a_ref - a_key[kr]).astype(BF16)
            off.append((_dot_nt(qt, kt), kr))
        yield
        st_scr[s] = jnp.exp2(a_tot) * st + upd
        outs = []
        for blk in range(n_sub):
            b0 = blk * SUB
            o_i = inter[b0:b0 + SUB]
            for j in range(SUB):
                r = (blk * SUB + j) * SUB
                o_i = o_i + att[r:r + SUB] * vf[b0 + j:b0 + j + 1]
            if off[blk] is not None:
                att_off, kr = off[blk]
                outs.append((o_i, _dot(att_off.astype(BF16), vc[kr])))
            else:
                outs.append((o_i, None))
        yield
        o_out[rows, col] = jnp.concatenate([o_i if o_off is None else o_i + o_off for o_i, o_off in outs], 0)

    def step(c, carry):
        _run_interleaved([run_stream(c, direction, hd) for direction in range(2) for hd in range(C_HEADS)])
        return carry

    lax.fori_loop(0, n_chunks, step, 0)

    blk_rows = min(256, seq_len)
    ng = ng_ref[...]

    def finish(i, carry):
        rows = pl.ds(pl.multiple_of(i * blk_rows, blk_rows), blk_rows)
        osum = of_scr[rows, :] + ob_scr[rows, :]
        parts = [_rms_lastdim(osum[:, hd * C_DV:(hd + 1) * C_DV], ng[:, hd * C_DV:(hd + 1) * C_DV])
                 for hd in range(C_HEADS)]
        y_ref[rows, :] = (jnp.concatenate(parts, 1) * _silu(cg_ref[rows, :])).astype(BF16)
        return carry

    lax.fori_loop(0, seq_len // blk_rows, finish, 0)

    if emit_state:
        for s in range(n_streams):
            so_ref[0, s // C_HEADS, s % C_HEADS] = st_scr[s].T


def _hgrn(q, ff, fb, v, cg, lb_logits, norm_g, layer, init, batch, seq_len, emit_state):
    n = q.shape[0]
    row = lambda width: pl.BlockSpec((seq_len, width), lambda b: (b, 0))
    const = lambda a: pl.BlockSpec(a.shape, lambda b: (0,) * a.ndim)
    ng = norm_g.reshape(1, C_HEADS * C_DV)
    state_spec = pl.BlockSpec((1, 2, C_HEADS, C_DK, C_DV), lambda b: (b, 0, 0, 0, 0))
    in_specs = [row(512)] * 5 + [const(lb_logits), const(ng)]
    args = [q, ff, fb, v, cg, lb_logits, ng]
    if init is not None:
        in_specs.append(state_spec)
        args.append(init)
    out_specs = [row(512)]
    out_shape = [jax.ShapeDtypeStruct((n, 512), BF16)]
    if emit_state:
        out_specs.append(state_spec)
        out_shape.append(jax.ShapeDtypeStruct((batch, 2, C_HEADS, C_DK, C_DV), F32))
    return pl.pallas_call(
        functools.partial(_hgrn_kernel, seq_len=seq_len, layer=layer, has_init=init is not None,
                          emit_state=emit_state),
        grid=(batch,),
        in_specs=in_specs,
        out_specs=out_specs,
        out_shape=out_shape,
        scratch_shapes=[pltpu.VMEM((2 * C_HEADS, C_DV, C_DK), F32),
                        pltpu.VMEM((seq_len, 512), F32),
                        pltpu.VMEM((seq_len, 512), F32)],
        compiler_params=_cparams(("parallel",)),
        name="hgrn2_scan",
    )(*args)


def _prep_w_in_even(w):
    d = w.shape[0]
    a, bq, bk, bv = w[:, :768], w[:, 768:1280], w[:, 1280:1792], w[:, 1792:2304]
    bg, bo = w[:, 2304:2320], w[:, 2320:2832]
    pad = jnp.zeros((d, 128 - bg.shape[1]), BF16)
    return jnp.concatenate([t.astype(BF16) for t in (a, bq, bk, bv, bo, bg)] + [pad], 1)


def _prep_ctx_kv(k_ctx, v_ctx):
    k_t = jnp.transpose(k_ctx, (0, 2, 3, 1)).astype(BF16)
    v_t = jnp.transpose(v_ctx, (0, 2, 3, 1)).astype(BF16)
    ones = jnp.ones(k_t.shape[:2] + (128, k_t.shape[3]), BF16)
    return jnp.concatenate([k_t, v_t, v_t, ones], 2)


def kernel(x_prompt, x_sample, c, cache_a_k, cache_a_v, state_b_C, state_b_n, state_b_m, state_c_S, cache_d_k, cache_d_v, c_ctx, ada_w, ada_b, ln_g, ln_b, ffn_w1, ffn_w3, ffn_w2, w_in_even, w_out_even, a_sink, b_gate_bias, b_norm_g, w_in_odd, w_out_odd, c_lb_logits, c_norm_g, d_q_norm, d_k_norm):
    batch_p, len_p, d = x_prompt.shape
    batch_s, len_s, _ = x_sample.shape

    cvec = jnp.concatenate([c_ctx[None], c, jnp.zeros((MOD_ROWS - 1 - batch_s, d), F32)], 0)
    mod_all = _modulation(cvec, ada_w, ada_b)

    groups = [
        dict(x=x_prompt.reshape(batch_p * len_p, d), batch=batch_p, seq=len_p, prompt=True,
             mod_index=_mod_index(0, None)),
        dict(x=x_sample.reshape(batch_s * len_s, d), batch=batch_s, seq=len_s, prompt=False,
             mod_index=_mod_index(1, len_s // TOKEN_TILE)),
    ]
    new = {}
    ffn_w = (ffn_w1.astype(BF16), ffn_w3.astype(BF16), ffn_w2.astype(BF16))

    for l in range(DEPTH):
        mod = mod_all[l].reshape(MOD_ROWS, 9, d)
        i = l // 2
        if l % 2 == 0:
            w_in = _prep_w_in_even(w_in_even[i])
            w_out = w_out_even[i].astype(BF16)
        else:
            w_in = w_in_odd[i].astype(BF16)
            w_out = w_out_odd[i].astype(BF16)

        for grp in groups:
            x, mi, nb, sl, prompt = grp["x"], grp["mod_index"], grp["batch"], grp["seq"], grp["prompt"]
            x = _ffn(x, mod, 0, ln_g[l, 0], ln_b[l, 0], *ffn_w, l, 0, mi)
            if l % 2 == 0:
                outs = _even_in(x, mod, w_in, mi, sl, rope=not prompt, cache=prompt)
                if prompt:
                    aq, akv, k_cache, v_cache, bq, bk, bv, bo, bg = outs
                    new["a_k"], new["a_v"] = _cache_layout(k_cache), _cache_layout(v_cache)
                    ya = _attention(aq, akv, None, a_sink[i], nb, sl, tq=Q_BLOCK, q_blocks=sl // Q_BLOCK, banded=False)
                    yb, c_new, n_new, m_new = _mlstm(bq, bk, bv, bg, bo, b_gate_bias[i], b_norm_g[i], None,
                                                     nb, sl, emit_state=True)
                    new["b_C"] = c_new[:, None]
                    new["b_n"] = n_new.reshape(nb, 1, 2, B_HEADS, B_DIM)
                    new["b_m"] = m_new[:, :, 0].reshape(nb, 1, 2, B_HEADS)
                else:
                    aq, akv, bq, bk, bv, bo, bg = outs
                    ya = _attention(aq, akv, _prep_ctx_kv(cache_a_k[:, i], cache_a_v[:, i]), a_sink[i], nb, sl,
                                    tq=Q_BLOCK, q_blocks=4, banded=True)
                    init = (state_b_C[:, i], state_b_n[:, i], state_b_m[:, i])
                    yb, = _mlstm(bq, bk, bv, bg, bo, b_gate_bias[i], b_norm_g[i], init, nb, sl, emit_state=False)
                mixed = (ya, yb)
            else:
                outs = _odd_in(x, mod, w_in, d_q_norm[i], d_k_norm[i], mi, sl, rope=not prompt, cache=prompt)
                if prompt:
                    cq, ff, fb, cv, cg, dq, dkv, k_cache, v_cache = outs
                    new["d_k"], new["d_v"] = _cache_layout(k_cache), _cache_layout(v_cache)
                    yc, s_new = _hgrn(cq, ff, fb, cv, cg, c_lb_logits, c_norm_g[i], l, None, nb, sl, emit_state=True)
                    new["c_S"] = s_new[:, None]
                    yd = _attention(dq, dkv, None, None, nb, sl, tq=Q_BLOCK, q_blocks=sl // Q_BLOCK, banded=False)
                else:
                    cq, ff, fb, cv, cg, dq, dkv = outs
                    yc, = _hgrn(cq, ff, fb, cv, cg, c_lb_logits, c_norm_g[i], l, state_c_S[:, i], nb, sl,
                                emit_state=False)
                    yd = _attention(dq, dkv, _prep_ctx_kv(cache_d_k[:, i], cache_d_v[:, i]), None, nb, sl,
                                    tq=Q_BLOCK, q_blocks=2, banded=False)
                mixed = (yc, yd)
            x = _ffn(x, mod, 2, ln_g[l, 2], ln_b[l, 2], *ffn_w, l, 1, mi,
                     mixer=mixed + (w_out, ln_g[l, 1], ln_b[l, 1]))
            grp["x"] = x

    y_prompt = groups[0]["x"].reshape(batch_p, len_p, d)
    y_sample = groups[1]["x"].reshape(batch_s, len_s, d)
    return (y_prompt, y_sample, new["a_k"], new["a_v"], new["b_C"], new["b_n"], new["b_m"], new["c_S"],
            new["d_k"], new["d_v"])
```

```python
import functools

import jax
import jax.numpy as jnp
import numpy as np
from jax import lax
from jax.experimental import pallas as pl
from jax.experimental.pallas import tpu as pltpu

F32 = jnp.float32
BF16 = jnp.bfloat16

D_MODEL = 1024
DEPTH = 2
GRID_W = 64
HEAD_DIM = 64
A_HEADS = 8
A_KV = 2
WINDOW = 128
B_HEADS = 4
B_DIM = 128
C_HEADS = 4
C_DK = 128
C_DV = 128
D_HEADS = 8
D_KV = 2
D_FF = 2816
Q_BLOCK = 128
ROPE_THETA = 10000.0
ALPHA = (2 * DEPTH) ** 0.25
NEG_INF = -1e30
LOG2_E = 1.4426950408889634
QK_SCALE = HEAD_DIM ** -0.5 * LOG2_E

MOD_ROWS = 8
FF_CHUNK = 256
TOKEN_TILE = 512
FFN_TILE = 1024
FFN_ROWS = 512
MLSTM_T = 64
HGRN_T = 64
SUB = 8
ATT_KEY_TILE = 512
VMEM_LIMIT = 56 * 1024 * 1024


def _cparams(sem):
    return pltpu.CompilerParams(dimension_semantics=sem, vmem_limit_bytes=VMEM_LIMIT)


def _dot(a, b):
    return jnp.dot(a, b, preferred_element_type=F32)


def _dot_nt(a, b):
    return lax.dot_general(a, b, (((1,), (1,)), ((), ())), preferred_element_type=F32)


def _dot_tn(a, b):
    return lax.dot_general(a, b, (((0,), (0,)), ((), ())), preferred_element_type=F32)


def _split3(x):
    hi = x.astype(BF16)
    r1 = x - hi.astype(F32)
    mid = r1.astype(BF16)
    lo = (r1 - mid.astype(F32)).astype(BF16)
    return hi, mid, lo


def _dot_sel(sel, x):
    hi, mid, lo = _split3(x)
    return _dot(sel, hi) + _dot(sel, mid) + _dot(sel, lo)


def _dot_sel_rhs(x, sel):
    hi = x.astype(BF16)
    lo = (x - hi.astype(F32)).astype(BF16)
    r = _dot(jnp.concatenate([hi, lo], 0), sel)
    return r[:x.shape[0]] + r[x.shape[0]:]


def _run_interleaved(gens):
    live = list(gens)
    while live:
        nxt = []
        for g in live:
            try:
                next(g)
                nxt.append(g)
            except StopIteration:
                pass
        live = nxt


def _run_staggered(gens):
    pending, live = list(gens), []
    while pending or live:
        if pending:
            live.insert(0, pending.pop(0))
        nxt = []
        for g in live:
            try:
                next(g)
                nxt.append(g)
            except StopIteration:
                pass
        live = nxt


def _silu(x):
    return x * jax.nn.sigmoid(x)


def _layernorm(z, g, b):
    mu = jnp.mean(z, -1, keepdims=True)
    zc = z - mu
    var = jnp.mean(zc * zc, -1, keepdims=True)
    return zc * lax.rsqrt(var + 1e-5) * g + b


def _rms_lastdim(x, g):
    return x * lax.rsqrt(jnp.mean(x * x, -1, keepdims=True) + 1e-6) * g


def _mod_index(group_start, request_len):
    if request_len is None:
        return lambda tm: (lambda i: (group_start, 0, 0))
    return lambda tm: (lambda i: (group_start + i // (request_len // tm), 0, 0))


def _mod_kernel(c_ref, w_ref, b_ref, o_ref):
    s = _silu(c_ref[...]).astype(BF16)
    o_ref[0] = _dot(s, w_ref[0].astype(BF16)) + b_ref[0]


def _modulation(cvec, ada_w, ada_b):
    depth, d, n = ada_w.shape
    tn = 1152
    return pl.pallas_call(
        _mod_kernel,
        grid=(depth, n // tn),
        in_specs=[pl.BlockSpec((MOD_ROWS, d), lambda l, j: (0, 0)),
                  pl.BlockSpec((1, d, tn), lambda l, j: (l, 0, j)),
                  pl.BlockSpec((1, 1, tn), lambda l, j: (l, 0, j))],
        out_specs=pl.BlockSpec((1, MOD_ROWS, tn), lambda l, j: (l, 0, j)),
        out_shape=jax.ShapeDtypeStruct((depth, MOD_ROWS, n), F32),
        compiler_params=_cparams(("parallel", "parallel")),
        name="modulation",
    )(cvec, ada_w, ada_b.reshape(depth, 1, n))


def _ffn_kernel(*refs, j, nf, mixer):
    it = iter(refs)
    x_ref, mod_ref = next(it), next(it)
    if mixer:
        ya_ref, yb_ref, wo_ref, g1_ref, b1_ref = (next(it) for _ in range(5))
    w1_ref, w3_ref, w2_ref, g_ref, b_ref, o_ref, h_ref, acc_ref = (next(it) for _ in range(8))
    m = mod_ref[0]
    shift, scale, gate = m[3 * j:3 * j + 1], m[3 * j + 1:3 * j + 2], m[3 * j + 2:3 * j + 3]
    blocks = [slice(r, r + FFN_ROWS) for r in range(0, x_ref.shape[0], FFN_ROWS)]
    if mixer:
        res_ref = next(it)
        half = ya_ref.shape[1]
        for rows in blocks:
            y = _dot(ya_ref[rows, :], wo_ref[:half, :]) + _dot(yb_ref[rows, :], wo_ref[half:, :])
            res_ref[rows, :] = _layernorm(ALPHA * x_ref[rows, :] + m[5:6] * y, g1_ref[...], b1_ref[...])
        x_ref = res_ref
    for rows in blocks:
        h_ref[rows, :] = (x_ref[rows, :] * (1.0 + scale) + shift).astype(BF16)
    for rows in blocks:
        for f in range(nf):
            cols = slice(f * FF_CHUNK, (f + 1) * FF_CHUNK)
            h = h_ref[rows, :]
            u = (_silu(_dot(h, w1_ref[0, 0, :, cols])) * _dot(h, w3_ref[0, 0, :, cols])).astype(BF16)
            y = _dot(u, w2_ref[0, 0, cols, :])
            if f == 0:
                acc_ref[rows, :] = y
            else:
                acc_ref[rows, :] += y
        z = ALPHA * x_ref[rows, :] + 0.5 * gate * acc_ref[rows, :]
        o_ref[rows, :] = _layernorm(z, g_ref[...], b_ref[...])


def _ffn(x, mod, j, g, b, w1, w3, w2, layer, which, mod_index, mixer=None):
    n, d = x.shape
    nf = w1.shape[-1] // FF_CHUNK
    tm = FFN_TILE
    row = lambda width: pl.BlockSpec((tm, width), lambda i: (i, 0))
    vec = pl.BlockSpec((1, d), lambda i: (0, 0))
    whole = lambda a: pl.BlockSpec((1, 1) + a.shape[2:], lambda i: (layer, which, 0, 0),
                                   pipeline_mode=pl.Buffered(1))
    in_specs = [row(d), pl.BlockSpec((1, 9, d), mod_index(tm))]
    args = [x, mod]
    scratch = [pltpu.VMEM((tm, d), BF16), pltpu.VMEM((tm, d), F32)]
    if mixer is not None:
        ya, yb, w_out, g1, b1 = mixer
        in_specs += [row(ya.shape[1]), row(yb.shape[1]),
                     pl.BlockSpec(w_out.shape, lambda i: (0, 0), pipeline_mode=pl.Buffered(1)), vec, vec]
        args += [ya, yb, w_out, g1.reshape(1, d), b1.reshape(1, d)]
        scratch.append(pltpu.VMEM((tm, d), F32))
    in_specs += [whole(w1), whole(w3), whole(w2), vec, vec]
    args += [w1, w3, w2, g.reshape(1, d), b.reshape(1, d)]
    return pl.pallas_call(
        functools.partial(_ffn_kernel, j=j, nf=nf, mixer=mixer is not None),
        grid=(n // tm,),
        in_specs=in_specs,
        out_specs=row(d),
        out_shape=jax.ShapeDtypeStruct((n, d), F32),
        scratch_shapes=scratch,
        compiler_params=_cparams(("parallel",)),
        name="ffn_sublayer",
    )(*args)


def _rope_tables(length):
    t = np.arange(length)
    nf = HEAD_DIM // 4
    inv = ROPE_THETA ** (-np.arange(nf, dtype=np.float64) / nf)
    ang_r = (t // GRID_W)[:, None] * inv[None]
    ang_c = (t % GRID_W)[:, None] * inv[None]
    cr, sr, cc, sc = np.cos(ang_r), np.sin(ang_r), np.cos(ang_c), np.sin(ang_c)
    z = np.zeros_like(cr)
    cos = np.concatenate([cr, cr, cc, cc], 1)
    sin_up = np.concatenate([-sr, z, -sc, z], 1)
    sin_dn = np.concatenate([z, sr, z, sc], 1)
    two = lambda a: jnp.asarray(np.concatenate([a, a], 1), F32)
    return two(cos), two(sin_up), two(sin_dn)


def _rope128(x, cos, sin_up, sin_dn):
    nf = HEAD_DIM // 4
    return x * cos + pltpu.roll(x, 128 - nf, 1) * sin_up + pltpu.roll(x, nf, 1) * sin_dn


def _rope(x, cos, sin_up, sin_dn):
    parts = [_rope128(x[:, c:c + 128], cos, sin_up, sin_dn) for c in range(0, x.shape[1], 128)]
    return parts[0] if len(parts) == 1 else jnp.concatenate(parts, 1)


EVEN_COLS = (512, 256, 512, 512, 512, 512, 128)


KV_WIDTH = 128 + A_KV * 256


def _store_cache(ck_ref, cv_ref, kv):
    seq_len = ck_ref.shape[2]
    for s in range(ck_ref.shape[0]):
        rows = slice(s * seq_len, (s + 1) * seq_len)
        ck_ref[s] = kv[rows, :128].T
        cv_ref[s] = kv[rows, 128:].T


def _in_proj_outputs(outs, n, tm, seq_len):
    specs, shapes = [], []
    for o in outs:
        if o == "cache":
            specs.append(pl.BlockSpec((tm // seq_len, 128, seq_len), lambda i: (i, 0, 0)))
            shapes.append(jax.ShapeDtypeStruct((n // seq_len, 128, seq_len), F32))
        else:
            specs.append(pl.BlockSpec((tm, o[0]), lambda i: (i, 0)))
            shapes.append(jax.ShapeDtypeStruct((n, o[0]), o[1]))
    return specs, shapes


def _cache_layout(c):
    nb, _, sl = c.shape
    return jnp.transpose(c.reshape(nb, 1, A_KV, HEAD_DIM, sl), (0, 1, 4, 2, 3))


def _kv_with_ones(kv):
    ones = jnp.ones((kv.shape[0], 128), kv.dtype)
    v0, v1 = kv[:, 128:192], kv[:, 192:256]
    return jnp.concatenate([kv[:, :128], v0, v0, ones, v1, v1, ones], 1)


def _even_in_kernel(*refs, rope, cache):
    it = iter(refs)
    x_ref, mod_ref, w_ref = next(it), next(it), next(it)
    tabs = (next(it), next(it), next(it)) if rope else None
    aq_ref, akv_ref = next(it), next(it)
    cache_refs = (next(it), next(it)) if cache else None
    bq_ref, bk_ref, bv_ref, bo_ref, bg_ref = next(it), next(it), next(it), next(it), next(it)

    m = mod_ref[0]
    h = (x_ref[...] * (1.0 + m[4:5]) + m[3:4]).astype(BF16)
    offs = [0]
    for c in EVEN_COLS:
        offs.append(offs[-1] + c)
    proj = lambda k: _dot(h, w_ref[:, offs[k]:offs[k + 1]])

    aq = proj(0)
    akv = proj(1)
    if cache:
        _store_cache(*cache_refs, akv)
    if rope:
        cos, s_up, s_dn = (t[...] for t in tabs)
        aq = _rope(aq, cos, s_up, s_dn)
        akv = jnp.concatenate([_rope(akv[:, :128], cos, s_up, s_dn), akv[:, 128:]], 1)
    aq_ref[...] = (aq * QK_SCALE).astype(BF16)
    akv_ref[...] = _kv_with_ones(akv).astype(BF16)
    bq_ref[...] = proj(2).astype(BF16)
    bk_ref[...] = (proj(3) * (B_DIM ** -0.5)).astype(BF16)
    bv_ref[...] = proj(4).astype(BF16)
    bo_ref[...] = proj(5)
    bg_ref[...] = proj(6)


def _even_in(x, mod, w, mod_index, seq_len, rope, cache):
    n, d = x.shape
    tm = TOKEN_TILE
    row = lambda width: pl.BlockSpec((tm, width), lambda i: (i, 0))
    in_specs = [row(d), pl.BlockSpec((1, 9, d), mod_index(tm)),
                pl.BlockSpec(w.shape, lambda i: (0, 0), pipeline_mode=pl.Buffered(1))]
    args = [x, mod, w]
    if rope:
        per_seq = seq_len // tm
        tabs = _rope_tables(seq_len)
        in_specs += [pl.BlockSpec((tm, 128), lambda i: (i % per_seq, 0))] * 3
        args += list(tabs)
    outs = [(512, BF16), (KV_WIDTH, BF16)] + (["cache", "cache"] if cache else []) + \
           [(512, BF16), (512, BF16), (512, BF16), (512, F32), (128, F32)]
    out_specs, out_shape = _in_proj_outputs(outs, n, tm, seq_len)
    return pl.pallas_call(
        functools.partial(_even_in_kernel, rope=rope, cache=cache),
        grid=(n // tm,),
        in_specs=in_specs,
        out_specs=out_specs,
        out_shape=out_shape,
        compiler_params=_cparams(("parallel",)),
        name="even_in_proj",
    )(*args)


ODD_COLS = (512, 512, 512, 512, 512, 512, 256)


def _head_rms(x, seg_ref, g):
    x2 = x * x
    hi = x2.astype(BF16)
    lo = (x2 - hi.astype(F32)).astype(BF16)
    w = seg_ref.shape[0]
    ms = [_dot(hi[:, c:c + w], seg_ref[...]) + _dot(lo[:, c:c + w], seg_ref[...]) for c in range(0, x.shape[1], w)]
    ms = ms[0] if len(ms) == 1 else jnp.concatenate(ms, 1)
    return x * lax.rsqrt(ms + 1e-6) * g


def _odd_in_kernel(*refs, rope, cache):
    it = iter(refs)
    x_ref, mod_ref, w_ref, segq_ref, segk_ref, qn_ref, kn_ref = (next(it) for _ in range(7))
    tabs = (next(it), next(it), next(it)) if rope else None
    q_ref, ff_ref, fb_ref, v_ref, cg_ref, dq_ref, dkv_ref = (next(it) for _ in range(7))
    cache_refs = (next(it), next(it)) if cache else None

    m = mod_ref[0]
    h = (x_ref[...] * (1.0 + m[4:5]) + m[3:4]).astype(BF16)
    offs = [0]
    for c in ODD_COLS:
        offs.append(offs[-1] + c)
    proj = lambda k: _dot(h, w_ref[:, offs[k]:offs[k + 1]])

    q_ref[...] = _silu(proj(0)).astype(BF16)
    ff_ref[...] = proj(1)
    fb_ref[...] = proj(2)
    v_ref[...] = proj(3).astype(BF16)
    cg_ref[...] = proj(4)
    dq = _head_rms(proj(5), segq_ref, qn_ref[...])
    dkv = proj(6)
    dk = _head_rms(dkv[:, :128], segk_ref, kn_ref[...])
    dv = dkv[:, 128:]
    if cache:
        _store_cache(*cache_refs, jnp.concatenate([dk, dv], 1))
    if rope:
        cos, s_up, s_dn = (t[...] for t in tabs)
        dq = _rope(dq, cos, s_up, s_dn)
        dk = _rope(dk, cos, s_up, s_dn)
    dq_ref[...] = (dq * QK_SCALE).astype(BF16)
    dkv_ref[...] = _kv_with_ones(jnp.concatenate([dk, dv], 1)).astype(BF16)


def _segment_mean_matrix(width):
    r = jnp.arange(width) // HEAD_DIM
    return jnp.where(r[:, None] == r[None, :], 1.0 / HEAD_DIM, 0.0).astype(BF16)


def _odd_in(x, mod, w, q_norm, k_norm, mod_index, seq_len, rope, cache):
    n, d = x.shape
    tm = TOKEN_TILE
    row = lambda width: pl.BlockSpec((tm, width), lambda i: (i, 0))
    const = lambda a: pl.BlockSpec(a.shape, lambda i: (0, 0))
    segq, segk = _segment_mean_matrix(256), _segment_mean_matrix(128)
    qn = jnp.tile(q_norm, D_HEADS).reshape(1, 512)
    kn = jnp.tile(k_norm, D_KV).reshape(1, 128)
    in_specs = [row(d), pl.BlockSpec((1, 9, d), mod_index(tm)),
                pl.BlockSpec(w.shape, lambda i: (0, 0), pipeline_mode=pl.Buffered(1)),
                const(segq), const(segk), const(qn), const(kn)]
    args = [x, mod, w, segq, segk, qn, kn]
    if rope:
        per_seq = seq_len // tm
        tabs = _rope_tables(seq_len)
        in_specs += [pl.BlockSpec((tm, 128), lambda i: (i % per_seq, 0))] * 3
        args += list(tabs)
    outs = [(512, BF16), (512, F32), (512, F32), (512, BF16), (512, F32), (512, BF16), (KV_WIDTH, BF16)] + \
           (["cache", "cache"] if cache else [])
    out_specs, out_shape = _in_proj_outputs(outs, n, tm, seq_len)
    return pl.pallas_call(
        functools.partial(_odd_in_kernel, rope=rope, cache=cache),
        grid=(n // tm,),
        in_specs=in_specs,
        out_specs=out_specs,
        out_shape=out_shape,
        compiler_params=_cparams(("parallel",)),
        name="odd_in_proj",
    )(*args)


def _attn_kernel(*refs, seq_len, tq, q_blocks, n_ctx, banded, has_sink):
    it = iter(refs)
    q_ref, kv_ref = next(it), next(it)
    ckv_ref = next(it) if n_ctx else None
    sink_ref = next(it) if has_sink else None
    o_ref = next(it)
    groups = A_HEADS // A_KV
    rows = groups * tq
    low_half = lax.broadcasted_iota(jnp.int32, (tq, 2 * HEAD_DIM), 1) < HEAD_DIM

    def run_kv_head(blk, kh):
        r0 = blk * tq
        j = pl.program_id(1) * q_blocks + blk
        kcol = slice(kh * HEAD_DIM, (kh + 1) * HEAD_DIM)
        vcol = slice(128 + kh * 256, 128 + (kh + 1) * 256)
        qs = jnp.concatenate([q_ref[r0:r0 + tq, (kh * groups + g) * HEAD_DIM:(kh * groups + g + 1) * HEAD_DIM]
                              for g in range(groups)], axis=0)
        tiles = []
        if banded:
            span = tq + 2 * WINDOW
            start = pl.multiple_of(jnp.clip(j * tq - WINDOW, 0, seq_len - span), WINDOW)
            qpos = j * tq + (lax.broadcasted_iota(jnp.int32, (rows, span), 0) & (tq - 1))
            kpos = start + lax.broadcasted_iota(jnp.int32, (rows, span), 1)
            band = jnp.abs(kpos - qpos) <= WINDOW
            tiles.append((kv_ref[pl.ds(start, span), kcol], kv_ref[pl.ds(start, span), vcol], band))
        else:
            tk = min(ATT_KEY_TILE, seq_len)
            for t in range(seq_len // tk):
                tiles.append((kv_ref[t * tk:(t + 1) * tk, kcol], kv_ref[t * tk:(t + 1) * tk, vcol], None))
        scores = [_dot_nt(qs, k_t) for k_t, _, _ in tiles]
        if n_ctx:
            tiles.append((None, ckv_ref[0, kh, HEAD_DIM:, :], None))
            scores.append(_dot(qs, ckv_ref[0, kh, :HEAD_DIM, :]))
        yield
        scores = [s if mask is None else jnp.where(mask, s, NEG_INF) for s, (_, _, mask) in zip(scores, tiles)]
        m = functools.reduce(jnp.maximum, [jnp.max(s, -1, keepdims=True) for s in scores])
        if has_sink:
            sink = jnp.concatenate([jnp.full((tq, 1), sink_ref[kh * groups + g] * LOG2_E, F32)
                                    for g in range(groups)], 0)
            m = jnp.maximum(m, sink)
        pv = functools.reduce(lambda a, b: a + b,
                              [(_dot_nt if k_t is None else _dot)(jnp.exp2(s - m).astype(BF16), v_t)
                               for s, (k_t, v_t, _) in zip(scores, tiles)])
        yield
        total = pv[:, 2 * HEAD_DIM:]
        if has_sink:
            total = total + jnp.exp2(sink - m)
        o = pv[:, :2 * HEAD_DIM] / total
        for g in range(0, groups, 2):
            pair = jnp.where(low_half, o[g * tq:(g + 1) * tq], o[(g + 1) * tq:(g + 2) * tq])
            c0 = (kh * groups + g) * HEAD_DIM
            o_ref[r0:r0 + tq, c0:c0 + 2 * HEAD_DIM] = pair.astype(BF16)

    run = _run_staggered if seq_len + n_ctx > 2 * ATT_KEY_TILE else _run_interleaved
    run([run_kv_head(blk, kh) for blk in range(q_blocks) for kh in range(A_KV)])


def _attention(q, kv, ctx_kv, sink, batch, seq_len, tq, q_blocks, banded):
    n_ctx = 0 if ctx_kv is None else ctx_kv.shape[-1]
    per_seq = seq_len // (tq * q_blocks)
    in_specs = [pl.BlockSpec((q_blocks * tq, 512), lambda b, j: (b * per_seq + j, 0)),
                pl.BlockSpec((seq_len, KV_WIDTH), lambda b, j: (b, 0))]
    args = [q, kv]
    if n_ctx:
        in_specs.append(pl.BlockSpec((1,) + ctx_kv.shape[1:], lambda b, j: (b, 0, 0, 0)))
        args.append(ctx_kv)
    if sink is not None:
        in_specs.append(pl.BlockSpec(memory_space=pltpu.SMEM))
        args.append(sink.reshape(-1).astype(F32))
    return pl.pallas_call(
        functools.partial(_attn_kernel, seq_len=seq_len, tq=tq, q_blocks=q_blocks, n_ctx=n_ctx, banded=banded,
                          has_sink=sink is not None),
        grid=(batch, per_seq),
        in_specs=in_specs,
        out_specs=pl.BlockSpec((q_blocks * tq, 512), lambda b, j: (b * per_seq + j, 0)),
        out_shape=jax.ShapeDtypeStruct(q.shape, BF16),
        compiler_params=_cparams(("parallel", "arbitrary")),
        name="gqa_attention",
    )(*args)


def _log_sigmoid(x):
    return jnp.minimum(x, 0.0) - jnp.log1p(jnp.exp(-jnp.abs(x)))


def _mlstm_kernel(*refs, seq_len, has_init, emit_state):
    it = iter(refs)
    q_ref, k_ref, v_ref, g_ref, bo_ref, gb_ref, ng_ref = (next(it) for _ in range(7))
    c0_ref, n0_ref, m0_ref = (next(it), next(it), next(it)) if has_init else (None, None, None)
    y_ref = next(it)
    co_ref, no_ref, mo_ref = (next(it), next(it), next(it)) if emit_state else (None, None, None)
    c_scr, n_scr, m_scr, hf_scr, hb_scr, ig_buf, bc_buf, ut_buf = (next(it) for _ in range(8))

    t_len = MLSTM_T
    n_chunks = seq_len // t_len
    n_streams = 2 * B_HEADS

    for s in range(n_streams):
        if has_init:
            c_scr[s] = c0_ref[0, s // B_HEADS, s % B_HEADS]
        else:
            c_scr[s] = jnp.zeros((B_DIM, B_DIM), F32)
    if has_init:
        n_scr[...] = n0_ref[0]
        m_scr[...] = m0_ref[0]
    else:
        n_scr[...] = jnp.zeros_like(n_scr)
        m_scr[...] = jnp.zeros_like(m_scr)

    ri = lax.broadcasted_iota(jnp.int32, (t_len, t_len), 0)
    ci = lax.broadcasted_iota(jnp.int32, (t_len, t_len), 1)
    lower = ci <= ri
    upper = ci >= ri
    tri_f = jnp.where(lower, 1.0, 0.0).astype(BF16)
    tri_b = jnp.where(upper, 1.0, 0.0).astype(BF16)
    gate_bias = gb_ref[...]
    sel_row = lax.broadcasted_iota(jnp.int32, (128, B_HEADS * B_DIM), 0)
    sel_head = lax.broadcasted_iota(jnp.int32, (128, B_HEADS * B_DIM), 1) // B_DIM
    gate_select = []
    for direction in range(2):
        i_base = 2 * direction * B_HEADS
        gate_select.append((jnp.where(sel_row == i_base + sel_head, 1.0, 0.0).astype(BF16),
                            jnp.where(sel_row == i_base + B_HEADS + sel_head, 1.0, 0.0).astype(BF16)))

    def chunk_rows(c, direction):
        chunk = c if direction == 0 else n_chunks - 1 - c
        return pl.ds(pl.multiple_of(chunk * t_len, t_len), t_len)

    def prepare_gates(c, direction):
        slot = c % 2
        rows = chunk_rows(jnp.minimum(c, n_chunks - 1), direction)
        tri = tri_f if direction == 0 else tri_b
        sel_i, sel_f = gate_select[direction]
        gates = g_ref[rows, :] + gate_bias
        hi, mid, lo = _split3(_log_sigmoid(gates))
        b3 = _dot(tri, jnp.concatenate([hi, mid, lo], 1))
        ig_buf[slot, direction] = _dot_sel_rhs(gates, sel_i)
        yield
        bc_all = b3[:, :128] + b3[:, 128:256] + b3[:, 256:]
        bc_rep = _dot_sel_rhs(bc_all, sel_f)
        ut_buf[slot, direction] = (gates - pltpu.roll(bc_all, 128 - B_HEADS, 1)).T
        yield
        bc_buf[slot, direction] = bc_rep

    def run_direction(c, direction):
        rows = chunk_rows(c, direction)
        slot = c % 2
        causal = lower if direction == 0 else upper
        last = t_len - 1 if direction == 0 else 0
        h_out = hf_scr if direction == 0 else hb_scr
        ig_rep, bc_rep, u_t = ig_buf[slot, direction], bc_buf[slot, direction], ut_buf[slot, direction]
        heads = []
        for hd in range(B_HEADS):
            s = direction * B_HEADS + hd
            col = slice(hd * B_DIM, (hd + 1) * B_DIM)
            qc, kc, vc = q_ref[rows, col], k_ref[rows, col], v_ref[rows, col]
            c_prev = c_scr[s]
            n_prev = n_scr[s:s + 1, :]
            qk = _dot_nt(qc, kc)
            q_state = _dot(qc, c_prev.astype(BF16))
            qn = _dot_nt(qc, jnp.broadcast_to(n_prev, (B_DIM, B_DIM)).astype(BF16))
            heads.append((s, col, qc, kc, vc, c_prev, n_prev, qk, q_state, qn))
        yield
        staged = []
        for (s, col, qc, kc, vc, c_prev, n_prev, qk, q_state, qn) in heads:
            i_col = 2 * direction * B_HEADS + (s % B_HEADS)
            ig, bc = ig_rep[:, col], bc_rep[:, col]
            dmat = jnp.where(causal, bc[:, :t_len] + u_t[i_col:i_col + 1, :], -jnp.inf)
            m_prev = m_scr[s:s + 1, :]
            inter = bc + m_prev
            m_t = jnp.maximum(inter, jnp.max(dmat, -1, keepdims=True))
            w = (jnp.exp(dmat - m_t[:, :t_len]) * qk).astype(BF16)
            wv = _dot(w, jnp.concatenate([vc, jnp.ones_like(vc)], 1))
            b_tot = bc[last:last + 1, :]
            g = b_tot - bc + ig
            m_new = jnp.maximum(b_tot + m_prev, jnp.max(g, 0, keepdims=True))
            kw = kc.astype(F32) * jnp.exp(g - m_new)
            upd = _dot_tn(kw.astype(BF16), vc)
            decay = jnp.exp(b_tot + m_prev - m_new)
            staged.append((s, col, inter, m_t, c_prev, n_prev, q_state, qn, wv, m_new, kw, upd, decay))
        yield
        for (s, col, inter, m_t, c_prev, n_prev, q_state, qn, wv, m_new, kw, upd, decay) in staged:
            a = jnp.exp(inter - m_t)
            num = a * q_state + wv[:, :B_DIM]
            den = a * qn + wv[:, B_DIM:]
            h_out[rows, col] = num / jnp.maximum(jnp.abs(den), jnp.exp(-m_t))
            c_scr[s] = decay * c_prev + upd
            n_scr[s:s + 1, :] = decay * n_prev + jnp.sum(kw, 0, keepdims=True)
            m_scr[s:s + 1, :] = m_new

    def chained(c, direction):
        yield from prepare_gates(c, direction)
        yield
        yield from run_direction(c, direction)

    def step(c, carry):
        _run_interleaved([chained(c, 0), chained(c, 1)])
        return carry

    lax.fori_loop(0, n_chunks, step, 0, unroll=2)

    blk = min(256, seq_len)
    ng = ng_ref[...]

    def finish(i, carry):
        rows = pl.ds(pl.multiple_of(i * blk, blk), blk)
        hsum = hf_scr[rows, :] + hb_scr[rows, :]
        gate = jax.nn.sigmoid(bo_ref[rows, :])
        parts = [_rms_lastdim(hsum[:, hd * B_DIM:(hd + 1) * B_DIM], ng[:, hd * B_DIM:(hd + 1) * B_DIM])
                 for hd in range(B_HEADS)]
        y_ref[rows, :] = (gate * jnp.concatenate(parts, 1)).astype(BF16)
        return carry

    lax.fori_loop(0, seq_len // blk, finish, 0)

    if emit_state:
        for s in range(n_streams):
            co_ref[0, s // B_HEADS, s % B_HEADS] = c_scr[s]
        no_ref[0] = n_scr[...]
        mo_ref[0] = m_scr[...]


def _mlstm(q, k, v, gates, bo, gate_bias, norm_g, init, batch, seq_len, emit_state):
    n = q.shape[0]
    row = lambda width: pl.BlockSpec((seq_len, width), lambda b: (b, 0))
    const = lambda a: pl.BlockSpec(a.shape, lambda b: (0,) * a.ndim)
    gb = jnp.zeros((1, 128), F32).at[0, :4 * B_HEADS].set(gate_bias.reshape(-1))
    ng = norm_g.reshape(1, B_HEADS * B_DIM)
    in_specs = [row(512), row(512), row(512), row(128), row(512), const(gb), const(ng)]
    args = [q, k, v, gates, bo, gb, ng]
    n_streams = 2 * B_HEADS
    if init is not None:
        c0, n0, m0 = init
        in_specs += [pl.BlockSpec((1, 2, B_HEADS, B_DIM, B_DIM), lambda b: (b, 0, 0, 0, 0)),
                     pl.BlockSpec((1, n_streams, B_DIM), lambda b: (b, 0, 0)),
                     pl.BlockSpec((1, n_streams, B_DIM), lambda b: (b, 0, 0))]
        args += [c0, n0.reshape(batch, n_streams, B_DIM),
                 jnp.broadcast_to(m0.reshape(batch, n_streams, 1), (batch, n_streams, B_DIM))]
    out_specs = [row(512)]
    out_shape = [jax.ShapeDtypeStruct((n, 512), BF16)]
    if emit_state:
        out_specs += [pl.BlockSpec((1, 2, B_HEADS, B_DIM, B_DIM), lambda b: (b, 0, 0, 0, 0)),
                      pl.BlockSpec((1, n_streams, B_DIM), lambda b: (b, 0, 0)),
                      pl.BlockSpec((1, n_streams, B_DIM), lambda b: (b, 0, 0))]
        out_shape += [jax.ShapeDtypeStruct((batch, 2, B_HEADS, B_DIM, B_DIM), F32),
                      jax.ShapeDtypeStruct((batch, n_streams, B_DIM), F32),
                      jax.ShapeDtypeStruct((batch, n_streams, B_DIM), F32)]
    return pl.pallas_call(
        functools.partial(_mlstm_kernel, seq_len=seq_len, has_init=init is not None, emit_state=emit_state),
        grid=(batch,),
        in_specs=in_specs,
        out_specs=out_specs,
        out_shape=out_shape,
        scratch_shapes=[pltpu.VMEM((n_streams, B_DIM, B_DIM), F32),
                        pltpu.VMEM((n_streams, B_DIM), F32),
                        pltpu.VMEM((n_streams, B_DIM), F32),
                        pltpu.VMEM((seq_len, 512), F32),
                        pltpu.VMEM((seq_len, 512), F32),
                        pltpu.VMEM((2, 2, MLSTM_T, B_HEADS * B_DIM), F32),
                        pltpu.VMEM((2, 2, MLSTM_T, B_HEADS * B_DIM), F32),
                        pltpu.VMEM((2, 2, 128, MLSTM_T), F32)],
        compiler_params=_cparams(("parallel",)),
        name="mlstm_scan",
    )(*args)


def _hgrn_kernel(*refs, seq_len, layer, has_init, emit_state):
    it = iter(refs)
    q_ref, ff_ref, fb_ref, v_ref, cg_ref, lbl_ref, ng_ref = (next(it) for _ in range(7))
    s0_ref = next(it) if has_init else None
    y_ref = next(it)
    so_ref = next(it) if emit_state else None
    st_scr, of_scr, ob_scr = next(it), next(it), next(it)

    t_len = HGRN_T
    n_sub = t_len // SUB
    n_chunks = seq_len // t_len
    n_streams = 2 * C_HEADS

    logits = lbl_ref[...]
    e = jnp.exp(logits - jnp.max(logits, 0, keepdims=True))
    sm = e / jnp.sum(e, 0, keepdims=True)
    lb = jnp.sum(sm[0:layer + 1], 0, keepdims=True) - sm[0:1]

    for s in range(n_streams):
        if has_init:
            st_scr[s] = s0_ref[0, s // C_HEADS, s % C_HEADS].T
        else:
            st_scr[s] = jnp.zeros((C_DV, C_DK), F32)

    ri = lax.broadcasted_iota(jnp.int32, (t_len, t_len), 0)
    ci = lax.broadcasted_iota(jnp.int32, (t_len, t_len), 1)
    tri_f = jnp.where(ci <= ri, 1.0, 0.0).astype(BF16)
    tri_b = jnp.where(ci >= ri, 1.0, 0.0).astype(BF16)
    sub_row = lax.broadcasted_iota(jnp.int32, (SUB, C_DK), 0)
    ones_dk = jnp.ones((C_DK, C_DK), BF16)

    def run_stream(c, direction, hd):
        chunk = c if direction == 0 else n_chunks - 1 - c
        rows = pl.ds(pl.multiple_of(chunk * t_len, t_len), t_len)
        f_ref = ff_ref if direction == 0 else fb_ref
        tri = tri_f if direction == 0 else tri_b
        last = t_len - 1 if direction == 0 else 0
        o_out = of_scr if direction == 0 else ob_scr
        s = direction * C_HEADS + hd
        col = slice(hd * C_DK, (hd + 1) * C_DK)
        lbh = lb[:, col]
        f = lbh + (1.0 - lbh) * jax.nn.sigmoid(f_ref[rows, col])
        log_k = jnp.log2(1.0 - f)
        hi, mid, lo = _split3(jnp.log2(f))
        a3 = _dot(tri, jnp.concatenate([hi, mid, lo], 1))
        yield
        a_cum = a3[:, :C_DK] + a3[:, C_DK:2 * C_DK] + a3[:, 2 * C_DK:]
        a_key = a_cum - log_k
        a_tot = a_cum[last:last + 1, :]
        qf = q_ref[rows, col].astype(F32)
        vc = v_ref[rows, col]
        vf = vc.astype(F32)
        st = st_scr[s]
        inter = _dot_nt((qf * jnp.exp2(a_cum)).astype(BF16), st.astype(BF16))
        upd = _dot_tn(vc, jnp.exp2(a_tot - a_key).astype(BF16))
        ps = []
        for blk in range(n_sub):
            b0 = blk * SUB
            a_i, q_i, ak_i = (t[b0:b0 + SUB] for t in (a_cum, qf, a_key))
            for j in range(SUB):
                seen = (sub_row >= j) if direction == 0 else (sub_row <= j)
                ps.append(jnp.where(seen, jnp.exp2(a_i - ak_i[j:j + 1]), 0.0) * q_i)
        att = _dot(jnp.concatenate(ps, 0).astype(BF16), ones_dk)
        off = []
        for blk in range(n_sub):
            b0 = blk * SUB
            if direction == 0 and blk > 0:
                a_ref, kr = a_cum[b0 - 1:b0], slice(0, b0)
            elif direction == 1 and blk < n_sub - 1:
                a_ref, kr = a_cum[b0 + SUB:b0 + SUB + 1], slice(b0 + SUB, t_len)
            else:
                off.append(None)
                continue
            qt = (qf[b0:b0 + SUB] * jnp.exp2(a_cum[b0:b0 + SUB] - a_ref)).astype(BF16)
            kt = jnp.exp2(a_ref - a_key[kr]).astype(BF16)
            off.append((_dot_nt(qt, kt), kr))
        yield
        st_scr[s] = jnp.exp2(a_tot) * st + upd
        outs = []
        for blk in range(n_sub):
            b0 = blk * SUB
            o_i = inter[b0:b0 + SUB]
            for j in range(SUB):
                r = (blk * SUB + j) * SUB
                o_i = o_i + att[r:r + SUB] * vf[b0 + j:b0 + j + 1]
            if off[blk] is not None:
                att_off, kr = off[blk]
                outs.append((o_i, _dot(att_off.astype(BF16), vc[kr])))
            else:
                outs.append((o_i, None))
        yield
        o_out[rows, col] = jnp.concatenate([o_i if o_off is None else o_i + o_off for o_i, o_off in outs], 0)

    def step(c, carry):
        _run_interleaved([run_stream(c, direction, hd) for direction in range(2) for hd in range(C_HEADS)])
        return carry

    lax.fori_loop(0, n_chunks, step, 0, unroll=2)

    blk_rows = min(256, seq_len)
    ng = ng_ref[...]

    def finish(i, carry):
        rows = pl.ds(pl.multiple_of(i * blk_rows, blk_rows), blk_rows)
        osum = of_scr[rows, :] + ob_scr[rows, :]
        parts = [_rms_lastdim(osum[:, hd * C_DV:(hd + 1) * C_DV], ng[:, hd * C_DV:(hd + 1) * C_DV])
                 for hd in range(C_HEADS)]
        y_ref[rows, :] = (jnp.concatenate(parts, 1) * _silu(cg_ref[rows, :])).astype(BF16)
        return carry

    lax.fori_loop(0, seq_len // blk_rows, finish, 0)

    if emit_state:
        for s in range(n_streams):
            so_ref[0, s // C_HEADS, s % C_HEADS] = st_scr[s].T


def _hgrn(q, ff, fb, v, cg, lb_logits, norm_g, layer, init, batch, seq_len, emit_state):
    n = q.shape[0]
    row = lambda width: pl.BlockSpec((seq_len, width), lambda b: (b, 0))
    const = lambda a: pl.BlockSpec(a.shape, lambda b: (0,) * a.ndim)
    ng = norm_g.reshape(1, C_HEADS * C_DV)
    state_spec = pl.BlockSpec((1, 2, C_HEADS, C_DK, C_DV), lambda b: (b, 0, 0, 0, 0))
    in_specs = [row(512)] * 5 + [const(lb_logits), const(ng)]
    args = [q, ff, fb, v, cg, lb_logits, ng]
    if init is not None:
        in_specs.append(state_spec)
        args.append(init)
    out_specs = [row(512)]
    out_shape = [jax.ShapeDtypeStruct((n, 512), BF16)]
    if emit_state:
        out_specs.append(state_spec)
        out_shape.append(jax.ShapeDtypeStruct((batch, 2, C_HEADS, C_DK, C_DV), F32))
    return pl.pallas_call(
        functools.partial(_hgrn_kernel, seq_len=seq_len, layer=layer, has_init=init is not None,
                          emit_state=emit_state),
        grid=(batch,),
        in_specs=in_specs,
        out_specs=out_specs,
        out_shape=out_shape,
        scratch_shapes=[pltpu.VMEM((2 * C_HEADS, C_DV, C_DK), F32),
                        pltpu.VMEM((seq_len, 512), F32),
                        pltpu.VMEM((seq_len, 512), F32)],
        compiler_params=_cparams(("parallel",)),
        name="hgrn2_scan",
    )(*args)


def _prep_w_in_even(w):
    d = w.shape[0]
    a, bq, bk, bv = w[:, :768], w[:, 768:1280], w[:, 1280:1792], w[:, 1792:2304]
    bg, bo = w[:, 2304:2320], w[:, 2320:2832]
    pad = jnp.zeros((d, 128 - bg.shape[1]), BF16)
    return jnp.concatenate([t.astype(BF16) for t in (a, bq, bk, bv, bo, bg)] + [pad], 1)


def _prep_ctx_kv(k_ctx, v_ctx):
    k_t = jnp.transpose(k_ctx, (0, 2, 3, 1)).astype(BF16)
    v_t = jnp.transpose(v_ctx, (0, 2, 3, 1)).astype(BF16)
    ones = jnp.ones(k_t.shape[:2] + (128, k_t.shape[3]), BF16)
    return jnp.concatenate([k_t, v_t, v_t, ones], 2)


def kernel(x_prompt, x_sample, c, cache_a_k, cache_a_v, state_b_C, state_b_n, state_b_m, state_c_S, cache_d_k, cache_d_v, c_ctx, ada_w, ada_b, ln_g, ln_b, ffn_w1, ffn_w3, ffn_w2, w_in_even, w_out_even, a_sink, b_gate_bias, b_norm_g, w_in_odd, w_out_odd, c_lb_logits, c_norm_g, d_q_norm, d_k_norm):
    batch_p, len_p, d = x_prompt.shape
    batch_s, len_s, _ = x_sample.shape

    cvec = jnp.concatenate([c_ctx[None], c, jnp.zeros((MOD_ROWS - 1 - batch_s, d), F32)], 0)
    mod_all = _modulation(cvec, ada_w, ada_b)

    groups = [
        dict(x=x_prompt.reshape(batch_p * len_p, d), batch=batch_p, seq=len_p, prompt=True,
             mod_index=_mod_index(0, None)),
        dict(x=x_sample.reshape(batch_s * len_s, d), batch=batch_s, seq=len_s, prompt=False,
             mod_index=_mod_index(1, len_s)),
    ]
    new = {}
    ffn_w = (ffn_w1.astype(BF16), ffn_w3.astype(BF16), ffn_w2.astype(BF16))

    for l in range(DEPTH):
        mod = mod_all[l].reshape(MOD_ROWS, 9, d)
        i = l // 2
        if l % 2 == 0:
            w_in = _prep_w_in_even(w_in_even[i])
            w_out = w_out_even[i].astype(BF16)
        else:
            w_in = w_in_odd[i].astype(BF16)
            w_out = w_out_odd[i].astype(BF16)

        for grp in groups:
            x, mi, nb, sl, prompt = grp["x"], grp["mod_index"], grp["batch"], grp["seq"], grp["prompt"]
            x = _ffn(x, mod, 0, ln_g[l, 0], ln_b[l, 0], *ffn_w, l, 0, mi)
            if l % 2 == 0:
                outs = _even_in(x, mod, w_in, mi, sl, rope=not prompt, cache=prompt)
                if prompt:
                    aq, akv, k_cache, v_cache, bq, bk, bv, bo, bg = outs
                    new["a_k"], new["a_v"] = _cache_layout(k_cache), _cache_layout(v_cache)
                    ya = _attention(aq, akv, None, a_sink[i], nb, sl, tq=Q_BLOCK, q_blocks=sl // Q_BLOCK, banded=False)
                    yb, c_new, n_new, m_new = _mlstm(bq, bk, bv, bg, bo, b_gate_bias[i], b_norm_g[i], None,
                                                     nb, sl, emit_state=True)
                    new["b_C"] = c_new[:, None]
                    new["b_n"] = n_new.reshape(nb, 1, 2, B_HEADS, B_DIM)
                    new["b_m"] = m_new[:, :, 0].reshape(nb, 1, 2, B_HEADS)
                else:
                    aq, akv, bq, bk, bv, bo, bg = outs
                    ya = _attention(aq, akv, _prep_ctx_kv(cache_a_k[:, i], cache_a_v[:, i]), a_sink[i], nb, sl,
                                    tq=Q_BLOCK, q_blocks=4, banded=True)
                    init = (state_b_C[:, i], state_b_n[:, i], state_b_m[:, i])
                    yb, = _mlstm(bq, bk, bv, bg, bo, b_gate_bias[i], b_norm_g[i], init, nb, sl, emit_state=False)
                mixed = (ya, yb)
            else:
                outs = _odd_in(x, mod, w_in, d_q_norm[i], d_k_norm[i], mi, sl, rope=not prompt, cache=prompt)
                if prompt:
                    cq, ff, fb, cv, cg, dq, dkv, k_cache, v_cache = outs
                    new["d_k"], new["d_v"] = _cache_layout(k_cache), _cache_layout(v_cache)
                    yc, s_new = _hgrn(cq, ff, fb, cv, cg, c_lb_logits, c_norm_g[i], l, None, nb, sl, emit_state=True)
                    new["c_S"] = s_new[:, None]
                    yd = _attention(dq, dkv, None, None, nb, sl, tq=Q_BLOCK, q_blocks=sl // Q_BLOCK, banded=False)
                else:
                    cq, ff, fb, cv, cg, dq, dkv = outs
                    yc, = _hgrn(cq, ff, fb, cv, cg, c_lb_logits, c_norm_g[i], l, state_c_S[:, i], nb, sl,
                                emit_state=False)
                    yd = _attention(dq, dkv, _prep_ctx_kv(cache_d_k[:, i], cache_d_v[:, i]), None, nb, sl,
                                    tq=Q_BLOCK, q_blocks=2, banded=False)
                mixed = (yc, yd)
            x = _ffn(x, mod, 2, ln_g[l, 2], ln_b[l, 2], *ffn_w, l, 1, mi,
                     mixer=mixed + (w_out, ln_g[l, 1], ln_b[l, 1]))
            grp["x"] = x

    y_prompt = groups[0]["x"].reshape(batch_p, len_p, d)
    y_sample = groups[1]["x"].reshape(batch_s, len_s, d)
    return (y_prompt, y_sample, new["a_k"], new["a_v"], new["b_C"], new["b_n"], new["b_m"], new["c_S"],
            new["d_k"], new["d_v"])
```

```python
import functools

import jax
import jax.numpy as jnp
import numpy as np
from jax import lax
from jax.experimental import pallas as pl
from jax.experimental.pallas import tpu as pltpu

F32 = jnp.float32
BF16 = jnp.bfloat16

D_MODEL = 1024
DEPTH = 2
GRID_W = 64
HEAD_DIM = 64
A_HEADS = 8
A_KV = 2
WINDOW = 128
B_HEADS = 4
B_DIM = 128
C_HEADS = 4
C_DK = 128
C_DV = 128
D_HEADS = 8
D_KV = 2
D_FF = 2816
Q_BLOCK = 128
ROPE_THETA = 10000.0
ALPHA = (2 * DEPTH) ** 0.25
NEG_INF = -1e30
LOG2_E = 1.4426950408889634
QK_SCALE = HEAD_DIM ** -0.5 * LOG2_E

MOD_ROWS = 8
FF_CHUNK = 256
TOKEN_TILE = 512
FFN_TILE = 512
FFN_ROWS = 512
MLSTM_T = 64
HGRN_T = 64
SUB = 8
ATT_KEY_TILE = 512
VMEM_LIMIT = 56 * 1024 * 1024


def _cparams(sem):
    return pltpu.CompilerParams(dimension_semantics=sem, vmem_limit_bytes=VMEM_LIMIT)


def _dot(a, b):
    return jnp.dot(a, b, preferred_element_type=F32)


def _dot_nt(a, b):
    return lax.dot_general(a, b, (((1,), (1,)), ((), ())), preferred_element_type=F32)


def _dot_tn(a, b):
    return lax.dot_general(a, b, (((0,), (0,)), ((), ())), preferred_element_type=F32)


def _split3(x):
    hi = x.astype(BF16)
    r1 = x - hi.astype(F32)
    mid = r1.astype(BF16)
    lo = (r1 - mid.astype(F32)).astype(BF16)
    return hi, mid, lo


def _dot_sel(sel, x):
    hi, mid, lo = _split3(x)
    return _dot(sel, hi) + _dot(sel, mid) + _dot(sel, lo)


def _dot_sel_rhs(x, sel):
    hi = x.astype(BF16)
    lo = (x - hi.astype(F32)).astype(BF16)
    r = _dot(jnp.concatenate([hi, lo], 0), sel)
    return r[:x.shape[0]] + r[x.shape[0]:]


def _run_interleaved(gens):
    live = list(gens)
    while live:
        nxt = []
        for g in live:
            try:
                next(g)
                nxt.append(g)
            except StopIteration:
                pass
        live = nxt


def _run_staggered(gens):
    pending, live = list(gens), []
    while pending or live:
        if pending:
            live.insert(0, pending.pop(0))
        nxt = []
        for g in live:
            try:
                next(g)
                nxt.append(g)
            except StopIteration:
                pass
        live = nxt


def _silu(x):
    return x * jax.nn.sigmoid(x)


def _layernorm(z, g, b):
    mu = jnp.mean(z, -1, keepdims=True)
    zc = z - mu
    var = jnp.mean(zc * zc, -1, keepdims=True)
    return zc * lax.rsqrt(var + 1e-5) * g + b


def _rms_lastdim(x, g):
    return x * lax.rsqrt(jnp.mean(x * x, -1, keepdims=True) + 1e-6) * g


def _mod_index(group_start, request_len):
    if request_len is None:
        return lambda tm: (lambda i: (group_start, 0, 0))
    return lambda tm: (lambda i: (group_start + i // (request_len // tm), 0, 0))


def _mod_kernel(c_ref, w_ref, b_ref, o_ref):
    s = _silu(c_ref[...]).astype(BF16)
    o_ref[0] = _dot(s, w_ref[0].astype(BF16)) + b_ref[0]


def _modulation(cvec, ada_w, ada_b):
    depth, d, n = ada_w.shape
    tn = 1152
    return pl.pallas_call(
        _mod_kernel,
        grid=(depth, n // tn),
        in_specs=[pl.BlockSpec((MOD_ROWS, d), lambda l, j: (0, 0)),
                  pl.BlockSpec((1, d, tn), lambda l, j: (l, 0, j)),
                  pl.BlockSpec((1, 1, tn), lambda l, j: (l, 0, j))],
        out_specs=pl.BlockSpec((1, MOD_ROWS, tn), lambda l, j: (l, 0, j)),
        out_shape=jax.ShapeDtypeStruct((depth, MOD_ROWS, n), F32),
        compiler_params=_cparams(("parallel", "parallel")),
        name="modulation",
    )(cvec, ada_w, ada_b.reshape(depth, 1, n))


def _ffn_kernel(*refs, j, nf, mixer, in_proj):
    it = iter(refs)
    x_ref, mod_ref = next(it), next(it)
    if mixer:
        ya_ref, yb_ref, wo_ref, g1_ref, b1_ref = (next(it) for _ in range(5))
    w1_ref, w3_ref, w2_ref, g_ref, b_ref = (next(it) for _ in range(5))
    proj_ins = [next(it) for _ in range(in_proj[1])] if in_proj else []
    o_ref = next(it)
    proj_outs = [next(it) for _ in range(in_proj[2])] if in_proj else []
    h_ref, acc_ref = next(it), next(it)
    m = mod_ref[0]
    shift, scale, gate = m[3 * j:3 * j + 1], m[3 * j + 1:3 * j + 2], m[3 * j + 2:3 * j + 3]
    blocks = [slice(r, r + FFN_ROWS) for r in range(0, x_ref.shape[0], FFN_ROWS)]
    if mixer:
        res_ref = next(it)
        half = ya_ref.shape[1]
        for rows in blocks:
            y = _dot(ya_ref[rows, :], wo_ref[:half, :]) + _dot(yb_ref[rows, :], wo_ref[half:, :])
            res_ref[rows, :] = _layernorm(ALPHA * x_ref[rows, :] + m[5:6] * y, g1_ref[...], b1_ref[...])
        x_ref = res_ref
    for rows in blocks:
        h_ref[rows, :] = (x_ref[rows, :] * (1.0 + scale) + shift).astype(BF16)
    for rows in blocks:
        for f in range(nf):
            cols = slice(f * FF_CHUNK, (f + 1) * FF_CHUNK)
            h = h_ref[rows, :]
            u = (_silu(_dot(h, w1_ref[0, 0, :, cols])) * _dot(h, w3_ref[0, 0, :, cols])).astype(BF16)
            y = _dot(u, w2_ref[0, 0, cols, :])
            if f == 0:
                acc_ref[rows, :] = y
            else:
                acc_ref[rows, :] += y
        z = ALPHA * x_ref[rows, :] + 0.5 * gate * acc_ref[rows, :]
        out = _layernorm(z, g_ref[...], b_ref[...])
        o_ref[rows, :] = out
        if in_proj:
            in_proj[0](out, m, proj_ins, proj_outs, in_proj[3], in_proj[4])


def _ffn(x, mod, j, g, b, w1, w3, w2, layer, which, mod_index, mixer=None, in_proj=None):
    n, d = x.shape
    nf = w1.shape[-1] // FF_CHUNK
    tm = FFN_TILE
    assert tm == FFN_ROWS or in_proj is None
    row = lambda width: pl.BlockSpec((tm, width), lambda i: (i, 0))
    vec = pl.BlockSpec((1, d), lambda i: (0, 0))
    whole = lambda a: pl.BlockSpec((1, 1) + a.shape[2:], lambda i: (layer, which, 0, 0),
                                   pipeline_mode=pl.Buffered(1))
    in_specs = [row(d), pl.BlockSpec((1, 9, d), mod_index(tm))]
    args = [x, mod]
    scratch = [pltpu.VMEM((tm, d), BF16), pltpu.VMEM((tm, d), F32)]
    if mixer is not None:
        ya, yb, w_out, g1, b1 = mixer
        in_specs += [row(ya.shape[1]), row(yb.shape[1]),
                     pl.BlockSpec(w_out.shape, lambda i: (0, 0), pipeline_mode=pl.Buffered(1)), vec, vec]
        args += [ya, yb, w_out, g1.reshape(1, d), b1.reshape(1, d)]
        scratch.append(pltpu.VMEM((tm, d), F32))
    in_specs += [whole(w1), whole(w3), whole(w2), vec, vec]
    args += [w1, w3, w2, g.reshape(1, d), b.reshape(1, d)]
    out_specs, out_shape, proj = [row(d)], [jax.ShapeDtypeStruct((n, d), F32)], None
    if in_proj is not None:
        body, io, kw = in_proj
        p_specs, p_args, (p_out_specs, p_out_shape) = io(n, tm, **kw)
        in_specs, args = in_specs + p_specs, args + p_args
        out_specs, out_shape = out_specs + p_out_specs, out_shape + p_out_shape
        proj = (body, len(p_args), len(p_out_specs), kw["rope"], kw["cache"])
    return pl.pallas_call(
        functools.partial(_ffn_kernel, j=j, nf=nf, mixer=mixer is not None, in_proj=proj),
        grid=(n // tm,),
        in_specs=in_specs,
        out_specs=out_specs,
        out_shape=out_shape,
        scratch_shapes=scratch,
        compiler_params=_cparams(("parallel",)),
        name="ffn_sublayer",
    )(*args)


def _rope_tables(length):
    t = np.arange(length)
    nf = HEAD_DIM // 4
    inv = ROPE_THETA ** (-np.arange(nf, dtype=np.float64) / nf)
    ang_r = (t // GRID_W)[:, None] * inv[None]
    ang_c = (t % GRID_W)[:, None] * inv[None]
    cr, sr, cc, sc = np.cos(ang_r), np.sin(ang_r), np.cos(ang_c), np.sin(ang_c)
    z = np.zeros_like(cr)
    cos = np.concatenate([cr, cr, cc, cc], 1)
    sin_up = np.concatenate([-sr, z, -sc, z], 1)
    sin_dn = np.concatenate([z, sr, z, sc], 1)
    two = lambda a: jnp.asarray(np.concatenate([a, a], 1), F32)
    return two(cos), two(sin_up), two(sin_dn)


def _rope128(x, cos, sin_up, sin_dn):
    nf = HEAD_DIM // 4
    return x * cos + pltpu.roll(x, 128 - nf, 1) * sin_up + pltpu.roll(x, nf, 1) * sin_dn


def _rope(x, cos, sin_up, sin_dn):
    parts = [_rope128(x[:, c:c + 128], cos, sin_up, sin_dn) for c in range(0, x.shape[1], 128)]
    return parts[0] if len(parts) == 1 else jnp.concatenate(parts, 1)


EVEN_COLS = (512, 256, 512, 512, 512, 512, 128)


KV_WIDTH = 128 + A_KV * 256


def _store_cache(ck_ref, cv_ref, kv):
    seq_len = ck_ref.shape[2]
    for s in range(ck_ref.shape[0]):
        rows = slice(s * seq_len, (s + 1) * seq_len)
        ck_ref[s] = kv[rows, :128].T
        cv_ref[s] = kv[rows, 128:].T


def _in_proj_outputs(outs, n, tm, seq_len):
    specs, shapes = [], []
    for o in outs:
        if o == "cache":
            specs.append(pl.BlockSpec((tm // seq_len, 128, seq_len), lambda i: (i, 0, 0)))
            shapes.append(jax.ShapeDtypeStruct((n // seq_len, 128, seq_len), F32))
        else:
            specs.append(pl.BlockSpec((tm, o[0]), lambda i: (i, 0)))
            shapes.append(jax.ShapeDtypeStruct((n, o[0]), o[1]))
    return specs, shapes


def _cache_layout(c):
    nb, _, sl = c.shape
    return jnp.transpose(c.reshape(nb, 1, A_KV, HEAD_DIM, sl), (0, 1, 4, 2, 3))


def _kv_with_ones(kv):
    ones = jnp.ones((kv.shape[0], 128), kv.dtype)
    v0, v1 = kv[:, 128:192], kv[:, 192:256]
    return jnp.concatenate([kv[:, :128], v0, v0, ones, v1, v1, ones], 1)


def _even_in_body(x, m, ins, outs, rope, cache):
    it = iter(ins)
    w_ref = next(it)
    tabs = (next(it), next(it), next(it)) if rope else None
    it = iter(outs)
    aq_ref, akv_ref = next(it), next(it)
    cache_refs = (next(it), next(it)) if cache else None
    bq_ref, bk_ref, bv_ref, bo_ref, bg_ref = next(it), next(it), next(it), next(it), next(it)

    h = (x * (1.0 + m[4:5]) + m[3:4]).astype(BF16)
    offs = [0]
    for c in EVEN_COLS:
        offs.append(offs[-1] + c)
    proj = lambda k: _dot(h, w_ref[:, offs[k]:offs[k + 1]])

    aq = proj(0)
    akv = proj(1)
    if cache:
        _store_cache(*cache_refs, akv)
    if rope:
        cos, s_up, s_dn = (t[...] for t in tabs)
        aq = _rope(aq, cos, s_up, s_dn)
        akv = jnp.concatenate([_rope(akv[:, :128], cos, s_up, s_dn), akv[:, 128:]], 1)
    aq_ref[...] = (aq * QK_SCALE).astype(BF16)
    akv_ref[...] = _kv_with_ones(akv).astype(BF16)
    bq_ref[...] = proj(2).astype(BF16)
    bk_ref[...] = (proj(3) * (B_DIM ** -0.5)).astype(BF16)
    bv_ref[...] = proj(4).astype(BF16)
    bo_ref[...] = proj(5)
    bg_ref[...] = proj(6)


def _rope_inputs(seq_len, tm):
    per_seq = seq_len // tm
    return [pl.BlockSpec((tm, 128), lambda i: (i % per_seq, 0))] * 3, list(_rope_tables(seq_len))


def _even_in_io(n, tm, seq_len, w, rope, cache):
    in_specs = [pl.BlockSpec(w.shape, lambda i: (0, 0), pipeline_mode=pl.Buffered(1))]
    args = [w]
    if rope:
        specs, tabs = _rope_inputs(seq_len, tm)
        in_specs, args = in_specs + specs, args + tabs
    outs = [(512, BF16), (KV_WIDTH, BF16)] + (["cache", "cache"] if cache else []) + \
           [(512, BF16), (512, BF16), (512, BF16), (512, F32), (128, F32)]
    return in_specs, args, _in_proj_outputs(outs, n, tm, seq_len)


ODD_COLS = (512, 512, 512, 512, 512, 512, 256)


def _head_rms(x, seg_ref, g):
    x2 = x * x
    hi = x2.astype(BF16)
    lo = (x2 - hi.astype(F32)).astype(BF16)
    w = seg_ref.shape[0]
    ms = [_dot(hi[:, c:c + w], seg_ref[...]) + _dot(lo[:, c:c + w], seg_ref[...]) for c in range(0, x.shape[1], w)]
    ms = ms[0] if len(ms) == 1 else jnp.concatenate(ms, 1)
    return x * lax.rsqrt(ms + 1e-6) * g


def _odd_in_body(x, m, ins, outs, rope, cache):
    it = iter(ins)
    w_ref, segq_ref, segk_ref, qn_ref, kn_ref = (next(it) for _ in range(5))
    tabs = (next(it), next(it), next(it)) if rope else None
    it = iter(outs)
    q_ref, ff_ref, fb_ref, v_ref, cg_ref, dq_ref, dkv_ref = (next(it) for _ in range(7))
    cache_refs = (next(it), next(it)) if cache else None

    h = (x * (1.0 + m[4:5]) + m[3:4]).astype(BF16)
    offs = [0]
    for c in ODD_COLS:
        offs.append(offs[-1] + c)
    proj = lambda k: _dot(h, w_ref[:, offs[k]:offs[k + 1]])

    q_ref[...] = _silu(proj(0)).astype(BF16)
    ff_ref[...] = proj(1)
    fb_ref[...] = proj(2)
    v_ref[...] = proj(3).astype(BF16)
    cg_ref[...] = proj(4)
    dq = _head_rms(proj(5), segq_ref, qn_ref[...])
    dkv = proj(6)
    dk = _head_rms(dkv[:, :128], segk_ref, kn_ref[...])
    dv = dkv[:, 128:]
    if cache:
        _store_cache(*cache_refs, jnp.concatenate([dk, dv], 1))
    if rope:
        cos, s_up, s_dn = (t[...] for t in tabs)
        dq = _rope(dq, cos, s_up, s_dn)
        dk = _rope(dk, cos, s_up, s_dn)
    dq_ref[...] = (dq * QK_SCALE).astype(BF16)
    dkv_ref[...] = _kv_with_ones(jnp.concatenate([dk, dv], 1)).astype(BF16)


def _segment_mean_matrix(width):
    r = jnp.arange(width) // HEAD_DIM
    return jnp.where(r[:, None] == r[None, :], 1.0 / HEAD_DIM, 0.0).astype(BF16)


def _odd_in_io(n, tm, seq_len, w, q_norm, k_norm, rope, cache):
    const = lambda a: pl.BlockSpec(a.shape, lambda i: (0, 0))
    segq, segk = _segment_mean_matrix(256), _segment_mean_matrix(128)
    qn = jnp.tile(q_norm, D_HEADS).reshape(1, 512)
    kn = jnp.tile(k_norm, D_KV).reshape(1, 128)
    in_specs = [pl.BlockSpec(w.shape, lambda i: (0, 0), pipeline_mode=pl.Buffered(1)),
                const(segq), const(segk), const(qn), const(kn)]
    args = [w, segq, segk, qn, kn]
    if rope:
        specs, tabs = _rope_inputs(seq_len, tm)
        in_specs, args = in_specs + specs, args + tabs
    outs = [(512, BF16), (512, F32), (512, F32), (512, BF16), (512, F32), (512, BF16), (KV_WIDTH, BF16)] + \
           (["cache", "cache"] if cache else [])
    return in_specs, args, _in_proj_outputs(outs, n, tm, seq_len)


def _attn_kernel(*refs, seq_len, tq, q_blocks, n_ctx, banded, has_sink):
    it = iter(refs)
    q_ref, kv_ref = next(it), next(it)
    ckv_ref = next(it) if n_ctx else None
    sink_ref = next(it) if has_sink else None
    o_ref = next(it)
    groups = A_HEADS // A_KV
    rows = groups * tq
    low_half = lax.broadcasted_iota(jnp.int32, (tq, 2 * HEAD_DIM), 1) < HEAD_DIM

    def run_kv_head(blk, kh):
        r0 = blk * tq
        j = pl.program_id(1) * q_blocks + blk
        kcol = slice(kh * HEAD_DIM, (kh + 1) * HEAD_DIM)
        vcol = slice(128 + kh * 256, 128 + (kh + 1) * 256)
        qs = jnp.concatenate([q_ref[r0:r0 + tq, (kh * groups + g) * HEAD_DIM:(kh * groups + g + 1) * HEAD_DIM]
                              for g in range(groups)], axis=0)
        tiles = []
        if banded:
            span = tq + 2 * WINDOW
            start = pl.multiple_of(jnp.clip(j * tq - WINDOW, 0, seq_len - span), WINDOW)
            qpos = j * tq + (lax.broadcasted_iota(jnp.int32, (rows, span), 0) & (tq - 1))
            kpos = start + lax.broadcasted_iota(jnp.int32, (rows, span), 1)
            band = jnp.abs(kpos - qpos) <= WINDOW
            tiles.append((kv_ref[pl.ds(start, span), kcol], kv_ref[pl.ds(start, span), vcol], band))
        else:
            tk = min(ATT_KEY_TILE, seq_len)
            for t in range(seq_len // tk):
                tiles.append((kv_ref[t * tk:(t + 1) * tk, kcol], kv_ref[t * tk:(t + 1) * tk, vcol], None))
        scores = [_dot_nt(qs, k_t) for k_t, _, _ in tiles]
        if n_ctx:
            tiles.append((None, ckv_ref[0, kh, HEAD_DIM:, :], None))
            scores.append(_dot(qs, ckv_ref[0, kh, :HEAD_DIM, :]))
        yield
        scores = [s if mask is None else jnp.where(mask, s, NEG_INF) for s, (_, _, mask) in zip(scores, tiles)]
        m = functools.reduce(jnp.maximum, [jnp.max(s, -1, keepdims=True) for s in scores])
        if has_sink:
            sink = jnp.concatenate([jnp.full((tq, 1), sink_ref[kh * groups + g] * LOG2_E, F32)
                                    for g in range(groups)], 0)
            m = jnp.maximum(m, sink)
        pv = functools.reduce(lambda a, b: a + b,
                              [(_dot_nt if k_t is None else _dot)(jnp.exp2(s - m).astype(BF16), v_t)
                               for s, (k_t, v_t, _) in zip(scores, tiles)])
        yield
        total = pv[:, 2 * HEAD_DIM:]
        if has_sink:
            total = total + jnp.exp2(sink - m)
        o = pv[:, :2 * HEAD_DIM] / total
        for g in range(0, groups, 2):
            pair = jnp.where(low_half, o[g * tq:(g + 1) * tq], o[(g + 1) * tq:(g + 2) * tq])
            c0 = (kh * groups + g) * HEAD_DIM
            o_ref[r0:r0 + tq, c0:c0 + 2 * HEAD_DIM] = pair.astype(BF16)

    run = _run_staggered if seq_len + n_ctx > 2 * ATT_KEY_TILE else _run_interleaved
    run([run_kv_head(blk, kh) for blk in range(q_blocks) for kh in range(A_KV)])


def _attention(q, kv, ctx_kv, sink, batch, seq_len, tq, q_blocks, banded):
    n_ctx = 0 if ctx_kv is None else ctx_kv.shape[-1]
    per_seq = seq_len // (tq * q_blocks)
    in_specs = [pl.BlockSpec((q_blocks * tq, 512), lambda b, j: (b * per_seq + j, 0)),
                pl.BlockSpec((seq_len, KV_WIDTH), lambda b, j: (b, 0))]
    args = [q, kv]
    if n_ctx:
        in_specs.append(pl.BlockSpec((1,) + ctx_kv.shape[1:], lambda b, j: (b, 0, 0, 0)))
        args.append(ctx_kv)
    if sink is not None:
        in_specs.append(pl.BlockSpec(memory_space=pltpu.SMEM))
        args.append(sink.reshape(-1).astype(F32))
    return pl.pallas_call(
        functools.partial(_attn_kernel, seq_len=seq_len, tq=tq, q_blocks=q_blocks, n_ctx=n_ctx, banded=banded,
                          has_sink=sink is not None),
        grid=(batch, per_seq),
        in_specs=in_specs,
        out_specs=pl.BlockSpec((q_blocks * tq, 512), lambda b, j: (b * per_seq + j, 0)),
        out_shape=jax.ShapeDtypeStruct(q.shape, BF16),
        compiler_params=_cparams(("parallel", "arbitrary")),
        name="gqa_attention",
    )(*args)


def _log_sigmoid(x):
    return jnp.minimum(x, 0.0) - jnp.log1p(jnp.exp(-jnp.abs(x)))


def _mlstm_kernel(*refs, seq_len, has_init, emit_state):
    it = iter(refs)
    q_ref, k_ref, v_ref, g_ref, bo_ref, gb_ref, ng_ref = (next(it) for _ in range(7))
    c0_ref, n0_ref, m0_ref = (next(it), next(it), next(it)) if has_init else (None, None, None)
    y_ref = next(it)
    co_ref, no_ref, mo_ref = (next(it), next(it), next(it)) if emit_state else (None, None, None)
    c_scr, n_scr, m_scr, hf_scr, hb_scr, ig_buf, bc_buf, ut_buf = (next(it) for _ in range(8))

    t_len = MLSTM_T
    n_chunks = seq_len // t_len
    n_streams = 2 * B_HEADS

    for s in range(n_streams):
        if has_init:
            c_scr[s] = c0_ref[0, s // B_HEADS, s % B_HEADS]
        else:
            c_scr[s] = jnp.zeros((B_DIM, B_DIM), F32)
    if has_init:
        n_scr[...] = n0_ref[0]
        m_scr[...] = m0_ref[0]
    else:
        n_scr[...] = jnp.zeros_like(n_scr)
        m_scr[...] = jnp.zeros_like(m_scr)

    ri = lax.broadcasted_iota(jnp.int32, (t_len, t_len), 0)
    ci = lax.broadcasted_iota(jnp.int32, (t_len, t_len), 1)
    lower = ci <= ri
    upper = ci >= ri
    tri_f = jnp.where(lower, 1.0, 0.0).astype(BF16)
    tri_b = jnp.where(upper, 1.0, 0.0).astype(BF16)
    gate_bias = gb_ref[...]
    sel_row = lax.broadcasted_iota(jnp.int32, (128, B_HEADS * B_DIM), 0)
    sel_head = lax.broadcasted_iota(jnp.int32, (128, B_HEADS * B_DIM), 1) // B_DIM
    gate_select = []
    for direction in range(2):
        i_base = 2 * direction * B_HEADS
        gate_select.append((jnp.where(sel_row == i_base + sel_head, 1.0, 0.0).astype(BF16),
                            jnp.where(sel_row == i_base + B_HEADS + sel_head, 1.0, 0.0).astype(BF16)))

    def chunk_rows(c, direction):
        chunk = c if direction == 0 else n_chunks - 1 - c
        return pl.ds(pl.multiple_of(chunk * t_len, t_len), t_len)

    def prepare_gates(c, direction):
        slot = c % 2
        rows = chunk_rows(jnp.minimum(c, n_chunks - 1), direction)
        tri = tri_f if direction == 0 else tri_b
        sel_i, sel_f = gate_select[direction]
        gates = g_ref[rows, :] + gate_bias
        hi, mid, lo = _split3(_log_sigmoid(gates))
        b3 = _dot(tri, jnp.concatenate([hi, mid, lo], 1))
        ig_buf[slot, direction] = _dot_sel_rhs(gates, sel_i)
        yield
        bc_all = b3[:, :128] + b3[:, 128:256] + b3[:, 256:]
        bc_rep = _dot_sel_rhs(bc_all, sel_f)
        ut_buf[slot, direction] = (gates - pltpu.roll(bc_all, 128 - B_HEADS, 1)).T
        yield
        bc_buf[slot, direction] = bc_rep

    def run_direction(c, direction):
        rows = chunk_rows(c, direction)
        slot = c % 2
        causal = lower if direction == 0 else upper
        last = t_len - 1 if direction == 0 else 0
        h_out = hf_scr if direction == 0 else hb_scr
        ig_rep, bc_rep, u_t = ig_buf[slot, direction], bc_buf[slot, direction], ut_buf[slot, direction]
        heads = []
        for hd in range(B_HEADS):
            s = direction * B_HEADS + hd
            col = slice(hd * B_DIM, (hd + 1) * B_DIM)
            qc, kc, vc = q_ref[rows, col], k_ref[rows, col], v_ref[rows, col]
            c_prev = c_scr[s]
            n_prev = n_scr[s:s + 1, :]
            qk = _dot_nt(qc, kc)
            q_state = _dot(qc, c_prev.astype(BF16))
            qn = _dot_nt(qc, jnp.broadcast_to(n_prev, (B_DIM, B_DIM)).astype(BF16))
            heads.append((s, col, qc, kc, vc, c_prev, n_prev, qk, q_state, qn))
        yield
        staged = []
        for (s, col, qc, kc, vc, c_prev, n_prev, qk, q_state, qn) in heads:
            i_col = 2 * direction * B_HEADS + (s % B_HEADS)
            ig, bc = ig_rep[:, col], bc_rep[:, col]
            dmat = jnp.where(causal, bc[:, :t_len] + u_t[i_col:i_col + 1, :], -jnp.inf)
            m_prev = m_scr[s:s + 1, :]
            inter = bc + m_prev
            m_t = jnp.maximum(inter, jnp.max(dmat, -1, keepdims=True))
            w = (jnp.exp(dmat - m_t[:, :t_len]) * qk).astype(BF16)
            wv = _dot(w, jnp.concatenate([vc, jnp.ones_like(vc)], 1))
            b_tot = bc[last:last + 1, :]
            g = b_tot - bc + ig
            m_new = jnp.maximum(b_tot + m_prev, jnp.max(g, 0, keepdims=True))
            kw = kc.astype(F32) * jnp.exp(g - m_new)
            upd = _dot_tn(kw.astype(BF16), vc)
            decay = jnp.exp(b_tot + m_prev - m_new)
            staged.append((s, col, inter, m_t, c_prev, n_prev, q_state, qn, wv, m_new, kw, upd, decay))
        yield
        for (s, col, inter, m_t, c_prev, n_prev, q_state, qn, wv, m_new, kw, upd, decay) in staged:
            a = jnp.exp(inter - m_t)
            num = a * q_state + wv[:, :B_DIM]
            den = a * qn + wv[:, B_DIM:]
            h_out[rows, col] = num / jnp.maximum(jnp.abs(den), jnp.exp(-m_t))
            c_scr[s] = decay * c_prev + upd
            n_scr[s:s + 1, :] = decay * n_prev + jnp.sum(kw, 0, keepdims=True)
            m_scr[s:s + 1, :] = m_new

    def chained(c, direction):
        yield from prepare_gates(c, direction)
        yield
        yield from run_direction(c, direction)

    def step(c, carry):
        _run_interleaved([chained(c, 0), chained(c, 1)])
        return carry

    lax.fori_loop(0, n_chunks, step, 0, unroll=2)

    blk = min(256, seq_len)
    ng = ng_ref[...]

    def finish(i, carry):
        rows = pl.ds(pl.multiple_of(i * blk, blk), blk)
        hsum = hf_scr[rows, :] + hb_scr[rows, :]
        gate = jax.nn.sigmoid(bo_ref[rows, :])
        parts = [_rms_lastdim(hsum[:, hd * B_DIM:(hd + 1) * B_DIM], ng[:, hd * B_DIM:(hd + 1) * B_DIM])
                 for hd in range(B_HEADS)]
        y_ref[rows, :] = (gate * jnp.concatenate(parts, 1)).astype(BF16)
        return carry

    lax.fori_loop(0, seq_len // blk, finish, 0)

    if emit_state:
        for s in range(n_streams):
            co_ref[0, s // B_HEADS, s % B_HEADS] = c_scr[s]
        no_ref[0] = n_scr[...]
        mo_ref[0] = m_scr[...]


def _mlstm(q, k, v, gates, bo, gate_bias, norm_g, init, batch, seq_len, emit_state):
    n = q.shape[0]
    row = lambda width: pl.BlockSpec((seq_len, width), lambda b: (b, 0))
    const = lambda a: pl.BlockSpec(a.shape, lambda b: (0,) * a.ndim)
    gb = jnp.zeros((1, 128), F32).at[0, :4 * B_HEADS].set(gate_bias.reshape(-1))
    ng = norm_g.reshape(1, B_HEADS * B_DIM)
    in_specs = [row(512), row(512), row(512), row(128), row(512), const(gb), const(ng)]
    args = [q, k, v, gates, bo, gb, ng]
    n_streams = 2 * B_HEADS
    if init is not None:
        c0, n0, m0 = init
        in_specs += [pl.BlockSpec((1, 2, B_HEADS, B_DIM, B_DIM), lambda b: (b, 0, 0, 0, 0)),
                     pl.BlockSpec((1, n_streams, B_DIM), lambda b: (b, 0, 0)),
                     pl.BlockSpec((1, n_streams, B_DIM), lambda b: (b, 0, 0))]
        args += [c0, n0.reshape(batch, n_streams, B_DIM),
                 jnp.broadcast_to(m0.reshape(batch, n_streams, 1), (batch, n_streams, B_DIM))]
    out_specs = [row(512)]
    out_shape = [jax.ShapeDtypeStruct((n, 512), BF16)]
    if emit_state:
        out_specs += [pl.BlockSpec((1, 2, B_HEADS, B_DIM, B_DIM), lambda b: (b, 0, 0, 0, 0)),
                      pl.BlockSpec((1, n_streams, B_DIM), lambda b: (b, 0, 0)),
                      pl.BlockSpec((1, n_streams, B_DIM), lambda b: (b, 0, 0))]
        out_shape += [jax.ShapeDtypeStruct((batch, 2, B_HEADS, B_DIM, B_DIM), F32),
                      jax.ShapeDtypeStruct((batch, n_streams, B_DIM), F32),
                      jax.ShapeDtypeStruct((batch, n_streams, B_DIM), F32)]
    return pl.pallas_call(
        functools.partial(_mlstm_kernel, seq_len=seq_len, has_init=init is not None, emit_state=emit_state),
        grid=(batch,),
        in_specs=in_specs,
        out_specs=out_specs,
        out_shape=out_shape,
        scratch_shapes=[pltpu.VMEM((n_streams, B_DIM, B_DIM), F32),
                        pltpu.VMEM((n_streams, B_DIM), F32),
                        pltpu.VMEM((n_streams, B_DIM), F32),
                        pltpu.VMEM((seq_len, 512), F32),
                        pltpu.VMEM((seq_len, 512), F32),
                        pltpu.VMEM((2, 2, MLSTM_T, B_HEADS * B_DIM), F32),
                        pltpu.VMEM((2, 2, MLSTM_T, B_HEADS * B_DIM), F32),
                        pltpu.VMEM((2, 2, 128, MLSTM_T), F32)],
        compiler_params=_cparams(("parallel",)),
        name="mlstm_scan",
    )(*args)


def _hgrn_kernel(*refs, seq_len, layer, has_init, emit_state):
    it = iter(refs)
    q_ref, ff_ref, fb_ref, v_ref, cg_ref, lbl_ref, ng_ref = (next(it) for _ in range(7))
    s0_ref = next(it) if has_init else None
    y_ref = next(it)
    so_ref = next(it) if emit_state else None
    st_scr, of_scr, ob_scr = next(it), next(it), next(it)

    t_len = HGRN_T
    n_sub = t_len // SUB
    n_chunks = seq_len // t_len
    n_streams = 2 * C_HEADS

    logits = lbl_ref[...]
    e = jnp.exp(logits - jnp.max(logits, 0, keepdims=True))
    sm = e / jnp.sum(e, 0, keepdims=True)
    lb = jnp.sum(sm[0:layer + 1], 0, keepdims=True) - sm[0:1]

    for s in range(n_streams):
        if has_init:
            st_scr[s] = s0_ref[0, s // C_HEADS, s % C_HEADS].T
        else:
            st_scr[s] = jnp.zeros((C_DV, C_DK), F32)

    ri = lax.broadcasted_iota(jnp.int32, (t_len, t_len), 0)
    ci = lax.broadcasted_iota(jnp.int32, (t_len, t_len), 1)
    tri_f = jnp.where(ci <= ri, 1.0, 0.0).astype(BF16)
    tri_b = jnp.where(ci >= ri, 1.0, 0.0).astype(BF16)
    sub_row = lax.broadcasted_iota(jnp.int32, (SUB, C_DK), 0)
    ones_dk = jnp.ones((C_DK, C_DK), BF16)

    def run_stream(c, direction, hd):
        chunk = c if direction == 0 else n_chunks - 1 - c
        rows = pl.ds(pl.multiple_of(chunk * t_len, t_len), t_len)
        f_ref = ff_ref if direction == 0 else fb_ref
        tri = tri_f if direction == 0 else tri_b
        last = t_len - 1 if direction == 0 else 0
        o_out = of_scr if direction == 0 else ob_scr
        s = direction * C_HEADS + hd
        col = slice(hd * C_DK, (hd + 1) * C_DK)
        lbh = lb[:, col]
        f = lbh + (1.0 - lbh) * jax.nn.sigmoid(f_ref[rows, col])
        log_k = jnp.log2(1.0 - f)
        hi, mid, lo = _split3(jnp.log2(f))
        a3 = _dot(tri, jnp.concatenate([hi, mid, lo], 1))
        yield
        a_cum = a3[:, :C_DK] + a3[:, C_DK:2 * C_DK] + a3[:, 2 * C_DK:]
        a_key = a_cum - log_k
        a_tot = a_cum[last:last + 1, :]
        qf = q_ref[rows, col].astype(F32)
        vc = v_ref[rows, col]
        vf = vc.astype(F32)
        st = st_scr[s]
        inter = _dot_nt((qf * jnp.exp2(a_cum)).astype(BF16), st.astype(BF16))
        upd = _dot_tn(vc, jnp.exp2(a_tot - a_key).astype(BF16))
        ps = []
        for blk in range(n_sub):
            b0 = blk * SUB
            a_i, q_i, ak_i = (t[b0:b0 + SUB] for t in (a_cum, qf, a_key))
            for j in range(SUB):
                seen = (sub_row >= j) if direction == 0 else (sub_row <= j)
                ps.append(jnp.where(seen, jnp.exp2(a_i - ak_i[j:j + 1]), 0.0) * q_i)
        att = _dot(jnp.concatenate(ps, 0).astype(BF16), ones_dk)
        off = []
        for blk in range(n_sub):
            b0 = blk * SUB
            if direction == 0 and blk > 0:
                a_ref, kr = a_cum[b0 - 1:b0], slice(0, b0)
            elif direction == 1 and blk < n_sub - 1:
                a_ref, kr = a_cum[b0 + SUB:b0 + SUB + 1], slice(b0 + SUB, t_len)
            else:
                off.append(None)
                continue
            qt = (qf[b0:b0 + SUB] * jnp.exp2(a_cum[b0:b0 + SUB] - a_ref)).astype(BF16)
            kt = jnp.exp2(a_ref - a_key[kr]).astype(BF16)
            off.append((_dot_nt(qt, kt), kr))
        yield
        st_scr[s] = jnp.exp2(a_tot) * st + upd
        outs = []
        for blk in range(n_sub):
            b0 = blk * SUB
            o_i = inter[b0:b0 + SUB]
            for j in range(SUB):
                r = (blk * SUB + j) * SUB
                o_i = o_i + att[r:r + SUB] * vf[b0 + j:b0 + j + 1]
            if off[blk] is not None:
                att_off, kr = off[blk]
                outs.append((o_i, _dot(att_off.astype(BF16), vc[kr])))
            else:
                outs.append((o_i, None))
        yield
        o_out[rows, col] = jnp.concatenate([o_i if o_off is None else o_i + o_off for o_i, o_off in outs], 0)

    def step(c, carry):
        _run_interleaved([run_stream(c, direction, hd) for direction in range(2) for hd in range(C_HEADS)])
        return carry

    lax.fori_loop(0, n_chunks, step, 0, unroll=2)

    blk_rows = min(256, seq_len)
    ng = ng_ref[...]

    def finish(i, carry):
        rows = pl.ds(pl.multiple_of(i * blk_rows, blk_rows), blk_rows)
        osum = of_scr[rows, :] + ob_scr[rows, :]
        parts = [_rms_lastdim(osum[:, hd * C_DV:(hd + 1) * C_DV], ng[:, hd * C_DV:(hd + 1) * C_DV])
                 for hd in range(C_HEADS)]
        y_ref[rows, :] = (jnp.concatenate(parts, 1) * _silu(cg_ref[rows, :])).astype(BF16)
        return carry

    lax.fori_loop(0, seq_len // blk_rows, finish, 0)

    if emit_state:
        for s in range(n_streams):
            so_ref[0, s // C_HEADS, s % C_HEADS] = st_scr[s].T


def _hgrn(q, ff, fb, v, cg, lb_logits, norm_g, layer, init, batch, seq_len, emit_state):
    n = q.shape[0]
    row = lambda width: pl.BlockSpec((seq_len, width), lambda b: (b, 0))
    const = lambda a: pl.BlockSpec(a.shape, lambda b: (0,) * a.ndim)
    ng = norm_g.reshape(1, C_HEADS * C_DV)
    state_spec = pl.BlockSpec((1, 2, C_HEADS, C_DK, C_DV), lambda b: (b, 0, 0, 0, 0))
    in_specs = [row(512)] * 5 + [const(lb_logits), const(ng)]
    args = [q, ff, fb, v, cg, lb_logits, ng]
    if init is not None:
        in_specs.append(state_spec)
        args.append(init)
    out_specs = [row(512)]
    out_shape = [jax.ShapeDtypeStruct((n, 512), BF16)]
    if emit_state:
        out_specs.append(state_spec)
        out_shape.append(jax.ShapeDtypeStruct((batch, 2, C_HEADS, C_DK, C_DV), F32))
    return pl.pallas_call(
        functools.partial(_hgrn_kernel, seq_len=seq_len, layer=layer, has_init=init is not None,
                          emit_state=emit_state),
        grid=(batch,),
        in_specs=in_specs,
        out_specs=out_specs,
        out_shape=out_shape,
        scratch_shapes=[pltpu.VMEM((2 * C_HEADS, C_DV, C_DK), F32),
                        pltpu.VMEM((seq_len, 512), F32),
                        pltpu.VMEM((seq_len, 512), F32)],
        compiler_params=_cparams(("parallel",)),
        name="hgrn2_scan",
    )(*args)


def _prep_w_in_even(w):
    d = w.shape[0]
    a, bq, bk, bv = w[:, :768], w[:, 768:1280], w[:, 1280:1792], w[:, 1792:2304]
    bg, bo = w[:, 2304:2320], w[:, 2320:2832]
    pad = jnp.zeros((d, 128 - bg.shape[1]), BF16)
    return jnp.concatenate([t.astype(BF16) for t in (a, bq, bk, bv, bo, bg)] + [pad], 1)


def _prep_ctx_kv(k_ctx, v_ctx):
    k_t = jnp.transpose(k_ctx, (0, 2, 3, 1)).astype(BF16)
    v_t = jnp.transpose(v_ctx, (0, 2, 3, 1)).astype(BF16)
    ones = jnp.ones(k_t.shape[:2] + (128, k_t.shape[3]), BF16)
    return jnp.concatenate([k_t, v_t, v_t, ones], 2)


def kernel(x_prompt, x_sample, c, cache_a_k, cache_a_v, state_b_C, state_b_n, state_b_m, state_c_S, cache_d_k, cache_d_v, c_ctx, ada_w, ada_b, ln_g, ln_b, ffn_w1, ffn_w3, ffn_w2, w_in_even, w_out_even, a_sink, b_gate_bias, b_norm_g, w_in_odd, w_out_odd, c_lb_logits, c_norm_g, d_q_norm, d_k_norm):
    batch_p, len_p, d = x_prompt.shape
    batch_s, len_s, _ = x_sample.shape

    cvec = jnp.concatenate([c_ctx[None], c, jnp.zeros((MOD_ROWS - 1 - batch_s, d), F32)], 0)
    mod_all = _modulation(cvec, ada_w, ada_b)

    groups = [
        dict(x=x_prompt.reshape(batch_p * len_p, d), batch=batch_p, seq=len_p, prompt=True,
             mod_index=_mod_index(0, None)),
        dict(x=x_sample.reshape(batch_s * len_s, d), batch=batch_s, seq=len_s, prompt=False,
             mod_index=_mod_index(1, len_s)),
    ]
    new = {}
    ffn_w = (ffn_w1.astype(BF16), ffn_w3.astype(BF16), ffn_w2.astype(BF16))

    for l in range(DEPTH):
        mod = mod_all[l].reshape(MOD_ROWS, 9, d)
        i = l // 2
        if l % 2 == 0:
            w_in = _prep_w_in_even(w_in_even[i])
            w_out = w_out_even[i].astype(BF16)
        else:
            w_in = w_in_odd[i].astype(BF16)
            w_out = w_out_odd[i].astype(BF16)

        for grp in groups:
            x, mi, nb, sl, prompt = grp["x"], grp["mod_index"], grp["batch"], grp["seq"], grp["prompt"]
            io_kw = dict(seq_len=sl, w=w_in, rope=not prompt, cache=prompt)
            if l % 2 == 0:
                in_proj = (_even_in_body, _even_in_io, io_kw)
            else:
                in_proj = (_odd_in_body, _odd_in_io, dict(io_kw, q_norm=d_q_norm[i], k_norm=d_k_norm[i]))
            x, *outs = _ffn(x, mod, 0, ln_g[l, 0], ln_b[l, 0], *ffn_w, l, 0, mi, in_proj=in_proj)
            if l % 2 == 0:
                if prompt:
                    aq, akv, k_cache, v_cache, bq, bk, bv, bo, bg = outs
                    new["a_k"], new["a_v"] = _cache_layout(k_cache), _cache_layout(v_cache)
                    ya = _attention(aq, akv, None, a_sink[i], nb, sl, tq=Q_BLOCK, q_blocks=sl // Q_BLOCK, banded=False)
                    yb, c_new, n_new, m_new = _mlstm(bq, bk, bv, bg, bo, b_gate_bias[i], b_norm_g[i], None,
                                                     nb, sl, emit_state=True)
                    new["b_C"] = c_new[:, None]
                    new["b_n"] = n_new.reshape(nb, 1, 2, B_HEADS, B_DIM)
                    new["b_m"] = m_new[:, :, 0].reshape(nb, 1, 2, B_HEADS)
                else:
                    aq, akv, bq, bk, bv, bo, bg = outs
                    ya = _attention(aq, akv, _prep_ctx_kv(cache_a_k[:, i], cache_a_v[:, i]), a_sink[i], nb, sl,
                                    tq=Q_BLOCK, q_blocks=4, banded=True)
                    init = (state_b_C[:, i], state_b_n[:, i], state_b_m[:, i])
                    yb, = _mlstm(bq, bk, bv, bg, bo, b_gate_bias[i], b_norm_g[i], init, nb, sl, emit_state=False)
                mixed = (ya, yb)
            else:
                if prompt:
                    cq, ff, fb, cv, cg, dq, dkv, k_cache, v_cache = outs
                    new["d_k"], new["d_v"] = _cache_layout(k_cache), _cache_layout(v_cache)
                    yc, s_new = _hgrn(cq, ff, fb, cv, cg, c_lb_logits, c_norm_g[i], l, None, nb, sl, emit_state=True)
                    new["c_S"] = s_new[:, None]
                    yd = _attention(dq, dkv, None, None, nb, sl, tq=Q_BLOCK, q_blocks=sl // Q_BLOCK, banded=False)
                else:
                    cq, ff, fb, cv, cg, dq, dkv = outs
                    yc, = _hgrn(cq, ff, fb, cv, cg, c_lb_logits, c_norm_g[i], l, state_c_S[:, i], nb, sl,
                                emit_state=False)
                    yd = _attention(dq, dkv, _prep_ctx_kv(cache_d_k[:, i], cache_d_v[:, i]), None, nb, sl,
                                    tq=Q_BLOCK, q_blocks=2, banded=False)
                mixed = (yc, yd)
            x, = _ffn(x, mod, 2, ln_g[l, 2], ln_b[l, 2], *ffn_w, l, 1, mi,
                      mixer=mixed + (w_out, ln_g[l, 1], ln_b[l, 1]))
            grp["x"] = x

    y_prompt = groups[0]["x"].reshape(batch_p, len_p, d)
    y_sample = groups[1]["x"].reshape(batch_s, len_s, d)
    return (y_prompt, y_sample, new["a_k"], new["a_v"], new["b_C"], new["b_n"], new["b_m"], new["c_S"],
            new["d_k"], new["d_v"])
```

```python
import functools

import jax
import jax.numpy as jnp
import numpy as np
from jax import lax
from jax.experimental import pallas as pl
from jax.experimental.pallas import tpu as pltpu

F32 = jnp.float32
BF16 = jnp.bfloat16

D_MODEL = 1024
DEPTH = 2
GRID_W = 64
HEAD_DIM = 64
A_HEADS = 8
A_KV = 2
WINDOW = 128
B_HEADS = 4
B_DIM = 128
C_HEADS = 4
C_DK = 128
C_DV = 128
D_HEADS = 8
D_KV = 2
Q_BLOCK = 128
ROPE_THETA = 10000.0
ALPHA = (2 * DEPTH) ** 0.25
NEG_INF = -1e30
LOG2_E = 1.4426950408889634
QK_SCALE = HEAD_DIM ** -0.5 * LOG2_E

MOD_ROWS = 8
FF_CHUNK = 256
FFN_TILE = 512
MLSTM_T = 64
HGRN_T = 64
SUB = 8
ATT_KEY_TILE = 512
VMEM_LIMIT = 56 * 1024 * 1024


def _cparams(sem):
    return pltpu.CompilerParams(dimension_semantics=sem, vmem_limit_bytes=VMEM_LIMIT)


def _dot(a, b):
    return jnp.dot(a, b, preferred_element_type=F32)


def _dot_nt(a, b):
    return lax.dot_general(a, b, (((1,), (1,)), ((), ())), preferred_element_type=F32)


def _dot_tn(a, b):
    return lax.dot_general(a, b, (((0,), (0,)), ((), ())), preferred_element_type=F32)


def _split3(x):
    hi = x.astype(BF16)
    r1 = x - hi.astype(F32)
    mid = r1.astype(BF16)
    lo = (r1 - mid.astype(F32)).astype(BF16)
    return hi, mid, lo


def _dot_sel_rhs(x, sel):
    hi = x.astype(BF16)
    lo = (x - hi.astype(F32)).astype(BF16)
    r = _dot(jnp.concatenate([hi, lo], 0), sel)
    return r[:x.shape[0]] + r[x.shape[0]:]


def _run_interleaved(gens):
    live = list(gens)
    while live:
        nxt = []
        for g in live:
            try:
                next(g)
                nxt.append(g)
            except StopIteration:
                pass
        live = nxt


def _run_staggered(gens):
    pending, live = list(gens), []
    while pending or live:
        if pending:
            live.insert(0, pending.pop(0))
        nxt = []
        for g in live:
            try:
                next(g)
                nxt.append(g)
            except StopIteration:
                pass
        live = nxt


def _silu(x):
    return x * jax.nn.sigmoid(x)


def _layernorm(z, g, b):
    mu = jnp.mean(z, -1, keepdims=True)
    zc = z - mu
    var = jnp.mean(zc * zc, -1, keepdims=True)
    return zc * lax.rsqrt(var + 1e-5) * g + b


def _rms_lastdim(x, g):
    return x * lax.rsqrt(jnp.mean(x * x, -1, keepdims=True) + 1e-6) * g


def _mod_index(group_start, request_len):
    if request_len is None:
        return lambda tm: (lambda i: (group_start, 0, 0))
    return lambda tm: (lambda i: (group_start + i // (request_len // tm), 0, 0))


def _mod_kernel(c_ref, w_ref, b_ref, o_ref):
    s = _silu(c_ref[...]).astype(BF16)
    o_ref[0] = _dot(s, w_ref[0].astype(BF16)) + b_ref[0]


def _modulation(cvec, ada_w, ada_b):
    depth, d, n = ada_w.shape
    tn = 1152
    return pl.pallas_call(
        _mod_kernel,
        grid=(depth, n // tn),
        in_specs=[pl.BlockSpec((MOD_ROWS, d), lambda l, j: (0, 0)),
                  pl.BlockSpec((1, d, tn), lambda l, j: (l, 0, j)),
                  pl.BlockSpec((1, 1, tn), lambda l, j: (l, 0, j))],
        out_specs=pl.BlockSpec((1, MOD_ROWS, tn), lambda l, j: (l, 0, j)),
        out_shape=jax.ShapeDtypeStruct((depth, MOD_ROWS, n), F32),
        compiler_params=_cparams(("parallel", "parallel")),
        name="modulation",
    )(cvec, ada_w, ada_b.reshape(depth, 1, n))


def _ffn_kernel(*refs, j, nf, mixer, in_proj):
    it = iter(refs)
    x_ref, mod_ref = next(it), next(it)
    if mixer:
        ya_ref, yb_ref, wo_ref, g1_ref, b1_ref = (next(it) for _ in range(5))
    w1_ref, w3_ref, w2_ref, g_ref, b_ref = (next(it) for _ in range(5))
    proj_ins = [next(it) for _ in range(in_proj[1])] if in_proj else []
    o_ref = next(it)
    proj_outs = [next(it) for _ in range(in_proj[2])] if in_proj else []
    h_ref, acc_ref = next(it), next(it)
    m = mod_ref[0]
    shift, scale, gate = m[3 * j:3 * j + 1], m[3 * j + 1:3 * j + 2], m[3 * j + 2:3 * j + 3]
    if mixer:
        res_ref = next(it)
        half = ya_ref.shape[1]
        y = _dot(ya_ref[...], wo_ref[:half, :]) + _dot(yb_ref[...], wo_ref[half:, :])
        res_ref[...] = _layernorm(ALPHA * x_ref[...] + m[5:6] * y, g1_ref[...], b1_ref[...])
        x_ref = res_ref
    h_ref[...] = (x_ref[...] * (1.0 + scale) + shift).astype(BF16)
    for f in range(nf):
        cols = slice(f * FF_CHUNK, (f + 1) * FF_CHUNK)
        h = h_ref[...]
        u = (_silu(_dot(h, w1_ref[0, 0, :, cols])) * _dot(h, w3_ref[0, 0, :, cols])).astype(BF16)
        y = _dot(u, w2_ref[0, 0, cols, :])
        if f == 0:
            acc_ref[...] = y
        else:
            acc_ref[...] += y
    out = _layernorm(ALPHA * x_ref[...] + 0.5 * gate * acc_ref[...], g_ref[...], b_ref[...])
    o_ref[...] = out
    if in_proj:
        in_proj[0](out, m, proj_ins, proj_outs, in_proj[3], in_proj[4])


def _ffn(x, mod, j, g, b, w1, w3, w2, layer, which, mod_index, mixer=None, in_proj=None):
    n, d = x.shape
    nf = w1.shape[-1] // FF_CHUNK
    tm = FFN_TILE
    row = lambda width: pl.BlockSpec((tm, width), lambda i: (i, 0))
    vec = pl.BlockSpec((1, d), lambda i: (0, 0))
    whole = lambda a: pl.BlockSpec((1, 1) + a.shape[2:], lambda i: (layer, which, 0, 0),
                                   pipeline_mode=pl.Buffered(1))
    in_specs = [row(d), pl.BlockSpec((1, 9, d), mod_index(tm))]
    args = [x, mod]
    scratch = [pltpu.VMEM((tm, d), BF16), pltpu.VMEM((tm, d), F32)]
    if mixer is not None:
        ya, yb, w_out, g1, b1 = mixer
        in_specs += [row(ya.shape[1]), row(yb.shape[1]),
                     pl.BlockSpec(w_out.shape, lambda i: (0, 0), pipeline_mode=pl.Buffered(1)), vec, vec]
        args += [ya, yb, w_out, g1.reshape(1, d), b1.reshape(1, d)]
        scratch.append(pltpu.VMEM((tm, d), F32))
    in_specs += [whole(w1), whole(w3), whole(w2), vec, vec]
    args += [w1, w3, w2, g.reshape(1, d), b.reshape(1, d)]
    out_specs, out_shape, proj = [row(d)], [jax.ShapeDtypeStruct((n, d), F32)], None
    if in_proj is not None:
        body, io, kw = in_proj
        p_specs, p_args, (p_out_specs, p_out_shape) = io(n, tm, **kw)
        in_specs, args = in_specs + p_specs, args + p_args
        out_specs, out_shape = out_specs + p_out_specs, out_shape + p_out_shape
        proj = (body, len(p_args), len(p_out_specs), kw["rope"], kw["cache"])
    return pl.pallas_call(
        functools.partial(_ffn_kernel, j=j, nf=nf, mixer=mixer is not None, in_proj=proj),
        grid=(n // tm,),
        in_specs=in_specs,
        out_specs=out_specs,
        out_shape=out_shape,
        scratch_shapes=scratch,
        compiler_params=_cparams(("parallel",)),
        name="ffn_sublayer",
    )(*args)


def _rope_tables(length):
    t = np.arange(length)
    nf = HEAD_DIM // 4
    inv = ROPE_THETA ** (-np.arange(nf, dtype=np.float64) / nf)
    ang_r = (t // GRID_W)[:, None] * inv[None]
    ang_c = (t % GRID_W)[:, None] * inv[None]
    cr, sr, cc, sc = np.cos(ang_r), np.sin(ang_r), np.cos(ang_c), np.sin(ang_c)
    z = np.zeros_like(cr)
    cos = np.concatenate([cr, cr, cc, cc], 1)
    sin_up = np.concatenate([-sr, z, -sc, z], 1)
    sin_dn = np.concatenate([z, sr, z, sc], 1)
    two = lambda a: jnp.asarray(np.concatenate([a, a], 1), F32)
    return two(cos), two(sin_up), two(sin_dn)


def _rope128(x, cos, sin_up, sin_dn):
    nf = HEAD_DIM // 4
    return x * cos + pltpu.roll(x, 128 - nf, 1) * sin_up + pltpu.roll(x, nf, 1) * sin_dn


def _rope(x, cos, sin_up, sin_dn):
    parts = [_rope128(x[:, c:c + 128], cos, sin_up, sin_dn) for c in range(0, x.shape[1], 128)]
    return parts[0] if len(parts) == 1 else jnp.concatenate(parts, 1)


EVEN_COLS = (512, 256, 512, 512, 512, 528)


KV_WIDTH = 128 + A_KV * 256


def _store_cache(ck_ref, cv_ref, kv):
    seq_len = ck_ref.shape[2]
    for s in range(ck_ref.shape[0]):
        rows = slice(s * seq_len, (s + 1) * seq_len)
        ck_ref[s] = kv[rows, :128].T
        cv_ref[s] = kv[rows, 128:].T


def _in_proj_outputs(outs, n, tm, seq_len):
    specs, shapes = [], []
    for o in outs:
        if o == "cache":
            specs.append(pl.BlockSpec((tm // seq_len, 128, seq_len), lambda i: (i, 0, 0)))
            shapes.append(jax.ShapeDtypeStruct((n // seq_len, 128, seq_len), F32))
        else:
            specs.append(pl.BlockSpec((tm, o[0]), lambda i: (i, 0)))
            shapes.append(jax.ShapeDtypeStruct((n, o[0]), o[1]))
    return specs, shapes


def _cache_layout(c):
    nb, _, sl = c.shape
    return jnp.transpose(c.reshape(nb, 1, A_KV, HEAD_DIM, sl), (0, 1, 4, 2, 3))


def _kv_with_ones(kv):
    ones = jnp.ones((kv.shape[0], 128), kv.dtype)
    v0, v1 = kv[:, 128:192], kv[:, 192:256]
    return jnp.concatenate([kv[:, :128], v0, v0, ones, v1, v1, ones], 1)


def _even_in_body(x, m, ins, outs, rope, cache):
    it = iter(ins)
    w_ref = next(it)
    tabs = (next(it), next(it), next(it)) if rope else None
    it = iter(outs)
    aq_ref, akv_ref = next(it), next(it)
    cache_refs = (next(it), next(it)) if cache else None
    bq_ref, bk_ref, bv_ref, bo_ref, bg_ref = next(it), next(it), next(it), next(it), next(it)

    h = (x * (1.0 + m[4:5]) + m[3:4]).astype(BF16)
    offs = [0]
    for c in EVEN_COLS:
        offs.append(offs[-1] + c)
    proj = lambda k: _dot(h, w_ref[:, offs[k]:offs[k + 1]])

    aq = proj(0)
    akv = proj(1)
    if cache:
        _store_cache(*cache_refs, akv)
    if rope:
        cos, s_up, s_dn = (t[...] for t in tabs)
        aq = _rope(aq, cos, s_up, s_dn)
        akv = jnp.concatenate([_rope(akv[:, :128], cos, s_up, s_dn), akv[:, 128:]], 1)
    aq_ref[...] = (aq * QK_SCALE).astype(BF16)
    akv_ref[...] = _kv_with_ones(akv).astype(BF16)
    bq_ref[...] = proj(2).astype(BF16)
    bk_ref[...] = (proj(3) * (B_DIM ** -0.5)).astype(BF16)
    bv_ref[...] = proj(4).astype(BF16)
    tail = proj(5)
    n_gates = 4 * B_HEADS
    bo_ref[...] = tail[:, n_gates:]
    bg_ref[...] = tail[:, :128]


def _rope_inputs(seq_len, tm):
    per_seq = seq_len // tm
    return [pl.BlockSpec((tm, 128), lambda i: (i % per_seq, 0))] * 3, list(_rope_tables(seq_len))


def _even_in_io(n, tm, seq_len, w, rope, cache):
    in_specs = [pl.BlockSpec(w.shape, lambda i: (0, 0), pipeline_mode=pl.Buffered(1))]
    args = [w]
    if rope:
        specs, tabs = _rope_inputs(seq_len, tm)
        in_specs, args = in_specs + specs, args + tabs
    outs = [(512, BF16), (KV_WIDTH, BF16)] + (["cache", "cache"] if cache else []) + \
           [(512, BF16), (512, BF16), (512, BF16), (512, F32), (128, F32)]
    return in_specs, args, _in_proj_outputs(outs, n, tm, seq_len)


ODD_COLS = (512, 512, 512, 512, 512, 512, 256)


def _head_rms(x, seg_ref, g):
    x2 = x * x
    hi = x2.astype(BF16)
    lo = (x2 - hi.astype(F32)).astype(BF16)
    w = seg_ref.shape[0]
    ms = [_dot(hi[:, c:c + w], seg_ref[...]) + _dot(lo[:, c:c + w], seg_ref[...]) for c in range(0, x.shape[1], w)]
    ms = ms[0] if len(ms) == 1 else jnp.concatenate(ms, 1)
    return x * lax.rsqrt(ms + 1e-6) * g


def _odd_in_body(x, m, ins, outs, rope, cache):
    it = iter(ins)
    w_ref, segq_ref, segk_ref, qn_ref, kn_ref = (next(it) for _ in range(5))
    tabs = (next(it), next(it), next(it)) if rope else None
    it = iter(outs)
    q_ref, ff_ref, fb_ref, v_ref, cg_ref, dq_ref, dkv_ref = (next(it) for _ in range(7))
    cache_refs = (next(it), next(it)) if cache else None

    h = (x * (1.0 + m[4:5]) + m[3:4]).astype(BF16)
    offs = [0]
    for c in ODD_COLS:
        offs.append(offs[-1] + c)
    proj = lambda k: _dot(h, w_ref[:, offs[k]:offs[k + 1]])

    q_ref[...] = _silu(proj(0)).astype(BF16)
    ff_ref[...] = proj(1)
    fb_ref[...] = proj(2)
    v_ref[...] = proj(3).astype(BF16)
    cg_ref[...] = proj(4)
    dq = _head_rms(proj(5), segq_ref, qn_ref[...])
    dkv = proj(6)
    dk = _head_rms(dkv[:, :128], segk_ref, kn_ref[...])
    dv = dkv[:, 128:]
    if cache:
        _store_cache(*cache_refs, jnp.concatenate([dk, dv], 1))
    if rope:
        cos, s_up, s_dn = (t[...] for t in tabs)
        dq = _rope(dq, cos, s_up, s_dn)
        dk = _rope(dk, cos, s_up, s_dn)
    dq_ref[...] = (dq * QK_SCALE).astype(BF16)
    dkv_ref[...] = _kv_with_ones(jnp.concatenate([dk, dv], 1)).astype(BF16)


def _segment_mean_matrix(width):
    r = jnp.arange(width) // HEAD_DIM
    return jnp.where(r[:, None] == r[None, :], 1.0 / HEAD_DIM, 0.0).astype(BF16)


def _odd_in_io(n, tm, seq_len, w, q_norm, k_norm, rope, cache):
    const = lambda a: pl.BlockSpec(a.shape, lambda i: (0, 0))
    segq, segk = _segment_mean_matrix(256), _segment_mean_matrix(128)
    qn = jnp.tile(q_norm, D_HEADS).reshape(1, 512)
    kn = jnp.tile(k_norm, D_KV).reshape(1, 128)
    in_specs = [pl.BlockSpec(w.shape, lambda i: (0, 0), pipeline_mode=pl.Buffered(1)),
                const(segq), const(segk), const(qn), const(kn)]
    args = [w, segq, segk, qn, kn]
    if rope:
        specs, tabs = _rope_inputs(seq_len, tm)
        in_specs, args = in_specs + specs, args + tabs
    outs = [(512, BF16), (512, F32), (512, F32), (512, BF16), (512, F32), (512, BF16), (KV_WIDTH, BF16)] + \
           (["cache", "cache"] if cache else [])
    return in_specs, args, _in_proj_outputs(outs, n, tm, seq_len)


def _attn_kernel(*refs, seq_len, tq, q_blocks, n_ctx, banded, has_sink):
    it = iter(refs)
    q_ref, kv_ref = next(it), next(it)
    ckv_ref = next(it) if n_ctx else None
    sink_ref = next(it) if has_sink else None
    o_ref = next(it)
    groups = A_HEADS // A_KV
    rows = groups * tq
    low_half = lax.broadcasted_iota(jnp.int32, (tq, 2 * HEAD_DIM), 1) < HEAD_DIM

    def run_kv_head(blk, kh):
        r0 = blk * tq
        j = pl.program_id(1) * q_blocks + blk
        kcol = slice(kh * HEAD_DIM, (kh + 1) * HEAD_DIM)
        vcol = slice(128 + kh * 256, 128 + (kh + 1) * 256)
        qs = jnp.concatenate([q_ref[r0:r0 + tq, (kh * groups + g) * HEAD_DIM:(kh * groups + g + 1) * HEAD_DIM]
                              for g in range(groups)], axis=0)
        tiles = []
        if banded:
            span = tq + 2 * WINDOW
            start = pl.multiple_of(jnp.clip(j * tq - WINDOW, 0, seq_len - span), WINDOW)
            qpos = j * tq + (lax.broadcasted_iota(jnp.int32, (rows, span), 0) & (tq - 1))
            kpos = start + lax.broadcasted_iota(jnp.int32, (rows, span), 1)
            band = jnp.abs(kpos - qpos) <= WINDOW
            tiles.append((kv_ref[pl.ds(start, span), kcol], kv_ref[pl.ds(start, span), vcol], band))
        else:
            tk = min(ATT_KEY_TILE, seq_len)
            for t in range(seq_len // tk):
                tiles.append((kv_ref[t * tk:(t + 1) * tk, kcol], kv_ref[t * tk:(t + 1) * tk, vcol], None))
        scores = [_dot_nt(qs, k_t) for k_t, _, _ in tiles]
        if n_ctx:
            tiles.append((None, ckv_ref[0, kh, HEAD_DIM:, :], None))
            scores.append(_dot(qs, ckv_ref[0, kh, :HEAD_DIM, :]))
        yield
        scores = [s if mask is None else jnp.where(mask, s, NEG_INF) for s, (_, _, mask) in zip(scores, tiles)]
        m = functools.reduce(jnp.maximum, [jnp.max(s, -1, keepdims=True) for s in scores])
        if has_sink:
            sink = jnp.concatenate([jnp.full((tq, 1), sink_ref[kh * groups + g] * LOG2_E, F32)
                                    for g in range(groups)], 0)
            m = jnp.maximum(m, sink)
        pv = functools.reduce(lambda a, b: a + b,
                              [(_dot_nt if k_t is None else _dot)(jnp.exp2(s - m).astype(BF16), v_t)
                               for s, (k_t, v_t, _) in zip(scores, tiles)])
        yield
        total = pv[:, 2 * HEAD_DIM:]
        if has_sink:
            total = total + jnp.exp2(sink - m)
        o = pv[:, :2 * HEAD_DIM] / total
        for g in range(0, groups, 2):
            pair = jnp.where(low_half, o[g * tq:(g + 1) * tq], o[(g + 1) * tq:(g + 2) * tq])
            c0 = (kh * groups + g) * HEAD_DIM
            o_ref[r0:r0 + tq, c0:c0 + 2 * HEAD_DIM] = pair.astype(BF16)

    run = _run_staggered if seq_len + n_ctx > 2 * ATT_KEY_TILE else _run_interleaved
    run([run_kv_head(blk, kh) for blk in range(q_blocks) for kh in range(A_KV)])


def _attention(q, kv, ctx_kv, sink, batch, seq_len, tq, q_blocks, banded):
    n_ctx = 0 if ctx_kv is None else ctx_kv.shape[-1]
    per_seq = seq_len // (tq * q_blocks)
    in_specs = [pl.BlockSpec((q_blocks * tq, 512), lambda b, j: (b * per_seq + j, 0)),
                pl.BlockSpec((seq_len, KV_WIDTH), lambda b, j: (b, 0))]
    args = [q, kv]
    if n_ctx:
        in_specs.append(pl.BlockSpec((1,) + ctx_kv.shape[1:], lambda b, j: (b, 0, 0, 0)))
        args.append(ctx_kv)
    if sink is not None:
        in_specs.append(pl.BlockSpec(memory_space=pltpu.SMEM))
        args.append(sink.reshape(-1).astype(F32))
    return pl.pallas_call(
        functools.partial(_attn_kernel, seq_len=seq_len, tq=tq, q_blocks=q_blocks, n_ctx=n_ctx, banded=banded,
                          has_sink=sink is not None),
        grid=(batch, per_seq),
        in_specs=in_specs,
        out_specs=pl.BlockSpec((q_blocks * tq, 512), lambda b, j: (b * per_seq + j, 0)),
        out_shape=jax.ShapeDtypeStruct(q.shape, BF16),
        compiler_params=_cparams(("parallel", "arbitrary")),
        name="gqa_attention",
    )(*args)


def _log_sigmoid(x):
    return jnp.minimum(x, 0.0) - jnp.log1p(jnp.exp(-jnp.abs(x)))


def _mlstm_kernel(*refs, seq_len, has_init, emit_state):
    it = iter(refs)
    q_ref, k_ref, v_ref, g_ref, bo_ref, gb_ref, ng_ref = (next(it) for _ in range(7))
    c0_ref, n0_ref, m0_ref = (next(it), next(it), next(it)) if has_init else (None, None, None)
    y_ref = next(it)
    co_ref, no_ref, mo_ref = (next(it), next(it), next(it)) if emit_state else (None, None, None)
    c_scr, n_scr, m_scr, hf_scr, hb_scr, ig_buf, bc_buf, ut_buf = (next(it) for _ in range(8))

    t_len = MLSTM_T
    n_chunks = seq_len // t_len
    n_streams = 2 * B_HEADS

    for s in range(n_streams):
        if has_init:
            c_scr[s] = c0_ref[0, s // B_HEADS, s % B_HEADS]
        else:
            c_scr[s] = jnp.zeros((B_DIM, B_DIM), F32)
    if has_init:
        n_scr[...] = n0_ref[0]
        m_scr[...] = m0_ref[0]
    else:
        n_scr[...] = jnp.zeros_like(n_scr)
        m_scr[...] = jnp.zeros_like(m_scr)

    ri = lax.broadcasted_iota(jnp.int32, (t_len, t_len), 0)
    ci = lax.broadcasted_iota(jnp.int32, (t_len, t_len), 1)
    lower = ci <= ri
    upper = ci >= ri
    tri_f = jnp.where(lower, 1.0, 0.0).astype(BF16)
    tri_b = jnp.where(upper, 1.0, 0.0).astype(BF16)
    gate_bias = gb_ref[...]
    sel_row = lax.broadcasted_iota(jnp.int32, (128, B_HEADS * B_DIM), 0)
    sel_head = lax.broadcasted_iota(jnp.int32, (128, B_HEADS * B_DIM), 1) // B_DIM
    gate_select = []
    for direction in range(2):
        i_base = 2 * direction * B_HEADS
        gate_select.append((jnp.where(sel_row == i_base + sel_head, 1.0, 0.0).astype(BF16),
                            jnp.where(sel_row == i_base + B_HEADS + sel_head, 1.0, 0.0).astype(BF16)))

    def chunk_rows(c, direction):
        chunk = c if direction == 0 else n_chunks - 1 - c
        return pl.ds(pl.multiple_of(chunk * t_len, t_len), t_len)

    def prepare_gates(c, direction):
        slot = c % 2
        rows = chunk_rows(jnp.minimum(c, n_chunks - 1), direction)
        tri = tri_f if direction == 0 else tri_b
        sel_i, sel_f = gate_select[direction]
        gates = g_ref[rows, :] + gate_bias
        hi, mid, lo = _split3(_log_sigmoid(gates))
        b3 = _dot(tri, jnp.concatenate([hi, mid, lo], 1))
        ig_buf[slot, direction] = _dot_sel_rhs(gates, sel_i)
        yield
        bc_all = b3[:, :128] + b3[:, 128:256] + b3[:, 256:]
        bc_rep = _dot_sel_rhs(bc_all, sel_f)
        ut_buf[slot, direction] = (gates - pltpu.roll(bc_all, 128 - B_HEADS, 1)).T
        yield
        bc_buf[slot, direction] = bc_rep

    def run_direction(c, direction):
        rows = chunk_rows(c, direction)
        slot = c % 2
        causal = lower if direction == 0 else upper
        last = t_len - 1 if direction == 0 else 0
        h_out = hf_scr if direction == 0 else hb_scr
        ig_rep, bc_rep, u_t = ig_buf[slot, direction], bc_buf[slot, direction], ut_buf[slot, direction]
        heads = []
        for hd in range(B_HEADS):
            s = direction * B_HEADS + hd
            col = slice(hd * B_DIM, (hd + 1) * B_DIM)
            qc, kc, vc = q_ref[rows, col], k_ref[rows, col], v_ref[rows, col]
            c_prev = c_scr[s]
            n_prev = n_scr[s:s + 1, :]
            qk = _dot_nt(qc, kc)
            q_state = _dot(qc, c_prev.astype(BF16))
            qn = _dot_nt(qc, jnp.broadcast_to(n_prev, (B_DIM, B_DIM)).astype(BF16))
            heads.append((s, col, qc, kc, vc, c_prev, n_prev, qk, q_state, qn))
        yield
        staged = []
        for (s, col, qc, kc, vc, c_prev, n_prev, qk, q_state, qn) in heads:
            i_col = 2 * direction * B_HEADS + (s % B_HEADS)
            ig, bc = ig_rep[:, col], bc_rep[:, col]
            dmat = jnp.where(causal, bc[:, :t_len] + u_t[i_col:i_col + 1, :], -jnp.inf)
            m_prev = m_scr[s:s + 1, :]
            inter = bc + m_prev
            m_t = jnp.maximum(inter, jnp.max(dmat, -1, keepdims=True))
            w = (jnp.exp(dmat - m_t[:, :t_len]) * qk).astype(BF16)
            wv = _dot(w, jnp.concatenate([vc, jnp.ones_like(vc)], 1))
            b_tot = bc[last:last + 1, :]
            g = b_tot - bc + ig
            m_new = jnp.maximum(b_tot + m_prev, jnp.max(g, 0, keepdims=True))
            kw = kc.astype(F32) * jnp.exp(g - m_new)
            upd = _dot_tn(kw.astype(BF16), vc)
            decay = jnp.exp(b_tot + m_prev - m_new)
            staged.append((s, col, inter, m_t, c_prev, n_prev, q_state, qn, wv, m_new, kw, upd, decay))
        yield
        for (s, col, inter, m_t, c_prev, n_prev, q_state, qn, wv, m_new, kw, upd, decay) in staged:
            a = jnp.exp(inter - m_t)
            num = a * q_state + wv[:, :B_DIM]
            den = a * qn + wv[:, B_DIM:]
            h_out[rows, col] = num / jnp.maximum(jnp.abs(den), jnp.exp(-m_t))
            c_scr[s] = decay * c_prev + upd
            n_scr[s:s + 1, :] = decay * n_prev + jnp.sum(kw, 0, keepdims=True)
            m_scr[s:s + 1, :] = m_new

    def chained(c, direction):
        yield from prepare_gates(c, direction)
        yield
        yield from run_direction(c, direction)

    def step(c, carry):
        _run_interleaved([chained(c, 0), chained(c, 1)])
        return carry

    lax.fori_loop(0, n_chunks, step, 0, unroll=2)

    blk = min(256, seq_len)
    ng = ng_ref[...]

    def finish(i, carry):
        rows = pl.ds(pl.multiple_of(i * blk, blk), blk)
        hsum = hf_scr[rows, :] + hb_scr[rows, :]
        gate = jax.nn.sigmoid(bo_ref[rows, :])
        parts = [_rms_lastdim(hsum[:, hd * B_DIM:(hd + 1) * B_DIM], ng[:, hd * B_DIM:(hd + 1) * B_DIM])
                 for hd in range(B_HEADS)]
        y_ref[rows, :] = (gate * jnp.concatenate(parts, 1)).astype(BF16)
        return carry

    lax.fori_loop(0, seq_len // blk, finish, 0)

    if emit_state:
        for s in range(n_streams):
            co_ref[0, s // B_HEADS, s % B_HEADS] = c_scr[s]
        no_ref[0] = n_scr[...]
        mo_ref[0] = m_scr[...]


def _mlstm(q, k, v, gates, bo, gate_bias, norm_g, init, batch, seq_len, emit_state):
    n = q.shape[0]
    row = lambda width: pl.BlockSpec((seq_len, width), lambda b: (b, 0))
    const = lambda a: pl.BlockSpec(a.shape, lambda b: (0,) * a.ndim)
    gb = jnp.zeros((1, 128), F32).at[0, :4 * B_HEADS].set(gate_bias.reshape(-1))
    ng = norm_g.reshape(1, B_HEADS * B_DIM)
    in_specs = [row(512), row(512), row(512), row(128), row(512), const(gb), const(ng)]
    args = [q, k, v, gates, bo, gb, ng]
    n_streams = 2 * B_HEADS
    if init is not None:
        c0, n0, m0 = init
        in_specs += [pl.BlockSpec((1, 2, B_HEADS, B_DIM, B_DIM), lambda b: (b, 0, 0, 0, 0)),
                     pl.BlockSpec((1, n_streams, B_DIM), lambda b: (b, 0, 0)),
                     pl.BlockSpec((1, n_streams, B_DIM), lambda b: (b, 0, 0))]
        args += [c0, n0.reshape(batch, n_streams, B_DIM),
                 jnp.broadcast_to(m0.reshape(batch, n_streams, 1), (batch, n_streams, B_DIM))]
    out_specs = [row(512)]
    out_shape = [jax.ShapeDtypeStruct((n, 512), BF16)]
    if emit_state:
        out_specs += [pl.BlockSpec((1, 2, B_HEADS, B_DIM, B_DIM), lambda b: (b, 0, 0, 0, 0)),
                      pl.BlockSpec((1, n_streams, B_DIM), lambda b: (b, 0, 0)),
                      pl.BlockSpec((1, n_streams, B_DIM), lambda b: (b, 0, 0))]
        out_shape += [jax.ShapeDtypeStruct((batch, 2, B_HEADS, B_DIM, B_DIM), F32),
                      jax.ShapeDtypeStruct((batch, n_streams, B_DIM), F32),
                      jax.ShapeDtypeStruct((batch, n_streams, B_DIM), F32)]
    return pl.pallas_call(
        functools.partial(_mlstm_kernel, seq_len=seq_len, has_init=init is not None, emit_state=emit_state),
        grid=(batch,),
        in_specs=in_specs,
        out_specs=out_specs,
        out_shape=out_shape,
        scratch_shapes=[pltpu.VMEM((n_streams, B_DIM, B_DIM), F32),
                        pltpu.VMEM((n_streams, B_DIM), F32),
                        pltpu.VMEM((n_streams, B_DIM), F32),
                        pltpu.VMEM((seq_len, 512), F32),
                        pltpu.VMEM((seq_len, 512), F32),
                        pltpu.VMEM((2, 2, MLSTM_T, B_HEADS * B_DIM), F32),
                        pltpu.VMEM((2, 2, MLSTM_T, B_HEADS * B_DIM), F32),
                        pltpu.VMEM((2, 2, 128, MLSTM_T), F32)],
        compiler_params=_cparams(("parallel",)),
        name="mlstm_scan",
    )(*args)


def _hgrn_kernel(*refs, seq_len, layer, has_init, emit_state):
    it = iter(refs)
    q_ref, ff_ref, fb_ref, v_ref, cg_ref, lbl_ref, ng_ref = (next(it) for _ in range(7))
    s0_ref = next(it) if has_init else None
    y_ref = next(it)
    so_ref = next(it) if emit_state else None
    st_scr, of_scr, ob_scr = next(it), next(it), next(it)

    t_len = HGRN_T
    n_sub = t_len // SUB
    n_chunks = seq_len // t_len
    n_streams = 2 * C_HEADS

    logits = lbl_ref[...]
    e = jnp.exp(logits - jnp.max(logits, 0, keepdims=True))
    sm = e / jnp.sum(e, 0, keepdims=True)
    lb = jnp.sum(sm[0:layer + 1], 0, keepdims=True) - sm[0:1]

    for s in range(n_streams):
        if has_init:
            st_scr[s] = s0_ref[0, s // C_HEADS, s % C_HEADS].T
        else:
            st_scr[s] = jnp.zeros((C_DV, C_DK), F32)

    ri = lax.broadcasted_iota(jnp.int32, (t_len, t_len), 0)
    ci = lax.broadcasted_iota(jnp.int32, (t_len, t_len), 1)
    tri_f = jnp.where(ci <= ri, 1.0, 0.0).astype(BF16)
    tri_b = jnp.where(ci >= ri, 1.0, 0.0).astype(BF16)
    sub_row = lax.broadcasted_iota(jnp.int32, (SUB, C_DK), 0)
    ones_dk = jnp.ones((C_DK, C_DK), BF16)

    def run_stream(c, direction, hd):
        chunk = c if direction == 0 else n_chunks - 1 - c
        rows = pl.ds(pl.multiple_of(chunk * t_len, t_len), t_len)
        f_ref = ff_ref if direction == 0 else fb_ref
        tri = tri_f if direction == 0 else tri_b
        last = t_len - 1 if direction == 0 else 0
        o_out = of_scr if direction == 0 else ob_scr
        s = direction * C_HEADS + hd
        col = slice(hd * C_DK, (hd + 1) * C_DK)
        lbh = lb[:, col]
        f = lbh + (1.0 - lbh) * jax.nn.sigmoid(f_ref[rows, col])
        log_k = jnp.log2(1.0 - f)
        hi, mid, lo = _split3(jnp.log2(f))
        a3 = _dot(tri, jnp.concatenate([hi, mid, lo], 1))
        yield
        a_cum = a3[:, :C_DK] + a3[:, C_DK:2 * C_DK] + a3[:, 2 * C_DK:]
        a_key = a_cum - log_k
        a_tot = a_cum[last:last + 1, :]
        qf = q_ref[rows, col].astype(F32)
        vc = v_ref[rows, col]
        vf = vc.astype(F32)
        st = st_scr[s]
        inter = _dot_nt((qf * jnp.exp2(a_cum)).astype(BF16), st.astype(BF16))
        upd = _dot_tn(vc, jnp.exp2(a_tot - a_key).astype(BF16))
        ps = []
        for blk in range(n_sub):
            b0 = blk * SUB
            a_i, q_i, ak_i = (t[b0:b0 + SUB] for t in (a_cum, qf, a_key))
            for j in range(SUB):
                seen = (sub_row >= j) if direction == 0 else (sub_row <= j)
                ps.append(jnp.where(seen, jnp.exp2(a_i - ak_i[j:j + 1]), 0.0) * q_i)
        att = _dot(jnp.concatenate(ps, 0).astype(BF16), ones_dk)
        off = []
        for blk in range(n_sub):
            b0 = blk * SUB
            if direction == 0 and blk > 0:
                a_ref, kr = a_cum[b0 - 1:b0], slice(0, b0)
            elif direction == 1 and blk < n_sub - 1:
                a_ref, kr = a_cum[b0 + SUB:b0 + SUB + 1], slice(b0 + SUB, t_len)
            else:
                off.append(None)
                continue
            qt = (qf[b0:b0 + SUB] * jnp.exp2(a_cum[b0:b0 + SUB] - a_ref)).astype(BF16)
            kt = jnp.exp2(a_ref - a_key[kr]).astype(BF16)
            off.append((_dot_nt(qt, kt), kr))
        yield
        st_scr[s] = jnp.exp2(a_tot) * st + upd
        outs = []
        for blk in range(n_sub):
            b0 = blk * SUB
            o_i = inter[b0:b0 + SUB]
            for j in range(SUB):
                r = (blk * SUB + j) * SUB
                o_i = o_i + att[r:r + SUB] * vf[b0 + j:b0 + j + 1]
            if off[blk] is not None:
                att_off, kr = off[blk]
                outs.append((o_i, _dot(att_off.astype(BF16), vc[kr])))
            else:
                outs.append((o_i, None))
        yield
        o_out[rows, col] = jnp.concatenate([o_i if o_off is None else o_i + o_off for o_i, o_off in outs], 0)

    def step(c, carry):
        _run_interleaved([run_stream(c, direction, hd) for direction in range(2) for hd in range(C_HEADS)])
        return carry

    lax.fori_loop(0, n_chunks, step, 0, unroll=2)

    blk_rows = min(256, seq_len)
    ng = ng_ref[...]

    def finish(i, carry):
        rows = pl.ds(pl.multiple_of(i * blk_rows, blk_rows), blk_rows)
        osum = of_scr[rows, :] + ob_scr[rows, :]
        parts = [_rms_lastdim(osum[:, hd * C_DV:(hd + 1) * C_DV], ng[:, hd * C_DV:(hd + 1) * C_DV])
                 for hd in range(C_HEADS)]
        y_ref[rows, :] = (jnp.concatenate(parts, 1) * _silu(cg_ref[rows, :])).astype(BF16)
        return carry

    lax.fori_loop(0, seq_len // blk_rows, finish, 0)

    if emit_state:
        for s in range(n_streams):
            so_ref[0, s // C_HEADS, s % C_HEADS] = st_scr[s].T


def _hgrn(q, ff, fb, v, cg, lb_logits, norm_g, layer, init, batch, seq_len, emit_state):
    n = q.shape[0]
    row = lambda width: pl.BlockSpec((seq_len, width), lambda b: (b, 0))
    const = lambda a: pl.BlockSpec(a.shape, lambda b: (0,) * a.ndim)
    ng = norm_g.reshape(1, C_HEADS * C_DV)
    state_spec = pl.BlockSpec((1, 2, C_HEADS, C_DK, C_DV), lambda b: (b, 0, 0, 0, 0))
    in_specs = [row(512)] * 5 + [const(lb_logits), const(ng)]
    args = [q, ff, fb, v, cg, lb_logits, ng]
    if init is not None:
        in_specs.append(state_spec)
        args.append(init)
    out_specs = [row(512)]
    out_shape = [jax.ShapeDtypeStruct((n, 512), BF16)]
    if emit_state:
        out_specs.append(state_spec)
        out_shape.append(jax.ShapeDtypeStruct((batch, 2, C_HEADS, C_DK, C_DV), F32))
    return pl.pallas_call(
        functools.partial(_hgrn_kernel, seq_len=seq_len, layer=layer, has_init=init is not None,
                          emit_state=emit_state),
        grid=(batch,),
        in_specs=in_specs,
        out_specs=out_specs,
        out_shape=out_shape,
        scratch_shapes=[pltpu.VMEM((2 * C_HEADS, C_DV, C_DK), F32),
                        pltpu.VMEM((seq_len, 512), F32),
                        pltpu.VMEM((seq_len, 512), F32)],
        compiler_params=_cparams(("parallel",)),
        name="hgrn2_scan",
    )(*args)


def _prep_ctx_kv(k_ctx, v_ctx):
    k_t = jnp.transpose(k_ctx, (0, 2, 3, 1)).astype(BF16)
    v_t = jnp.transpose(v_ctx, (0, 2, 3, 1)).astype(BF16)
    ones = jnp.ones(k_t.shape[:2] + (128, k_t.shape[3]), BF16)
    return jnp.concatenate([k_t, v_t, v_t, ones], 2)


def kernel(x_prompt, x_sample, c, cache_a_k, cache_a_v, state_b_C, state_b_n, state_b_m, state_c_S, cache_d_k, cache_d_v, c_ctx, ada_w, ada_b, ln_g, ln_b, ffn_w1, ffn_w3, ffn_w2, w_in_even, w_out_even, a_sink, b_gate_bias, b_norm_g, w_in_odd, w_out_odd, c_lb_logits, c_norm_g, d_q_norm, d_k_norm):
    batch_p, len_p, d = x_prompt.shape
    batch_s, len_s, _ = x_sample.shape

    cvec = jnp.concatenate([c_ctx[None], c, jnp.zeros((MOD_ROWS - 1 - batch_s, d), F32)], 0)
    mod_all = _modulation(cvec, ada_w, ada_b)

    groups = [
        dict(x=x_prompt.reshape(batch_p * len_p, d), batch=batch_p, seq=len_p, prompt=True,
             mod_index=_mod_index(0, None)),
        dict(x=x_sample.reshape(batch_s * len_s, d), batch=batch_s, seq=len_s, prompt=False,
             mod_index=_mod_index(1, len_s)),
    ]
    new = {}
    ffn_w = (ffn_w1.astype(BF16), ffn_w3.astype(BF16), ffn_w2.astype(BF16))

    for l in range(DEPTH):
        mod = mod_all[l].reshape(MOD_ROWS, 9, d)
        i = l // 2
        if l % 2 == 0:
            w_in = w_in_even[i].astype(BF16)
            w_out = w_out_even[i].astype(BF16)
        else:
            w_in = w_in_odd[i].astype(BF16)
            w_out = w_out_odd[i].astype(BF16)

        for grp in groups:
            x, mi, nb, sl, prompt = grp["x"], grp["mod_index"], grp["batch"], grp["seq"], grp["prompt"]
            io_kw = dict(seq_len=sl, w=w_in, rope=not prompt, cache=prompt)
            if l % 2 == 0:
                in_proj = (_even_in_body, _even_in_io, io_kw)
            else:
                in_proj = (_odd_in_body, _odd_in_io, dict(io_kw, q_norm=d_q_norm[i], k_norm=d_k_norm[i]))
            x, *outs = _ffn(x, mod, 0, ln_g[l, 0], ln_b[l, 0], *ffn_w, l, 0, mi, in_proj=in_proj)
            if l % 2 == 0:
                if prompt:
                    aq, akv, k_cache, v_cache, bq, bk, bv, bo, bg = outs
                    new["a_k"], new["a_v"] = _cache_layout(k_cache), _cache_layout(v_cache)
                    ya = _attention(aq, akv, None, a_sink[i], nb, sl, tq=Q_BLOCK, q_blocks=sl // Q_BLOCK, banded=False)
                    yb, c_new, n_new, m_new = _mlstm(bq, bk, bv, bg, bo, b_gate_bias[i], b_norm_g[i], None,
                                                     nb, sl, emit_state=True)
                    new["b_C"] = c_new[:, None]
                    new["b_n"] = n_new.reshape(nb, 1, 2, B_HEADS, B_DIM)
                    new["b_m"] = m_new[:, :, 0].reshape(nb, 1, 2, B_HEADS)
                else:
                    aq, akv, bq, bk, bv, bo, bg = outs
                    ya = _attention(aq, akv, _prep_ctx_kv(cache_a_k[:, i], cache_a_v[:, i]), a_sink[i], nb, sl,
                                    tq=Q_BLOCK, q_blocks=4, banded=True)
                    init = (state_b_C[:, i], state_b_n[:, i], state_b_m[:, i])
                    yb, = _mlstm(bq, bk, bv, bg, bo, b_gate_bias[i], b_norm_g[i], init, nb, sl, emit_state=False)
                mixed = (ya, yb)
            else:
                if prompt:
                    cq, ff, fb, cv, cg, dq, dkv, k_cache, v_cache = outs
                    new["d_k"], new["d_v"] = _cache_layout(k_cache), _cache_layout(v_cache)
                    yc, s_new = _hgrn(cq, ff, fb, cv, cg, c_lb_logits, c_norm_g[i], l, None, nb, sl, emit_state=True)
                    new["c_S"] = s_new[:, None]
                    yd = _attention(dq, dkv, None, None, nb, sl, tq=Q_BLOCK, q_blocks=sl // Q_BLOCK, banded=False)
                else:
                    cq, ff, fb, cv, cg, dq, dkv = outs
                    yc, = _hgrn(cq, ff, fb, cv, cg, c_lb_logits, c_norm_g[i], l, state_c_S[:, i], nb, sl,
                                emit_state=False)
                    yd = _attention(dq, dkv, _prep_ctx_kv(cache_d_k[:, i], cache_d_v[:, i]), None, nb, sl,
                                    tq=Q_BLOCK, q_blocks=2, banded=False)
                mixed = (yc, yd)
            x, = _ffn(x, mod, 2, ln_g[l, 2], ln_b[l, 2], *ffn_w, l, 1, mi,
                      mixer=mixed + (w_out, ln_g[l, 1], ln_b[l, 1]))
            grp["x"] = x

    y_prompt = groups[0]["x"].reshape(batch_p, len_p, d)
    y_sample = groups[1]["x"].reshape(batch_s, len_s, d)
    return (y_prompt, y_sample, new["a_k"], new["a_v"], new["b_C"], new["b_n"], new["b_m"], new["c_S"],
            new["d_k"], new["d_v"])
```

```python
import functools

import jax
import jax.numpy as jnp
import numpy as np
from jax import lax
from jax.experimental import pallas as pl
from jax.experimental.pallas import tpu as pltpu

F32 = jnp.float32
BF16 = jnp.bfloat16

D_MODEL = 1024
DEPTH = 2
GRID_W = 64
HEAD_DIM = 64
A_HEADS = 8
A_KV = 2
WINDOW = 128
B_HEADS = 4
B_DIM = 128
C_HEADS = 4
C_DK = 128
C_DV = 128
D_HEADS = 8
D_KV = 2
Q_BLOCK = 128
ROPE_THETA = 10000.0
ALPHA = (2 * DEPTH) ** 0.25
NEG_INF = -1e30
LOG2_E = 1.4426950408889634
QK_SCALE = HEAD_DIM ** -0.5 * LOG2_E

MOD_ROWS = 8
FF_CHUNK = 256
FFN_TILE = 512
MLSTM_T = 64
HGRN_T = 64
SUB = 8
ATT_KEY_TILE = 512
VMEM_LIMIT = 56 * 1024 * 1024


def _cparams(sem):
    return pltpu.CompilerParams(dimension_semantics=sem, vmem_limit_bytes=VMEM_LIMIT)


def _dot(a, b):
    return jnp.dot(a, b, preferred_element_type=F32)


def _dot_nt(a, b):
    return lax.dot_general(a, b, (((1,), (1,)), ((), ())), preferred_element_type=F32)


def _dot_tn(a, b):
    return lax.dot_general(a, b, (((0,), (0,)), ((), ())), preferred_element_type=F32)


def _split3(x):
    hi = x.astype(BF16)
    r1 = x - hi.astype(F32)
    mid = r1.astype(BF16)
    lo = (r1 - mid.astype(F32)).astype(BF16)
    return hi, mid, lo


def _dot_sel_rhs(x, sel):
    hi = x.astype(BF16)
    lo = (x - hi.astype(F32)).astype(BF16)
    r = _dot(jnp.concatenate([hi, lo], 0), sel)
    return r[:x.shape[0]] + r[x.shape[0]:]


def _run_interleaved(gens):
    live = list(gens)
    while live:
        nxt = []
        for g in live:
            try:
                next(g)
                nxt.append(g)
            except StopIteration:
                pass
        live = nxt


def _run_staggered(gens):
    pending, live = list(gens), []
    while pending or live:
        if pending:
            live.insert(0, pending.pop(0))
        nxt = []
        for g in live:
            try:
                next(g)
                nxt.append(g)
            except StopIteration:
                pass
        live = nxt


def _silu(x):
    return x * jax.nn.sigmoid(x)


def _layernorm(z, g, b):
    mu = jnp.mean(z, -1, keepdims=True)
    zc = z - mu
    var = jnp.mean(zc * zc, -1, keepdims=True)
    return zc * lax.rsqrt(var + 1e-5) * g + b


def _rms_lastdim(x, g):
    return x * lax.rsqrt(jnp.mean(x * x, -1, keepdims=True) + 1e-6) * g


class _Layout:
    def __init__(self, batch_p, len_p, batch_s, len_s):
        self.tm = FFN_TILE
        self.len_p, self.len_s = len_p, len_s
        self.n_p, self.n_s = batch_p * len_p, batch_s * len_s
        self.n = self.n_p + self.n_s
        self.steps_p = self.n_p // self.tm
        self.steps = self.n // self.tm

    def all_rows(self, i):
        return (i, 0)

    def prompt_rows(self, i):
        return (jnp.minimum(i, self.steps_p - 1), 0)

    def sample_rows(self, i):
        return (jnp.maximum(i - self.steps_p, 0), 0)

    def mod_row(self, i):
        req = jnp.maximum(i - self.steps_p, 0) // (self.len_s // self.tm)
        return (jnp.where(i < self.steps_p, 0, 1 + req), 0, 0)

    def rope_rows(self, i):
        return (jnp.maximum(i - self.steps_p, 0) % (self.len_s // self.tm), 0)


def _mod_kernel(c_ref, w_ref, b_ref, o_ref):
    s = _silu(c_ref[...]).astype(BF16)
    o_ref[0] = _dot(s, w_ref[0].astype(BF16)) + b_ref[0]


def _modulation(cvec, ada_w, ada_b):
    depth, d, n = ada_w.shape
    tn = 1152
    return pl.pallas_call(
        _mod_kernel,
        grid=(depth, n // tn),
        in_specs=[pl.BlockSpec((MOD_ROWS, d), lambda l, j: (0, 0)),
                  pl.BlockSpec((1, d, tn), lambda l, j: (l, 0, j)),
                  pl.BlockSpec((1, 1, tn), lambda l, j: (l, 0, j))],
        out_specs=pl.BlockSpec((1, MOD_ROWS, tn), lambda l, j: (l, 0, j)),
        out_shape=jax.ShapeDtypeStruct((depth, MOD_ROWS, n), F32),
        compiler_params=_cparams(("parallel", "parallel")),
        name="modulation",
    )(cvec, ada_w, ada_b.reshape(depth, 1, n))


def _ffn_kernel(*refs, j, nf, steps_p, n_x, n_out, mixer, in_proj):
    it = iter(refs)
    x_refs = [next(it) for _ in range(n_x)]
    mod_ref = next(it)
    if mixer:
        ya_p, yb_p, ya_s, yb_s, wo_ref, g1_ref, b1_ref = (next(it) for _ in range(7))
    w1_ref, w3_ref, w2_ref, g_ref, b_ref = (next(it) for _ in range(5))
    proj_ins = [next(it) for _ in range(in_proj[1])] if in_proj else []
    o_refs = [next(it) for _ in range(n_out)]
    proj_outs = [next(it) for _ in range(in_proj[2])] if in_proj else []
    cache_refs = (next(it), next(it)) if in_proj else None
    h_ref, acc_ref, res_ref = next(it), next(it), next(it)

    is_prompt = pl.program_id(0) < steps_p
    pick = lambda p_ref, s_ref: jnp.where(is_prompt, p_ref[...], s_ref[...])
    m = mod_ref[0]
    shift, scale, gate = m[3 * j:3 * j + 1], m[3 * j + 1:3 * j + 2], m[3 * j + 2:3 * j + 3]
    x = x_refs[0][...] if n_x == 1 else pick(*x_refs)
    if mixer:
        ya, yb = pick(ya_p, ya_s), pick(yb_p, yb_s)
        half = ya.shape[1]
        y = _dot(ya, wo_ref[:half, :]) + _dot(yb, wo_ref[half:, :])
        x = _layernorm(ALPHA * x + m[5:6] * y, g1_ref[...], b1_ref[...])
    res_ref[...] = x
    h_ref[...] = (x * (1.0 + scale) + shift).astype(BF16)
    for f in range(nf):
        cols = slice(f * FF_CHUNK, (f + 1) * FF_CHUNK)
        h = h_ref[...]
        u = (_silu(_dot(h, w1_ref[0, 0, :, cols])) * _dot(h, w3_ref[0, 0, :, cols])).astype(BF16)
        y = _dot(u, w2_ref[0, 0, cols, :])
        if f == 0:
            acc_ref[...] = y
        else:
            acc_ref[...] += y
    out = _layernorm(ALPHA * res_ref[...] + 0.5 * gate * acc_ref[...], g_ref[...], b_ref[...])
    if n_out == 1:
        o_refs[0][...] = out
    else:
        res_ref[...] = out

        @pl.when(is_prompt)
        def _():
            o_refs[0][...] = res_ref[...]

        @pl.when(jnp.logical_not(is_prompt))
        def _():
            o_refs[1][...] = res_ref[...]

    if in_proj:
        @pl.when(is_prompt)
        def _():
            in_proj[0](o_refs[0][...], m, proj_ins, proj_outs, rope=False, cache_refs=cache_refs)

        @pl.when(jnp.logical_not(is_prompt))
        def _():
            in_proj[0](o_refs[0][...], m, proj_ins, proj_outs, rope=True, cache_refs=None)


def _ffn(xs, mod, j, g, b, w1, w3, w2, layer, which, lay, mixer=None, in_proj=None, split_out=False):
    d = xs[0].shape[1]
    nf = w1.shape[-1] // FF_CHUNK
    tm = lay.tm
    spec = lambda width, index: pl.BlockSpec((tm, width), index)
    vec = pl.BlockSpec((1, d), lambda i: (0, 0))
    whole = lambda a: pl.BlockSpec((1, 1) + a.shape[2:], lambda i: (layer, which, 0, 0),
                                   pipeline_mode=pl.Buffered(1))
    pair = (lay.prompt_rows, lay.sample_rows)
    in_specs = [spec(d, lay.all_rows)] if len(xs) == 1 else [spec(d, index) for index in pair]
    in_specs.append(pl.BlockSpec((1, 9, d), lay.mod_row))
    args = list(xs) + [mod]
    if mixer is not None:
        ya_p, yb_p, ya_s, yb_s, w_out, g1, b1 = mixer
        in_specs += [spec(ya_p.shape[1], pair[0]), spec(yb_p.shape[1], pair[0]),
                     spec(ya_s.shape[1], pair[1]), spec(yb_s.shape[1], pair[1]),
                     pl.BlockSpec(w_out.shape, lambda i: (0, 0), pipeline_mode=pl.Buffered(1)), vec, vec]
        args += [ya_p, yb_p, ya_s, yb_s, w_out, g1.reshape(1, d), b1.reshape(1, d)]
    in_specs += [whole(w1), whole(w3), whole(w2), vec, vec]
    args += [w1, w3, w2, g.reshape(1, d), b.reshape(1, d)]
    if split_out:
        out_specs = [spec(d, index) for index in pair]
        out_shape = [jax.ShapeDtypeStruct((lay.n_p, d), F32), jax.ShapeDtypeStruct((lay.n_s, d), F32)]
    else:
        out_specs, out_shape = [spec(d, lay.all_rows)], [jax.ShapeDtypeStruct((lay.n, d), F32)]
    n_out, proj = len(out_specs), None
    if in_proj is not None:
        body, io, kw = in_proj
        p_specs, p_args, (p_out_specs, p_out_shape) = io(lay, **kw)
        c_specs, c_shape = _cache_outputs(lay)
        in_specs, args = in_specs + p_specs, args + p_args
        out_specs, out_shape = out_specs + p_out_specs + c_specs, out_shape + p_out_shape + c_shape
        proj = (body, len(p_args), len(p_out_specs))
    return pl.pallas_call(
        functools.partial(_ffn_kernel, j=j, nf=nf, steps_p=lay.steps_p, n_x=len(xs), n_out=n_out,
                          mixer=mixer is not None, in_proj=proj),
        grid=(lay.steps,),
        in_specs=in_specs,
        out_specs=out_specs,
        out_shape=out_shape,
        scratch_shapes=[pltpu.VMEM((tm, d), BF16), pltpu.VMEM((tm, d), F32), pltpu.VMEM((tm, d), F32)],
        compiler_params=_cparams(("arbitrary",)),
        name="ffn_sublayer",
    )(*args)


def _rope_tables(length):
    t = np.arange(length)
    nf = HEAD_DIM // 4
    inv = ROPE_THETA ** (-np.arange(nf, dtype=np.float64) / nf)
    ang_r = (t // GRID_W)[:, None] * inv[None]
    ang_c = (t % GRID_W)[:, None] * inv[None]
    cr, sr, cc, sc = np.cos(ang_r), np.sin(ang_r), np.cos(ang_c), np.sin(ang_c)
    z = np.zeros_like(cr)
    cos = np.concatenate([cr, cr, cc, cc], 1)
    sin_up = np.concatenate([-sr, z, -sc, z], 1)
    sin_dn = np.concatenate([z, sr, z, sc], 1)
    two = lambda a: jnp.asarray(np.concatenate([a, a], 1), F32)
    return two(cos), two(sin_up), two(sin_dn)


def _rope128(x, cos, sin_up, sin_dn):
    nf = HEAD_DIM // 4
    return x * cos + pltpu.roll(x, 128 - nf, 1) * sin_up + pltpu.roll(x, nf, 1) * sin_dn


def _rope(x, cos, sin_up, sin_dn):
    parts = [_rope128(x[:, c:c + 128], cos, sin_up, sin_dn) for c in range(0, x.shape[1], 128)]
    return parts[0] if len(parts) == 1 else jnp.concatenate(parts, 1)


EVEN_COLS = (512, 256, 512, 512, 512, 528)


KV_WIDTH = 128 + A_KV * 256


def _store_cache(ck_ref, cv_ref, kv):
    seq_len = ck_ref.shape[2]
    for s in range(ck_ref.shape[0]):
        rows = slice(s * seq_len, (s + 1) * seq_len)
        ck_ref[s] = kv[rows, :128].T
        cv_ref[s] = kv[rows, 128:].T


def _in_proj_outputs(outs, lay):
    specs = [pl.BlockSpec((lay.tm, width), lay.all_rows) for width, _ in outs]
    shapes = [jax.ShapeDtypeStruct((lay.n, width), dtype) for width, dtype in outs]
    return specs, shapes


def _cache_outputs(lay):
    per_step = lay.tm // lay.len_p
    index = lambda i: lay.prompt_rows(i) + (0,)
    specs = [pl.BlockSpec((per_step, 128, lay.len_p), index)] * 2
    shapes = [jax.ShapeDtypeStruct((lay.n_p // lay.len_p, 128, lay.len_p), F32)] * 2
    return specs, shapes


def _cache_layout(c):
    nb, _, sl = c.shape
    return jnp.transpose(c.reshape(nb, 1, A_KV, HEAD_DIM, sl), (0, 1, 4, 2, 3))


def _kv_with_ones(kv):
    ones = jnp.ones((kv.shape[0], 128), kv.dtype)
    v0, v1 = kv[:, 128:192], kv[:, 192:256]
    return jnp.concatenate([kv[:, :128], v0, v0, ones, v1, v1, ones], 1)


def _even_in_body(x, m, ins, outs, rope, cache_refs):
    w_ref, *tabs = ins
    aq_ref, akv_ref, bq_ref, bk_ref, bv_ref, bo_ref, bg_ref = outs
    cache = cache_refs is not None

    h = (x * (1.0 + m[4:5]) + m[3:4]).astype(BF16)
    offs = [0]
    for c in EVEN_COLS:
        offs.append(offs[-1] + c)
    proj = lambda k: _dot(h, w_ref[:, offs[k]:offs[k + 1]])

    aq = proj(0)
    akv = proj(1)
    if cache:
        _store_cache(*cache_refs, akv)
    if rope:
        cos, s_up, s_dn = (t[...] for t in tabs)
        aq = _rope(aq, cos, s_up, s_dn)
        akv = jnp.concatenate([_rope(akv[:, :128], cos, s_up, s_dn), akv[:, 128:]], 1)
    aq_ref[...] = (aq * QK_SCALE).astype(BF16)
    akv_ref[...] = _kv_with_ones(akv).astype(BF16)
    bq_ref[...] = proj(2).astype(BF16)
    bk_ref[...] = (proj(3) * (B_DIM ** -0.5)).astype(BF16)
    bv_ref[...] = proj(4).astype(BF16)
    tail = proj(5)
    n_gates = 4 * B_HEADS
    bo_ref[...] = tail[:, n_gates:]
    bg_ref[...] = tail[:, :128]


def _rope_inputs(lay):
    return [pl.BlockSpec((lay.tm, 128), lay.rope_rows)] * 3, list(_rope_tables(lay.len_s))


def _even_in_io(lay, w):
    specs, tabs = _rope_inputs(lay)
    in_specs = [pl.BlockSpec(w.shape, lambda i: (0, 0), pipeline_mode=pl.Buffered(1))] + specs
    outs = [(512, BF16), (KV_WIDTH, BF16), (512, BF16), (512, BF16), (512, BF16), (512, F32), (128, F32)]
    return in_specs, [w] + tabs, _in_proj_outputs(outs, lay)


ODD_COLS = (512, 512, 512, 512, 512, 512, 256)


def _head_rms(x, seg_ref, g):
    x2 = x * x
    hi = x2.astype(BF16)
    lo = (x2 - hi.astype(F32)).astype(BF16)
    w = seg_ref.shape[0]
    ms = [_dot(hi[:, c:c + w], seg_ref[...]) + _dot(lo[:, c:c + w], seg_ref[...]) for c in range(0, x.shape[1], w)]
    ms = ms[0] if len(ms) == 1 else jnp.concatenate(ms, 1)
    return x * lax.rsqrt(ms + 1e-6) * g


def _odd_in_body(x, m, ins, outs, rope, cache_refs):
    w_ref, segq_ref, segk_ref, qn_ref, kn_ref, *tabs = ins
    q_ref, ff_ref, fb_ref, v_ref, cg_ref, dq_ref, dkv_ref = outs
    cache = cache_refs is not None

    h = (x * (1.0 + m[4:5]) + m[3:4]).astype(BF16)
    offs = [0]
    for c in ODD_COLS:
        offs.append(offs[-1] + c)
    proj = lambda k: _dot(h, w_ref[:, offs[k]:offs[k + 1]])

    q_ref[...] = _silu(proj(0)).astype(BF16)
    ff_ref[...] = proj(1)
    fb_ref[...] = proj(2)
    v_ref[...] = proj(3).astype(BF16)
    cg_ref[...] = proj(4)
    dq = _head_rms(proj(5), segq_ref, qn_ref[...])
    dkv = proj(6)
    dk = _head_rms(dkv[:, :128], segk_ref, kn_ref[...])
    dv = dkv[:, 128:]
    if cache:
        _store_cache(*cache_refs, jnp.concatenate([dk, dv], 1))
    if rope:
        cos, s_up, s_dn = (t[...] for t in tabs)
        dq = _rope(dq, cos, s_up, s_dn)
        dk = _rope(dk, cos, s_up, s_dn)
    dq_ref[...] = (dq * QK_SCALE).astype(BF16)
    dkv_ref[...] = _kv_with_ones(jnp.concatenate([dk, dv], 1)).astype(BF16)


def _segment_mean_matrix(width):
    r = jnp.arange(width) // HEAD_DIM
    return jnp.where(r[:, None] == r[None, :], 1.0 / HEAD_DIM, 0.0).astype(BF16)


def _odd_in_io(lay, w, q_norm, k_norm):
    const = lambda a: pl.BlockSpec(a.shape, lambda i: (0, 0))
    segq, segk = _segment_mean_matrix(256), _segment_mean_matrix(128)
    qn = jnp.tile(q_norm, D_HEADS).reshape(1, 512)
    kn = jnp.tile(k_norm, D_KV).reshape(1, 128)
    specs, tabs = _rope_inputs(lay)
    in_specs = [pl.BlockSpec(w.shape, lambda i: (0, 0), pipeline_mode=pl.Buffered(1)),
                const(segq), const(segk), const(qn), const(kn)] + specs
    outs = [(512, BF16), (512, F32), (512, F32), (512, BF16), (512, F32), (512, BF16), (KV_WIDTH, BF16)]
    return in_specs, [w, segq, segk, qn, kn] + tabs, _in_proj_outputs(outs, lay)


def _attn_kernel(*refs, seq_len, tq, q_blocks, n_ctx, banded, has_sink):
    it = iter(refs)
    q_ref, kv_ref = next(it), next(it)
    ckv_ref = next(it) if n_ctx else None
    sink_ref = next(it) if has_sink else None
    o_ref = next(it)
    groups = A_HEADS // A_KV
    rows = groups * tq
    low_half = lax.broadcasted_iota(jnp.int32, (tq, 2 * HEAD_DIM), 1) < HEAD_DIM

    def run_kv_head(blk, kh):
        r0 = blk * tq
        j = pl.program_id(1) * q_blocks + blk
        kcol = slice(kh * HEAD_DIM, (kh + 1) * HEAD_DIM)
        vcol = slice(128 + kh * 256, 128 + (kh + 1) * 256)
        qs = jnp.concatenate([q_ref[r0:r0 + tq, (kh * groups + g) * HEAD_DIM:(kh * groups + g + 1) * HEAD_DIM]
                              for g in range(groups)], axis=0)
        tiles = []
        if banded:
            span = tq + 2 * WINDOW
            start = pl.multiple_of(jnp.clip(j * tq - WINDOW, 0, seq_len - span), WINDOW)
            qpos = j * tq + (lax.broadcasted_iota(jnp.int32, (rows, span), 0) & (tq - 1))
            kpos = start + lax.broadcasted_iota(jnp.int32, (rows, span), 1)
            band = jnp.abs(kpos - qpos) <= WINDOW
            tiles.append((kv_ref[pl.ds(start, span), kcol], kv_ref[pl.ds(start, span), vcol], band))
        else:
            tk = min(ATT_KEY_TILE, seq_len)
            for t in range(seq_len // tk):
                tiles.append((kv_ref[t * tk:(t + 1) * tk, kcol], kv_ref[t * tk:(t + 1) * tk, vcol], None))
        scores = [_dot_nt(qs, k_t) for k_t, _, _ in tiles]
        if n_ctx:
            tiles.append((None, ckv_ref[0, kh, HEAD_DIM:, :], None))
            scores.append(_dot(qs, ckv_ref[0, kh, :HEAD_DIM, :]))
        yield
        scores = [s if mask is None else jnp.where(mask, s, NEG_INF) for s, (_, _, mask) in zip(scores, tiles)]
        m = functools.reduce(jnp.maximum, [jnp.max(s, -1, keepdims=True) for s in scores])
        if has_sink:
            sink = jnp.concatenate([jnp.full((tq, 1), sink_ref[kh * groups + g] * LOG2_E, F32)
                                    for g in range(groups)], 0)
            m = jnp.maximum(m, sink)
        pv = functools.reduce(lambda a, b: a + b,
                              [(_dot_nt if k_t is None else _dot)(jnp.exp2(s - m).astype(BF16), v_t)
                               for s, (k_t, v_t, _) in zip(scores, tiles)])
        yield
        total = pv[:, 2 * HEAD_DIM:]
        if has_sink:
            total = total + jnp.exp2(sink - m)
        o = pv[:, :2 * HEAD_DIM] / total
        for g in range(0, groups, 2):
            pair = jnp.where(low_half, o[g * tq:(g + 1) * tq], o[(g + 1) * tq:(g + 2) * tq])
            c0 = (kh * groups + g) * HEAD_DIM
            o_ref[r0:r0 + tq, c0:c0 + 2 * HEAD_DIM] = pair.astype(BF16)

    run = _run_staggered if seq_len + n_ctx > 2 * ATT_KEY_TILE else _run_interleaved
    run([run_kv_head(blk, kh) for blk in range(q_blocks) for kh in range(A_KV)])


def _attention(q, kv, ctx_kv, sink, batch, seq_len, row0, tq, q_blocks, banded):
    n_ctx = 0 if ctx_kv is None else ctx_kv.shape[-1]
    per_seq = seq_len // (tq * q_blocks)
    q0, kv0 = row0 // (tq * q_blocks), row0 // seq_len
    in_specs = [pl.BlockSpec((q_blocks * tq, 512), lambda b, j: (q0 + b * per_seq + j, 0)),
                pl.BlockSpec((seq_len, KV_WIDTH), lambda b, j: (kv0 + b, 0))]
    args = [q, kv]
    if n_ctx:
        in_specs.append(pl.BlockSpec((1,) + ctx_kv.shape[1:], lambda b, j: (b, 0, 0, 0)))
        args.append(ctx_kv)
    if sink is not None:
        in_specs.append(pl.BlockSpec(memory_space=pltpu.SMEM))
        args.append(sink.reshape(-1).astype(F32))
    return pl.pallas_call(
        functools.partial(_attn_kernel, seq_len=seq_len, tq=tq, q_blocks=q_blocks, n_ctx=n_ctx, banded=banded,
                          has_sink=sink is not None),
        grid=(batch, per_seq),
        in_specs=in_specs,
        out_specs=pl.BlockSpec((q_blocks * tq, 512), lambda b, j: (b * per_seq + j, 0)),
        out_shape=jax.ShapeDtypeStruct((batch * seq_len, 512), BF16),
        compiler_params=_cparams(("parallel", "arbitrary")),
        name="gqa_attention",
    )(*args)


def _log_sigmoid(x):
    return jnp.minimum(x, 0.0) - jnp.log1p(jnp.exp(-jnp.abs(x)))


def _mlstm_kernel(*refs, seq_len, has_init, emit_state):
    it = iter(refs)
    q_ref, k_ref, v_ref, g_ref, bo_ref, gb_ref, ng_ref = (next(it) for _ in range(7))
    c0_ref, n0_ref, m0_ref = (next(it), next(it), next(it)) if has_init else (None, None, None)
    y_ref = next(it)
    co_ref, no_ref, mo_ref = (next(it), next(it), next(it)) if emit_state else (None, None, None)
    c_scr, n_scr, m_scr, hf_scr, hb_scr, ig_buf, bc_buf, ut_buf = (next(it) for _ in range(8))

    t_len = MLSTM_T
    n_chunks = seq_len // t_len
    n_streams = 2 * B_HEADS

    for s in range(n_streams):
        if has_init:
            c_scr[s] = c0_ref[0, s // B_HEADS, s % B_HEADS]
        else:
            c_scr[s] = jnp.zeros((B_DIM, B_DIM), F32)
    if has_init:
        n_scr[...] = n0_ref[0]
        m_scr[...] = m0_ref[0]
    else:
        n_scr[...] = jnp.zeros_like(n_scr)
        m_scr[...] = jnp.zeros_like(m_scr)

    ri = lax.broadcasted_iota(jnp.int32, (t_len, t_len), 0)
    ci = lax.broadcasted_iota(jnp.int32, (t_len, t_len), 1)
    lower = ci <= ri
    upper = ci >= ri
    tri_f = jnp.where(lower, 1.0, 0.0).astype(BF16)
    tri_b = jnp.where(upper, 1.0, 0.0).astype(BF16)
    gate_bias = gb_ref[...]
    sel_row = lax.broadcasted_iota(jnp.int32, (128, B_HEADS * B_DIM), 0)
    sel_head = lax.broadcasted_iota(jnp.int32, (128, B_HEADS * B_DIM), 1) // B_DIM
    gate_select = []
    for direction in range(2):
        i_base = 2 * direction * B_HEADS
        gate_select.append((jnp.where(sel_row == i_base + sel_head, 1.0, 0.0).astype(BF16),
                            jnp.where(sel_row == i_base + B_HEADS + sel_head, 1.0, 0.0).astype(BF16)))

    def chunk_rows(c, direction):
        chunk = c if direction == 0 else n_chunks - 1 - c
        return pl.ds(pl.multiple_of(chunk * t_len, t_len), t_len)

    def prepare_gates(c, direction):
        slot = c % 2
        rows = chunk_rows(jnp.minimum(c, n_chunks - 1), direction)
        tri = tri_f if direction == 0 else tri_b
        sel_i, sel_f = gate_select[direction]
        gates = g_ref[rows, :] + gate_bias
        hi, mid, lo = _split3(_log_sigmoid(gates))
        b3 = _dot(tri, jnp.concatenate([hi, mid, lo], 1))
        ig_buf[slot, direction] = _dot_sel_rhs(gates, sel_i)
        yield
        bc_all = b3[:, :128] + b3[:, 128:256] + b3[:, 256:]
        bc_rep = _dot_sel_rhs(bc_all, sel_f)
        ut_buf[slot, direction] = (gates - pltpu.roll(bc_all, 128 - B_HEADS, 1)).T
        yield
        bc_buf[slot, direction] = bc_rep

    def run_direction(c, direction):
        rows = chunk_rows(c, direction)
        slot = c % 2
        causal = lower if direction == 0 else upper
        last = t_len - 1 if direction == 0 else 0
        h_out = hf_scr if direction == 0 else hb_scr
        ig_rep, bc_rep, u_t = ig_buf[slot, direction], bc_buf[slot, direction], ut_buf[slot, direction]
        heads = []
        for hd in range(B_HEADS):
            s = direction * B_HEADS + hd
            col = slice(hd * B_DIM, (hd + 1) * B_DIM)
            qc, kc, vc = q_ref[rows, col], k_ref[rows, col], v_ref[rows, col]
            c_prev = c_scr[s]
            n_prev = n_scr[s:s + 1, :]
            qk = _dot_nt(qc, kc)
            q_state = _dot(qc, c_prev.astype(BF16))
            qn = _dot_nt(qc, jnp.broadcast_to(n_prev, (B_DIM, B_DIM)).astype(BF16))
            heads.append((s, col, qc, kc, vc, c_prev, n_prev, qk, q_state, qn))
        yield
        staged = []
        for (s, col, qc, kc, vc, c_prev, n_prev, qk, q_state, qn) in heads:
            i_col = 2 * direction * B_HEADS + (s % B_HEADS)
            ig, bc = ig_rep[:, col], bc_rep[:, col]
            dmat = jnp.where(causal, bc[:, :t_len] + u_t[i_col:i_col + 1, :], -jnp.inf)
            m_prev = m_scr[s:s + 1, :]
            inter = bc + m_prev
            m_t = jnp.maximum(inter, jnp.max(dmat, -1, keepdims=True))
            w = (jnp.exp(dmat - m_t[:, :t_len]) * qk).astype(BF16)
            wv = _dot(w, jnp.concatenate([vc, jnp.ones_like(vc)], 1))
            b_tot = bc[last:last + 1, :]
            g = b_tot - bc + ig
            m_new = jnp.maximum(b_tot + m_prev, jnp.max(g, 0, keepdims=True))
            kw = kc.astype(F32) * jnp.exp(g - m_new)
            upd = _dot_tn(kw.astype(BF16), vc)
            decay = jnp.exp(b_tot + m_prev - m_new)
            staged.append((s, col, inter, m_t, c_prev, n_prev, q_state, qn, wv, m_new, kw, upd, decay))
        yield
        for (s, col, inter, m_t, c_prev, n_prev, q_state, qn, wv, m_new, kw, upd, decay) in staged:
            a = jnp.exp(inter - m_t)
            num = a * q_state + wv[:, :B_DIM]
            den = a * qn + wv[:, B_DIM:]
            h_out[rows, col] = num / jnp.maximum(jnp.abs(den), jnp.exp(-m_t))
            c_scr[s] = decay * c_prev + upd
            n_scr[s:s + 1, :] = decay * n_prev + jnp.sum(kw, 0, keepdims=True)
            m_scr[s:s + 1, :] = m_new

    def chained(c, direction):
        yield from prepare_gates(c, direction)
        yield
        yield from run_direction(c, direction)

    def step(c, carry):
        _run_interleaved([chained(c, 0), chained(c, 1)])
        return carry

    lax.fori_loop(0, n_chunks, step, 0, unroll=2)

    blk = min(256, seq_len)
    ng = ng_ref[...]

    def finish(i, carry):
        rows = pl.ds(pl.multiple_of(i * blk, blk), blk)
        hsum = hf_scr[rows, :] + hb_scr[rows, :]
        gate = jax.nn.sigmoid(bo_ref[rows, :])
        parts = [_rms_lastdim(hsum[:, hd * B_DIM:(hd + 1) * B_DIM], ng[:, hd * B_DIM:(hd + 1) * B_DIM])
                 for hd in range(B_HEADS)]
        y_ref[rows, :] = (gate * jnp.concatenate(parts, 1)).astype(BF16)
        return carry

    lax.fori_loop(0, seq_len // blk, finish, 0)

    if emit_state:
        for s in range(n_streams):
            co_ref[0, s // B_HEADS, s % B_HEADS] = c_scr[s]
        no_ref[0] = n_scr[...]
        mo_ref[0] = m_scr[...]


def _mlstm(q, k, v, gates, bo, gate_bias, norm_g, init, batch, seq_len, row0, emit_state):
    n = batch * seq_len
    seq0 = row0 // seq_len
    in_row = lambda width: pl.BlockSpec((seq_len, width), lambda b: (seq0 + b, 0))
    row = lambda width: pl.BlockSpec((seq_len, width), lambda b: (b, 0))
    const = lambda a: pl.BlockSpec(a.shape, lambda b: (0,) * a.ndim)
    gb = jnp.zeros((1, 128), F32).at[0, :4 * B_HEADS].set(gate_bias.reshape(-1))
    ng = norm_g.reshape(1, B_HEADS * B_DIM)
    in_specs = [in_row(512), in_row(512), in_row(512), in_row(128), in_row(512), const(gb), const(ng)]
    args = [q, k, v, gates, bo, gb, ng]
    n_streams = 2 * B_HEADS
    if init is not None:
        c0, n0, m0 = init
        in_specs += [pl.BlockSpec((1, 2, B_HEADS, B_DIM, B_DIM), lambda b: (b, 0, 0, 0, 0)),
                     pl.BlockSpec((1, n_streams, B_DIM), lambda b: (b, 0, 0)),
                     pl.BlockSpec((1, n_streams, B_DIM), lambda b: (b, 0, 0))]
        args += [c0, n0.reshape(batch, n_streams, B_DIM),
                 jnp.broadcast_to(m0.reshape(batch, n_streams, 1), (batch, n_streams, B_DIM))]
    out_specs = [row(512)]
    out_shape = [jax.ShapeDtypeStruct((n, 512), BF16)]
    if emit_state:
        out_specs += [pl.BlockSpec((1, 2, B_HEADS, B_DIM, B_DIM), lambda b: (b, 0, 0, 0, 0)),
                      pl.BlockSpec((1, n_streams, B_DIM), lambda b: (b, 0, 0)),
                      pl.BlockSpec((1, n_streams, B_DIM), lambda b: (b, 0, 0))]
        out_shape += [jax.ShapeDtypeStruct((batch, 2, B_HEADS, B_DIM, B_DIM), F32),
                      jax.ShapeDtypeStruct((batch, n_streams, B_DIM), F32),
                      jax.ShapeDtypeStruct((batch, n_streams, B_DIM), F32)]
    return pl.pallas_call(
        functools.partial(_mlstm_kernel, seq_len=seq_len, has_init=init is not None, emit_state=emit_state),
        grid=(batch,),
        in_specs=in_specs,
        out_specs=out_specs,
        out_shape=out_shape,
        scratch_shapes=[pltpu.VMEM((n_streams, B_DIM, B_DIM), F32),
                        pltpu.VMEM((n_streams, B_DIM), F32),
                        pltpu.VMEM((n_streams, B_DIM), F32),
                        pltpu.VMEM((seq_len, 512), F32),
                        pltpu.VMEM((seq_len, 512), F32),
                        pltpu.VMEM((2, 2, MLSTM_T, B_HEADS * B_DIM), F32),
                        pltpu.VMEM((2, 2, MLSTM_T, B_HEADS * B_DIM), F32),
                        pltpu.VMEM((2, 2, 128, MLSTM_T), F32)],
        compiler_params=_cparams(("parallel",)),
        name="mlstm_scan",
    )(*args)


def _hgrn_kernel(*refs, seq_len, layer, has_init, emit_state):
    it = iter(refs)
    q_ref, ff_ref, fb_ref, v_ref, cg_ref, lbl_ref, ng_ref = (next(it) for _ in range(7))
    s0_ref = next(it) if has_init else None
    y_ref = next(it)
    so_ref = next(it) if emit_state else None
    st_scr, of_scr, ob_scr = next(it), next(it), next(it)

    t_len = HGRN_T
    n_sub = t_len // SUB
    n_chunks = seq_len // t_len
    n_streams = 2 * C_HEADS

    logits = lbl_ref[...]
    e = jnp.exp(logits - jnp.max(logits, 0, keepdims=True))
    sm = e / jnp.sum(e, 0, keepdims=True)
    lb = jnp.sum(sm[0:layer + 1], 0, keepdims=True) - sm[0:1]

    for s in range(n_streams):
        if has_init:
            st_scr[s] = s0_ref[0, s // C_HEADS, s % C_HEADS].T
        else:
            st_scr[s] = jnp.zeros((C_DV, C_DK), F32)

    ri = lax.broadcasted_iota(jnp.int32, (t_len, t_len), 0)
    ci = lax.broadcasted_iota(jnp.int32, (t_len, t_len), 1)
    tri_f = jnp.where(ci <= ri, 1.0, 0.0).astype(BF16)
    tri_b = jnp.where(ci >= ri, 1.0, 0.0).astype(BF16)
    sub_row = lax.broadcasted_iota(jnp.int32, (SUB, C_DK), 0)
    ones_dk = jnp.ones((C_DK, C_DK), BF16)

    def run_stream(c, direction, hd):
        chunk = c if direction == 0 else n_chunks - 1 - c
        rows = pl.ds(pl.multiple_of(chunk * t_len, t_len), t_len)
        f_ref = ff_ref if direction == 0 else fb_ref
        tri = tri_f if direction == 0 else tri_b
        last = t_len - 1 if direction == 0 else 0
        o_out = of_scr if direction == 0 else ob_scr
        s = direction * C_HEADS + hd
        col = slice(hd * C_DK, (hd + 1) * C_DK)
        lbh = lb[:, col]
        f = lbh + (1.0 - lbh) * jax.nn.sigmoid(f_ref[rows, col])
        log_k = jnp.log2(1.0 - f)
        hi, mid, lo = _split3(jnp.log2(f))
        a3 = _dot(tri, jnp.concatenate([hi, mid, lo], 1))
        yield
        a_cum = a3[:, :C_DK] + a3[:, C_DK:2 * C_DK] + a3[:, 2 * C_DK:]
        a_key = a_cum - log_k
        a_tot = a_cum[last:last + 1, :]
        qf = q_ref[rows, col].astype(F32)
        vc = v_ref[rows, col]
        vf = vc.astype(F32)
        st = st_scr[s]
        inter = _dot_nt((qf * jnp.exp2(a_cum)).astype(BF16), st.astype(BF16))
        upd = _dot_tn(vc, jnp.exp2(a_tot - a_key).astype(BF16))
        ps = []
        for blk in range(n_sub):
            b0 = blk * SUB
            a_i, q_i, ak_i = (t[b0:b0 + SUB] for t in (a_cum, qf, a_key))
            for j in range(SUB):
                seen = (sub_row >= j) if direction == 0 else (sub_row <= j)
                ps.append(jnp.where(seen, jnp.exp2(a_i - ak_i[j:j + 1]), 0.0) * q_i)
        att = _dot(jnp.concatenate(ps, 0).astype(BF16), ones_dk)
        off = []
        for blk in range(n_sub):
            b0 = blk * SUB
            if direction == 0 and blk > 0:
                a_ref, kr = a_cum[b0 - 1:b0], slice(0, b0)
            elif direction == 1 and blk < n_sub - 1:
                a_ref, kr = a_cum[b0 + SUB:b0 + SUB + 1], slice(b0 + SUB, t_len)
            else:
                off.append(None)
                continue
            qt = (qf[b0:b0 + SUB] * jnp.exp2(a_cum[b0:b0 + SUB] - a_ref)).astype(BF16)
            kt = jnp.exp2(a_ref - a_key[kr]).astype(BF16)
            off.append((_dot_nt(qt, kt), kr))
        yield
        st_scr[s] = jnp.exp2(a_tot) * st + upd
        outs = []
        for blk in range(n_sub):
            b0 = blk * SUB
            o_i = inter[b0:b0 + SUB]
            for j in range(SUB):
                r = (blk * SUB + j) * SUB
                o_i = o_i + att[r:r + SUB] * vf[b0 + j:b0 + j + 1]
            if off[blk] is not None:
                att_off, kr = off[blk]
                outs.append((o_i, _dot(att_off.astype(BF16), vc[kr])))
            else:
                outs.append((o_i, None))
        yield
        o_out[rows, col] = jnp.concatenate([o_i if o_off is None else o_i + o_off for o_i, o_off in outs], 0)

    def step(c, carry):
        _run_interleaved([run_stream(c, direction, hd) for direction in range(2) for hd in range(C_HEADS)])
        return carry

    lax.fori_loop(0, n_chunks, step, 0, unroll=2)

    blk_rows = min(256, seq_len)
    ng = ng_ref[...]

    def finish(i, carry):
        rows = pl.ds(pl.multiple_of(i * blk_rows, blk_rows), blk_rows)
        osum = of_scr[rows, :] + ob_scr[rows, :]
        parts = [_rms_lastdim(osum[:, hd * C_DV:(hd + 1) * C_DV], ng[:, hd * C_DV:(hd + 1) * C_DV])
                 for hd in range(C_HEADS)]
        y_ref[rows, :] = (jnp.concatenate(parts, 1) * _silu(cg_ref[rows, :])).astype(BF16)
        return carry

    lax.fori_loop(0, seq_len // blk_rows, finish, 0)

    if emit_state:
        for s in range(n_streams):
            so_ref[0, s // C_HEADS, s % C_HEADS] = st_scr[s].T


def _hgrn(q, ff, fb, v, cg, lb_logits, norm_g, layer, init, batch, seq_len, row0, emit_state):
    n = batch * seq_len
    seq0 = row0 // seq_len
    in_row = lambda width: pl.BlockSpec((seq_len, width), lambda b: (seq0 + b, 0))
    row = lambda width: pl.BlockSpec((seq_len, width), lambda b: (b, 0))
    const = lambda a: pl.BlockSpec(a.shape, lambda b: (0,) * a.ndim)
    ng = norm_g.reshape(1, C_HEADS * C_DV)
    state_spec = pl.BlockSpec((1, 2, C_HEADS, C_DK, C_DV), lambda b: (b, 0, 0, 0, 0))
    in_specs = [in_row(512)] * 5 + [const(lb_logits), const(ng)]
    args = [q, ff, fb, v, cg, lb_logits, ng]
    if init is not None:
        in_specs.append(state_spec)
        args.append(init)
    out_specs = [row(512)]
    out_shape = [jax.ShapeDtypeStruct((n, 512), BF16)]
    if emit_state:
        out_specs.append(state_spec)
        out_shape.append(jax.ShapeDtypeStruct((batch, 2, C_HEADS, C_DK, C_DV), F32))
    return pl.pallas_call(
        functools.partial(_hgrn_kernel, seq_len=seq_len, layer=layer, has_init=init is not None,
                          emit_state=emit_state),
        grid=(batch,),
        in_specs=in_specs,
        out_specs=out_specs,
        out_shape=out_shape,
        scratch_shapes=[pltpu.VMEM((2 * C_HEADS, C_DV, C_DK), F32),
                        pltpu.VMEM((seq_len, 512), F32),
                        pltpu.VMEM((seq_len, 512), F32)],
        compiler_params=_cparams(("parallel",)),
        name="hgrn2_scan",
    )(*args)


def _prep_ctx_kv(k_ctx, v_ctx):
    k_t = jnp.transpose(k_ctx, (0, 2, 3, 1)).astype(BF16)
    v_t = jnp.transpose(v_ctx, (0, 2, 3, 1)).astype(BF16)
    ones = jnp.ones(k_t.shape[:2] + (128, k_t.shape[3]), BF16)
    return jnp.concatenate([k_t, v_t, v_t, ones], 2)


def kernel(x_prompt, x_sample, c, cache_a_k, cache_a_v, state_b_C, state_b_n, state_b_m, state_c_S, cache_d_k, cache_d_v, c_ctx, ada_w, ada_b, ln_g, ln_b, ffn_w1, ffn_w3, ffn_w2, w_in_even, w_out_even, a_sink, b_gate_bias, b_norm_g, w_in_odd, w_out_odd, c_lb_logits, c_norm_g, d_q_norm, d_k_norm):
    batch_p, len_p, d = x_prompt.shape
    batch_s, len_s, _ = x_sample.shape

    cvec = jnp.concatenate([c_ctx[None], c, jnp.zeros((MOD_ROWS - 1 - batch_s, d), F32)], 0)
    mod_all = _modulation(cvec, ada_w, ada_b)

    lay = _Layout(batch_p, len_p, batch_s, len_s)
    new = {}
    ffn_w = (ffn_w1.astype(BF16), ffn_w3.astype(BF16), ffn_w2.astype(BF16))
    x = [x_prompt.reshape(lay.n_p, d), x_sample.reshape(lay.n_s, d)]

    for l in range(DEPTH):
        mod = mod_all[l].reshape(MOD_ROWS, 9, d)
        i = l // 2
        if l % 2 == 0:
            in_proj = (_even_in_body, _even_in_io, dict(w=w_in_even[i].astype(BF16)))
            w_out = w_out_even[i].astype(BF16)
        else:
            in_proj = (_odd_in_body, _odd_in_io,
                       dict(w=w_in_odd[i].astype(BF16), q_norm=d_q_norm[i], k_norm=d_k_norm[i]))
            w_out = w_out_odd[i].astype(BF16)

        x_all, *proj, k_cache, v_cache = _ffn(x, mod, 0, ln_g[l, 0], ln_b[l, 0], *ffn_w, l, 0, lay, in_proj=in_proj)
        prompt = dict(batch=batch_p, seq_len=len_p, row0=0)
        sample = dict(batch=batch_s, seq_len=len_s, row0=lay.n_p)
        if l % 2 == 0:
            aq, akv, bq, bk, bv, bo, bg = proj
            new["a_k"], new["a_v"] = _cache_layout(k_cache), _cache_layout(v_cache)
            scan = (bq, bk, bv, bg, bo, b_gate_bias[i], b_norm_g[i])
            ya_p = _attention(aq, akv, None, a_sink[i], **prompt, tq=Q_BLOCK, q_blocks=len_p // Q_BLOCK, banded=False)
            yb_p, c_new, n_new, m_new = _mlstm(*scan, None, **prompt, emit_state=True)
            new["b_C"] = c_new[:, None]
            new["b_n"] = n_new.reshape(batch_p, 1, 2, B_HEADS, B_DIM)
            new["b_m"] = m_new[:, :, 0].reshape(batch_p, 1, 2, B_HEADS)
            ya_s = _attention(aq, akv, _prep_ctx_kv(cache_a_k[:, i], cache_a_v[:, i]), a_sink[i], **sample,
                              tq=Q_BLOCK, q_blocks=4, banded=True)
            yb_s, = _mlstm(*scan, (state_b_C[:, i], state_b_n[:, i], state_b_m[:, i]), **sample, emit_state=False)
        else:
            cq, ff, fb, cv, cg, dq, dkv = proj
            new["d_k"], new["d_v"] = _cache_layout(k_cache), _cache_layout(v_cache)
            scan = (cq, ff, fb, cv, cg, c_lb_logits, c_norm_g[i], l)
            ya_p, s_new = _hgrn(*scan, None, **prompt, emit_state=True)
            new["c_S"] = s_new[:, None]
            yb_p = _attention(dq, dkv, None, None, **prompt, tq=Q_BLOCK, q_blocks=len_p // Q_BLOCK, banded=False)
            ya_s, = _hgrn(*scan, state_c_S[:, i], **sample, emit_state=False)
            yb_s = _attention(dq, dkv, _prep_ctx_kv(cache_d_k[:, i], cache_d_v[:, i]), None, **sample,
                              tq=Q_BLOCK, q_blocks=2, banded=False)
        x = _ffn([x_all], mod, 2, ln_g[l, 2], ln_b[l, 2], *ffn_w, l, 1, lay,
                 mixer=(ya_p, yb_p, ya_s, yb_s, w_out, ln_g[l, 1], ln_b[l, 1]), split_out=l == DEPTH - 1)

    y_prompt, y_sample = x
    return (y_prompt.reshape(batch_p, len_p, d), y_sample.reshape(batch_s, len_s, d), new["a_k"], new["a_v"],
            new["b_C"], new["b_n"], new["b_m"], new["c_S"], new["d_k"], new["d_v"])
```

```python
import functools

import jax
import jax.numpy as jnp
import numpy as np
from jax import lax
from jax.experimental import pallas as pl
from jax.experimental.pallas import tpu as pltpu

F32 = jnp.float32
BF16 = jnp.bfloat16

D_MODEL = 1024
DEPTH = 2
GRID_W = 64
HEAD_DIM = 64
A_HEADS = 8
A_KV = 2
WINDOW = 128
B_HEADS = 4
B_DIM = 128
C_HEADS = 4
C_DK = 128
C_DV = 128
D_HEADS = 8
D_KV = 2
Q_BLOCK = 128
ROPE_THETA = 10000.0
ALPHA = (2 * DEPTH) ** 0.25
NEG_INF = -1e30
LOG2_E = 1.4426950408889634
QK_SCALE = HEAD_DIM ** -0.5 * LOG2_E

MOD_ROWS = 8
FF_CHUNK = 256
FFN_TILE = 512
WEIGHT_CAST_STEPS = 16
MLSTM_T = 64
HGRN_T = 64
SUB = 8
ATT_KEY_TILE = 512
VMEM_LIMIT = 58 * 1024 * 1024


def _cparams(sem):
    return pltpu.CompilerParams(dimension_semantics=sem, vmem_limit_bytes=VMEM_LIMIT)


def _dot(a, b):
    return jnp.dot(a, b, preferred_element_type=F32)


def _dot_nt(a, b):
    return lax.dot_general(a, b, (((1,), (1,)), ((), ())), preferred_element_type=F32)


def _dot_tn(a, b):
    return lax.dot_general(a, b, (((0,), (0,)), ((), ())), preferred_element_type=F32)


def _split3(x):
    hi = x.astype(BF16)
    r1 = x - hi.astype(F32)
    mid = r1.astype(BF16)
    lo = (r1 - mid.astype(F32)).astype(BF16)
    return hi, mid, lo


def _dot_sel_rhs(x, sel):
    hi = x.astype(BF16)
    lo = (x - hi.astype(F32)).astype(BF16)
    r = _dot(jnp.concatenate([hi, lo], 0), sel)
    return r[:x.shape[0]] + r[x.shape[0]:]


def _run_interleaved(gens):
    live = list(gens)
    while live:
        nxt = []
        for g in live:
            try:
                next(g)
                nxt.append(g)
            except StopIteration:
                pass
        live = nxt


def _run_staggered(gens):
    pending, live = list(gens), []
    while pending or live:
        if pending:
            live.insert(0, pending.pop(0))
        nxt = []
        for g in live:
            try:
                next(g)
                nxt.append(g)
            except StopIteration:
                pass
        live = nxt


def _silu(x):
    return x * jax.nn.sigmoid(x)


def _layernorm(z, g, b):
    mu = jnp.mean(z, -1, keepdims=True)
    zc = z - mu
    var = jnp.mean(zc * zc, -1, keepdims=True)
    return zc * lax.rsqrt(var + 1e-5) * g + b


def _rms_lastdim(x, g):
    return x * lax.rsqrt(jnp.mean(x * x, -1, keepdims=True) + 1e-6) * g


class _Layout:
    def __init__(self, batch_p, len_p, batch_s, len_s):
        self.tm = FFN_TILE
        self.len_p, self.len_s = len_p, len_s
        self.n_p, self.n_s = batch_p * len_p, batch_s * len_s
        self.n = self.n_p + self.n_s
        self.steps_p = self.n_p // self.tm
        self.steps = self.n // self.tm

    def all_rows(self, i):
        return (i, 0)

    def prompt_rows(self, i):
        return (jnp.minimum(i, self.steps_p - 1), 0)

    def sample_rows(self, i):
        return (jnp.maximum(i - self.steps_p, 0), 0)

    def mod_row(self, i):
        req = jnp.maximum(i - self.steps_p, 0) // (self.len_s // self.tm)
        return (jnp.where(i < self.steps_p, 0, 1 + req), 0, 0)

    def rope_rows(self, i):
        return (jnp.maximum(i - self.steps_p, 0) % (self.len_s // self.tm), 0)


def _mod_kernel(c_ref, w_ref, b_ref, o_ref):
    s = _silu(c_ref[...]).astype(BF16)
    o_ref[0] = _dot(s, w_ref[0].astype(BF16)) + b_ref[0]


def _modulation(cvec, ada_w, ada_b):
    depth, d, n = ada_w.shape
    tn = 1152
    return pl.pallas_call(
        _mod_kernel,
        grid=(depth, n // tn),
        in_specs=[pl.BlockSpec((MOD_ROWS, d), lambda l, j: (0, 0)),
                  pl.BlockSpec((1, d, tn), lambda l, j: (l, 0, j)),
                  pl.BlockSpec((1, 1, tn), lambda l, j: (l, 0, j))],
        out_specs=pl.BlockSpec((1, MOD_ROWS, tn), lambda l, j: (l, 0, j)),
        out_shape=jax.ShapeDtypeStruct((depth, MOD_ROWS, n), F32),
        compiler_params=_cparams(("parallel", "parallel")),
        name="modulation",
    )(cvec, ada_w, ada_b.reshape(depth, 1, n))


def _ffn_kernel(*refs, j, nf, steps_p, n_x, n_out, mixer, in_proj, cast_next):
    it = iter(refs)
    x_refs = [next(it) for _ in range(n_x)]
    mod_ref = next(it)
    if mixer:
        ya_p, yb_p, ya_s, yb_s, wo_ref, g1_ref, b1_ref = (next(it) for _ in range(7))
    w1_ref, w3_ref, w2_ref, g_ref, b_ref = (next(it) for _ in range(5))
    proj_ins = [next(it) for _ in range(in_proj[1])] if in_proj else []
    cast_ins = [next(it) for _ in range(3)] if cast_next else []
    o_refs = [next(it) for _ in range(n_out)]
    proj_outs = [next(it) for _ in range(in_proj[2])] if in_proj else []
    cache_refs = (next(it), next(it)) if in_proj else None
    cast_outs = [next(it) for _ in range(3)] if cast_next else []
    h_ref, acc_ref, res_ref = next(it), next(it), next(it)

    if cast_next:
        @pl.when(pl.program_id(0) < WEIGHT_CAST_STEPS)
        def _():
            for src, dst in zip(cast_ins, cast_outs):
                dst[...] = src[...].astype(BF16)

    is_prompt = pl.program_id(0) < steps_p
    pick = lambda p_ref, s_ref: jnp.where(is_prompt, p_ref[...], s_ref[...])
    m = mod_ref[0]
    shift, scale, gate = m[3 * j:3 * j + 1], m[3 * j + 1:3 * j + 2], m[3 * j + 2:3 * j + 3]
    x = x_refs[0][...] if n_x == 1 else pick(*x_refs)
    if mixer:
        ya, yb = pick(ya_p, ya_s), pick(yb_p, yb_s)
        half = ya.shape[1]
        y = _dot(ya, wo_ref[:half, :]) + _dot(yb, wo_ref[half:, :])
        x = _layernorm(ALPHA * x + m[5:6] * y, g1_ref[...], b1_ref[...])
    res_ref[...] = x
    h_ref[...] = (x * (1.0 + scale) + shift).astype(BF16)
    for f in range(nf):
        cols = slice(f * FF_CHUNK, (f + 1) * FF_CHUNK)
        h = h_ref[...]
        u = (_silu(_dot(h, w1_ref[0, 0, :, cols])) * _dot(h, w3_ref[0, 0, :, cols])).astype(BF16)
        y = _dot(u, w2_ref[0, 0, cols, :])
        if f == 0:
            acc_ref[...] = y
        else:
            acc_ref[...] += y
    out = _layernorm(ALPHA * res_ref[...] + 0.5 * gate * acc_ref[...], g_ref[...], b_ref[...])
    if n_out == 1:
        o_refs[0][...] = out
    else:
        res_ref[...] = out

        @pl.when(is_prompt)
        def _():
            o_refs[0][...] = res_ref[...]

        @pl.when(jnp.logical_not(is_prompt))
        def _():
            o_refs[1][...] = res_ref[...]

    if in_proj:
        @pl.when(is_prompt)
        def _():
            in_proj[0](o_refs[0][...], m, proj_ins, proj_outs, rope=False, cache_refs=cache_refs)

        @pl.when(jnp.logical_not(is_prompt))
        def _():
            in_proj[0](o_refs[0][...], m, proj_ins, proj_outs, rope=True, cache_refs=None)


def _ffn(xs, mod, j, g, b, weights, lay, mixer=None, in_proj=None, split_out=False, next_weights=None):
    w1, w3, w2 = weights
    d = xs[0].shape[1]
    nf = w1.shape[-1] // FF_CHUNK
    tm = lay.tm
    spec = lambda width, index: pl.BlockSpec((tm, width), index)
    vec = pl.BlockSpec((1, d), lambda i: (0, 0))
    whole = lambda a: pl.BlockSpec(a.shape, lambda i: (0, 0, 0, 0), pipeline_mode=pl.Buffered(1))
    pair = (lay.prompt_rows, lay.sample_rows)
    in_specs = [spec(d, lay.all_rows)] if len(xs) == 1 else [spec(d, index) for index in pair]
    in_specs.append(pl.BlockSpec((1, 9, d), lay.mod_row))
    args = list(xs) + [mod]
    if mixer is not None:
        ya_p, yb_p, ya_s, yb_s, w_out, g1, b1 = mixer
        in_specs += [spec(ya_p.shape[1], pair[0]), spec(yb_p.shape[1], pair[0]),
                     spec(ya_s.shape[1], pair[1]), spec(yb_s.shape[1], pair[1]),
                     pl.BlockSpec(w_out.shape, lambda i: (0, 0), pipeline_mode=pl.Buffered(1)), vec, vec]
        args += [ya_p, yb_p, ya_s, yb_s, w_out, g1.reshape(1, d), b1.reshape(1, d)]
    in_specs += [whole(w1), whole(w3), whole(w2), vec, vec]
    args += [w1, w3, w2, g.reshape(1, d), b.reshape(1, d)]
    if split_out:
        out_specs = [spec(d, index) for index in pair]
        out_shape = [jax.ShapeDtypeStruct((lay.n_p, d), F32), jax.ShapeDtypeStruct((lay.n_s, d), F32)]
    else:
        out_specs, out_shape = [spec(d, lay.all_rows)], [jax.ShapeDtypeStruct((lay.n, d), F32)]
    n_out, proj = len(out_specs), None
    if in_proj is not None:
        body, io, kw = in_proj
        p_specs, p_args, (p_out_specs, p_out_shape) = io(lay, **kw)
        c_specs, c_shape = _cache_outputs(lay)
        in_specs, args = in_specs + p_specs, args + p_args
        out_specs, out_shape = out_specs + p_out_specs + c_specs, out_shape + p_out_shape + c_shape
        proj = (body, len(p_args), len(p_out_specs))
    if next_weights is not None:
        *stacks, layer, which = next_weights
        slab_index = lambda i: jnp.minimum(i, WEIGHT_CAST_STEPS - 1)
        for a in stacks:
            block = (1, 1, a.shape[2] // WEIGHT_CAST_STEPS, a.shape[3])
            in_specs.append(pl.BlockSpec(block, lambda i: (layer, which, slab_index(i), 0)))
            out_specs.append(pl.BlockSpec(block, lambda i: (0, 0, slab_index(i), 0)))
            out_shape.append(jax.ShapeDtypeStruct((1, 1) + a.shape[2:], BF16))
        args += stacks
    return pl.pallas_call(
        functools.partial(_ffn_kernel, j=j, nf=nf, steps_p=lay.steps_p, n_x=len(xs), n_out=n_out,
                          mixer=mixer is not None, in_proj=proj, cast_next=next_weights is not None),
        grid=(lay.steps,),
        in_specs=in_specs,
        out_specs=out_specs,
        out_shape=out_shape,
        scratch_shapes=[pltpu.VMEM((tm, d), BF16), pltpu.VMEM((tm, d), F32), pltpu.VMEM((tm, d), F32)],
        compiler_params=_cparams(("arbitrary",)),
        name="ffn_sublayer",
    )(*args)


def _rope_tables(length):
    t = np.arange(length)
    nf = HEAD_DIM // 4
    inv = ROPE_THETA ** (-np.arange(nf, dtype=np.float64) / nf)
    ang_r = (t // GRID_W)[:, None] * inv[None]
    ang_c = (t % GRID_W)[:, None] * inv[None]
    cr, sr, cc, sc = np.cos(ang_r), np.sin(ang_r), np.cos(ang_c), np.sin(ang_c)
    z = np.zeros_like(cr)
    cos = np.concatenate([cr, cr, cc, cc], 1)
    sin_up = np.concatenate([-sr, z, -sc, z], 1)
    sin_dn = np.concatenate([z, sr, z, sc], 1)
    two = lambda a: jnp.asarray(np.concatenate([a, a], 1), F32)
    return two(cos), two(sin_up), two(sin_dn)


def _rope128(x, cos, sin_up, sin_dn):
    nf = HEAD_DIM // 4
    return x * cos + pltpu.roll(x, 128 - nf, 1) * sin_up + pltpu.roll(x, nf, 1) * sin_dn


def _rope(x, cos, sin_up, sin_dn):
    parts = [_rope128(x[:, c:c + 128], cos, sin_up, sin_dn) for c in range(0, x.shape[1], 128)]
    return parts[0] if len(parts) == 1 else jnp.concatenate(parts, 1)


EVEN_COLS = (512, 256, 512, 512, 512, 528)


KV_WIDTH = 128 + A_KV * 256


def _store_cache(ck_ref, cv_ref, kv):
    seq_len = ck_ref.shape[2]
    for s in range(ck_ref.shape[0]):
        rows = slice(s * seq_len, (s + 1) * seq_len)
        ck_ref[s] = kv[rows, :128].T
        cv_ref[s] = kv[rows, 128:].T


def _in_proj_outputs(outs, lay):
    specs = [pl.BlockSpec((lay.tm, width), lay.all_rows) for width, _ in outs]
    shapes = [jax.ShapeDtypeStruct((lay.n, width), dtype) for width, dtype in outs]
    return specs, shapes


def _cache_outputs(lay):
    per_step = lay.tm // lay.len_p
    index = lambda i: lay.prompt_rows(i) + (0,)
    specs = [pl.BlockSpec((per_step, 128, lay.len_p), index)] * 2
    shapes = [jax.ShapeDtypeStruct((lay.n_p // lay.len_p, 128, lay.len_p), F32)] * 2
    return specs, shapes


def _cache_layout(c):
    nb, _, sl = c.shape
    return jnp.transpose(c.reshape(nb, 1, A_KV, HEAD_DIM, sl), (0, 1, 4, 2, 3))


def _kv_with_ones(kv):
    ones = jnp.ones((kv.shape[0], 128), kv.dtype)
    v0, v1 = kv[:, 128:192], kv[:, 192:256]
    return jnp.concatenate([kv[:, :128], v0, v0, ones, v1, v1, ones], 1)


def _even_in_body(x, m, ins, outs, rope, cache_refs):
    w_ref, *tabs = ins
    aq_ref, akv_ref, bq_ref, bk_ref, bv_ref, bo_ref, bg_ref = outs
    cache = cache_refs is not None

    h = (x * (1.0 + m[4:5]) + m[3:4]).astype(BF16)
    offs = [0]
    for c in EVEN_COLS:
        offs.append(offs[-1] + c)
    proj = lambda k: _dot(h, w_ref[:, offs[k]:offs[k + 1]])

    aq = proj(0)
    akv = proj(1)
    if cache:
        _store_cache(*cache_refs, akv)
    if rope:
        cos, s_up, s_dn = (t[...] for t in tabs)
        aq = _rope(aq, cos, s_up, s_dn)
        akv = jnp.concatenate([_rope(akv[:, :128], cos, s_up, s_dn), akv[:, 128:]], 1)
    aq_ref[...] = (aq * QK_SCALE).astype(BF16)
    akv_ref[...] = _kv_with_ones(akv).astype(BF16)
    bq_ref[...] = proj(2).astype(BF16)
    bk_ref[...] = (proj(3) * (B_DIM ** -0.5)).astype(BF16)
    bv_ref[...] = proj(4).astype(BF16)
    tail = proj(5)
    n_gates = 4 * B_HEADS
    bo_ref[...] = tail[:, n_gates:]
    bg_ref[...] = tail[:, :128]


def _rope_inputs(lay):
    return [pl.BlockSpec((lay.tm, 128), lay.rope_rows)] * 3, list(_rope_tables(lay.len_s))


def _even_in_io(lay, w):
    specs, tabs = _rope_inputs(lay)
    in_specs = [pl.BlockSpec(w.shape, lambda i: (0, 0), pipeline_mode=pl.Buffered(1))] + specs
    outs = [(512, BF16), (KV_WIDTH, BF16), (512, BF16), (512, BF16), (512, BF16), (512, F32), (128, F32)]
    return in_specs, [w] + tabs, _in_proj_outputs(outs, lay)


ODD_COLS = (512, 512, 512, 512, 512, 512, 256)


def _head_rms(x, seg_ref, g):
    x2 = x * x
    hi = x2.astype(BF16)
    lo = (x2 - hi.astype(F32)).astype(BF16)
    w = seg_ref.shape[0]
    ms = [_dot(hi[:, c:c + w], seg_ref[...]) + _dot(lo[:, c:c + w], seg_ref[...]) for c in range(0, x.shape[1], w)]
    ms = ms[0] if len(ms) == 1 else jnp.concatenate(ms, 1)
    return x * lax.rsqrt(ms + 1e-6) * g


def _odd_in_body(x, m, ins, outs, rope, cache_refs):
    w_ref, segq_ref, segk_ref, qn_ref, kn_ref, *tabs = ins
    q_ref, ff_ref, fb_ref, v_ref, cg_ref, dq_ref, dkv_ref = outs
    cache = cache_refs is not None

    h = (x * (1.0 + m[4:5]) + m[3:4]).astype(BF16)
    offs = [0]
    for c in ODD_COLS:
        offs.append(offs[-1] + c)
    proj = lambda k: _dot(h, w_ref[:, offs[k]:offs[k + 1]])

    q_ref[...] = _silu(proj(0)).astype(BF16)
    ff_ref[...] = proj(1)
    fb_ref[...] = proj(2)
    v_ref[...] = proj(3).astype(BF16)
    cg_ref[...] = proj(4)
    dq = _head_rms(proj(5), segq_ref, qn_ref[...])
    dkv = proj(6)
    dk = _head_rms(dkv[:, :128], segk_ref, kn_ref[...])
    dv = dkv[:, 128:]
    if cache:
        _store_cache(*cache_refs, jnp.concatenate([dk, dv], 1))
    if rope:
        cos, s_up, s_dn = (t[...] for t in tabs)
        dq = _rope(dq, cos, s_up, s_dn)
        dk = _rope(dk, cos, s_up, s_dn)
    dq_ref[...] = (dq * QK_SCALE).astype(BF16)
    dkv_ref[...] = _kv_with_ones(jnp.concatenate([dk, dv], 1)).astype(BF16)


def _segment_mean_matrix(width):
    r = jnp.arange(width) // HEAD_DIM
    return jnp.where(r[:, None] == r[None, :], 1.0 / HEAD_DIM, 0.0).astype(BF16)


def _odd_in_io(lay, w, q_norm, k_norm):
    const = lambda a: pl.BlockSpec(a.shape, lambda i: (0, 0))
    segq, segk = _segment_mean_matrix(256), _segment_mean_matrix(128)
    qn = jnp.tile(q_norm, D_HEADS).reshape(1, 512)
    kn = jnp.tile(k_norm, D_KV).reshape(1, 128)
    specs, tabs = _rope_inputs(lay)
    in_specs = [pl.BlockSpec(w.shape, lambda i: (0, 0), pipeline_mode=pl.Buffered(1)),
                const(segq), const(segk), const(qn), const(kn)] + specs
    outs = [(512, BF16), (512, F32), (512, F32), (512, BF16), (512, F32), (512, BF16), (KV_WIDTH, BF16)]
    return in_specs, [w, segq, segk, qn, kn] + tabs, _in_proj_outputs(outs, lay)


def _attn_kernel(*refs, seq_len, tq, q_blocks, n_ctx, banded, has_sink):
    it = iter(refs)
    q_ref, kv_ref = next(it), next(it)
    ckv_ref = next(it) if n_ctx else None
    sink_ref = next(it) if has_sink else None
    o_ref = next(it)
    groups = A_HEADS // A_KV
    rows = groups * tq
    low_half = lax.broadcasted_iota(jnp.int32, (tq, 2 * HEAD_DIM), 1) < HEAD_DIM

    def run_kv_head(blk, kh):
        r0 = blk * tq
        j = pl.program_id(1) * q_blocks + blk
        kcol = slice(kh * HEAD_DIM, (kh + 1) * HEAD_DIM)
        vcol = slice(128 + kh * 256, 128 + (kh + 1) * 256)
        qs = jnp.concatenate([q_ref[r0:r0 + tq, (kh * groups + g) * HEAD_DIM:(kh * groups + g + 1) * HEAD_DIM]
                              for g in range(groups)], axis=0)
        tiles = []
        if banded:
            span = tq + 2 * WINDOW
            start = pl.multiple_of(jnp.clip(j * tq - WINDOW, 0, seq_len - span), WINDOW)
            qpos = j * tq + (lax.broadcasted_iota(jnp.int32, (rows, span), 0) & (tq - 1))
            kpos = start + lax.broadcasted_iota(jnp.int32, (rows, span), 1)
            band = jnp.abs(kpos - qpos) <= WINDOW
            tiles.append((kv_ref[pl.ds(start, span), kcol], kv_ref[pl.ds(start, span), vcol], band))
        else:
            tk = min(ATT_KEY_TILE, seq_len)
            for t in range(seq_len // tk):
                tiles.append((kv_ref[t * tk:(t + 1) * tk, kcol], kv_ref[t * tk:(t + 1) * tk, vcol], None))
        scores = [_dot_nt(qs, k_t) for k_t, _, _ in tiles]
        if n_ctx:
            tiles.append((None, ckv_ref[0, kh, HEAD_DIM:, :], None))
            scores.append(_dot(qs, ckv_ref[0, kh, :HEAD_DIM, :]))
        yield
        scores = [s if mask is None else jnp.where(mask, s, NEG_INF) for s, (_, _, mask) in zip(scores, tiles)]
        m = functools.reduce(jnp.maximum, [jnp.max(s, -1, keepdims=True) for s in scores])
        if has_sink:
            sink = jnp.concatenate([jnp.full((tq, 1), sink_ref[kh * groups + g] * LOG2_E, F32)
                                    for g in range(groups)], 0)
            m = jnp.maximum(m, sink)
        pv = functools.reduce(lambda a, b: a + b,
                              [(_dot_nt if k_t is None else _dot)(jnp.exp2(s - m).astype(BF16), v_t)
                               for s, (k_t, v_t, _) in zip(scores, tiles)])
        yield
        total = pv[:, 2 * HEAD_DIM:]
        if has_sink:
            total = total + jnp.exp2(sink - m)
        o = pv[:, :2 * HEAD_DIM] / total
        for g in range(0, groups, 2):
            pair = jnp.where(low_half, o[g * tq:(g + 1) * tq], o[(g + 1) * tq:(g + 2) * tq])
            c0 = (kh * groups + g) * HEAD_DIM
            o_ref[r0:r0 + tq, c0:c0 + 2 * HEAD_DIM] = pair.astype(BF16)

    run = _run_staggered if seq_len + n_ctx > 2 * ATT_KEY_TILE else _run_interleaved
    run([run_kv_head(blk, kh) for blk in range(q_blocks) for kh in range(A_KV)])


def _attention(q, kv, ctx_kv, sink, batch, seq_len, row0, tq, q_blocks, banded):
    n_ctx = 0 if ctx_kv is None else ctx_kv.shape[-1]
    per_seq = seq_len // (tq * q_blocks)
    q0, kv0 = row0 // (tq * q_blocks), row0 // seq_len
    in_specs = [pl.BlockSpec((q_blocks * tq, 512), lambda b, j: (q0 + b * per_seq + j, 0)),
                pl.BlockSpec((seq_len, KV_WIDTH), lambda b, j: (kv0 + b, 0))]
    args = [q, kv]
    if n_ctx:
        in_specs.append(pl.BlockSpec((1,) + ctx_kv.shape[1:], lambda b, j: (b, 0, 0, 0)))
        args.append(ctx_kv)
    if sink is not None:
        in_specs.append(pl.BlockSpec(memory_space=pltpu.SMEM))
        args.append(sink.reshape(-1).astype(F32))
    return pl.pallas_call(
        functools.partial(_attn_kernel, seq_len=seq_len, tq=tq, q_blocks=q_blocks, n_ctx=n_ctx, banded=banded,
                          has_sink=sink is not None),
        grid=(batch, per_seq),
        in_specs=in_specs,
        out_specs=pl.BlockSpec((q_blocks * tq, 512), lambda b, j: (b * per_seq + j, 0)),
        out_shape=jax.ShapeDtypeStruct((batch * seq_len, 512), BF16),
        compiler_params=_cparams(("parallel", "arbitrary")),
        name="gqa_attention",
    )(*args)


def _log_sigmoid(x):
    return jnp.minimum(x, 0.0) - jnp.log1p(jnp.exp(-jnp.abs(x)))


def _mlstm_kernel(*refs, seq_len, has_init, emit_state):
    it = iter(refs)
    q_ref, k_ref, v_ref, g_ref, bo_ref, gb_ref, ng_ref = (next(it) for _ in range(7))
    c0_ref, n0_ref, m0_ref = (next(it), next(it), next(it)) if has_init else (None, None, None)
    y_ref = next(it)
    co_ref, no_ref, mo_ref = (next(it), next(it), next(it)) if emit_state else (None, None, None)
    c_scr, n_scr, m_scr, hf_scr, hb_scr, ig_buf, bc_buf, ut_buf = (next(it) for _ in range(8))

    t_len = MLSTM_T
    n_chunks = seq_len // t_len
    n_streams = 2 * B_HEADS

    for s in range(n_streams):
        if has_init:
            c_scr[s] = c0_ref[0, s // B_HEADS, s % B_HEADS]
        else:
            c_scr[s] = jnp.zeros((B_DIM, B_DIM), F32)
    if has_init:
        n_scr[...] = n0_ref[0]
        m_scr[...] = m0_ref[0]
    else:
        n_scr[...] = jnp.zeros_like(n_scr)
        m_scr[...] = jnp.zeros_like(m_scr)

    ri = lax.broadcasted_iota(jnp.int32, (t_len, t_len), 0)
    ci = lax.broadcasted_iota(jnp.int32, (t_len, t_len), 1)
    lower = ci <= ri
    upper = ci >= ri
    tri_f = jnp.where(lower, 1.0, 0.0).astype(BF16)
    tri_b = jnp.where(upper, 1.0, 0.0).astype(BF16)
    gate_bias = gb_ref[...]
    sel_row = lax.broadcasted_iota(jnp.int32, (128, B_HEADS * B_DIM), 0)
    sel_head = lax.broadcasted_iota(jnp.int32, (128, B_HEADS * B_DIM), 1) // B_DIM
    gate_select = []
    for direction in range(2):
        i_base = 2 * direction * B_HEADS
        gate_select.append((jnp.where(sel_row == i_base + sel_head, 1.0, 0.0).astype(BF16),
                            jnp.where(sel_row == i_base + B_HEADS + sel_head, 1.0, 0.0).astype(BF16)))

    def chunk_rows(c, direction):
        chunk = c if direction == 0 else n_chunks - 1 - c
        return pl.ds(pl.multiple_of(chunk * t_len, t_len), t_len)

    def prepare_gates(c, direction):
        slot = c % 2
        rows = chunk_rows(jnp.minimum(c, n_chunks - 1), direction)
        tri = tri_f if direction == 0 else tri_b
        sel_i, sel_f = gate_select[direction]
        gates = g_ref[rows, :] + gate_bias
        hi, mid, lo = _split3(_log_sigmoid(gates))
        b3 = _dot(tri, jnp.concatenate([hi, mid, lo], 1))
        ig_buf[slot, direction] = _dot_sel_rhs(gates, sel_i)
        yield
        bc_all = b3[:, :128] + b3[:, 128:256] + b3[:, 256:]
        bc_rep = _dot_sel_rhs(bc_all, sel_f)
        ut_buf[slot, direction] = (gates - pltpu.roll(bc_all, 128 - B_HEADS, 1)).T
        yield
        bc_buf[slot, direction] = bc_rep

    def run_direction(c, direction):
        rows = chunk_rows(c, direction)
        slot = c % 2
        causal = lower if direction == 0 else upper
        last = t_len - 1 if direction == 0 else 0
        h_out = hf_scr if direction == 0 else hb_scr
        ig_rep, bc_rep, u_t = ig_buf[slot, direction], bc_buf[slot, direction], ut_buf[slot, direction]
        heads = []
        for hd in range(B_HEADS):
            s = direction * B_HEADS + hd
            col = slice(hd * B_DIM, (hd + 1) * B_DIM)
            qc, kc, vc = q_ref[rows, col], k_ref[rows, col], v_ref[rows, col]
            c_prev = c_scr[s]
            n_prev = n_scr[s:s + 1, :]
            qk = _dot_nt(qc, kc)
            q_state = _dot(qc, c_prev.astype(BF16))
            qn = _dot_nt(qc, jnp.broadcast_to(n_prev, (B_DIM, B_DIM)).astype(BF16))
            heads.append((s, col, qc, kc, vc, c_prev, n_prev, qk, q_state, qn))
        yield
        staged = []
        for (s, col, qc, kc, vc, c_prev, n_prev, qk, q_state, qn) in heads:
            i_col = 2 * direction * B_HEADS + (s % B_HEADS)
            ig, bc = ig_rep[:, col], bc_rep[:, col]
            dmat = jnp.where(causal, bc[:, :t_len] + u_t[i_col:i_col + 1, :], -jnp.inf)
            m_prev = m_scr[s:s + 1, :]
            inter = bc + m_prev
            m_t = jnp.maximum(inter, jnp.max(dmat, -1, keepdims=True))
            w = (jnp.exp(dmat - m_t[:, :t_len]) * qk).astype(BF16)
            wv = _dot(w, jnp.concatenate([vc, jnp.ones_like(vc)], 1))
            b_tot = bc[last:last + 1, :]
            g = b_tot - bc + ig
            m_new = jnp.maximum(b_tot + m_prev, jnp.max(g, 0, keepdims=True))
            kw = kc.astype(F32) * jnp.exp(g - m_new)
            upd = _dot_tn(kw.astype(BF16), vc)
            decay = jnp.exp(b_tot + m_prev - m_new)
            staged.append((s, col, inter, m_t, c_prev, n_prev, q_state, qn, wv, m_new, kw, upd, decay))
        yield
        for (s, col, inter, m_t, c_prev, n_prev, q_state, qn, wv, m_new, kw, upd, decay) in staged:
            a = jnp.exp(inter - m_t)
            num = a * q_state + wv[:, :B_DIM]
            den = a * qn + wv[:, B_DIM:]
            h_out[rows, col] = num / jnp.maximum(jnp.abs(den), jnp.exp(-m_t))
            c_scr[s] = decay * c_prev + upd
            n_scr[s:s + 1, :] = decay * n_prev + jnp.sum(kw, 0, keepdims=True)
            m_scr[s:s + 1, :] = m_new

    def chained(c, direction):
        yield from prepare_gates(c, direction)
        yield
        yield from run_direction(c, direction)

    def step(c, carry):
        _run_interleaved([chained(c, 0), chained(c, 1)])
        return carry

    lax.fori_loop(0, n_chunks, step, 0, unroll=2)

    blk = min(256, seq_len)
    ng = ng_ref[...]

    def finish(i, carry):
        rows = pl.ds(pl.multiple_of(i * blk, blk), blk)
        hsum = hf_scr[rows, :] + hb_scr[rows, :]
        gate = jax.nn.sigmoid(bo_ref[rows, :])
        parts = [_rms_lastdim(hsum[:, hd * B_DIM:(hd + 1) * B_DIM], ng[:, hd * B_DIM:(hd + 1) * B_DIM])
                 for hd in range(B_HEADS)]
        y_ref[rows, :] = (gate * jnp.concatenate(parts, 1)).astype(BF16)
        return carry

    lax.fori_loop(0, seq_len // blk, finish, 0)

    if emit_state:
        for s in range(n_streams):
            co_ref[0, s // B_HEADS, s % B_HEADS] = c_scr[s]
        no_ref[0] = n_scr[...]
        mo_ref[0] = m_scr[...]


def _mlstm(q, k, v, gates, bo, gate_bias, norm_g, init, batch, seq_len, row0, emit_state):
    n = batch * seq_len
    seq0 = row0 // seq_len
    in_row = lambda width: pl.BlockSpec((seq_len, width), lambda b: (seq0 + b, 0))
    row = lambda width: pl.BlockSpec((seq_len, width), lambda b: (b, 0))
    const = lambda a: pl.BlockSpec(a.shape, lambda b: (0,) * a.ndim)
    gb = jnp.zeros((1, 128), F32).at[0, :4 * B_HEADS].set(gate_bias.reshape(-1))
    ng = norm_g.reshape(1, B_HEADS * B_DIM)
    in_specs = [in_row(512), in_row(512), in_row(512), in_row(128), in_row(512), const(gb), const(ng)]
    args = [q, k, v, gates, bo, gb, ng]
    n_streams = 2 * B_HEADS
    if init is not None:
        c0, n0, m0 = init
        in_specs += [pl.BlockSpec((1, 2, B_HEADS, B_DIM, B_DIM), lambda b: (b, 0, 0, 0, 0)),
                     pl.BlockSpec((1, n_streams, B_DIM), lambda b: (b, 0, 0)),
                     pl.BlockSpec((1, n_streams, B_DIM), lambda b: (b, 0, 0))]
        args += [c0, n0.reshape(batch, n_streams, B_DIM),
                 jnp.broadcast_to(m0.reshape(batch, n_streams, 1), (batch, n_streams, B_DIM))]
    out_specs = [row(512)]
    out_shape = [jax.ShapeDtypeStruct((n, 512), BF16)]
    if emit_state:
        out_specs += [pl.BlockSpec((1, 2, B_HEADS, B_DIM, B_DIM), lambda b: (b, 0, 0, 0, 0)),
                      pl.BlockSpec((1, n_streams, B_DIM), lambda b: (b, 0, 0)),
                      pl.BlockSpec((1, n_streams, B_DIM), lambda b: (b, 0, 0))]
        out_shape += [jax.ShapeDtypeStruct((batch, 2, B_HEADS, B_DIM, B_DIM), F32),
                      jax.ShapeDtypeStruct((batch, n_streams, B_DIM), F32),
                      jax.ShapeDtypeStruct((batch, n_streams, B_DIM), F32)]
    return pl.pallas_call(
        functools.partial(_mlstm_kernel, seq_len=seq_len, has_init=init is not None, emit_state=emit_state),
        grid=(batch,),
        in_specs=in_specs,
        out_specs=out_specs,
        out_shape=out_shape,
        scratch_shapes=[pltpu.VMEM((n_streams, B_DIM, B_DIM), F32),
                        pltpu.VMEM((n_streams, B_DIM), F32),
                        pltpu.VMEM((n_streams, B_DIM), F32),
                        pltpu.VMEM((seq_len, 512), F32),
                        pltpu.VMEM((seq_len, 512), F32),
                        pltpu.VMEM((2, 2, MLSTM_T, B_HEADS * B_DIM), F32),
                        pltpu.VMEM((2, 2, MLSTM_T, B_HEADS * B_DIM), F32),
                        pltpu.VMEM((2, 2, 128, MLSTM_T), F32)],
        compiler_params=_cparams(("parallel",)),
        name="mlstm_scan",
    )(*args)


def _hgrn_kernel(*refs, seq_len, layer, has_init, emit_state):
    it = iter(refs)
    q_ref, ff_ref, fb_ref, v_ref, cg_ref, lbl_ref, ng_ref = (next(it) for _ in range(7))
    s0_ref = next(it) if has_init else None
    y_ref = next(it)
    so_ref = next(it) if emit_state else None
    st_scr, of_scr, ob_scr = next(it), next(it), next(it)

    t_len = HGRN_T
    n_sub = t_len // SUB
    n_chunks = seq_len // t_len
    n_streams = 2 * C_HEADS

    logits = lbl_ref[...]
    e = jnp.exp(logits - jnp.max(logits, 0, keepdims=True))
    sm = e / jnp.sum(e, 0, keepdims=True)
    lb = jnp.sum(sm[0:layer + 1], 0, keepdims=True) - sm[0:1]

    for s in range(n_streams):
        if has_init:
            st_scr[s] = s0_ref[0, s // C_HEADS, s % C_HEADS].T
        else:
            st_scr[s] = jnp.zeros((C_DV, C_DK), F32)

    ri = lax.broadcasted_iota(jnp.int32, (t_len, t_len), 0)
    ci = lax.broadcasted_iota(jnp.int32, (t_len, t_len), 1)
    tri_f = jnp.where(ci <= ri, 1.0, 0.0).astype(BF16)
    tri_b = jnp.where(ci >= ri, 1.0, 0.0).astype(BF16)
    sub_row = lax.broadcasted_iota(jnp.int32, (SUB, C_DK), 0)
    ones_dk = jnp.ones((C_DK, C_DK), BF16)

    def run_stream(c, direction, hd):
        chunk = c if direction == 0 else n_chunks - 1 - c
        rows = pl.ds(pl.multiple_of(chunk * t_len, t_len), t_len)
        f_ref = ff_ref if direction == 0 else fb_ref
        tri = tri_f if direction == 0 else tri_b
        last = t_len - 1 if direction == 0 else 0
        o_out = of_scr if direction == 0 else ob_scr
        s = direction * C_HEADS + hd
        col = slice(hd * C_DK, (hd + 1) * C_DK)
        lbh = lb[:, col]
        f = lbh + (1.0 - lbh) * jax.nn.sigmoid(f_ref[rows, col])
        log_k = jnp.log2(1.0 - f)
        hi, mid, lo = _split3(jnp.log2(f))
        a3 = _dot(tri, jnp.concatenate([hi, mid, lo], 1))
        yield
        a_cum = a3[:, :C_DK] + a3[:, C_DK:2 * C_DK] + a3[:, 2 * C_DK:]
        a_key = a_cum - log_k
        a_tot = a_cum[last:last + 1, :]
        qf = q_ref[rows, col].astype(F32)
        vc = v_ref[rows, col]
        vf = vc.astype(F32)
        st = st_scr[s]
        inter = _dot_nt((qf * jnp.exp2(a_cum)).astype(BF16), st.astype(BF16))
        upd = _dot_tn(vc, jnp.exp2(a_tot - a_key).astype(BF16))
        ps = []
        for blk in range(n_sub):
            b0 = blk * SUB
            a_i, q_i, ak_i = (t[b0:b0 + SUB] for t in (a_cum, qf, a_key))
            for j in range(SUB):
                seen = (sub_row >= j) if direction == 0 else (sub_row <= j)
                ps.append(jnp.where(seen, jnp.exp2(a_i - ak_i[j:j + 1]), 0.0) * q_i)
        att = _dot(jnp.concatenate(ps, 0).astype(BF16), ones_dk)
        off = []
        for blk in range(n_sub):
            b0 = blk * SUB
            if direction == 0 and blk > 0:
                a_ref, kr = a_cum[b0 - 1:b0], slice(0, b0)
            elif direction == 1 and blk < n_sub - 1:
                a_ref, kr = a_cum[b0 + SUB:b0 + SUB + 1], slice(b0 + SUB, t_len)
            else:
                off.append(None)
                continue
            qt = (qf[b0:b0 + SUB] * jnp.exp2(a_cum[b0:b0 + SUB] - a_ref)).astype(BF16)
            kt = jnp.exp2(a_ref - a_key[kr]).astype(BF16)
            off.append((_dot_nt(qt, kt), kr))
        yield
        st_scr[s] = jnp.exp2(a_tot) * st + upd
        outs = []
        for blk in range(n_sub):
            b0 = blk * SUB
            o_i = inter[b0:b0 + SUB]
            for j in range(SUB):
                r = (blk * SUB + j) * SUB
                o_i = o_i + att[r:r + SUB] * vf[b0 + j:b0 + j + 1]
            if off[blk] is not None:
                att_off, kr = off[blk]
                outs.append((o_i, _dot(att_off.astype(BF16), vc[kr])))
            else:
                outs.append((o_i, None))
        yield
        o_out[rows, col] = jnp.concatenate([o_i if o_off is None else o_i + o_off for o_i, o_off in outs], 0)

    def step(c, carry):
        _run_interleaved([run_stream(c, direction, hd) for direction in range(2) for hd in range(C_HEADS)])
        return carry

    lax.fori_loop(0, n_chunks, step, 0, unroll=2)

    blk_rows = min(256, seq_len)
    ng = ng_ref[...]

    def finish(i, carry):
        rows = pl.ds(pl.multiple_of(i * blk_rows, blk_rows), blk_rows)
        osum = of_scr[rows, :] + ob_scr[rows, :]
        parts = [_rms_lastdim(osum[:, hd * C_DV:(hd + 1) * C_DV], ng[:, hd * C_DV:(hd + 1) * C_DV])
                 for hd in range(C_HEADS)]
        y_ref[rows, :] = (jnp.concatenate(parts, 1) * _silu(cg_ref[rows, :])).astype(BF16)
        return carry

    lax.fori_loop(0, seq_len // blk_rows, finish, 0)

    if emit_state:
        for s in range(n_streams):
            so_ref[0, s // C_HEADS, s % C_HEADS] = st_scr[s].T


def _hgrn(q, ff, fb, v, cg, lb_logits, norm_g, layer, init, batch, seq_len, row0, emit_state):
    n = batch * seq_len
    seq0 = row0 // seq_len
    in_row = lambda width: pl.BlockSpec((seq_len, width), lambda b: (seq0 + b, 0))
    row = lambda width: pl.BlockSpec((seq_len, width), lambda b: (b, 0))
    const = lambda a: pl.BlockSpec(a.shape, lambda b: (0,) * a.ndim)
    ng = norm_g.reshape(1, C_HEADS * C_DV)
    state_spec = pl.BlockSpec((1, 2, C_HEADS, C_DK, C_DV), lambda b: (b, 0, 0, 0, 0))
    in_specs = [in_row(512)] * 5 + [const(lb_logits), const(ng)]
    args = [q, ff, fb, v, cg, lb_logits, ng]
    if init is not None:
        in_specs.append(state_spec)
        args.append(init)
    out_specs = [row(512)]
    out_shape = [jax.ShapeDtypeStruct((n, 512), BF16)]
    if emit_state:
        out_specs.append(state_spec)
        out_shape.append(jax.ShapeDtypeStruct((batch, 2, C_HEADS, C_DK, C_DV), F32))
    return pl.pallas_call(
        functools.partial(_hgrn_kernel, seq_len=seq_len, layer=layer, has_init=init is not None,
                          emit_state=emit_state),
        grid=(batch,),
        in_specs=in_specs,
        out_specs=out_specs,
        out_shape=out_shape,
        scratch_shapes=[pltpu.VMEM((2 * C_HEADS, C_DV, C_DK), F32),
                        pltpu.VMEM((seq_len, 512), F32),
                        pltpu.VMEM((seq_len, 512), F32)],
        compiler_params=_cparams(("parallel",)),
        name="hgrn2_scan",
    )(*args)


def _prep_ctx_kv(k_ctx, v_ctx):
    k_t = jnp.transpose(k_ctx, (0, 2, 3, 1)).astype(BF16)
    v_t = jnp.transpose(v_ctx, (0, 2, 3, 1)).astype(BF16)
    ones = jnp.ones(k_t.shape[:2] + (128, k_t.shape[3]), BF16)
    return jnp.concatenate([k_t, v_t, v_t, ones], 2)


def kernel(x_prompt, x_sample, c, cache_a_k, cache_a_v, state_b_C, state_b_n, state_b_m, state_c_S, cache_d_k, cache_d_v, c_ctx, ada_w, ada_b, ln_g, ln_b, ffn_w1, ffn_w3, ffn_w2, w_in_even, w_out_even, a_sink, b_gate_bias, b_norm_g, w_in_odd, w_out_odd, c_lb_logits, c_norm_g, d_q_norm, d_k_norm):
    batch_p, len_p, d = x_prompt.shape
    batch_s, len_s, _ = x_sample.shape

    cvec = jnp.concatenate([c_ctx[None], c, jnp.zeros((MOD_ROWS - 1 - batch_s, d), F32)], 0)
    mod_all = _modulation(cvec, ada_w, ada_b)

    lay = _Layout(batch_p, len_p, batch_s, len_s)
    new = {}
    ffn_stacks = (ffn_w1, ffn_w3, ffn_w2)
    ffn_w = tuple(a[0:1, 0:1].astype(BF16) for a in ffn_stacks)
    x = [x_prompt.reshape(lay.n_p, d), x_sample.reshape(lay.n_s, d)]

    for l in range(DEPTH):
        mod = mod_all[l].reshape(MOD_ROWS, 9, d)
        i = l // 2
        if l % 2 == 0:
            in_proj = (_even_in_body, _even_in_io, dict(w=w_in_even[i].astype(BF16)))
            w_out = w_out_even[i].astype(BF16)
        else:
            in_proj = (_odd_in_body, _odd_in_io,
                       dict(w=w_in_odd[i].astype(BF16), q_norm=d_q_norm[i], k_norm=d_k_norm[i]))
            w_out = w_out_odd[i].astype(BF16)

        x_all, *proj, k_cache, v_cache, w1, w3, w2 = _ffn(x, mod, 0, ln_g[l, 0], ln_b[l, 0], ffn_w, lay,
                                                         in_proj=in_proj, next_weights=ffn_stacks + (l, 1))
        ffn_w = (w1, w3, w2)
        prompt = dict(batch=batch_p, seq_len=len_p, row0=0)
        sample = dict(batch=batch_s, seq_len=len_s, row0=lay.n_p)
        if l % 2 == 0:
            aq, akv, bq, bk, bv, bo, bg = proj
            new["a_k"], new["a_v"] = _cache_layout(k_cache), _cache_layout(v_cache)
            scan = (bq, bk, bv, bg, bo, b_gate_bias[i], b_norm_g[i])
            ya_p = _attention(aq, akv, None, a_sink[i], **prompt, tq=Q_BLOCK, q_blocks=len_p // Q_BLOCK, banded=False)
            yb_p, c_new, n_new, m_new = _mlstm(*scan, None, **prompt, emit_state=True)
            new["b_C"] = c_new[:, None]
            new["b_n"] = n_new.reshape(batch_p, 1, 2, B_HEADS, B_DIM)
            new["b_m"] = m_new[:, :, 0].reshape(batch_p, 1, 2, B_HEADS)
            ya_s = _attention(aq, akv, _prep_ctx_kv(cache_a_k[:, i], cache_a_v[:, i]), a_sink[i], **sample,
                              tq=Q_BLOCK, q_blocks=4, banded=True)
            yb_s, = _mlstm(*scan, (state_b_C[:, i], state_b_n[:, i], state_b_m[:, i]), **sample, emit_state=False)
        else:
            cq, ff, fb, cv, cg, dq, dkv = proj
            new["d_k"], new["d_v"] = _cache_layout(k_cache), _cache_layout(v_cache)
            scan = (cq, ff, fb, cv, cg, c_lb_logits, c_norm_g[i], l)
            ya_p, s_new = _hgrn(*scan, None, **prompt, emit_state=True)
            new["c_S"] = s_new[:, None]
            yb_p = _attention(dq, dkv, None, None, **prompt, tq=Q_BLOCK, q_blocks=len_p // Q_BLOCK, banded=False)
            ya_s, = _hgrn(*scan, state_c_S[:, i], **sample, emit_state=False)
            yb_s = _attention(dq, dkv, _prep_ctx_kv(cache_d_k[:, i], cache_d_v[:, i]), None, **sample,
                              tq=Q_BLOCK, q_blocks=2, banded=False)
        last = l == DEPTH - 1
        x = _ffn([x_all], mod, 2, ln_g[l, 2], ln_b[l, 2], ffn_w, lay,
                 mixer=(ya_p, yb_p, ya_s, yb_s, w_out, ln_g[l, 1], ln_b[l, 1]), split_out=last,
                 next_weights=None if last else ffn_stacks + (l + 1, 0))
        if not last:
            *x, w1, w3, w2 = x
            ffn_w = (w1, w3, w2)

    y_prompt, y_sample = x
    return (y_prompt.reshape(batch_p, len_p, d), y_sample.reshape(batch_s, len_s, d), new["a_k"], new["a_v"],
            new["b_C"], new["b_n"], new["b_m"], new["c_S"], new["d_k"], new["d_v"])
```

```python
import functools

import jax
import jax.numpy as jnp
import numpy as np
from jax import lax
from jax.experimental import pallas as pl
from jax.experimental.pallas import tpu as pltpu

F32 = jnp.float32
BF16 = jnp.bfloat16

DEPTH = 2
GRID_W = 64
HEAD_DIM = 64
A_HEADS = 8
A_KV = 2
WINDOW = 128
B_HEADS = 4
B_DIM = 128
C_HEADS = 4
C_DK = 128
C_DV = 128
D_HEADS = 8
D_KV = 2
Q_BLOCK = 128
ROPE_THETA = 10000.0
ALPHA = (2 * DEPTH) ** 0.25
NEG_INF = -1e30
LOG2_E = 1.4426950408889634
QK_SCALE = HEAD_DIM ** -0.5 * LOG2_E

MOD_ROWS = 8
FF_CHUNK = 256
FFN_TILE = 512
WEIGHT_CAST_STEPS = 16
MLSTM_T = 64
HGRN_T = 64
SUB = 8
ATT_KEY_TILE = 512
VMEM_LIMIT = 58 * 1024 * 1024


def _cparams(sem):
    return pltpu.CompilerParams(dimension_semantics=sem, vmem_limit_bytes=VMEM_LIMIT)


def _dot(a, b):
    return jnp.dot(a, b, preferred_element_type=F32)


def _dot_nt(a, b):
    return lax.dot_general(a, b, (((1,), (1,)), ((), ())), preferred_element_type=F32)


def _dot_tn(a, b):
    return lax.dot_general(a, b, (((0,), (0,)), ((), ())), preferred_element_type=F32)


def _split3(x):
    hi = x.astype(BF16)
    r1 = x - hi.astype(F32)
    mid = r1.astype(BF16)
    lo = (r1 - mid.astype(F32)).astype(BF16)
    return hi, mid, lo


def _dot_sel_rhs(x, sel):
    hi = x.astype(BF16)
    lo = (x - hi.astype(F32)).astype(BF16)
    r = _dot(jnp.concatenate([hi, lo], 0), sel)
    return r[:x.shape[0]] + r[x.shape[0]:]


def _run_interleaved(gens):
    live = list(gens)
    while live:
        nxt = []
        for g in live:
            try:
                next(g)
                nxt.append(g)
            except StopIteration:
                pass
        live = nxt


def _run_staggered(gens):
    pending, live = list(gens), []
    while pending or live:
        if pending:
            live.insert(0, pending.pop(0))
        nxt = []
        for g in live:
            try:
                next(g)
                nxt.append(g)
            except StopIteration:
                pass
        live = nxt


def _silu(x):
    return x * jax.nn.sigmoid(x)


def _layernorm(z, g, b):
    mu = jnp.mean(z, -1, keepdims=True)
    zc = z - mu
    var = jnp.mean(zc * zc, -1, keepdims=True)
    return zc * lax.rsqrt(var + 1e-5) * g + b


def _rms_lastdim(x, g):
    return x * lax.rsqrt(jnp.mean(x * x, -1, keepdims=True) + 1e-6) * g


class _Layout:
    def __init__(self, batch_p, len_p, batch_s, len_s):
        self.tm = FFN_TILE
        self.len_p, self.len_s = len_p, len_s
        self.n_p, self.n_s = batch_p * len_p, batch_s * len_s
        self.n = self.n_p + self.n_s
        self.steps_p = self.n_p // self.tm
        self.steps = self.n // self.tm

    def all_rows(self, i):
        return (i, 0)

    def prompt_rows(self, i):
        return (jnp.minimum(i, self.steps_p - 1), 0)

    def sample_rows(self, i):
        return (jnp.maximum(i - self.steps_p, 0), 0)

    def mod_row(self, i):
        req = jnp.maximum(i - self.steps_p, 0) // (self.len_s // self.tm)
        return (jnp.where(i < self.steps_p, 0, 1 + req), 0, 0)

    def rope_rows(self, i):
        return (jnp.maximum(i - self.steps_p, 0) % (self.len_s // self.tm), 0)


def _mod_kernel(c_ref, w_ref, b_ref, o_ref):
    s = _silu(c_ref[...]).astype(BF16)
    o_ref[0] = _dot(s, w_ref[0].astype(BF16)) + b_ref[0]


def _modulation(cvec, ada_w, ada_b):
    depth, d, n = ada_w.shape
    tn = 2304
    return pl.pallas_call(
        _mod_kernel,
        grid=(depth, n // tn),
        in_specs=[pl.BlockSpec((MOD_ROWS, d), lambda l, j: (0, 0)),
                  pl.BlockSpec((1, d, tn), lambda l, j: (l, 0, j)),
                  pl.BlockSpec((1, 1, tn), lambda l, j: (l, 0, j))],
        out_specs=pl.BlockSpec((1, MOD_ROWS, tn), lambda l, j: (l, 0, j)),
        out_shape=jax.ShapeDtypeStruct((depth, MOD_ROWS, n), F32),
        compiler_params=_cparams(("parallel", "parallel")),
        name="modulation",
    )(cvec, ada_w, ada_b.reshape(depth, 1, n))


def _ffn_kernel(*refs, j, nf, steps_p, n_x, n_out, mixer, in_proj, cast_next):
    it = iter(refs)
    x_refs = [next(it) for _ in range(n_x)]
    mod_ref = next(it)
    if mixer:
        ya_p, yb_p, ya_s, yb_s, wo_ref, g1_ref, b1_ref = (next(it) for _ in range(7))
    w1_ref, w3_ref, w2_ref, g_ref, b_ref = (next(it) for _ in range(5))
    proj_ins = [next(it) for _ in range(in_proj[1])] if in_proj else []
    cast_ins = [next(it) for _ in range(3)] if cast_next else []
    o_refs = [next(it) for _ in range(n_out)]
    proj_outs = [next(it) for _ in range(in_proj[2])] if in_proj else []
    cache_refs = (next(it), next(it)) if in_proj else None
    cast_outs = [next(it) for _ in range(3)] if cast_next else []
    h_ref, acc_ref, res_ref = next(it), next(it), next(it)

    if cast_next:
        @pl.when(pl.program_id(0) < WEIGHT_CAST_STEPS)
        def _():
            for src, dst in zip(cast_ins, cast_outs):
                dst[...] = src[...].astype(BF16)

    is_prompt = pl.program_id(0) < steps_p
    pick = lambda p_ref, s_ref: jnp.where(is_prompt, p_ref[...], s_ref[...])
    m = mod_ref[0]
    shift, scale, gate = m[3 * j:3 * j + 1], m[3 * j + 1:3 * j + 2], m[3 * j + 2:3 * j + 3]
    x = x_refs[0][...] if n_x == 1 else pick(*x_refs)
    if mixer:
        ya, yb = pick(ya_p, ya_s), pick(yb_p, yb_s)
        half = ya.shape[1]
        y = _dot(ya, wo_ref[:half, :]) + _dot(yb, wo_ref[half:, :])
        x = _layernorm(ALPHA * x + m[5:6] * y, g1_ref[...], b1_ref[...])
    res_ref[...] = x
    h_ref[...] = (x * (1.0 + scale) + shift).astype(BF16)
    for f in range(nf):
        cols = slice(f * FF_CHUNK, (f + 1) * FF_CHUNK)
        h = h_ref[...]
        u = (_silu(_dot(h, w1_ref[0, 0, :, cols])) * _dot(h, w3_ref[0, 0, :, cols])).astype(BF16)
        y = _dot(u, w2_ref[0, 0, cols, :])
        if f == 0:
            acc_ref[...] = y
        else:
            acc_ref[...] += y
    out = _layernorm(ALPHA * res_ref[...] + 0.5 * gate * acc_ref[...], g_ref[...], b_ref[...])
    if n_out == 1:
        o_refs[0][...] = out
    else:
        res_ref[...] = out

        @pl.when(is_prompt)
        def _():
            o_refs[0][...] = res_ref[...]

        @pl.when(jnp.logical_not(is_prompt))
        def _():
            o_refs[1][...] = res_ref[...]

    if in_proj:
        @pl.when(is_prompt)
        def _():
            in_proj[0](o_refs[0][...], m, proj_ins, proj_outs, rope=False, cache_refs=cache_refs)

        @pl.when(jnp.logical_not(is_prompt))
        def _():
            in_proj[0](o_refs[0][...], m, proj_ins, proj_outs, rope=True, cache_refs=None)


def _ffn(xs, mod, j, g, b, weights, lay, mixer=None, in_proj=None, split_out=False, next_weights=None):
    w1, w3, w2 = weights
    d = xs[0].shape[1]
    nf = w1.shape[-1] // FF_CHUNK
    tm = lay.tm
    spec = lambda width, index: pl.BlockSpec((tm, width), index)
    vec = pl.BlockSpec((1, d), lambda i: (0, 0))
    whole = lambda a: pl.BlockSpec(a.shape, lambda i: (0, 0, 0, 0), pipeline_mode=pl.Buffered(1))
    pair = (lay.prompt_rows, lay.sample_rows)
    in_specs = [spec(d, lay.all_rows)] if len(xs) == 1 else [spec(d, index) for index in pair]
    in_specs.append(pl.BlockSpec((1, 9, d), lay.mod_row))
    args = list(xs) + [mod]
    if mixer is not None:
        ya_p, yb_p, ya_s, yb_s, w_out, g1, b1 = mixer
        in_specs += [spec(ya_p.shape[1], pair[0]), spec(yb_p.shape[1], pair[0]),
                     spec(ya_s.shape[1], pair[1]), spec(yb_s.shape[1], pair[1]),
                     pl.BlockSpec(w_out.shape, lambda i: (0, 0), pipeline_mode=pl.Buffered(1)), vec, vec]
        args += [ya_p, yb_p, ya_s, yb_s, w_out, g1.reshape(1, d), b1.reshape(1, d)]
    in_specs += [whole(w1), whole(w3), whole(w2), vec, vec]
    args += [w1, w3, w2, g.reshape(1, d), b.reshape(1, d)]
    if split_out:
        out_specs = [spec(d, index) for index in pair]
        out_shape = [jax.ShapeDtypeStruct((lay.n_p, d), F32), jax.ShapeDtypeStruct((lay.n_s, d), F32)]
    else:
        out_specs, out_shape = [spec(d, lay.all_rows)], [jax.ShapeDtypeStruct((lay.n, d), F32)]
    n_out, proj = len(out_specs), None
    if in_proj is not None:
        body, io, kw = in_proj
        p_specs, p_args, (p_out_specs, p_out_shape) = io(lay, **kw)
        c_specs, c_shape = _cache_outputs(lay)
        in_specs, args = in_specs + p_specs, args + p_args
        out_specs, out_shape = out_specs + p_out_specs + c_specs, out_shape + p_out_shape + c_shape
        proj = (body, len(p_args), len(p_out_specs))
    if next_weights is not None:
        *stacks, layer, which = next_weights
        slab_index = lambda i: jnp.minimum(i, WEIGHT_CAST_STEPS - 1)
        for a in stacks:
            block = (1, 1, a.shape[2] // WEIGHT_CAST_STEPS, a.shape[3])
            in_specs.append(pl.BlockSpec(block, lambda i: (layer, which, slab_index(i), 0)))
            out_specs.append(pl.BlockSpec(block, lambda i: (0, 0, slab_index(i), 0)))
            out_shape.append(jax.ShapeDtypeStruct((1, 1) + a.shape[2:], BF16))
        args += stacks
    return pl.pallas_call(
        functools.partial(_ffn_kernel, j=j, nf=nf, steps_p=lay.steps_p, n_x=len(xs), n_out=n_out,
                          mixer=mixer is not None, in_proj=proj, cast_next=next_weights is not None),
        grid=(lay.steps,),
        in_specs=in_specs,
        out_specs=out_specs,
        out_shape=out_shape,
        scratch_shapes=[pltpu.VMEM((tm, d), BF16), pltpu.VMEM((tm, d), F32), pltpu.VMEM((tm, d), F32)],
        compiler_params=_cparams(("arbitrary",)),
        name="ffn_sublayer",
    )(*args)


def _rope_tables(length):
    t = np.arange(length)
    nf = HEAD_DIM // 4
    inv = ROPE_THETA ** (-np.arange(nf, dtype=np.float64) / nf)
    ang_r = (t // GRID_W)[:, None] * inv[None]
    ang_c = (t % GRID_W)[:, None] * inv[None]
    cr, sr, cc, sc = np.cos(ang_r), np.sin(ang_r), np.cos(ang_c), np.sin(ang_c)
    z = np.zeros_like(cr)
    cos = np.concatenate([cr, cr, cc, cc], 1)
    sin_up = np.concatenate([-sr, z, -sc, z], 1)
    sin_dn = np.concatenate([z, sr, z, sc], 1)
    two = lambda a: jnp.asarray(np.concatenate([a, a], 1), F32)
    return two(cos), two(sin_up), two(sin_dn)


def _rope128(x, cos, sin_up, sin_dn):
    nf = HEAD_DIM // 4
    return x * cos + pltpu.roll(x, 128 - nf, 1) * sin_up + pltpu.roll(x, nf, 1) * sin_dn


def _rope(x, cos, sin_up, sin_dn):
    parts = [_rope128(x[:, c:c + 128], cos, sin_up, sin_dn) for c in range(0, x.shape[1], 128)]
    return parts[0] if len(parts) == 1 else jnp.concatenate(parts, 1)


EVEN_COLS = (512, 256, 512, 512, 512, 528)


KV_WIDTH = 128 + A_KV * 256


def _store_cache(ck_ref, cv_ref, kv):
    seq_len = ck_ref.shape[2]
    for s in range(ck_ref.shape[0]):
        rows = slice(s * seq_len, (s + 1) * seq_len)
        ck_ref[s] = kv[rows, :128].T
        cv_ref[s] = kv[rows, 128:].T


def _in_proj_outputs(outs, lay):
    specs = [pl.BlockSpec((lay.tm, width), lay.all_rows) for width, _ in outs]
    shapes = [jax.ShapeDtypeStruct((lay.n, width), dtype) for width, dtype in outs]
    return specs, shapes


def _cache_outputs(lay):
    per_step = lay.tm // lay.len_p
    index = lambda i: lay.prompt_rows(i) + (0,)
    specs = [pl.BlockSpec((per_step, 128, lay.len_p), index)] * 2
    shapes = [jax.ShapeDtypeStruct((lay.n_p // lay.len_p, 128, lay.len_p), F32)] * 2
    return specs, shapes


def _cache_layout(c):
    nb, _, sl = c.shape
    return jnp.transpose(c.reshape(nb, 1, A_KV, HEAD_DIM, sl), (0, 1, 4, 2, 3))


def _kv_with_ones(kv):
    ones = jnp.ones((kv.shape[0], 128), kv.dtype)
    v0, v1 = kv[:, 128:192], kv[:, 192:256]
    return jnp.concatenate([kv[:, :128], v0, v0, ones, v1, v1, ones], 1)


def _even_in_body(x, m, ins, outs, rope, cache_refs):
    w_ref, *tabs = ins
    aq_ref, akv_ref, bq_ref, bk_ref, bv_ref, bo_ref, bg_ref = outs
    cache = cache_refs is not None

    h = (x * (1.0 + m[4:5]) + m[3:4]).astype(BF16)
    offs = [0]
    for c in EVEN_COLS:
        offs.append(offs[-1] + c)
    proj = lambda k: _dot(h, w_ref[:, offs[k]:offs[k + 1]])

    aq = proj(0)
    akv = proj(1)
    if cache:
        _store_cache(*cache_refs, akv)
    if rope:
        cos, s_up, s_dn = (t[...] for t in tabs)
        aq = _rope(aq, cos, s_up, s_dn)
        akv = jnp.concatenate([_rope(akv[:, :128], cos, s_up, s_dn), akv[:, 128:]], 1)
    aq_ref[...] = (aq * QK_SCALE).astype(BF16)
    akv_ref[...] = _kv_with_ones(akv).astype(BF16)
    bq_ref[...] = proj(2).astype(BF16)
    bk_ref[...] = (proj(3) * (B_DIM ** -0.5)).astype(BF16)
    bv_ref[...] = proj(4).astype(BF16)
    tail = proj(5)
    n_gates = 4 * B_HEADS
    bo_ref[...] = tail[:, n_gates:]
    bg_ref[...] = tail[:, :128]


def _rope_inputs(lay):
    return [pl.BlockSpec((lay.tm, 128), lay.rope_rows)] * 3, list(_rope_tables(lay.len_s))


def _even_in_io(lay, w):
    specs, tabs = _rope_inputs(lay)
    in_specs = [pl.BlockSpec(w.shape, lambda i: (0, 0), pipeline_mode=pl.Buffered(1))] + specs
    outs = [(512, BF16), (KV_WIDTH, BF16), (512, BF16), (512, BF16), (512, BF16), (512, F32), (128, F32)]
    return in_specs, [w] + tabs, _in_proj_outputs(outs, lay)


ODD_COLS = (512, 512, 512, 512, 512, 512, 256)


def _head_rms(x, seg_ref, g):
    x2 = x * x
    hi = x2.astype(BF16)
    lo = (x2 - hi.astype(F32)).astype(BF16)
    w = seg_ref.shape[0]
    ms = [_dot(hi[:, c:c + w], seg_ref[...]) + _dot(lo[:, c:c + w], seg_ref[...]) for c in range(0, x.shape[1], w)]
    ms = ms[0] if len(ms) == 1 else jnp.concatenate(ms, 1)
    return x * lax.rsqrt(ms + 1e-6) * g


def _odd_in_body(x, m, ins, outs, rope, cache_refs):
    w_ref, segq_ref, segk_ref, qn_ref, kn_ref, *tabs = ins
    q_ref, ff_ref, fb_ref, v_ref, cg_ref, dq_ref, dkv_ref = outs
    cache = cache_refs is not None

    h = (x * (1.0 + m[4:5]) + m[3:4]).astype(BF16)
    offs = [0]
    for c in ODD_COLS:
        offs.append(offs[-1] + c)
    proj = lambda k: _dot(h, w_ref[:, offs[k]:offs[k + 1]])

    q_ref[...] = _silu(proj(0)).astype(BF16)
    ff_ref[...] = proj(1)
    fb_ref[...] = proj(2)
    v_ref[...] = proj(3).astype(BF16)
    cg_ref[...] = proj(4)
    dq = _head_rms(proj(5), segq_ref, qn_ref[...])
    dkv = proj(6)
    dk = _head_rms(dkv[:, :128], segk_ref, kn_ref[...])
    dv = dkv[:, 128:]
    if cache:
        _store_cache(*cache_refs, jnp.concatenate([dk, dv], 1))
    if rope:
        cos, s_up, s_dn = (t[...] for t in tabs)
        dq = _rope(dq, cos, s_up, s_dn)
        dk = _rope(dk, cos, s_up, s_dn)
    dq_ref[...] = (dq * QK_SCALE).astype(BF16)
    dkv_ref[...] = _kv_with_ones(jnp.concatenate([dk, dv], 1)).astype(BF16)


def _segment_mean_matrix(width):
    r = jnp.arange(width) // HEAD_DIM
    return jnp.where(r[:, None] == r[None, :], 1.0 / HEAD_DIM, 0.0).astype(BF16)


def _odd_in_io(lay, w, q_norm, k_norm):
    const = lambda a: pl.BlockSpec(a.shape, lambda i: (0, 0))
    segq, segk = _segment_mean_matrix(256), _segment_mean_matrix(128)
    qn = jnp.tile(q_norm, D_HEADS).reshape(1, 512)
    kn = jnp.tile(k_norm, D_KV).reshape(1, 128)
    specs, tabs = _rope_inputs(lay)
    in_specs = [pl.BlockSpec(w.shape, lambda i: (0, 0), pipeline_mode=pl.Buffered(1)),
                const(segq), const(segk), const(qn), const(kn)] + specs
    outs = [(512, BF16), (512, F32), (512, F32), (512, BF16), (512, F32), (512, BF16), (KV_WIDTH, BF16)]
    return in_specs, [w, segq, segk, qn, kn] + tabs, _in_proj_outputs(outs, lay)


def _attn_kernel(*refs, seq_len, tq, q_blocks, n_ctx, banded, has_sink):
    it = iter(refs)
    q_ref, kv_ref = next(it), next(it)
    ckv_ref = next(it) if n_ctx else None
    sink_ref = next(it) if has_sink else None
    o_ref = next(it)
    groups = A_HEADS // A_KV
    rows = groups * tq
    low_half = lax.broadcasted_iota(jnp.int32, (tq, 2 * HEAD_DIM), 1) < HEAD_DIM

    def run_kv_head(blk, kh):
        r0 = blk * tq
        j = pl.program_id(1) * q_blocks + blk
        kcol = slice(kh * HEAD_DIM, (kh + 1) * HEAD_DIM)
        vcol = slice(128 + kh * 256, 128 + (kh + 1) * 256)
        qs = jnp.concatenate([q_ref[r0:r0 + tq, (kh * groups + g) * HEAD_DIM:(kh * groups + g + 1) * HEAD_DIM]
                              for g in range(groups)], axis=0)
        tiles = []
        if banded:
            span = tq + 2 * WINDOW
            start = pl.multiple_of(jnp.clip(j * tq - WINDOW, 0, seq_len - span), WINDOW)
            qpos = j * tq + (lax.broadcasted_iota(jnp.int32, (rows, span), 0) & (tq - 1))
            kpos = start + lax.broadcasted_iota(jnp.int32, (rows, span), 1)
            band = jnp.abs(kpos - qpos) <= WINDOW
            tiles.append((kv_ref[pl.ds(start, span), kcol], kv_ref[pl.ds(start, span), vcol], band))
        else:
            tk = min(ATT_KEY_TILE, seq_len)
            for t in range(seq_len // tk):
                tiles.append((kv_ref[t * tk:(t + 1) * tk, kcol], kv_ref[t * tk:(t + 1) * tk, vcol], None))
        scores = [_dot_nt(qs, k_t) for k_t, _, _ in tiles]
        if n_ctx:
            tiles.append((None, ckv_ref[0, kh, HEAD_DIM:, :], None))
            scores.append(_dot(qs, ckv_ref[0, kh, :HEAD_DIM, :]))
        yield
        scores = [s if mask is None else jnp.where(mask, s, NEG_INF) for s, (_, _, mask) in zip(scores, tiles)]
        m = functools.reduce(jnp.maximum, [jnp.max(s, -1, keepdims=True) for s in scores])
        if has_sink:
            sink = jnp.concatenate([jnp.full((tq, 1), sink_ref[kh * groups + g] * LOG2_E, F32)
                                    for g in range(groups)], 0)
            m = jnp.maximum(m, sink)
        pv = functools.reduce(lambda a, b: a + b,
                              [(_dot_nt if k_t is None else _dot)(jnp.exp2(s - m).astype(BF16), v_t)
                               for s, (k_t, v_t, _) in zip(scores, tiles)])
        yield
        total = pv[:, 2 * HEAD_DIM:]
        if has_sink:
            total = total + jnp.exp2(sink - m)
        o = pv[:, :2 * HEAD_DIM] / total
        for g in range(0, groups, 2):
            pair = jnp.where(low_half, o[g * tq:(g + 1) * tq], o[(g + 1) * tq:(g + 2) * tq])
            c0 = (kh * groups + g) * HEAD_DIM
            o_ref[r0:r0 + tq, c0:c0 + 2 * HEAD_DIM] = pair.astype(BF16)

    run = _run_staggered if seq_len + n_ctx > 2 * ATT_KEY_TILE else _run_interleaved
    run([run_kv_head(blk, kh) for blk in range(q_blocks) for kh in range(A_KV)])


def _attention(q, kv, ctx_kv, sink, batch, seq_len, row0, tq, q_blocks, banded):
    n_ctx = 0 if ctx_kv is None else ctx_kv.shape[-1]
    per_seq = seq_len // (tq * q_blocks)
    q0, kv0 = row0 // (tq * q_blocks), row0 // seq_len
    in_specs = [pl.BlockSpec((q_blocks * tq, 512), lambda b, j: (q0 + b * per_seq + j, 0)),
                pl.BlockSpec((seq_len, KV_WIDTH), lambda b, j: (kv0 + b, 0))]
    args = [q, kv]
    if n_ctx:
        in_specs.append(pl.BlockSpec((1,) + ctx_kv.shape[1:], lambda b, j: (b, 0, 0, 0)))
        args.append(ctx_kv)
    if sink is not None:
        in_specs.append(pl.BlockSpec(memory_space=pltpu.SMEM))
        args.append(sink.reshape(-1).astype(F32))
    return pl.pallas_call(
        functools.partial(_attn_kernel, seq_len=seq_len, tq=tq, q_blocks=q_blocks, n_ctx=n_ctx, banded=banded,
                          has_sink=sink is not None),
        grid=(batch, per_seq),
        in_specs=in_specs,
        out_specs=pl.BlockSpec((q_blocks * tq, 512), lambda b, j: (b * per_seq + j, 0)),
        out_shape=jax.ShapeDtypeStruct((batch * seq_len, 512), BF16),
        compiler_params=_cparams(("parallel", "arbitrary")),
        name="gqa_attention",
    )(*args)


def _log_sigmoid(x):
    return jnp.minimum(x, 0.0) - jnp.log1p(jnp.exp(-jnp.abs(x)))


def _mlstm_kernel(*refs, seq_len, has_init, emit_state):
    it = iter(refs)
    q_ref, k_ref, v_ref, g_ref, bo_ref, gb_ref, ng_ref = (next(it) for _ in range(7))
    c0_ref, n0_ref, m0_ref = (next(it), next(it), next(it)) if has_init else (None, None, None)
    y_ref = next(it)
    co_ref, no_ref, mo_ref = (next(it), next(it), next(it)) if emit_state else (None, None, None)
    c_scr, n_scr, m_scr, hf_scr, hb_scr, ig_buf, bc_buf, ut_buf = (next(it) for _ in range(8))

    t_len = MLSTM_T
    n_chunks = seq_len // t_len
    n_streams = 2 * B_HEADS

    for s in range(n_streams):
        if has_init:
            c_scr[s] = c0_ref[0, s // B_HEADS, s % B_HEADS]
        else:
            c_scr[s] = jnp.zeros((B_DIM, B_DIM), F32)
    if has_init:
        n_scr[...] = n0_ref[0]
        m_scr[...] = m0_ref[0]
    else:
        n_scr[...] = jnp.zeros_like(n_scr)
        m_scr[...] = jnp.zeros_like(m_scr)

    ri = lax.broadcasted_iota(jnp.int32, (t_len, t_len), 0)
    ci = lax.broadcasted_iota(jnp.int32, (t_len, t_len), 1)
    lower = ci <= ri
    upper = ci >= ri
    tri_f = jnp.where(lower, 1.0, 0.0).astype(BF16)
    tri_b = jnp.where(upper, 1.0, 0.0).astype(BF16)
    gate_bias = gb_ref[...]
    sel_row = lax.broadcasted_iota(jnp.int32, (128, B_HEADS * B_DIM), 0)
    sel_head = lax.broadcasted_iota(jnp.int32, (128, B_HEADS * B_DIM), 1) // B_DIM
    gate_select = []
    for direction in range(2):
        i_base = 2 * direction * B_HEADS
        gate_select.append((jnp.where(sel_row == i_base + sel_head, 1.0, 0.0).astype(BF16),
                            jnp.where(sel_row == i_base + B_HEADS + sel_head, 1.0, 0.0).astype(BF16)))

    def chunk_rows(c, direction):
        chunk = c if direction == 0 else n_chunks - 1 - c
        return pl.ds(pl.multiple_of(chunk * t_len, t_len), t_len)

    def prepare_gates(c, direction):
        slot = c % 2
        rows = chunk_rows(jnp.minimum(c, n_chunks - 1), direction)
        tri = tri_f if direction == 0 else tri_b
        sel_i, sel_f = gate_select[direction]
        gates = g_ref[rows, :] + gate_bias
        hi, mid, lo = _split3(_log_sigmoid(gates))
        b3 = _dot(tri, jnp.concatenate([hi, mid, lo], 1))
        ig_buf[slot, direction] = _dot_sel_rhs(gates, sel_i)
        yield
        bc_all = b3[:, :128] + b3[:, 128:256] + b3[:, 256:]
        bc_rep = _dot_sel_rhs(bc_all, sel_f)
        ut_buf[slot, direction] = (gates - pltpu.roll(bc_all, 128 - B_HEADS, 1)).T
        yield
        bc_buf[slot, direction] = bc_rep

    def run_direction(c, direction):
        rows = chunk_rows(c, direction)
        slot = c % 2
        causal = lower if direction == 0 else upper
        last = t_len - 1 if direction == 0 else 0
        h_out = hf_scr if direction == 0 else hb_scr
        ig_rep, bc_rep, u_t = ig_buf[slot, direction], bc_buf[slot, direction], ut_buf[slot, direction]
        heads = []
        for hd in range(B_HEADS):
            s = direction * B_HEADS + hd
            col = slice(hd * B_DIM, (hd + 1) * B_DIM)
            qc, kc, vc = q_ref[rows, col], k_ref[rows, col], v_ref[rows, col]
            c_prev = c_scr[s]
            n_prev = n_scr[s:s + 1, :]
            qk = _dot_nt(qc, kc)
            q_state = _dot(qc, c_prev.astype(BF16))
            qn = _dot_nt(qc, jnp.broadcast_to(n_prev, (B_DIM, B_DIM)).astype(BF16))
            heads.append((s, col, qc, kc, vc, c_prev, n_prev, qk, q_state, qn))
        yield
        staged = []
        for (s, col, qc, kc, vc, c_prev, n_prev, qk, q_state, qn) in heads:
            i_col = 2 * direction * B_HEADS + (s % B_HEADS)
            ig, bc = ig_rep[:, col], bc_rep[:, col]
            dmat = jnp.where(causal, bc[:, :t_len] + u_t[i_col:i_col + 1, :], -jnp.inf)
            m_prev = m_scr[s:s + 1, :]
            inter = bc + m_prev
            m_t = jnp.maximum(inter, jnp.max(dmat, -1, keepdims=True))
            w = (jnp.exp(dmat - m_t[:, :t_len]) * qk).astype(BF16)
            wv = _dot(w, jnp.concatenate([vc, jnp.ones_like(vc)], 1))
            b_tot = bc[last:last + 1, :]
            g = b_tot - bc + ig
            m_new = jnp.maximum(b_tot + m_prev, jnp.max(g, 0, keepdims=True))
            kw = kc.astype(F32) * jnp.exp(g - m_new)
            upd = _dot_tn(kw.astype(BF16), vc)
            decay = jnp.exp(b_tot + m_prev - m_new)
            staged.append((s, col, inter, m_t, c_prev, n_prev, q_state, qn, wv, m_new, kw, upd, decay))
        yield
        for (s, col, inter, m_t, c_prev, n_prev, q_state, qn, wv, m_new, kw, upd, decay) in staged:
            a = jnp.exp(inter - m_t)
            num = a * q_state + wv[:, :B_DIM]
            den = a * qn + wv[:, B_DIM:]
            h_out[rows, col] = num / jnp.maximum(jnp.abs(den), jnp.exp(-m_t))
            c_scr[s] = decay * c_prev + upd
            n_scr[s:s + 1, :] = decay * n_prev + jnp.sum(kw, 0, keepdims=True)
            m_scr[s:s + 1, :] = m_new

    def chained(c, direction):
        yield from prepare_gates(c, direction)
        yield
        yield from run_direction(c, direction)

    def step(c, carry):
        _run_interleaved([chained(c, 0), chained(c, 1)])
        return carry

    lax.fori_loop(0, n_chunks, step, 0, unroll=4)

    blk = min(256, seq_len)
    ng = ng_ref[...]

    def finish(i, carry):
        rows = pl.ds(pl.multiple_of(i * blk, blk), blk)
        hsum = hf_scr[rows, :] + hb_scr[rows, :]
        gate = jax.nn.sigmoid(bo_ref[rows, :])
        parts = [_rms_lastdim(hsum[:, hd * B_DIM:(hd + 1) * B_DIM], ng[:, hd * B_DIM:(hd + 1) * B_DIM])
                 for hd in range(B_HEADS)]
        y_ref[rows, :] = (gate * jnp.concatenate(parts, 1)).astype(BF16)
        return carry

    lax.fori_loop(0, seq_len // blk, finish, 0)

    if emit_state:
        for s in range(n_streams):
            co_ref[0, s // B_HEADS, s % B_HEADS] = c_scr[s]
        no_ref[0] = n_scr[...]
        mo_ref[0] = m_scr[...]


def _mlstm(q, k, v, gates, bo, gate_bias, norm_g, init, batch, seq_len, row0, emit_state):
    n = batch * seq_len
    seq0 = row0 // seq_len
    in_row = lambda width: pl.BlockSpec((seq_len, width), lambda b: (seq0 + b, 0))
    row = lambda width: pl.BlockSpec((seq_len, width), lambda b: (b, 0))
    const = lambda a: pl.BlockSpec(a.shape, lambda b: (0,) * a.ndim)
    gb = jnp.zeros((1, 128), F32).at[0, :4 * B_HEADS].set(gate_bias.reshape(-1))
    ng = norm_g.reshape(1, B_HEADS * B_DIM)
    in_specs = [in_row(512), in_row(512), in_row(512), in_row(128), in_row(512), const(gb), const(ng)]
    args = [q, k, v, gates, bo, gb, ng]
    n_streams = 2 * B_HEADS
    if init is not None:
        c0, n0, m0 = init
        in_specs += [pl.BlockSpec((1, 2, B_HEADS, B_DIM, B_DIM), lambda b: (b, 0, 0, 0, 0)),
                     pl.BlockSpec((1, n_streams, B_DIM), lambda b: (b, 0, 0)),
                     pl.BlockSpec((1, n_streams, B_DIM), lambda b: (b, 0, 0))]
        args += [c0, n0.reshape(batch, n_streams, B_DIM),
                 jnp.broadcast_to(m0.reshape(batch, n_streams, 1), (batch, n_streams, B_DIM))]
    out_specs = [row(512)]
    out_shape = [jax.ShapeDtypeStruct((n, 512), BF16)]
    if emit_state:
        out_specs += [pl.BlockSpec((1, 2, B_HEADS, B_DIM, B_DIM), lambda b: (b, 0, 0, 0, 0)),
                      pl.BlockSpec((1, n_streams, B_DIM), lambda b: (b, 0, 0)),
                      pl.BlockSpec((1, n_streams, B_DIM), lambda b: (b, 0, 0))]
        out_shape += [jax.ShapeDtypeStruct((batch, 2, B_HEADS, B_DIM, B_DIM), F32),
                      jax.ShapeDtypeStruct((batch, n_streams, B_DIM), F32),
                      jax.ShapeDtypeStruct((batch, n_streams, B_DIM), F32)]
    return pl.pallas_call(
        functools.partial(_mlstm_kernel, seq_len=seq_len, has_init=init is not None, emit_state=emit_state),
        grid=(batch,),
        in_specs=in_specs,
        out_specs=out_specs,
        out_shape=out_shape,
        scratch_shapes=[pltpu.VMEM((n_streams, B_DIM, B_DIM), F32),
                        pltpu.VMEM((n_streams, B_DIM), F32),
                        pltpu.VMEM((n_streams, B_DIM), F32),
                        pltpu.VMEM((seq_len, 512), F32),
                        pltpu.VMEM((seq_len, 512), F32),
                        pltpu.VMEM((2, 2, MLSTM_T, B_HEADS * B_DIM), F32),
                        pltpu.VMEM((2, 2, MLSTM_T, B_HEADS * B_DIM), F32),
                        pltpu.VMEM((2, 2, 128, MLSTM_T), F32)],
        compiler_params=_cparams(("parallel",)),
        name="mlstm_scan",
    )(*args)


def _hgrn_kernel(*refs, seq_len, layer, has_init, emit_state):
    it = iter(refs)
    q_ref, ff_ref, fb_ref, v_ref, cg_ref, lbl_ref, ng_ref = (next(it) for _ in range(7))
    s0_ref = next(it) if has_init else None
    y_ref = next(it)
    so_ref = next(it) if emit_state else None
    st_scr, of_scr, ob_scr = next(it), next(it), next(it)

    t_len = HGRN_T
    n_sub = t_len // SUB
    n_chunks = seq_len // t_len
    n_streams = 2 * C_HEADS

    logits = lbl_ref[...]
    e = jnp.exp(logits - jnp.max(logits, 0, keepdims=True))
    sm = e / jnp.sum(e, 0, keepdims=True)
    lb = jnp.sum(sm[0:layer + 1], 0, keepdims=True) - sm[0:1]

    for s in range(n_streams):
        if has_init:
            st_scr[s] = s0_ref[0, s // C_HEADS, s % C_HEADS].T
        else:
            st_scr[s] = jnp.zeros((C_DV, C_DK), F32)

    ri = lax.broadcasted_iota(jnp.int32, (t_len, t_len), 0)
    ci = lax.broadcasted_iota(jnp.int32, (t_len, t_len), 1)
    tri_f = jnp.where(ci <= ri, 1.0, 0.0).astype(BF16)
    tri_b = jnp.where(ci >= ri, 1.0, 0.0).astype(BF16)
    sub_row = lax.broadcasted_iota(jnp.int32, (SUB, C_DK), 0)
    ones_dk = jnp.ones((C_DK, C_DK), BF16)

    def run_stream(c, direction, hd):
        chunk = c if direction == 0 else n_chunks - 1 - c
        rows = pl.ds(pl.multiple_of(chunk * t_len, t_len), t_len)
        f_ref = ff_ref if direction == 0 else fb_ref
        tri = tri_f if direction == 0 else tri_b
        last = t_len - 1 if direction == 0 else 0
        o_out = of_scr if direction == 0 else ob_scr
        s = direction * C_HEADS + hd
        col = slice(hd * C_DK, (hd + 1) * C_DK)
        lbh = lb[:, col]
        f = lbh + (1.0 - lbh) * jax.nn.sigmoid(f_ref[rows, col])
        log_k = jnp.log2(1.0 - f)
        hi, mid, lo = _split3(jnp.log2(f))
        a3 = _dot(tri, jnp.concatenate([hi, mid, lo], 1))
        yield
        a_cum = a3[:, :C_DK] + a3[:, C_DK:2 * C_DK] + a3[:, 2 * C_DK:]
        a_key = a_cum - log_k
        a_tot = a_cum[last:last + 1, :]
        qf = q_ref[rows, col].astype(F32)
        vc = v_ref[rows, col]
        vf = vc.astype(F32)
        st = st_scr[s]
        inter = _dot_nt((qf * jnp.exp2(a_cum)).astype(BF16), st.astype(BF16))
        upd = _dot_tn(vc, jnp.exp2(a_tot - a_key).astype(BF16))
        ps = []
        for blk in range(n_sub):
            b0 = blk * SUB
            a_i, q_i, ak_i = (t[b0:b0 + SUB] for t in (a_cum, qf, a_key))
            for j in range(SUB):
                seen = (sub_row >= j) if direction == 0 else (sub_row <= j)
                ps.append(jnp.where(seen, jnp.exp2(a_i - ak_i[j:j + 1]), 0.0) * q_i)
        att = _dot(jnp.concatenate(ps, 0).astype(BF16), ones_dk)
        off = []
        for blk in range(n_sub):
            b0 = blk * SUB
            if direction == 0 and blk > 0:
                a_ref, kr = a_cum[b0 - 1:b0], slice(0, b0)
            elif direction == 1 and blk < n_sub - 1:
                a_ref, kr = a_cum[b0 + SUB:b0 + SUB + 1], slice(b0 + SUB, t_len)
            else:
                off.append(None)
                continue
            qt = (qf[b0:b0 + SUB] * jnp.exp2(a_cum[b0:b0 + SUB] - a_ref)).astype(BF16)
            kt = jnp.exp2(a_ref - a_key[kr]).astype(BF16)
            off.append((_dot_nt(qt, kt), kr))
        yield
        st_scr[s] = jnp.exp2(a_tot) * st + upd
        outs = []
        for blk in range(n_sub):
            b0 = blk * SUB
            o_i = inter[b0:b0 + SUB]
            for j in range(SUB):
                r = (blk * SUB + j) * SUB
                o_i = o_i + att[r:r + SUB] * vf[b0 + j:b0 + j + 1]
            if off[blk] is not None:
                att_off, kr = off[blk]
                outs.append((o_i, _dot(att_off.astype(BF16), vc[kr])))
            else:
                outs.append((o_i, None))
        yield
        o_out[rows, col] = jnp.concatenate([o_i if o_off is None else o_i + o_off for o_i, o_off in outs], 0)

    def step(c, carry):
        _run_interleaved([run_stream(c, direction, hd) for direction in range(2) for hd in range(C_HEADS)])
        return carry

    lax.fori_loop(0, n_chunks, step, 0, unroll=4)

    blk_rows = min(256, seq_len)
    ng = ng_ref[...]

    def finish(i, carry):
        rows = pl.ds(pl.multiple_of(i * blk_rows, blk_rows), blk_rows)
        osum = of_scr[rows, :] + ob_scr[rows, :]
        parts = [_rms_lastdim(osum[:, hd * C_DV:(hd + 1) * C_DV], ng[:, hd * C_DV:(hd + 1) * C_DV])
                 for hd in range(C_HEADS)]
        y_ref[rows, :] = (jnp.concatenate(parts, 1) * _silu(cg_ref[rows, :])).astype(BF16)
        return carry

    lax.fori_loop(0, seq_len // blk_rows, finish, 0)

    if emit_state:
        for s in range(n_streams):
            so_ref[0, s // C_HEADS, s % C_HEADS] = st_scr[s].T


def _hgrn(q, ff, fb, v, cg, lb_logits, norm_g, layer, init, batch, seq_len, row0, emit_state):
    n = batch * seq_len
    seq0 = row0 // seq_len
    in_row = lambda width: pl.BlockSpec((seq_len, width), lambda b: (seq0 + b, 0))
    row = lambda width: pl.BlockSpec((seq_len, width), lambda b: (b, 0))
    const = lambda a: pl.BlockSpec(a.shape, lambda b: (0,) * a.ndim)
    ng = norm_g.reshape(1, C_HEADS * C_DV)
    state_spec = pl.BlockSpec((1, 2, C_HEADS, C_DK, C_DV), lambda b: (b, 0, 0, 0, 0))
    in_specs = [in_row(512)] * 5 + [const(lb_logits), const(ng)]
    args = [q, ff, fb, v, cg, lb_logits, ng]
    if init is not None:
        in_specs.append(state_spec)
        args.append(init)
    out_specs = [row(512)]
    out_shape = [jax.ShapeDtypeStruct((n, 512), BF16)]
    if emit_state:
        out_specs.append(state_spec)
        out_shape.append(jax.ShapeDtypeStruct((batch, 2, C_HEADS, C_DK, C_DV), F32))
    return pl.pallas_call(
        functools.partial(_hgrn_kernel, seq_len=seq_len, layer=layer, has_init=init is not None,
                          emit_state=emit_state),
        grid=(batch,),
        in_specs=in_specs,
        out_specs=out_specs,
        out_shape=out_shape,
        scratch_shapes=[pltpu.VMEM((2 * C_HEADS, C_DV, C_DK), F32),
                        pltpu.VMEM((seq_len, 512), F32),
                        pltpu.VMEM((seq_len, 512), F32)],
        compiler_params=_cparams(("parallel",)),
        name="hgrn2_scan",
    )(*args)


def _prep_ctx_kv(k_ctx, v_ctx):
    k_t = jnp.transpose(k_ctx, (0, 2, 3, 1)).astype(BF16)
    v_t = jnp.transpose(v_ctx, (0, 2, 3, 1)).astype(BF16)
    ones = jnp.ones(k_t.shape[:2] + (128, k_t.shape[3]), BF16)
    return jnp.concatenate([k_t, v_t, v_t, ones], 2)


def kernel(x_prompt, x_sample, c, cache_a_k, cache_a_v, state_b_C, state_b_n, state_b_m, state_c_S, cache_d_k, cache_d_v, c_ctx, ada_w, ada_b, ln_g, ln_b, ffn_w1, ffn_w3, ffn_w2, w_in_even, w_out_even, a_sink, b_gate_bias, b_norm_g, w_in_odd, w_out_odd, c_lb_logits, c_norm_g, d_q_norm, d_k_norm):
    batch_p, len_p, d = x_prompt.shape
    batch_s, len_s, _ = x_sample.shape

    cvec = jnp.concatenate([c_ctx[None], c, jnp.zeros((MOD_ROWS - 1 - batch_s, d), F32)], 0)
    mod_all = _modulation(cvec, ada_w, ada_b)

    lay = _Layout(batch_p, len_p, batch_s, len_s)
    new = {}
    ffn_stacks = (ffn_w1, ffn_w3, ffn_w2)
    ffn_w = tuple(a[0:1, 0:1].astype(BF16) for a in ffn_stacks)
    x = [x_prompt.reshape(lay.n_p, d), x_sample.reshape(lay.n_s, d)]

    for l in range(DEPTH):
        mod = mod_all[l].reshape(MOD_ROWS, 9, d)
        i = l // 2
        if l % 2 == 0:
            in_proj = (_even_in_body, _even_in_io, dict(w=w_in_even[i].astype(BF16)))
            w_out = w_out_even[i].astype(BF16)
        else:
            in_proj = (_odd_in_body, _odd_in_io,
                       dict(w=w_in_odd[i].astype(BF16), q_norm=d_q_norm[i], k_norm=d_k_norm[i]))
            w_out = w_out_odd[i].astype(BF16)

        x_all, *proj, k_cache, v_cache, w1, w3, w2 = _ffn(x, mod, 0, ln_g[l, 0], ln_b[l, 0], ffn_w, lay,
                                                         in_proj=in_proj, next_weights=ffn_stacks + (l, 1))
        ffn_w = (w1, w3, w2)
        prompt = dict(batch=batch_p, seq_len=len_p, row0=0)
        sample = dict(batch=batch_s, seq_len=len_s, row0=lay.n_p)
        if l % 2 == 0:
            aq, akv, bq, bk, bv, bo, bg = proj
            new["a_k"], new["a_v"] = _cache_layout(k_cache), _cache_layout(v_cache)
            scan = (bq, bk, bv, bg, bo, b_gate_bias[i], b_norm_g[i])
            ya_p = _attention(aq, akv, None, a_sink[i], **prompt, tq=Q_BLOCK, q_blocks=len_p // Q_BLOCK, banded=False)
            yb_p, c_new, n_new, m_new = _mlstm(*scan, None, **prompt, emit_state=True)
            new["b_C"] = c_new[:, None]
            new["b_n"] = n_new.reshape(batch_p, 1, 2, B_HEADS, B_DIM)
            new["b_m"] = m_new[:, :, 0].reshape(batch_p, 1, 2, B_HEADS)
            ya_s = _attention(aq, akv, _prep_ctx_kv(cache_a_k[:, i], cache_a_v[:, i]), a_sink[i], **sample,
                              tq=Q_BLOCK, q_blocks=4, banded=True)
            yb_s, = _mlstm(*scan, (state_b_C[:, i], state_b_n[:, i], state_b_m[:, i]), **sample, emit_state=False)
        else:
            cq, ff, fb, cv, cg, dq, dkv = proj
            new["d_k"], new["d_v"] = _cache_layout(k_cache), _cache_layout(v_cache)
            scan = (cq, ff, fb, cv, cg, c_lb_logits, c_norm_g[i], l)
            ya_p, s_new = _hgrn(*scan, None, **prompt, emit_state=True)
            new["c_S"] = s_new[:, None]
            yb_p = _attention(dq, dkv, None, None, **prompt, tq=Q_BLOCK, q_blocks=len_p // Q_BLOCK, banded=False)
            ya_s, = _hgrn(*scan, state_c_S[:, i], **sample, emit_state=False)
            yb_s = _attention(dq, dkv, _prep_ctx_kv(cache_d_k[:, i], cache_d_v[:, i]), None, **sample,
                              tq=Q_BLOCK, q_blocks=2, banded=False)
        last = l == DEPTH - 1
        x = _ffn([x_all], mod, 2, ln_g[l, 2], ln_b[l, 2], ffn_w, lay,
                 mixer=(ya_p, yb_p, ya_s, yb_s, w_out, ln_g[l, 1], ln_b[l, 1]), split_out=last,
                 next_weights=None if last else ffn_stacks + (l + 1, 0))
        if not last:
            *x, w1, w3, w2 = x
            ffn_w = (w1, w3, w2)

    y_prompt, y_sample = x
    return (y_prompt.reshape(batch_p, len_p, d), y_sample.reshape(batch_s, len_s, d), new["a_k"], new["a_v"],
            new["b_C"], new["b_n"], new["b_m"], new["c_S"], new["d_k"], new["d_v"])
```

```python
import functools

import jax
import jax.numpy as jnp
import numpy as np
from jax import lax
from jax.experimental import pallas as pl
from jax.experimental.pallas import tpu as pltpu

F32 = jnp.float32
BF16 = jnp.bfloat16

DEPTH = 2
GRID_W = 64
HEAD_DIM = 64
A_HEADS = 8
A_KV = 2
WINDOW = 128
B_HEADS = 4
B_DIM = 128
C_HEADS = 4
C_DK = 128
C_DV = 128
D_HEADS = 8
D_KV = 2
Q_BLOCK = 128
ROPE_THETA = 10000.0
ALPHA = (2 * DEPTH) ** 0.25
NEG_INF = -1e30
LOG2_E = 1.4426950408889634
QK_SCALE = HEAD_DIM ** -0.5 * LOG2_E

MOD_ROWS = 8
FF_CHUNK = 256
FFN_TILE = 512
WEIGHT_CAST_STEPS = 16
MLSTM_T = 64
HGRN_T = 64
SUB = 8
SCAN_UNROLL = 8
MOD_TILE = 2304
ATT_KEY_TILE = 512
VMEM_LIMIT = 58 * 1024 * 1024


def _cparams(sem):
    return pltpu.CompilerParams(dimension_semantics=sem, vmem_limit_bytes=VMEM_LIMIT)


def _dot(a, b):
    return jnp.dot(a, b, preferred_element_type=F32)


def _dot_nt(a, b):
    return lax.dot_general(a, b, (((1,), (1,)), ((), ())), preferred_element_type=F32)


def _dot_tn(a, b):
    return lax.dot_general(a, b, (((0,), (0,)), ((), ())), preferred_element_type=F32)


def _split3(x):
    hi = x.astype(BF16)
    r1 = x - hi.astype(F32)
    mid = r1.astype(BF16)
    lo = (r1 - mid.astype(F32)).astype(BF16)
    return hi, mid, lo


def _dot_sel_rhs(x, sel):
    hi = x.astype(BF16)
    lo = (x - hi.astype(F32)).astype(BF16)
    r = _dot(jnp.concatenate([hi, lo], 0), sel)
    return r[:x.shape[0]] + r[x.shape[0]:]


def _run_interleaved(gens):
    live = list(gens)
    while live:
        nxt = []
        for g in live:
            try:
                next(g)
                nxt.append(g)
            except StopIteration:
                pass
        live = nxt


def _run_staggered(gens):
    pending, live = list(gens), []
    while pending or live:
        if pending:
            live.insert(0, pending.pop(0))
        nxt = []
        for g in live:
            try:
                next(g)
                nxt.append(g)
            except StopIteration:
                pass
        live = nxt


def _silu(x):
    return x * jax.nn.sigmoid(x)


def _layernorm(z, g, b):
    mu = jnp.mean(z, -1, keepdims=True)
    zc = z - mu
    var = jnp.mean(zc * zc, -1, keepdims=True)
    return zc * lax.rsqrt(var + 1e-5) * g + b


def _rms_lastdim(x, g):
    return x * lax.rsqrt(jnp.mean(x * x, -1, keepdims=True) + 1e-6) * g


class _Layout:
    def __init__(self, batch_p, len_p, batch_s, len_s):
        self.tm = FFN_TILE
        assert self.tm % len_p == 0 and len_s % self.tm == 0 and (batch_p * len_p) % self.tm == 0
        self.len_p, self.len_s = len_p, len_s
        self.n_p, self.n_s = batch_p * len_p, batch_s * len_s
        self.n = self.n_p + self.n_s
        self.steps_p = self.n_p // self.tm
        self.steps = self.n // self.tm

    def all_rows(self, i):
        return (i, 0)

    def prompt_rows(self, i):
        return (jnp.minimum(i, self.steps_p - 1), 0)

    def sample_rows(self, i):
        return (jnp.maximum(i - self.steps_p, 0), 0)

    def mod_row(self, i):
        req = jnp.maximum(i - self.steps_p, 0) // (self.len_s // self.tm)
        return (jnp.where(i < self.steps_p, 0, 1 + req), 0, 0)

    def rope_rows(self, i):
        return (jnp.maximum(i - self.steps_p, 0) % (self.len_s // self.tm), 0)


def _mod_kernel(c_ref, w_ref, b_ref, o_ref):
    s = _silu(c_ref[...]).astype(BF16)
    o_ref[0] = _dot(s, w_ref[0].astype(BF16)) + b_ref[0]


def _modulation(cvec, ada_w, ada_b):
    depth, d, n = ada_w.shape
    tn = MOD_TILE
    return pl.pallas_call(
        _mod_kernel,
        grid=(depth, n // tn),
        in_specs=[pl.BlockSpec((MOD_ROWS, d), lambda l, j: (0, 0)),
                  pl.BlockSpec((1, d, tn), lambda l, j: (l, 0, j)),
                  pl.BlockSpec((1, 1, tn), lambda l, j: (l, 0, j))],
        out_specs=pl.BlockSpec((1, MOD_ROWS, tn), lambda l, j: (l, 0, j)),
        out_shape=jax.ShapeDtypeStruct((depth, MOD_ROWS, n), F32),
        compiler_params=_cparams(("parallel", "parallel")),
        name="modulation",
    )(cvec, ada_w, ada_b.reshape(depth, 1, n))


def _ffn_kernel(*refs, j, nf, steps_p, n_x, n_out, mixer, in_proj, cast_next):
    it = iter(refs)
    x_refs = [next(it) for _ in range(n_x)]
    mod_ref = next(it)
    if mixer:
        ya_p, yb_p, ya_s, yb_s, wo_ref, g1_ref, b1_ref = (next(it) for _ in range(7))
    w1_ref, w3_ref, w2_ref, g_ref, b_ref = (next(it) for _ in range(5))
    proj_ins = [next(it) for _ in range(in_proj[1])] if in_proj else []
    cast_ins = [next(it) for _ in range(3)] if cast_next else []
    o_refs = [next(it) for _ in range(n_out)]
    proj_outs = [next(it) for _ in range(in_proj[2])] if in_proj else []
    cache_refs = (next(it), next(it)) if in_proj else None
    cast_outs = [next(it) for _ in range(3)] if cast_next else []
    h_ref, acc_ref, res_ref = next(it), next(it), next(it)

    if cast_next:
        @pl.when(pl.program_id(0) < WEIGHT_CAST_STEPS)
        def _():
            for src, dst in zip(cast_ins, cast_outs):
                dst[...] = src[...].astype(BF16)

    is_prompt = pl.program_id(0) < steps_p
    pick = lambda p_ref, s_ref: jnp.where(is_prompt, p_ref[...], s_ref[...])
    m = mod_ref[0]
    shift, scale, gate = m[3 * j:3 * j + 1], m[3 * j + 1:3 * j + 2], m[3 * j + 2:3 * j + 3]
    x = x_refs[0][...] if n_x == 1 else pick(*x_refs)
    if mixer:
        ya, yb = pick(ya_p, ya_s), pick(yb_p, yb_s)
        half = ya.shape[1]
        y = _dot(ya, wo_ref[:half, :]) + _dot(yb, wo_ref[half:, :])
        x = _layernorm(ALPHA * x + m[5:6] * y, g1_ref[...], b1_ref[...])
    res_ref[...] = x
    h_ref[...] = (x * (1.0 + scale) + shift).astype(BF16)
    for f in range(nf):
        cols = slice(f * FF_CHUNK, (f + 1) * FF_CHUNK)
        h = h_ref[...]
        u = (_silu(_dot(h, w1_ref[0, 0, :, cols])) * _dot(h, w3_ref[0, 0, :, cols])).astype(BF16)
        y = _dot(u, w2_ref[0, 0, cols, :])
        if f == 0:
            acc_ref[...] = y
        else:
            acc_ref[...] += y
    out = _layernorm(ALPHA * res_ref[...] + 0.5 * gate * acc_ref[...], g_ref[...], b_ref[...])
    if n_out == 1:
        o_refs[0][...] = out
    else:
        res_ref[...] = out

        @pl.when(is_prompt)
        def _():
            o_refs[0][...] = res_ref[...]

        @pl.when(jnp.logical_not(is_prompt))
        def _():
            o_refs[1][...] = res_ref[...]

    if in_proj:
        @pl.when(is_prompt)
        def _():
            in_proj[0](o_refs[0][...], m, proj_ins, proj_outs, rope=False, cache_refs=cache_refs)

        @pl.when(jnp.logical_not(is_prompt))
        def _():
            in_proj[0](o_refs[0][...], m, proj_ins, proj_outs, rope=True, cache_refs=None)


def _ffn(xs, mod, j, g, b, weights, lay, mixer=None, in_proj=None, split_out=False, next_weights=None):
    w1, w3, w2 = weights
    d = xs[0].shape[1]
    nf = w1.shape[-1] // FF_CHUNK
    tm = lay.tm
    spec = lambda width, index: pl.BlockSpec((tm, width), index)
    vec = pl.BlockSpec((1, d), lambda i: (0, 0))
    whole = lambda a: pl.BlockSpec(a.shape, lambda i: (0, 0, 0, 0), pipeline_mode=pl.Buffered(1))
    pair = (lay.prompt_rows, lay.sample_rows)
    in_specs = [spec(d, lay.all_rows)] if len(xs) == 1 else [spec(d, index) for index in pair]
    in_specs.append(pl.BlockSpec((1, 9, d), lay.mod_row))
    args = list(xs) + [mod]
    if mixer is not None:
        ya_p, yb_p, ya_s, yb_s, w_out, g1, b1 = mixer
        in_specs += [spec(ya_p.shape[1], pair[0]), spec(yb_p.shape[1], pair[0]),
                     spec(ya_s.shape[1], pair[1]), spec(yb_s.shape[1], pair[1]),
                     pl.BlockSpec(w_out.shape, lambda i: (0, 0), pipeline_mode=pl.Buffered(1)), vec, vec]
        args += [ya_p, yb_p, ya_s, yb_s, w_out, g1.reshape(1, d), b1.reshape(1, d)]
    in_specs += [whole(w1), whole(w3), whole(w2), vec, vec]
    args += [w1, w3, w2, g.reshape(1, d), b.reshape(1, d)]
    if split_out:
        out_specs = [spec(d, index) for index in pair]
        out_shape = [jax.ShapeDtypeStruct((lay.n_p, d), F32), jax.ShapeDtypeStruct((lay.n_s, d), F32)]
    else:
        out_specs, out_shape = [spec(d, lay.all_rows)], [jax.ShapeDtypeStruct((lay.n, d), F32)]
    n_out, proj = len(out_specs), None
    if in_proj is not None:
        body, io, kw = in_proj
        p_specs, p_args, (p_out_specs, p_out_shape) = io(lay, **kw)
        c_specs, c_shape = _cache_outputs(lay)
        in_specs, args = in_specs + p_specs, args + p_args
        out_specs, out_shape = out_specs + p_out_specs + c_specs, out_shape + p_out_shape + c_shape
        proj = (body, len(p_args), len(p_out_specs))
    if next_weights is not None:
        *stacks, layer, which = next_weights
        slab_index = lambda i: jnp.minimum(i, WEIGHT_CAST_STEPS - 1)
        for a in stacks:
            block = (1, 1, a.shape[2] // WEIGHT_CAST_STEPS, a.shape[3])
            in_specs.append(pl.BlockSpec(block, lambda i: (layer, which, slab_index(i), 0)))
            out_specs.append(pl.BlockSpec(block, lambda i: (0, 0, slab_index(i), 0)))
            out_shape.append(jax.ShapeDtypeStruct((1, 1) + a.shape[2:], BF16))
        args += stacks
    return pl.pallas_call(
        functools.partial(_ffn_kernel, j=j, nf=nf, steps_p=lay.steps_p, n_x=len(xs), n_out=n_out,
                          mixer=mixer is not None, in_proj=proj, cast_next=next_weights is not None),
        grid=(lay.steps,),
        in_specs=in_specs,
        out_specs=out_specs,
        out_shape=out_shape,
        scratch_shapes=[pltpu.VMEM((tm, d), BF16), pltpu.VMEM((tm, d), F32), pltpu.VMEM((tm, d), F32)],
        compiler_params=_cparams(("arbitrary",)),
        name="ffn_sublayer",
    )(*args)


def _rope_tables(length):
    t = np.arange(length)
    nf = HEAD_DIM // 4
    inv = ROPE_THETA ** (-np.arange(nf, dtype=np.float64) / nf)
    ang_r = (t // GRID_W)[:, None] * inv[None]
    ang_c = (t % GRID_W)[:, None] * inv[None]
    cr, sr, cc, sc = np.cos(ang_r), np.sin(ang_r), np.cos(ang_c), np.sin(ang_c)
    z = np.zeros_like(cr)
    cos = np.concatenate([cr, cr, cc, cc], 1)
    sin_up = np.concatenate([-sr, z, -sc, z], 1)
    sin_dn = np.concatenate([z, sr, z, sc], 1)
    two = lambda a: jnp.asarray(np.concatenate([a, a], 1), F32)
    return two(cos), two(sin_up), two(sin_dn)


def _rope128(x, cos, sin_up, sin_dn):
    nf = HEAD_DIM // 4
    return x * cos + pltpu.roll(x, 128 - nf, 1) * sin_up + pltpu.roll(x, nf, 1) * sin_dn


def _rope(x, cos, sin_up, sin_dn):
    parts = [_rope128(x[:, c:c + 128], cos, sin_up, sin_dn) for c in range(0, x.shape[1], 128)]
    return parts[0] if len(parts) == 1 else jnp.concatenate(parts, 1)


EVEN_COLS = (512, 256, 512, 512, 512, 528)


KV_WIDTH = 128 + A_KV * 256


def _store_cache(ck_ref, cv_ref, kv):
    seq_len = ck_ref.shape[2]
    for s in range(ck_ref.shape[0]):
        rows = slice(s * seq_len, (s + 1) * seq_len)
        ck_ref[s] = kv[rows, :128].T
        cv_ref[s] = kv[rows, 128:].T


def _in_proj_outputs(outs, lay):
    specs = [pl.BlockSpec((lay.tm, width), lay.all_rows) for width, _ in outs]
    shapes = [jax.ShapeDtypeStruct((lay.n, width), dtype) for width, dtype in outs]
    return specs, shapes


def _cache_outputs(lay):
    per_step = lay.tm // lay.len_p
    index = lambda i: lay.prompt_rows(i) + (0,)
    specs = [pl.BlockSpec((per_step, 128, lay.len_p), index)] * 2
    shapes = [jax.ShapeDtypeStruct((lay.n_p // lay.len_p, 128, lay.len_p), F32)] * 2
    return specs, shapes


def _cache_layout(c):
    nb, _, sl = c.shape
    return jnp.transpose(c.reshape(nb, 1, A_KV, HEAD_DIM, sl), (0, 1, 4, 2, 3))


def _kv_with_ones(kv):
    ones = jnp.ones((kv.shape[0], 128), kv.dtype)
    v0, v1 = kv[:, 128:192], kv[:, 192:256]
    return jnp.concatenate([kv[:, :128], v0, v0, ones, v1, v1, ones], 1)


def _even_in_body(x, m, ins, outs, rope, cache_refs):
    w_ref, *tabs = ins
    aq_ref, akv_ref, bq_ref, bk_ref, bv_ref, bo_ref, bg_ref = outs
    cache = cache_refs is not None

    h = (x * (1.0 + m[4:5]) + m[3:4]).astype(BF16)
    offs = [0]
    for c in EVEN_COLS:
        offs.append(offs[-1] + c)
    proj = lambda k: _dot(h, w_ref[:, offs[k]:offs[k + 1]])

    aq = proj(0)
    akv = proj(1)
    if cache:
        _store_cache(*cache_refs, akv)
    if rope:
        cos, s_up, s_dn = (t[...] for t in tabs)
        aq = _rope(aq, cos, s_up, s_dn)
        akv = jnp.concatenate([_rope(akv[:, :128], cos, s_up, s_dn), akv[:, 128:]], 1)
    aq_ref[...] = (aq * QK_SCALE).astype(BF16)
    akv_ref[...] = _kv_with_ones(akv).astype(BF16)
    bq_ref[...] = proj(2).astype(BF16)
    bk_ref[...] = (proj(3) * (B_DIM ** -0.5)).astype(BF16)
    bv_ref[...] = proj(4).astype(BF16)
    tail = proj(5)
    n_gates = 4 * B_HEADS
    bo_ref[...] = tail[:, n_gates:]
    bg_ref[...] = tail[:, :128]


def _rope_inputs(lay):
    return [pl.BlockSpec((lay.tm, 128), lay.rope_rows)] * 3, list(_rope_tables(lay.len_s))


def _even_in_io(lay, w):
    specs, tabs = _rope_inputs(lay)
    in_specs = [pl.BlockSpec(w.shape, lambda i: (0, 0), pipeline_mode=pl.Buffered(1))] + specs
    outs = [(512, BF16), (KV_WIDTH, BF16), (512, BF16), (512, BF16), (512, BF16), (512, F32), (128, F32)]
    return in_specs, [w] + tabs, _in_proj_outputs(outs, lay)


ODD_COLS = (512, 512, 512, 512, 512, 512, 256)


def _head_rms(x, seg_ref, g):
    x2 = x * x
    hi = x2.astype(BF16)
    lo = (x2 - hi.astype(F32)).astype(BF16)
    w = seg_ref.shape[0]
    ms = [_dot(hi[:, c:c + w], seg_ref[...]) + _dot(lo[:, c:c + w], seg_ref[...]) for c in range(0, x.shape[1], w)]
    ms = ms[0] if len(ms) == 1 else jnp.concatenate(ms, 1)
    return x * lax.rsqrt(ms + 1e-6) * g


def _odd_in_body(x, m, ins, outs, rope, cache_refs):
    w_ref, segq_ref, segk_ref, qn_ref, kn_ref, *tabs = ins
    q_ref, ff_ref, fb_ref, v_ref, cg_ref, dq_ref, dkv_ref = outs
    cache = cache_refs is not None

    h = (x * (1.0 + m[4:5]) + m[3:4]).astype(BF16)
    offs = [0]
    for c in ODD_COLS:
        offs.append(offs[-1] + c)
    proj = lambda k: _dot(h, w_ref[:, offs[k]:offs[k + 1]])

    q_ref[...] = _silu(proj(0)).astype(BF16)
    ff_ref[...] = proj(1)
    fb_ref[...] = proj(2)
    v_ref[...] = proj(3).astype(BF16)
    cg_ref[...] = proj(4)
    dq = _head_rms(proj(5), segq_ref, qn_ref[...])
    dkv = proj(6)
    dk = _head_rms(dkv[:, :128], segk_ref, kn_ref[...])
    dv = dkv[:, 128:]
    if cache:
        _store_cache(*cache_refs, jnp.concatenate([dk, dv], 1))
    if rope:
        cos, s_up, s_dn = (t[...] for t in tabs)
        dq = _rope(dq, cos, s_up, s_dn)
        dk = _rope(dk, cos, s_up, s_dn)
    dq_ref[...] = (dq * QK_SCALE).astype(BF16)
    dkv_ref[...] = _kv_with_ones(jnp.concatenate([dk, dv], 1)).astype(BF16)


def _segment_mean_matrix(width):
    r = jnp.arange(width) // HEAD_DIM
    return jnp.where(r[:, None] == r[None, :], 1.0 / HEAD_DIM, 0.0).astype(BF16)


def _odd_in_io(lay, w, q_norm, k_norm):
    const = lambda a: pl.BlockSpec(a.shape, lambda i: (0, 0))
    segq, segk = _segment_mean_matrix(256), _segment_mean_matrix(128)
    qn = jnp.tile(q_norm, D_HEADS).reshape(1, 512)
    kn = jnp.tile(k_norm, D_KV).reshape(1, 128)
    specs, tabs = _rope_inputs(lay)
    in_specs = [pl.BlockSpec(w.shape, lambda i: (0, 0), pipeline_mode=pl.Buffered(1)),
                const(segq), const(segk), const(qn), const(kn)] + specs
    outs = [(512, BF16), (512, F32), (512, F32), (512, BF16), (512, F32), (512, BF16), (KV_WIDTH, BF16)]
    return in_specs, [w, segq, segk, qn, kn] + tabs, _in_proj_outputs(outs, lay)


def _attn_kernel(*refs, seq_len, tq, q_blocks, n_ctx, banded, has_sink):
    it = iter(refs)
    q_ref, kv_ref = next(it), next(it)
    ckv_ref = next(it) if n_ctx else None
    sink_ref = next(it) if has_sink else None
    o_ref = next(it)
    groups = A_HEADS // A_KV
    rows = groups * tq
    low_half = lax.broadcasted_iota(jnp.int32, (tq, 2 * HEAD_DIM), 1) < HEAD_DIM

    def run_kv_head(blk, kh):
        r0 = blk * tq
        j = pl.program_id(1) * q_blocks + blk
        kcol = slice(kh * HEAD_DIM, (kh + 1) * HEAD_DIM)
        vcol = slice(128 + kh * 256, 128 + (kh + 1) * 256)
        qs = jnp.concatenate([q_ref[r0:r0 + tq, (kh * groups + g) * HEAD_DIM:(kh * groups + g + 1) * HEAD_DIM]
                              for g in range(groups)], axis=0)
        tiles = []
        if banded:
            span = tq + 2 * WINDOW
            start = pl.multiple_of(jnp.clip(j * tq - WINDOW, 0, seq_len - span), WINDOW)
            qpos = j * tq + (lax.broadcasted_iota(jnp.int32, (rows, span), 0) & (tq - 1))
            kpos = start + lax.broadcasted_iota(jnp.int32, (rows, span), 1)
            band = jnp.abs(kpos - qpos) <= WINDOW
            tiles.append((kv_ref[pl.ds(start, span), kcol], kv_ref[pl.ds(start, span), vcol], band))
        else:
            tk = min(ATT_KEY_TILE, seq_len)
            for t in range(seq_len // tk):
                tiles.append((kv_ref[t * tk:(t + 1) * tk, kcol], kv_ref[t * tk:(t + 1) * tk, vcol], None))
        scores = [_dot_nt(qs, k_t) for k_t, _, _ in tiles]
        if n_ctx:
            tiles.append((None, ckv_ref[0, kh, HEAD_DIM:, :], None))
            scores.append(_dot(qs, ckv_ref[0, kh, :HEAD_DIM, :]))
        yield
        scores = [s if mask is None else jnp.where(mask, s, NEG_INF) for s, (_, _, mask) in zip(scores, tiles)]
        m = functools.reduce(jnp.maximum, [jnp.max(s, -1, keepdims=True) for s in scores])
        if has_sink:
            sink = jnp.concatenate([jnp.full((tq, 1), sink_ref[kh * groups + g] * LOG2_E, F32)
                                    for g in range(groups)], 0)
            m = jnp.maximum(m, sink)
        pv = functools.reduce(lambda a, b: a + b,
                              [(_dot_nt if k_t is None else _dot)(jnp.exp2(s - m).astype(BF16), v_t)
                               for s, (k_t, v_t, _) in zip(scores, tiles)])
        yield
        total = pv[:, 2 * HEAD_DIM:]
        if has_sink:
            total = total + jnp.exp2(sink - m)
        o = pv[:, :2 * HEAD_DIM] / total
        for g in range(0, groups, 2):
            pair = jnp.where(low_half, o[g * tq:(g + 1) * tq], o[(g + 1) * tq:(g + 2) * tq])
            c0 = (kh * groups + g) * HEAD_DIM
            o_ref[r0:r0 + tq, c0:c0 + 2 * HEAD_DIM] = pair.astype(BF16)

    run = _run_staggered if seq_len + n_ctx > 2 * ATT_KEY_TILE else _run_interleaved
    run([run_kv_head(blk, kh) for blk in range(q_blocks) for kh in range(A_KV)])


def _attention(q, kv, ctx_kv, sink, batch, seq_len, row0, tq, q_blocks, banded):
    n_ctx = 0 if ctx_kv is None else ctx_kv.shape[-1]
    per_seq = seq_len // (tq * q_blocks)
    q0, kv0 = row0 // (tq * q_blocks), row0 // seq_len
    in_specs = [pl.BlockSpec((q_blocks * tq, 512), lambda b, j: (q0 + b * per_seq + j, 0)),
                pl.BlockSpec((seq_len, KV_WIDTH), lambda b, j: (kv0 + b, 0))]
    args = [q, kv]
    if n_ctx:
        in_specs.append(pl.BlockSpec((1,) + ctx_kv.shape[1:], lambda b, j: (b, 0, 0, 0)))
        args.append(ctx_kv)
    if sink is not None:
        in_specs.append(pl.BlockSpec(memory_space=pltpu.SMEM))
        args.append(sink.reshape(-1).astype(F32))
    return pl.pallas_call(
        functools.partial(_attn_kernel, seq_len=seq_len, tq=tq, q_blocks=q_blocks, n_ctx=n_ctx, banded=banded,
                          has_sink=sink is not None),
        grid=(batch, per_seq),
        in_specs=in_specs,
        out_specs=pl.BlockSpec((q_blocks * tq, 512), lambda b, j: (b * per_seq + j, 0)),
        out_shape=jax.ShapeDtypeStruct((batch * seq_len, 512), BF16),
        compiler_params=_cparams(("parallel", "arbitrary")),
        name="gqa_attention",
    )(*args)


def _log_sigmoid(x):
    return jnp.minimum(x, 0.0) - jnp.log1p(jnp.exp(-jnp.abs(x)))


def _mlstm_kernel(*refs, seq_len, has_init, emit_state):
    it = iter(refs)
    q_ref, k_ref, v_ref, g_ref, bo_ref, gb_ref, ng_ref = (next(it) for _ in range(7))
    c0_ref, n0_ref, m0_ref = (next(it), next(it), next(it)) if has_init else (None, None, None)
    y_ref = next(it)
    co_ref, no_ref, mo_ref = (next(it), next(it), next(it)) if emit_state else (None, None, None)
    c_scr, n_scr, m_scr, hf_scr, hb_scr, ig_buf, bc_buf, ut_buf = (next(it) for _ in range(8))

    t_len = MLSTM_T
    n_chunks = seq_len // t_len
    n_streams = 2 * B_HEADS

    for s in range(n_streams):
        if has_init:
            c_scr[s] = c0_ref[0, s // B_HEADS, s % B_HEADS]
        else:
            c_scr[s] = jnp.zeros((B_DIM, B_DIM), F32)
    if has_init:
        n_scr[...] = n0_ref[0]
        m_scr[...] = m0_ref[0]
    else:
        n_scr[...] = jnp.zeros_like(n_scr)
        m_scr[...] = jnp.zeros_like(m_scr)

    ri = lax.broadcasted_iota(jnp.int32, (t_len, t_len), 0)
    ci = lax.broadcasted_iota(jnp.int32, (t_len, t_len), 1)
    lower = ci <= ri
    upper = ci >= ri
    tri_f = jnp.where(lower, 1.0, 0.0).astype(BF16)
    tri_b = jnp.where(upper, 1.0, 0.0).astype(BF16)
    gate_bias = gb_ref[...]
    sel_row = lax.broadcasted_iota(jnp.int32, (128, B_HEADS * B_DIM), 0)
    sel_head = lax.broadcasted_iota(jnp.int32, (128, B_HEADS * B_DIM), 1) // B_DIM
    gate_select = []
    for direction in range(2):
        i_base = 2 * direction * B_HEADS
        gate_select.append((jnp.where(sel_row == i_base + sel_head, 1.0, 0.0).astype(BF16),
                            jnp.where(sel_row == i_base + B_HEADS + sel_head, 1.0, 0.0).astype(BF16)))

    def chunk_rows(c, direction):
        chunk = c if direction == 0 else n_chunks - 1 - c
        return pl.ds(pl.multiple_of(chunk * t_len, t_len), t_len)

    def prepare_gates(c, direction):
        slot = c % 2
        rows = chunk_rows(jnp.minimum(c, n_chunks - 1), direction)
        tri = tri_f if direction == 0 else tri_b
        sel_i, sel_f = gate_select[direction]
        gates = g_ref[rows, :] + gate_bias
        hi, mid, lo = _split3(_log_sigmoid(gates))
        b3 = _dot(tri, jnp.concatenate([hi, mid, lo], 1))
        ig_buf[slot, direction] = _dot_sel_rhs(gates, sel_i)
        yield
        bc_all = b3[:, :128] + b3[:, 128:256] + b3[:, 256:]
        bc_rep = _dot_sel_rhs(bc_all, sel_f)
        ut_buf[slot, direction] = (gates - pltpu.roll(bc_all, 128 - B_HEADS, 1)).T
        yield
        bc_buf[slot, direction] = bc_rep

    def run_direction(c, direction):
        rows = chunk_rows(c, direction)
        slot = c % 2
        causal = lower if direction == 0 else upper
        last = t_len - 1 if direction == 0 else 0
        h_out = hf_scr if direction == 0 else hb_scr
        ig_rep, bc_rep, u_t = ig_buf[slot, direction], bc_buf[slot, direction], ut_buf[slot, direction]
        heads = []
        for hd in range(B_HEADS):
            s = direction * B_HEADS + hd
            col = slice(hd * B_DIM, (hd + 1) * B_DIM)
            qc, kc, vc = q_ref[rows, col], k_ref[rows, col], v_ref[rows, col]
            c_prev = c_scr[s]
            n_prev = n_scr[s:s + 1, :]
            qk = _dot_nt(qc, kc)
            q_state = _dot(qc, c_prev.astype(BF16))
            qn = _dot_nt(qc, jnp.broadcast_to(n_prev, (B_DIM, B_DIM)).astype(BF16))
            heads.append((s, col, qc, kc, vc, c_prev, n_prev, qk, q_state, qn))
        yield
        staged = []
        for (s, col, qc, kc, vc, c_prev, n_prev, qk, q_state, qn) in heads:
            i_col = 2 * direction * B_HEADS + (s % B_HEADS)
            ig, bc = ig_rep[:, col], bc_rep[:, col]
            dmat = jnp.where(causal, bc[:, :t_len] + u_t[i_col:i_col + 1, :], -jnp.inf)
            m_prev = m_scr[s:s + 1, :]
            inter = bc + m_prev
            m_t = jnp.maximum(inter, jnp.max(dmat, -1, keepdims=True))
            w = (jnp.exp(dmat - m_t[:, :t_len]) * qk).astype(BF16)
            wv = _dot(w, jnp.concatenate([vc, jnp.ones_like(vc)], 1))
            b_tot = bc[last:last + 1, :]
            g = b_tot - bc + ig
            m_new = jnp.maximum(b_tot + m_prev, jnp.max(g, 0, keepdims=True))
            kw = kc.astype(F32) * jnp.exp(g - m_new)
            upd = _dot_tn(kw.astype(BF16), vc)
            decay = jnp.exp(b_tot + m_prev - m_new)
            staged.append((s, col, inter, m_t, c_prev, n_prev, q_state, qn, wv, m_new, kw, upd, decay))
        yield
        for (s, col, inter, m_t, c_prev, n_prev, q_state, qn, wv, m_new, kw, upd, decay) in staged:
            a = jnp.exp(inter - m_t)
            num = a * q_state + wv[:, :B_DIM]
            den = a * qn + wv[:, B_DIM:]
            h_out[rows, col] = num / jnp.maximum(jnp.abs(den), jnp.exp(-m_t))
            c_scr[s] = decay * c_prev + upd
            n_scr[s:s + 1, :] = decay * n_prev + jnp.sum(kw, 0, keepdims=True)
            m_scr[s:s + 1, :] = m_new

    def chained(c, direction):
        yield from prepare_gates(c, direction)
        yield
        yield from run_direction(c, direction)

    def step(c, carry):
        _run_interleaved([chained(c, 0), chained(c, 1)])
        return carry

    lax.fori_loop(0, n_chunks, step, 0, unroll=min(SCAN_UNROLL, n_chunks))

    blk = min(256, seq_len)
    ng = ng_ref[...]

    def finish(i, carry):
        rows = pl.ds(pl.multiple_of(i * blk, blk), blk)
        hsum = hf_scr[rows, :] + hb_scr[rows, :]
        gate = jax.nn.sigmoid(bo_ref[rows, :])
        parts = [_rms_lastdim(hsum[:, hd * B_DIM:(hd + 1) * B_DIM], ng[:, hd * B_DIM:(hd + 1) * B_DIM])
                 for hd in range(B_HEADS)]
        y_ref[rows, :] = (gate * jnp.concatenate(parts, 1)).astype(BF16)
        return carry

    lax.fori_loop(0, seq_len // blk, finish, 0)

    if emit_state:
        for s in range(n_streams):
            co_ref[0, s // B_HEADS, s % B_HEADS] = c_scr[s]
        no_ref[0] = n_scr[...]
        mo_ref[0] = m_scr[...]


def _mlstm(q, k, v, gates, bo, gate_bias, norm_g, init, batch, seq_len, row0, emit_state):
    n = batch * seq_len
    seq0 = row0 // seq_len
    in_row = lambda width: pl.BlockSpec((seq_len, width), lambda b: (seq0 + b, 0))
    row = lambda width: pl.BlockSpec((seq_len, width), lambda b: (b, 0))
    const = lambda a: pl.BlockSpec(a.shape, lambda b: (0,) * a.ndim)
    gb = jnp.zeros((1, 128), F32).at[0, :4 * B_HEADS].set(gate_bias.reshape(-1))
    ng = norm_g.reshape(1, B_HEADS * B_DIM)
    in_specs = [in_row(512), in_row(512), in_row(512), in_row(128), in_row(512), const(gb), const(ng)]
    args = [q, k, v, gates, bo, gb, ng]
    n_streams = 2 * B_HEADS
    if init is not None:
        c0, n0, m0 = init
        in_specs += [pl.BlockSpec((1, 2, B_HEADS, B_DIM, B_DIM), lambda b: (b, 0, 0, 0, 0)),
                     pl.BlockSpec((1, n_streams, B_DIM), lambda b: (b, 0, 0)),
                     pl.BlockSpec((1, n_streams, B_DIM), lambda b: (b, 0, 0))]
        args += [c0, n0.reshape(batch, n_streams, B_DIM),
                 jnp.broadcast_to(m0.reshape(batch, n_streams, 1), (batch, n_streams, B_DIM))]
    out_specs = [row(512)]
    out_shape = [jax.ShapeDtypeStruct((n, 512), BF16)]
    if emit_state:
        out_specs += [pl.BlockSpec((1, 2, B_HEADS, B_DIM, B_DIM), lambda b: (b, 0, 0, 0, 0)),
                      pl.BlockSpec((1, n_streams, B_DIM), lambda b: (b, 0, 0)),
                      pl.BlockSpec((1, n_streams, B_DIM), lambda b: (b, 0, 0))]
        out_shape += [jax.ShapeDtypeStruct((batch, 2, B_HEADS, B_DIM, B_DIM), F32),
                      jax.ShapeDtypeStruct((batch, n_streams, B_DIM), F32),
                      jax.ShapeDtypeStruct((batch, n_streams, B_DIM), F32)]
    return pl.pallas_call(
        functools.partial(_mlstm_kernel, seq_len=seq_len, has_init=init is not None, emit_state=emit_state),
        grid=(batch,),
        in_specs=in_specs,
        out_specs=out_specs,
        out_shape=out_shape,
        scratch_shapes=[pltpu.VMEM((n_streams, B_DIM, B_DIM), F32),
                        pltpu.VMEM((n_streams, B_DIM), F32),
                        pltpu.VMEM((n_streams, B_DIM), F32),
                        pltpu.VMEM((seq_len, 512), F32),
                        pltpu.VMEM((seq_len, 512), F32),
                        pltpu.VMEM((2, 2, MLSTM_T, B_HEADS * B_DIM), F32),
                        pltpu.VMEM((2, 2, MLSTM_T, B_HEADS * B_DIM), F32),
                        pltpu.VMEM((2, 2, 128, MLSTM_T), F32)],
        compiler_params=_cparams(("parallel",)),
        name="mlstm_scan",
    )(*args)


def _hgrn_kernel(*refs, seq_len, layer, has_init, emit_state):
    it = iter(refs)
    q_ref, ff_ref, fb_ref, v_ref, cg_ref, lbl_ref, ng_ref = (next(it) for _ in range(7))
    s0_ref = next(it) if has_init else None
    y_ref = next(it)
    so_ref = next(it) if emit_state else None
    st_scr, of_scr, ob_scr = next(it), next(it), next(it)

    t_len = HGRN_T
    n_sub = t_len // SUB
    n_chunks = seq_len // t_len
    n_streams = 2 * C_HEADS

    logits = lbl_ref[...]
    e = jnp.exp(logits - jnp.max(logits, 0, keepdims=True))
    sm = e / jnp.sum(e, 0, keepdims=True)
    lb = jnp.sum(sm[0:layer + 1], 0, keepdims=True) - sm[0:1]

    for s in range(n_streams):
        if has_init:
            st_scr[s] = s0_ref[0, s // C_HEADS, s % C_HEADS].T
        else:
            st_scr[s] = jnp.zeros((C_DV, C_DK), F32)

    ri = lax.broadcasted_iota(jnp.int32, (t_len, t_len), 0)
    ci = lax.broadcasted_iota(jnp.int32, (t_len, t_len), 1)
    tri_f = jnp.where(ci <= ri, 1.0, 0.0).astype(BF16)
    tri_b = jnp.where(ci >= ri, 1.0, 0.0).astype(BF16)
    sub_row = lax.broadcasted_iota(jnp.int32, (SUB, C_DK), 0)
    ones_dk = jnp.ones((C_DK, C_DK), BF16)

    def run_stream(c, direction, hd):
        chunk = c if direction == 0 else n_chunks - 1 - c
        rows = pl.ds(pl.multiple_of(chunk * t_len, t_len), t_len)
        f_ref = ff_ref if direction == 0 else fb_ref
        tri = tri_f if direction == 0 else tri_b
        last = t_len - 1 if direction == 0 else 0
        o_out = of_scr if direction == 0 else ob_scr
        s = direction * C_HEADS + hd
        col = slice(hd * C_DK, (hd + 1) * C_DK)
        lbh = lb[:, col]
        f = lbh + (1.0 - lbh) * jax.nn.sigmoid(f_ref[rows, col])
        log_k = jnp.log2(1.0 - f)
        hi, mid, lo = _split3(jnp.log2(f))
        a3 = _dot(tri, jnp.concatenate([hi, mid, lo], 1))
        yield
        a_cum = a3[:, :C_DK] + a3[:, C_DK:2 * C_DK] + a3[:, 2 * C_DK:]
        a_key = a_cum - log_k
        a_tot = a_cum[last:last + 1, :]
        qf = q_ref[rows, col].astype(F32)
        vc = v_ref[rows, col]
        vf = vc.astype(F32)
        st = st_scr[s]
        inter = _dot_nt((qf * jnp.exp2(a_cum)).astype(BF16), st.astype(BF16))
        upd = _dot_tn(vc, jnp.exp2(a_tot - a_key).astype(BF16))
        ps = []
        for blk in range(n_sub):
            b0 = blk * SUB
            a_i, q_i, ak_i = (t[b0:b0 + SUB] for t in (a_cum, qf, a_key))
            for j in range(SUB):
                seen = (sub_row >= j) if direction == 0 else (sub_row <= j)
                ps.append(jnp.where(seen, jnp.exp2(a_i - ak_i[j:j + 1]), 0.0) * q_i)
        att = _dot(jnp.concatenate(ps, 0).astype(BF16), ones_dk)
        off = []
        for blk in range(n_sub):
            b0 = blk * SUB
            if direction == 0 and blk > 0:
                a_ref, kr = a_cum[b0 - 1:b0], slice(0, b0)
            elif direction == 1 and blk < n_sub - 1:
                a_ref, kr = a_cum[b0 + SUB:b0 + SUB + 1], slice(b0 + SUB, t_len)
            else:
                off.append(None)
                continue
            qt = (qf[b0:b0 + SUB] * jnp.exp2(a_cum[b0:b0 + SUB] - a_ref)).astype(BF16)
            kt = jnp.exp2(a_ref - a_key[kr]).astype(BF16)
            off.append((_dot_nt(qt, kt), kr))
        yield
        st_scr[s] = jnp.exp2(a_tot) * st + upd
        outs = []
        for blk in range(n_sub):
            b0 = blk * SUB
            o_i = inter[b0:b0 + SUB]
            for j in range(SUB):
                r = (blk * SUB + j) * SUB
                o_i = o_i + att[r:r + SUB] * vf[b0 + j:b0 + j + 1]
            if off[blk] is not None:
                att_off, kr = off[blk]
                outs.append((o_i, _dot(att_off.astype(BF16), vc[kr])))
            else:
                outs.append((o_i, None))
        yield
        o_out[rows, col] = jnp.concatenate([o_i if o_off is None else o_i + o_off for o_i, o_off in outs], 0)

    def step(c, carry):
        _run_interleaved([run_stream(c, direction, hd) for direction in range(2) for hd in range(C_HEADS)])
        return carry

    lax.fori_loop(0, n_chunks, step, 0, unroll=min(SCAN_UNROLL, n_chunks))

    blk_rows = min(256, seq_len)
    ng = ng_ref[...]

    def finish(i, carry):
        rows = pl.ds(pl.multiple_of(i * blk_rows, blk_rows), blk_rows)
        osum = of_scr[rows, :] + ob_scr[rows, :]
        parts = [_rms_lastdim(osum[:, hd * C_DV:(hd + 1) * C_DV], ng[:, hd * C_DV:(hd + 1) * C_DV])
                 for hd in range(C_HEADS)]
        y_ref[rows, :] = (jnp.concatenate(parts, 1) * _silu(cg_ref[rows, :])).astype(BF16)
        return carry

    lax.fori_loop(0, seq_len // blk_rows, finish, 0)

    if emit_state:
        for s in range(n_streams):
            so_ref[0, s // C_HEADS, s % C_HEADS] = st_scr[s].T


def _hgrn(q, ff, fb, v, cg, lb_logits, norm_g, layer, init, batch, seq_len, row0, emit_state):
    n = batch * seq_len
    seq0 = row0 // seq_len
    in_row = lambda width: pl.BlockSpec((seq_len, width), lambda b: (seq0 + b, 0))
    row = lambda width: pl.BlockSpec((seq_len, width), lambda b: (b, 0))
    const = lambda a: pl.BlockSpec(a.shape, lambda b: (0,) * a.ndim)
    ng = norm_g.reshape(1, C_HEADS * C_DV)
    state_spec = pl.BlockSpec((1, 2, C_HEADS, C_DK, C_DV), lambda b: (b, 0, 0, 0, 0))
    in_specs = [in_row(512)] * 5 + [const(lb_logits), const(ng)]
    args = [q, ff, fb, v, cg, lb_logits, ng]
    if init is not None:
        in_specs.append(state_spec)
        args.append(init)
    out_specs = [row(512)]
    out_shape = [jax.ShapeDtypeStruct((n, 512), BF16)]
    if emit_state:
        out_specs.append(state_spec)
        out_shape.append(jax.ShapeDtypeStruct((batch, 2, C_HEADS, C_DK, C_DV), F32))
    return pl.pallas_call(
        functools.partial(_hgrn_kernel, seq_len=seq_len, layer=layer, has_init=init is not None,
                          emit_state=emit_state),
        grid=(batch,),
        in_specs=in_specs,
        out_specs=out_specs,
        out_shape=out_shape,
        scratch_shapes=[pltpu.VMEM((2 * C_HEADS, C_DV, C_DK), F32),
                        pltpu.VMEM((seq_len, 512), F32),
                        pltpu.VMEM((seq_len, 512), F32)],
        compiler_params=_cparams(("parallel",)),
        name="hgrn2_scan",
    )(*args)


def _prep_ctx_kv(k_ctx, v_ctx):
    k_t = jnp.transpose(k_ctx, (0, 2, 3, 1)).astype(BF16)
    v_t = jnp.transpose(v_ctx, (0, 2, 3, 1)).astype(BF16)
    ones = jnp.ones(k_t.shape[:2] + (128, k_t.shape[3]), BF16)
    return jnp.concatenate([k_t, v_t, v_t, ones], 2)


def kernel(x_prompt, x_sample, c, cache_a_k, cache_a_v, state_b_C, state_b_n, state_b_m, state_c_S, cache_d_k, cache_d_v, c_ctx, ada_w, ada_b, ln_g, ln_b, ffn_w1, ffn_w3, ffn_w2, w_in_even, w_out_even, a_sink, b_gate_bias, b_norm_g, w_in_odd, w_out_odd, c_lb_logits, c_norm_g, d_q_norm, d_k_norm):
    batch_p, len_p, d = x_prompt.shape
    batch_s, len_s, _ = x_sample.shape

    cvec = jnp.concatenate([c_ctx[None], c, jnp.zeros((MOD_ROWS - 1 - batch_s, d), F32)], 0)
    mod_all = _modulation(cvec, ada_w, ada_b)

    lay = _Layout(batch_p, len_p, batch_s, len_s)
    new = {}
    ffn_stacks = (ffn_w1, ffn_w3, ffn_w2)
    ffn_w = tuple(a[0:1, 0:1].astype(BF16) for a in ffn_stacks)
    x = [x_prompt.reshape(lay.n_p, d), x_sample.reshape(lay.n_s, d)]

    for l in range(DEPTH):
        mod = mod_all[l].reshape(MOD_ROWS, 9, d)
        i = l // 2
        if l % 2 == 0:
            in_proj = (_even_in_body, _even_in_io, dict(w=w_in_even[i].astype(BF16)))
            w_out = w_out_even[i].astype(BF16)
        else:
            in_proj = (_odd_in_body, _odd_in_io,
                       dict(w=w_in_odd[i].astype(BF16), q_norm=d_q_norm[i], k_norm=d_k_norm[i]))
            w_out = w_out_odd[i].astype(BF16)

        x_all, *proj, k_cache, v_cache, w1, w3, w2 = _ffn(x, mod, 0, ln_g[l, 0], ln_b[l, 0], ffn_w, lay,
                                                         in_proj=in_proj, next_weights=ffn_stacks + (l, 1))
        ffn_w = (w1, w3, w2)
        prompt = dict(batch=batch_p, seq_len=len_p, row0=0)
        sample = dict(batch=batch_s, seq_len=len_s, row0=lay.n_p)
        if l % 2 == 0:
            aq, akv, bq, bk, bv, bo, bg = proj
            new["a_k"], new["a_v"] = _cache_layout(k_cache), _cache_layout(v_cache)
            scan = (bq, bk, bv, bg, bo, b_gate_bias[i], b_norm_g[i])
            ya_p = _attention(aq, akv, None, a_sink[i], **prompt, tq=Q_BLOCK, q_blocks=len_p // Q_BLOCK, banded=False)
            yb_p, c_new, n_new, m_new = _mlstm(*scan, None, **prompt, emit_state=True)
            new["b_C"] = c_new[:, None]
            new["b_n"] = n_new.reshape(batch_p, 1, 2, B_HEADS, B_DIM)
            new["b_m"] = m_new[:, :, 0].reshape(batch_p, 1, 2, B_HEADS)
            ya_s = _attention(aq, akv, _prep_ctx_kv(cache_a_k[:, i], cache_a_v[:, i]), a_sink[i], **sample,
                              tq=Q_BLOCK, q_blocks=4, banded=True)
            yb_s, = _mlstm(*scan, (state_b_C[:, i], state_b_n[:, i], state_b_m[:, i]), **sample, emit_state=False)
        else:
            cq, ff, fb, cv, cg, dq, dkv = proj
            new["d_k"], new["d_v"] = _cache_layout(k_cache), _cache_layout(v_cache)
            scan = (cq, ff, fb, cv, cg, c_lb_logits, c_norm_g[i], l)
            ya_p, s_new = _hgrn(*scan, None, **prompt, emit_state=True)
            new["c_S"] = s_new[:, None]
            yb_p = _attention(dq, dkv, None, None, **prompt, tq=Q_BLOCK, q_blocks=len_p // Q_BLOCK, banded=False)
            ya_s, = _hgrn(*scan, state_c_S[:, i], **sample, emit_state=False)
            yb_s = _attention(dq, dkv, _prep_ctx_kv(cache_d_k[:, i], cache_d_v[:, i]), None, **sample,
                              tq=Q_BLOCK, q_blocks=2, banded=False)
        last = l == DEPTH - 1
        x = _ffn([x_all], mod, 2, ln_g[l, 2], ln_b[l, 2], ffn_w, lay,
                 mixer=(ya_p, yb_p, ya_s, yb_s, w_out, ln_g[l, 1], ln_b[l, 1]), split_out=last,
                 next_weights=None if last else ffn_stacks + (l + 1, 0))
        if not last:
            *x, w1, w3, w2 = x
            ffn_w = (w1, w3, w2)

    y_prompt, y_sample = x
    return (y_prompt.reshape(batch_p, len_p, d), y_sample.reshape(batch_s, len_s, d), new["a_k"], new["a_v"],
            new["b_C"], new["b_n"], new["b_m"], new["c_S"], new["d_k"], new["d_v"])
```

```python
import functools

import jax
import jax.numpy as jnp
import numpy as np
from jax import lax
from jax.experimental import pallas as pl
from jax.experimental.pallas import tpu as pltpu

F32 = jnp.float32
BF16 = jnp.bfloat16

DEPTH = 2
GRID_W = 64
HEAD_DIM = 64
A_HEADS = 8
A_KV = 2
WINDOW = 128
B_HEADS = 4
B_DIM = 128
C_HEADS = 4
C_DK = 128
C_DV = 128
D_HEADS = 8
D_KV = 2
Q_BLOCK = 128
ROPE_THETA = 10000.0
ALPHA = (2 * DEPTH) ** 0.25
NEG_INF = -1e30
LOG2_E = 1.4426950408889634
QK_SCALE = HEAD_DIM ** -0.5 * LOG2_E

MOD_ROWS = 8
FF_CHUNK = 256
FFN_TILE = 512
WEIGHT_CAST_STEPS = 16
MLSTM_T = 64
HGRN_T = 64
SUB = 8
SCAN_UNROLL = 8
MOD_TILE = 2304
ATT_KEY_TILE = 512
VMEM_LIMIT = 58 * 1024 * 1024


def _cparams(sem):
    return pltpu.CompilerParams(dimension_semantics=sem, vmem_limit_bytes=VMEM_LIMIT)


def _dot(a, b):
    return jnp.dot(a, b, preferred_element_type=F32)


def _dot_nt(a, b):
    return lax.dot_general(a, b, (((1,), (1,)), ((), ())), preferred_element_type=F32)


def _dot_tn(a, b):
    return lax.dot_general(a, b, (((0,), (0,)), ((), ())), preferred_element_type=F32)


def _split3(x):
    hi = x.astype(BF16)
    r1 = x - hi.astype(F32)
    mid = r1.astype(BF16)
    lo = (r1 - mid.astype(F32)).astype(BF16)
    return hi, mid, lo


def _dot_sel_rhs(x, sel):
    hi = x.astype(BF16)
    lo = (x - hi.astype(F32)).astype(BF16)
    r = _dot(jnp.concatenate([hi, lo], 0), sel)
    return r[:x.shape[0]] + r[x.shape[0]:]


def _run_interleaved(gens):
    live = list(gens)
    while live:
        nxt = []
        for g in live:
            try:
                next(g)
                nxt.append(g)
            except StopIteration:
                pass
        live = nxt


def _run_staggered(gens):
    pending, live = list(gens), []
    while pending or live:
        if pending:
            live.insert(0, pending.pop(0))
        nxt = []
        for g in live:
            try:
                next(g)
                nxt.append(g)
            except StopIteration:
                pass
        live = nxt


def _silu(x):
    return x * jax.nn.sigmoid(x)


def _layernorm(z, g, b):
    mu = jnp.mean(z, -1, keepdims=True)
    zc = z - mu
    var = jnp.mean(zc * zc, -1, keepdims=True)
    return zc * lax.rsqrt(var + 1e-5) * g + b


def _rms_lastdim(x, g):
    return x * lax.rsqrt(jnp.mean(x * x, -1, keepdims=True) + 1e-6) * g


class _Layout:
    def __init__(self, batch_p, len_p, batch_s, len_s):
        self.tm = FFN_TILE
        assert self.tm % len_p == 0 and len_s % self.tm == 0 and (batch_p * len_p) % self.tm == 0
        self.len_p, self.len_s = len_p, len_s
        self.n_p, self.n_s = batch_p * len_p, batch_s * len_s
        self.n = self.n_p + self.n_s
        self.steps_p = self.n_p // self.tm
        self.steps = self.n // self.tm

    def all_rows(self, i):
        return (i, 0)

    def prompt_rows(self, i):
        return (jnp.minimum(i, self.steps_p - 1), 0)

    def sample_rows(self, i):
        return (jnp.maximum(i - self.steps_p, 0), 0)

    def mod_row(self, i):
        req = jnp.maximum(i - self.steps_p, 0) // (self.len_s // self.tm)
        return (jnp.where(i < self.steps_p, 0, 1 + req), 0, 0)

    def rope_rows(self, i):
        return (jnp.maximum(i - self.steps_p, 0) % (self.len_s // self.tm), 0)


def _mod_kernel(c_ref, w_ref, b_ref, o_ref):
    s = _silu(c_ref[...]).astype(BF16)
    o_ref[0] = _dot(s, w_ref[0].astype(BF16)) + b_ref[0]


def _modulation(cvec, ada_w, ada_b):
    depth, d, n = ada_w.shape
    tn = MOD_TILE
    return pl.pallas_call(
        _mod_kernel,
        grid=(depth, n // tn),
        in_specs=[pl.BlockSpec((MOD_ROWS, d), lambda l, j: (0, 0)),
                  pl.BlockSpec((1, d, tn), lambda l, j: (l, 0, j)),
                  pl.BlockSpec((1, 1, tn), lambda l, j: (l, 0, j))],
        out_specs=pl.BlockSpec((1, MOD_ROWS, tn), lambda l, j: (l, 0, j)),
        out_shape=jax.ShapeDtypeStruct((depth, MOD_ROWS, n), F32),
        compiler_params=_cparams(("parallel", "parallel")),
        name="modulation",
    )(cvec, ada_w, ada_b.reshape(depth, 1, n))


def _ffn_kernel(*refs, j, nf, steps_p, n_x, n_out, mixer, in_proj, cast_next):
    it = iter(refs)
    x_refs = [next(it) for _ in range(n_x)]
    mod_ref = next(it)
    if mixer:
        ya_p, yb_p, ya_s, yb_s, wo_ref, g1_ref, b1_ref = (next(it) for _ in range(7))
    w1_ref, w3_ref, w2_ref, g_ref, b_ref = (next(it) for _ in range(5))
    proj_ins = [next(it) for _ in range(in_proj[1])] if in_proj else []
    cast_ins = [next(it) for _ in range(3)] if cast_next else []
    o_refs = [next(it) for _ in range(n_out)]
    proj_outs = [next(it) for _ in range(in_proj[2])] if in_proj else []
    cache_refs = (next(it), next(it)) if in_proj else None
    cast_outs = [next(it) for _ in range(3)] if cast_next else []
    h_ref, acc_ref, res_ref = next(it), next(it), next(it)

    if cast_next:
        @pl.when(pl.program_id(0) < WEIGHT_CAST_STEPS)
        def _():
            for src, dst in zip(cast_ins, cast_outs):
                dst[...] = src[...].astype(BF16)

    is_prompt = pl.program_id(0) < steps_p
    pick = lambda p_ref, s_ref: jnp.where(is_prompt, p_ref[...], s_ref[...])
    m = mod_ref[0]
    shift, scale, gate = m[3 * j:3 * j + 1], m[3 * j + 1:3 * j + 2], m[3 * j + 2:3 * j + 3]
    x = x_refs[0][...] if n_x == 1 else pick(*x_refs)
    if mixer:
        ya, yb = pick(ya_p, ya_s), pick(yb_p, yb_s)
        half = ya.shape[1]
        y = _dot(ya, wo_ref[:half, :]) + _dot(yb, wo_ref[half:, :])
        x = _layernorm(ALPHA * x + m[5:6] * y, g1_ref[...], b1_ref[...])
    res_ref[...] = x
    h_ref[...] = (x * (1.0 + scale) + shift).astype(BF16)
    for f in range(nf):
        cols = slice(f * FF_CHUNK, (f + 1) * FF_CHUNK)
        h = h_ref[...]
        u = (_silu(_dot(h, w1_ref[0, 0, :, cols])) * _dot(h, w3_ref[0, 0, :, cols])).astype(BF16)
        y = _dot(u, w2_ref[0, 0, cols, :])
        if f == 0:
            acc_ref[...] = y
        else:
            acc_ref[...] += y
    out = _layernorm(ALPHA * res_ref[...] + 0.5 * gate * acc_ref[...], g_ref[...], b_ref[...])
    if n_out == 1:
        o_refs[0][...] = out
    else:
        res_ref[...] = out

        @pl.when(is_prompt)
        def _():
            o_refs[0][...] = res_ref[...]

        @pl.when(jnp.logical_not(is_prompt))
        def _():
            o_refs[1][...] = res_ref[...]

    if in_proj:
        @pl.when(is_prompt)
        def _():
            in_proj[0](o_refs[0][...], m, proj_ins, proj_outs, rope=False, cache_refs=cache_refs)

        @pl.when(jnp.logical_not(is_prompt))
        def _():
            in_proj[0](o_refs[0][...], m, proj_ins, proj_outs, rope=True, cache_refs=None)


def _ffn(xs, mod, j, g, b, weights, lay, mixer=None, in_proj=None, split_out=False, next_weights=None):
    w1, w3, w2 = weights
    d = xs[0].shape[1]
    nf = w1.shape[-1] // FF_CHUNK
    tm = lay.tm
    spec = lambda width, index: pl.BlockSpec((tm, width), index)
    vec = pl.BlockSpec((1, d), lambda i: (0, 0))
    whole = lambda a: pl.BlockSpec(a.shape, lambda i: (0, 0, 0, 0), pipeline_mode=pl.Buffered(1))
    pair = (lay.prompt_rows, lay.sample_rows)
    in_specs = [spec(d, lay.all_rows)] if len(xs) == 1 else [spec(d, index) for index in pair]
    in_specs.append(pl.BlockSpec((1, 9, d), lay.mod_row))
    args = list(xs) + [mod]
    if mixer is not None:
        ya_p, yb_p, ya_s, yb_s, w_out, g1, b1 = mixer
        in_specs += [spec(ya_p.shape[1], pair[0]), spec(yb_p.shape[1], pair[0]),
                     spec(ya_s.shape[1], pair[1]), spec(yb_s.shape[1], pair[1]),
                     pl.BlockSpec(w_out.shape, lambda i: (0, 0), pipeline_mode=pl.Buffered(1)), vec, vec]
        args += [ya_p, yb_p, ya_s, yb_s, w_out, g1.reshape(1, d), b1.reshape(1, d)]
    in_specs += [whole(w1), whole(w3), whole(w2), vec, vec]
    args += [w1, w3, w2, g.reshape(1, d), b.reshape(1, d)]
    if split_out:
        out_specs = [spec(d, index) for index in pair]
        out_shape = [jax.ShapeDtypeStruct((lay.n_p, d), F32), jax.ShapeDtypeStruct((lay.n_s, d), F32)]
    else:
        out_specs, out_shape = [spec(d, lay.all_rows)], [jax.ShapeDtypeStruct((lay.n, d), F32)]
    n_out, proj = len(out_specs), None
    if in_proj is not None:
        body, io, kw = in_proj
        p_specs, p_args, (p_out_specs, p_out_shape) = io(lay, **kw)
        c_specs, c_shape = _cache_outputs(lay)
        in_specs, args = in_specs + p_specs, args + p_args
        out_specs, out_shape = out_specs + p_out_specs + c_specs, out_shape + p_out_shape + c_shape
        proj = (body, len(p_args), len(p_out_specs))
    if next_weights is not None:
        *stacks, layer, which = next_weights
        slab_index = lambda i: jnp.minimum(i, WEIGHT_CAST_STEPS - 1)
        for a in stacks:
            block = (1, 1, a.shape[2] // WEIGHT_CAST_STEPS, a.shape[3])
            in_specs.append(pl.BlockSpec(block, lambda i: (layer, which, slab_index(i), 0)))
            out_specs.append(pl.BlockSpec(block, lambda i: (0, 0, slab_index(i), 0)))
            out_shape.append(jax.ShapeDtypeStruct((1, 1) + a.shape[2:], BF16))
        args += stacks
    return pl.pallas_call(
        functools.partial(_ffn_kernel, j=j, nf=nf, steps_p=lay.steps_p, n_x=len(xs), n_out=n_out,
                          mixer=mixer is not None, in_proj=proj, cast_next=next_weights is not None),
        grid=(lay.steps,),
        in_specs=in_specs,
        out_specs=out_specs,
        out_shape=out_shape,
        scratch_shapes=[pltpu.VMEM((tm, d), BF16), pltpu.VMEM((tm, d), F32), pltpu.VMEM((tm, d), F32)],
        compiler_params=_cparams(("arbitrary",)),
        name="ffn_sublayer",
    )(*args)


def _rope_tables(length):
    t = np.arange(length)
    nf = HEAD_DIM // 4
    inv = ROPE_THETA ** (-np.arange(nf, dtype=np.float64) / nf)
    ang_r = (t // GRID_W)[:, None] * inv[None]
    ang_c = (t % GRID_W)[:, None] * inv[None]
    cr, sr, cc, sc = np.cos(ang_r), np.sin(ang_r), np.cos(ang_c), np.sin(ang_c)
    z = np.zeros_like(cr)
    cos = np.concatenate([cr, cr, cc, cc], 1)
    sin_up = np.concatenate([-sr, z, -sc, z], 1)
    sin_dn = np.concatenate([z, sr, z, sc], 1)
    two = lambda a: jnp.asarray(np.concatenate([a, a], 1), F32)
    return two(cos), two(sin_up), two(sin_dn)


def _rope128(x, cos, sin_up, sin_dn):
    nf = HEAD_DIM // 4
    return x * cos + pltpu.roll(x, 128 - nf, 1) * sin_up + pltpu.roll(x, nf, 1) * sin_dn


def _rope(x, cos, sin_up, sin_dn):
    parts = [_rope128(x[:, c:c + 128], cos, sin_up, sin_dn) for c in range(0, x.shape[1], 128)]
    return parts[0] if len(parts) == 1 else jnp.concatenate(parts, 1)


EVEN_COLS = (512, 256, 512, 512, 512, 528)


KV_WIDTH = 128 + A_KV * 256


def _store_cache(ck_ref, cv_ref, kv):
    seq_len = ck_ref.shape[2]
    for s in range(ck_ref.shape[0]):
        rows = slice(s * seq_len, (s + 1) * seq_len)
        ck_ref[s] = kv[rows, :128].T
        cv_ref[s] = kv[rows, 128:].T


def _in_proj_outputs(outs, lay):
    specs = [pl.BlockSpec((lay.tm, width), lay.all_rows) for width, _ in outs]
    shapes = [jax.ShapeDtypeStruct((lay.n, width), dtype) for width, dtype in outs]
    return specs, shapes


def _cache_outputs(lay):
    per_step = lay.tm // lay.len_p
    index = lambda i: lay.prompt_rows(i) + (0,)
    specs = [pl.BlockSpec((per_step, 128, lay.len_p), index)] * 2
    shapes = [jax.ShapeDtypeStruct((lay.n_p // lay.len_p, 128, lay.len_p), F32)] * 2
    return specs, shapes


def _cache_layout(c):
    nb, _, sl = c.shape
    return jnp.transpose(c.reshape(nb, 1, A_KV, HEAD_DIM, sl), (0, 1, 4, 2, 3))


def _kv_with_ones(kv):
    ones = jnp.ones((kv.shape[0], 128), kv.dtype)
    v0, v1 = kv[:, 128:192], kv[:, 192:256]
    return jnp.concatenate([kv[:, :128], v0, v0, ones, v1, v1, ones], 1)


def _even_in_body(x, m, ins, outs, rope, cache_refs):
    w_ref, *tabs = ins
    aq_ref, akv_ref, bq_ref, bk_ref, bv_ref, bo_ref, bg_ref = outs
    cache = cache_refs is not None

    h = (x * (1.0 + m[4:5]) + m[3:4]).astype(BF16)
    offs = [0]
    for c in EVEN_COLS:
        offs.append(offs[-1] + c)
    proj = lambda k: _dot(h, w_ref[:, offs[k]:offs[k + 1]])

    aq = proj(0)
    akv = proj(1)
    if cache:
        _store_cache(*cache_refs, akv)
    if rope:
        cos, s_up, s_dn = (t[...] for t in tabs)
        aq = _rope(aq, cos, s_up, s_dn)
        akv = jnp.concatenate([_rope(akv[:, :128], cos, s_up, s_dn), akv[:, 128:]], 1)
    aq_ref[...] = (aq * QK_SCALE).astype(BF16)
    akv_ref[...] = _kv_with_ones(akv).astype(BF16)
    bq_ref[...] = proj(2).astype(BF16)
    bk_ref[...] = (proj(3) * (B_DIM ** -0.5)).astype(BF16)
    bv_ref[...] = proj(4).astype(BF16)
    tail = proj(5)
    n_gates = 4 * B_HEADS
    bo_ref[...] = tail[:, n_gates:]
    bg_ref[...] = tail[:, :128]


def _rope_inputs(lay):
    return [pl.BlockSpec((lay.tm, 128), lay.rope_rows)] * 3, list(_rope_tables(lay.len_s))


def _even_in_io(lay, w):
    specs, tabs = _rope_inputs(lay)
    in_specs = [pl.BlockSpec(w.shape, lambda i: (0, 0), pipeline_mode=pl.Buffered(1))] + specs
    outs = [(512, BF16), (KV_WIDTH, BF16), (512, BF16), (512, BF16), (512, BF16), (512, F32), (128, F32)]
    return in_specs, [w] + tabs, _in_proj_outputs(outs, lay)


ODD_COLS = (512, 512, 512, 512, 512, 512, 256)


def _head_rms(x, seg_ref, g):
    x2 = x * x
    hi = x2.astype(BF16)
    lo = (x2 - hi.astype(F32)).astype(BF16)
    w = seg_ref.shape[0]
    ms = [_dot(hi[:, c:c + w], seg_ref[...]) + _dot(lo[:, c:c + w], seg_ref[...]) for c in range(0, x.shape[1], w)]
    ms = ms[0] if len(ms) == 1 else jnp.concatenate(ms, 1)
    return x * lax.rsqrt(ms + 1e-6) * g


def _odd_in_body(x, m, ins, outs, rope, cache_refs):
    w_ref, segq_ref, segk_ref, qn_ref, kn_ref, *tabs = ins
    q_ref, ff_ref, fb_ref, v_ref, cg_ref, dq_ref, dkv_ref = outs
    cache = cache_refs is not None

    h = (x * (1.0 + m[4:5]) + m[3:4]).astype(BF16)
    offs = [0]
    for c in ODD_COLS:
        offs.append(offs[-1] + c)
    proj = lambda k: _dot(h, w_ref[:, offs[k]:offs[k + 1]])

    q_ref[...] = _silu(proj(0)).astype(BF16)
    ff_ref[...] = proj(1)
    fb_ref[...] = proj(2)
    v_ref[...] = proj(3).astype(BF16)
    cg_ref[...] = proj(4)
    dq = _head_rms(proj(5), segq_ref, qn_ref[...])
    dkv = proj(6)
    dk = _head_rms(dkv[:, :128], segk_ref, kn_ref[...])
    dv = dkv[:, 128:]
    if cache:
        _store_cache(*cache_refs, jnp.concatenate([dk, dv], 1))
    if rope:
        cos, s_up, s_dn = (t[...] for t in tabs)
        dq = _rope(dq, cos, s_up, s_dn)
        dk = _rope(dk, cos, s_up, s_dn)
    dq_ref[...] = (dq * QK_SCALE).astype(BF16)
    dkv_ref[...] = _kv_with_ones(jnp.concatenate([dk, dv], 1)).astype(BF16)


def _segment_mean_matrix(width):
    r = jnp.arange(width) // HEAD_DIM
    return jnp.where(r[:, None] == r[None, :], 1.0 / HEAD_DIM, 0.0).astype(BF16)


def _odd_in_io(lay, w, q_norm, k_norm):
    const = lambda a: pl.BlockSpec(a.shape, lambda i: (0, 0))
    segq, segk = _segment_mean_matrix(256), _segment_mean_matrix(128)
    qn = jnp.tile(q_norm, D_HEADS).reshape(1, 512)
    kn = jnp.tile(k_norm, D_KV).reshape(1, 128)
    specs, tabs = _rope_inputs(lay)
    in_specs = [pl.BlockSpec(w.shape, lambda i: (0, 0), pipeline_mode=pl.Buffered(1)),
                const(segq), const(segk), const(qn), const(kn)] + specs
    outs = [(512, BF16), (512, F32), (512, F32), (512, BF16), (512, F32), (512, BF16), (KV_WIDTH, BF16)]
    return in_specs, [w, segq, segk, qn, kn] + tabs, _in_proj_outputs(outs, lay)


def _attn_kernel(*refs, seq_len, tq, q_blocks, n_ctx, banded, has_sink):
    it = iter(refs)
    q_ref, kv_ref = next(it), next(it)
    ckv_ref = next(it) if n_ctx else None
    sink_ref = next(it) if has_sink else None
    o_ref = next(it)
    groups = A_HEADS // A_KV
    rows = groups * tq
    low_half = lax.broadcasted_iota(jnp.int32, (tq, 2 * HEAD_DIM), 1) < HEAD_DIM

    def run_kv_head(blk, kh):
        r0 = blk * tq
        j = pl.program_id(1) * q_blocks + blk
        kcol = slice(kh * HEAD_DIM, (kh + 1) * HEAD_DIM)
        vcol = slice(128 + kh * 256, 128 + (kh + 1) * 256)
        qs = jnp.concatenate([q_ref[r0:r0 + tq, (kh * groups + g) * HEAD_DIM:(kh * groups + g + 1) * HEAD_DIM]
                              for g in range(groups)], axis=0)
        tiles = []
        if banded:
            span = tq + 2 * WINDOW
            start = pl.multiple_of(jnp.clip(j * tq - WINDOW, 0, seq_len - span), WINDOW)
            qpos = j * tq + (lax.broadcasted_iota(jnp.int32, (rows, span), 0) & (tq - 1))
            kpos = start + lax.broadcasted_iota(jnp.int32, (rows, span), 1)
            band = jnp.abs(kpos - qpos) <= WINDOW
            tiles.append((kv_ref[pl.ds(start, span), kcol], kv_ref[pl.ds(start, span), vcol], band))
        else:
            tk = min(ATT_KEY_TILE, seq_len)
            for t in range(seq_len // tk):
                tiles.append((kv_ref[t * tk:(t + 1) * tk, kcol], kv_ref[t * tk:(t + 1) * tk, vcol], None))
        scores = [_dot_nt(qs, k_t) for k_t, _, _ in tiles]
        if n_ctx:
            tiles.append((None, ckv_ref[0, kh, HEAD_DIM:, :], None))
            scores.append(_dot(qs, ckv_ref[0, kh, :HEAD_DIM, :]))
        yield
        scores = [s if mask is None else jnp.where(mask, s, NEG_INF) for s, (_, _, mask) in zip(scores, tiles)]
        m = functools.reduce(jnp.maximum, [jnp.max(s, -1, keepdims=True) for s in scores])
        if has_sink:
            sink = jnp.concatenate([jnp.full((tq, 1), sink_ref[kh * groups + g] * LOG2_E, F32)
                                    for g in range(groups)], 0)
            m = jnp.maximum(m, sink)
        pv = functools.reduce(lambda a, b: a + b,
                              [(_dot_nt if k_t is None else _dot)(jnp.exp2(s - m).astype(BF16), v_t)
                               for s, (k_t, v_t, _) in zip(scores, tiles)])
        yield
        total = pv[:, 2 * HEAD_DIM:]
        if has_sink:
            total = total + jnp.exp2(sink - m)
        o = pv[:, :2 * HEAD_DIM] / total
        for g in range(0, groups, 2):
            pair = jnp.where(low_half, o[g * tq:(g + 1) * tq], o[(g + 1) * tq:(g + 2) * tq])
            c0 = (kh * groups + g) * HEAD_DIM
            o_ref[r0:r0 + tq, c0:c0 + 2 * HEAD_DIM] = pair.astype(BF16)

    run = _run_staggered if seq_len + n_ctx > 2 * ATT_KEY_TILE else _run_interleaved
    run([run_kv_head(blk, kh) for blk in range(q_blocks) for kh in range(A_KV)])


def _attention(q, kv, ctx_kv, sink, batch, seq_len, row0, tq, q_blocks, banded):
    n_ctx = 0 if ctx_kv is None else ctx_kv.shape[-1]
    per_seq = seq_len // (tq * q_blocks)
    q0, kv0 = row0 // (tq * q_blocks), row0 // seq_len
    in_specs = [pl.BlockSpec((q_blocks * tq, 512), lambda b, j: (q0 + b * per_seq + j, 0)),
                pl.BlockSpec((seq_len, KV_WIDTH), lambda b, j: (kv0 + b, 0))]
    args = [q, kv]
    if n_ctx:
        in_specs.append(pl.BlockSpec((1,) + ctx_kv.shape[1:], lambda b, j: (b, 0, 0, 0)))
        args.append(ctx_kv)
    if sink is not None:
        in_specs.append(pl.BlockSpec(memory_space=pltpu.SMEM))
        args.append(sink.reshape(-1).astype(F32))
    return pl.pallas_call(
        functools.partial(_attn_kernel, seq_len=seq_len, tq=tq, q_blocks=q_blocks, n_ctx=n_ctx, banded=banded,
                          has_sink=sink is not None),
        grid=(batch, per_seq),
        in_specs=in_specs,
        out_specs=pl.BlockSpec((q_blocks * tq, 512), lambda b, j: (b * per_seq + j, 0)),
        out_shape=jax.ShapeDtypeStruct((batch * seq_len, 512), BF16),
        compiler_params=_cparams(("parallel", "arbitrary")),
        name="gqa_attention",
    )(*args)


def _log_sigmoid(x):
    return jnp.minimum(x, 0.0) - jnp.log1p(jnp.exp(-jnp.abs(x)))


def _mlstm_kernel(*refs, seq_len, has_init, emit_state):
    it = iter(refs)
    q_ref, k_ref, v_ref, g_ref, bo_ref, gb_ref, ng_ref = (next(it) for _ in range(7))
    c0_ref, n0_ref, m0_ref = (next(it), next(it), next(it)) if has_init else (None, None, None)
    y_ref = next(it)
    co_ref, no_ref, mo_ref = (next(it), next(it), next(it)) if emit_state else (None, None, None)
    c_scr, n_scr, m_scr, hf_scr, hb_scr, ig_buf, bc_buf, ut_buf = (next(it) for _ in range(8))

    t_len = MLSTM_T
    n_chunks = seq_len // t_len
    n_streams = 2 * B_HEADS

    for s in range(n_streams):
        if has_init:
            c_scr[s] = c0_ref[0, s // B_HEADS, s % B_HEADS]
        else:
            c_scr[s] = jnp.zeros((B_DIM, B_DIM), F32)
    if has_init:
        n_scr[...] = n0_ref[0]
        m_scr[...] = m0_ref[0]
    else:
        n_scr[...] = jnp.zeros_like(n_scr)
        m_scr[...] = jnp.zeros_like(m_scr)

    ri = lax.broadcasted_iota(jnp.int32, (t_len, t_len), 0)
    ci = lax.broadcasted_iota(jnp.int32, (t_len, t_len), 1)
    lower = ci <= ri
    upper = ci >= ri
    tri_f = jnp.where(lower, 1.0, 0.0).astype(BF16)
    tri_b = jnp.where(upper, 1.0, 0.0).astype(BF16)
    gate_bias = gb_ref[...]
    sel_row = lax.broadcasted_iota(jnp.int32, (128, B_HEADS * B_DIM), 0)
    sel_head = lax.broadcasted_iota(jnp.int32, (128, B_HEADS * B_DIM), 1) // B_DIM
    gate_select = []
    for direction in range(2):
        i_base = 2 * direction * B_HEADS
        gate_select.append((jnp.where(sel_row == i_base + sel_head, 1.0, 0.0).astype(BF16),
                            jnp.where(sel_row == i_base + B_HEADS + sel_head, 1.0, 0.0).astype(BF16)))

    def chunk_rows(c, direction):
        chunk = c if direction == 0 else n_chunks - 1 - c
        return pl.ds(pl.multiple_of(chunk * t_len, t_len), t_len)

    def prepare_gates(c, direction):
        slot = c % 2
        rows = chunk_rows(jnp.minimum(c, n_chunks - 1), direction)
        tri = tri_f if direction == 0 else tri_b
        sel_i, sel_f = gate_select[direction]
        gates = g_ref[rows, :] + gate_bias
        hi, mid, lo = _split3(_log_sigmoid(gates))
        b3 = _dot(tri, jnp.concatenate([hi, mid, lo], 1))
        ig_buf[slot, direction] = _dot_sel_rhs(gates, sel_i)
        yield
        bc_all = b3[:, :128] + b3[:, 128:256] + b3[:, 256:]
        bc_rep = _dot_sel_rhs(bc_all, sel_f)
        ut_buf[slot, direction] = (gates - pltpu.roll(bc_all, 128 - B_HEADS, 1)).T
        yield
        bc_buf[slot, direction] = bc_rep

    def run_direction(c, direction):
        rows = chunk_rows(c, direction)
        slot = c % 2
        causal = lower if direction == 0 else upper
        last = t_len - 1 if direction == 0 else 0
        h_out = hf_scr if direction == 0 else hb_scr
        ig_rep, bc_rep, u_t = ig_buf[slot, direction], bc_buf[slot, direction], ut_buf[slot, direction]
        heads = []
        for hd in range(B_HEADS):
            s = direction * B_HEADS + hd
            col = slice(hd * B_DIM, (hd + 1) * B_DIM)
            qc, kc, vc = q_ref[rows, col], k_ref[rows, col], v_ref[rows, col]
            c_prev = c_scr[s]
            n_prev = n_scr[s:s + 1, :]
            qk = _dot_nt(qc, kc)
            q_state = _dot(qc, c_prev.astype(BF16))
            qn = _dot_nt(qc, jnp.broadcast_to(n_prev, (B_DIM, B_DIM)).astype(BF16))
            heads.append((s, col, qc, kc, vc, c_prev, n_prev, qk, q_state, qn))
        yield
        staged = []
        for (s, col, qc, kc, vc, c_prev, n_prev, qk, q_state, qn) in heads:
            i_col = 2 * direction * B_HEADS + (s % B_HEADS)
            ig, bc = ig_rep[:, col], bc_rep[:, col]
            dmat = jnp.where(causal, bc[:, :t_len] + u_t[i_col:i_col + 1, :], -jnp.inf)
            m_prev = m_scr[s:s + 1, :]
            inter = bc + m_prev
            m_t = jnp.maximum(inter, jnp.max(dmat, -1, keepdims=True))
            w = (jnp.exp(dmat - m_t[:, :t_len]) * qk).astype(BF16)
            wv = _dot(w, jnp.concatenate([vc, jnp.ones_like(vc)], 1))
            b_tot = bc[last:last + 1, :]
            g = b_tot - bc + ig
            m_new = jnp.maximum(b_tot + m_prev, jnp.max(g, 0, keepdims=True))
            kw = kc.astype(F32) * jnp.exp(g - m_new)
            upd = _dot_tn(kw.astype(BF16), vc)
            decay = jnp.exp(b_tot + m_prev - m_new)
            staged.append((s, col, inter, m_t, c_prev, n_prev, q_state, qn, wv, m_new, kw, upd, decay))
        yield
        for (s, col, inter, m_t, c_prev, n_prev, q_state, qn, wv, m_new, kw, upd, decay) in staged:
            a = jnp.exp(inter - m_t)
            num = a * q_state + wv[:, :B_DIM]
            den = a * qn + wv[:, B_DIM:]
            h_out[rows, col] = num / jnp.maximum(jnp.abs(den), jnp.exp(-m_t))
            c_scr[s] = decay * c_prev + upd
            n_scr[s:s + 1, :] = decay * n_prev + jnp.sum(kw, 0, keepdims=True)
            m_scr[s:s + 1, :] = m_new

    def chained(c, direction):
        yield from prepare_gates(c, direction)
        yield
        yield from run_direction(c, direction)

    def step(c, carry):
        _run_interleaved([chained(c, 0), chained(c, 1)])
        return carry

    lax.fori_loop(0, n_chunks, step, 0, unroll=min(SCAN_UNROLL, n_chunks))

    blk = min(256, seq_len)
    ng = ng_ref[...]

    def finish(i, carry):
        rows = pl.ds(pl.multiple_of(i * blk, blk), blk)
        hsum = hf_scr[rows, :] + hb_scr[rows, :]
        gate = jax.nn.sigmoid(bo_ref[rows, :])
        parts = [_rms_lastdim(hsum[:, hd * B_DIM:(hd + 1) * B_DIM], ng[:, hd * B_DIM:(hd + 1) * B_DIM])
                 for hd in range(B_HEADS)]
        y_ref[rows, :] = (gate * jnp.concatenate(parts, 1)).astype(BF16)
        return carry

    lax.fori_loop(0, seq_len // blk, finish, 0)

    if emit_state:
        for s in range(n_streams):
            co_ref[0, s // B_HEADS, s % B_HEADS] = c_scr[s]
        no_ref[0] = n_scr[...]
        mo_ref[0] = m_scr[...]


def _mlstm(q, k, v, gates, bo, gate_bias, norm_g, init, batch, seq_len, row0, emit_state):
    n = batch * seq_len
    seq0 = row0 // seq_len
    in_row = lambda width: pl.BlockSpec((seq_len, width), lambda b: (seq0 + b, 0))
    row = lambda width: pl.BlockSpec((seq_len, width), lambda b: (b, 0))
    const = lambda a: pl.BlockSpec(a.shape, lambda b: (0,) * a.ndim)
    gb = jnp.zeros((1, 128), F32).at[0, :4 * B_HEADS].set(gate_bias.reshape(-1))
    ng = norm_g.reshape(1, B_HEADS * B_DIM)
    in_specs = [in_row(512), in_row(512), in_row(512), in_row(128), in_row(512), const(gb), const(ng)]
    args = [q, k, v, gates, bo, gb, ng]
    n_streams = 2 * B_HEADS
    if init is not None:
        c0, n0, m0 = init
        in_specs += [pl.BlockSpec((1, 2, B_HEADS, B_DIM, B_DIM), lambda b: (b, 0, 0, 0, 0)),
                     pl.BlockSpec((1, n_streams, B_DIM), lambda b: (b, 0, 0)),
                     pl.BlockSpec((1, n_streams, B_DIM), lambda b: (b, 0, 0))]
        args += [c0, n0.reshape(batch, n_streams, B_DIM),
                 jnp.broadcast_to(m0.reshape(batch, n_streams, 1), (batch, n_streams, B_DIM))]
    out_specs = [row(512)]
    out_shape = [jax.ShapeDtypeStruct((n, 512), BF16)]
    if emit_state:
        out_specs += [pl.BlockSpec((1, 2, B_HEADS, B_DIM, B_DIM), lambda b: (b, 0, 0, 0, 0)),
                      pl.BlockSpec((1, n_streams, B_DIM), lambda b: (b, 0, 0)),
                      pl.BlockSpec((1, n_streams, B_DIM), lambda b: (b, 0, 0))]
        out_shape += [jax.ShapeDtypeStruct((batch, 2, B_HEADS, B_DIM, B_DIM), F32),
                      jax.ShapeDtypeStruct((batch, n_streams, B_DIM), F32),
                      jax.ShapeDtypeStruct((batch, n_streams, B_DIM), F32)]
    return pl.pallas_call(
        functools.partial(_mlstm_kernel, seq_len=seq_len, has_init=init is not None, emit_state=emit_state),
        grid=(batch,),
        in_specs=in_specs,
        out_specs=out_specs,
        out_shape=out_shape,
        scratch_shapes=[pltpu.VMEM((n_streams, B_DIM, B_DIM), F32),
                        pltpu.VMEM((n_streams, B_DIM), F32),
                        pltpu.VMEM((n_streams, B_DIM), F32),
                        pltpu.VMEM((seq_len, 512), F32),
                        pltpu.VMEM((seq_len, 512), F32),
                        pltpu.VMEM((2, 2, MLSTM_T, B_HEADS * B_DIM), F32),
                        pltpu.VMEM((2, 2, MLSTM_T, B_HEADS * B_DIM), F32),
                        pltpu.VMEM((2, 2, 128, MLSTM_T), F32)],
        compiler_params=_cparams(("parallel",)),
        name="mlstm_scan",
    )(*args)


def _hgrn_kernel(*refs, seq_len, layer, has_init, emit_state):
    it = iter(refs)
    q_ref, ff_ref, fb_ref, v_ref, cg_ref, lbl_ref, ng_ref = (next(it) for _ in range(7))
    s0_ref = next(it) if has_init else None
    y_ref = next(it)
    so_ref = next(it) if emit_state else None
    st_scr, of_scr, ob_scr, ak_scr, vf_scr = (next(it) for _ in range(5))

    t_len = HGRN_T
    n_sub = t_len // SUB
    n_chunks = seq_len // t_len
    n_streams = 2 * C_HEADS

    logits = lbl_ref[...]
    e = jnp.exp(logits - jnp.max(logits, 0, keepdims=True))
    sm = e / jnp.sum(e, 0, keepdims=True)
    lb = jnp.sum(sm[0:layer + 1], 0, keepdims=True) - sm[0:1]

    for s in range(n_streams):
        if has_init:
            st_scr[s] = s0_ref[0, s // C_HEADS, s % C_HEADS].T
        else:
            st_scr[s] = jnp.zeros((C_DV, C_DK), F32)

    ri = lax.broadcasted_iota(jnp.int32, (t_len, t_len), 0)
    ci = lax.broadcasted_iota(jnp.int32, (t_len, t_len), 1)
    tri_f = jnp.where(ci <= ri, 1.0, 0.0).astype(BF16)
    tri_b = jnp.where(ci >= ri, 1.0, 0.0).astype(BF16)
    sub_row = lax.broadcasted_iota(jnp.int32, (SUB, C_DK), 0)
    ones_dk = jnp.ones((C_DK, C_DK), BF16)

    def run_stream(c, direction, hd):
        chunk = c if direction == 0 else n_chunks - 1 - c
        rows = pl.ds(pl.multiple_of(chunk * t_len, t_len), t_len)
        f_ref = ff_ref if direction == 0 else fb_ref
        tri = tri_f if direction == 0 else tri_b
        last = t_len - 1 if direction == 0 else 0
        o_out = of_scr if direction == 0 else ob_scr
        s = direction * C_HEADS + hd
        col = slice(hd * C_DK, (hd + 1) * C_DK)
        lbh = lb[:, col]
        f = lbh + (1.0 - lbh) * jax.nn.sigmoid(f_ref[rows, col])
        log_k = jnp.log2(1.0 - f)
        hi, mid, lo = _split3(jnp.log2(f))
        a3 = _dot(tri, jnp.concatenate([hi, mid, lo], 1))
        yield
        a_cum = a3[:, :C_DK] + a3[:, C_DK:2 * C_DK] + a3[:, 2 * C_DK:]
        a_key = a_cum - log_k
        ak_scr[s] = a_key
        a_tot = a_cum[last:last + 1, :]
        qf = q_ref[rows, col].astype(F32)
        vc = v_ref[rows, col]
        vf_scr[s] = vc.astype(F32)
        st = st_scr[s]
        inter = _dot_nt((qf * jnp.exp2(a_cum)).astype(BF16), st.astype(BF16))
        upd = _dot_tn(vc, jnp.exp2(a_tot - a_key).astype(BF16))
        ps = []
        for blk in range(n_sub):
            b0 = blk * SUB
            a_i, q_i = a_cum[b0:b0 + SUB], qf[b0:b0 + SUB]
            for j in range(SUB):
                seen = (sub_row >= j) if direction == 0 else (sub_row <= j)
                key_row = ak_scr[s, pl.ds(b0 + j, SUB, stride=0), :]
                ps.append(jnp.where(seen, jnp.exp2(a_i - key_row), 0.0) * q_i)
        att = _dot(jnp.concatenate(ps, 0).astype(BF16), ones_dk)
        off = []
        for blk in range(n_sub):
            b0 = blk * SUB
            if direction == 0 and blk > 0:
                a_ref, kr = a_cum[b0 - 1:b0], slice(0, b0)
            elif direction == 1 and blk < n_sub - 1:
                a_ref, kr = a_cum[b0 + SUB:b0 + SUB + 1], slice(b0 + SUB, t_len)
            else:
                off.append(None)
                continue
            qt = (qf[b0:b0 + SUB] * jnp.exp2(a_cum[b0:b0 + SUB] - a_ref)).astype(BF16)
            kt = jnp.exp2(a_ref - a_key[kr]).astype(BF16)
            off.append((_dot_nt(qt, kt), kr))
        yield
        st_scr[s] = jnp.exp2(a_tot) * st + upd
        outs = []
        for blk in range(n_sub):
            b0 = blk * SUB
            o_i = inter[b0:b0 + SUB]
            for j in range(SUB):
                r = (blk * SUB + j) * SUB
                o_i = o_i + att[r:r + SUB] * vf_scr[s, pl.ds(b0 + j, SUB, stride=0), :]
            if off[blk] is not None:
                att_off, kr = off[blk]
                outs.append((o_i, _dot(att_off.astype(BF16), vc[kr])))
            else:
                outs.append((o_i, None))
        yield
        o_out[rows, col] = jnp.concatenate([o_i if o_off is None else o_i + o_off for o_i, o_off in outs], 0)

    def step(c, carry):
        _run_interleaved([run_stream(c, direction, hd) for direction in range(2) for hd in range(C_HEADS)])
        return carry

    lax.fori_loop(0, n_chunks, step, 0, unroll=min(SCAN_UNROLL, n_chunks))

    blk_rows = min(256, seq_len)
    ng = ng_ref[...]

    def finish(i, carry):
        rows = pl.ds(pl.multiple_of(i * blk_rows, blk_rows), blk_rows)
        osum = of_scr[rows, :] + ob_scr[rows, :]
        parts = [_rms_lastdim(osum[:, hd * C_DV:(hd + 1) * C_DV], ng[:, hd * C_DV:(hd + 1) * C_DV])
                 for hd in range(C_HEADS)]
        y_ref[rows, :] = (jnp.concatenate(parts, 1) * _silu(cg_ref[rows, :])).astype(BF16)
        return carry

    lax.fori_loop(0, seq_len // blk_rows, finish, 0)

    if emit_state:
        for s in range(n_streams):
            so_ref[0, s // C_HEADS, s % C_HEADS] = st_scr[s].T


def _hgrn(q, ff, fb, v, cg, lb_logits, norm_g, layer, init, batch, seq_len, row0, emit_state):
    n = batch * seq_len
    seq0 = row0 // seq_len
    in_row = lambda width: pl.BlockSpec((seq_len, width), lambda b: (seq0 + b, 0))
    row = lambda width: pl.BlockSpec((seq_len, width), lambda b: (b, 0))
    const = lambda a: pl.BlockSpec(a.shape, lambda b: (0,) * a.ndim)
    ng = norm_g.reshape(1, C_HEADS * C_DV)
    state_spec = pl.BlockSpec((1, 2, C_HEADS, C_DK, C_DV), lambda b: (b, 0, 0, 0, 0))
    in_specs = [in_row(512)] * 5 + [const(lb_logits), const(ng)]
    args = [q, ff, fb, v, cg, lb_logits, ng]
    if init is not None:
        in_specs.append(state_spec)
        args.append(init)
    out_specs = [row(512)]
    out_shape = [jax.ShapeDtypeStruct((n, 512), BF16)]
    if emit_state:
        out_specs.append(state_spec)
        out_shape.append(jax.ShapeDtypeStruct((batch, 2, C_HEADS, C_DK, C_DV), F32))
    return pl.pallas_call(
        functools.partial(_hgrn_kernel, seq_len=seq_len, layer=layer, has_init=init is not None,
                          emit_state=emit_state),
        grid=(batch,),
        in_specs=in_specs,
        out_specs=out_specs,
        out_shape=out_shape,
        scratch_shapes=[pltpu.VMEM((2 * C_HEADS, C_DV, C_DK), F32),
                        pltpu.VMEM((seq_len, 512), F32),
                        pltpu.VMEM((seq_len, 512), F32),
                        pltpu.VMEM((2 * C_HEADS, HGRN_T, C_DK), F32),
                        pltpu.VMEM((2 * C_HEADS, HGRN_T, C_DV), F32)],
        compiler_params=_cparams(("parallel",)),
        name="hgrn2_scan",
    )(*args)


def _prep_ctx_kv(k_ctx, v_ctx):
    k_t = jnp.transpose(k_ctx, (0, 2, 3, 1)).astype(BF16)
    v_t = jnp.transpose(v_ctx, (0, 2, 3, 1)).astype(BF16)
    ones = jnp.ones(k_t.shape[:2] + (128, k_t.shape[3]), BF16)
    return jnp.concatenate([k_t, v_t, v_t, ones], 2)


def kernel(x_prompt, x_sample, c, cache_a_k, cache_a_v, state_b_C, state_b_n, state_b_m, state_c_S, cache_d_k, cache_d_v, c_ctx, ada_w, ada_b, ln_g, ln_b, ffn_w1, ffn_w3, ffn_w2, w_in_even, w_out_even, a_sink, b_gate_bias, b_norm_g, w_in_odd, w_out_odd, c_lb_logits, c_norm_g, d_q_norm, d_k_norm):
    batch_p, len_p, d = x_prompt.shape
    batch_s, len_s, _ = x_sample.shape

    cvec = jnp.concatenate([c_ctx[None], c, jnp.zeros((MOD_ROWS - 1 - batch_s, d), F32)], 0)
    mod_all = _modulation(cvec, ada_w, ada_b)

    lay = _Layout(batch_p, len_p, batch_s, len_s)
    new = {}
    ffn_stacks = (ffn_w1, ffn_w3, ffn_w2)
    ffn_w = tuple(a[0:1, 0:1].astype(BF16) for a in ffn_stacks)
    x = [x_prompt.reshape(lay.n_p, d), x_sample.reshape(lay.n_s, d)]

    for l in range(DEPTH):
        mod = mod_all[l].reshape(MOD_ROWS, 9, d)
        i = l // 2
        if l % 2 == 0:
            in_proj = (_even_in_body, _even_in_io, dict(w=w_in_even[i].astype(BF16)))
            w_out = w_out_even[i].astype(BF16)
        else:
            in_proj = (_odd_in_body, _odd_in_io,
                       dict(w=w_in_odd[i].astype(BF16), q_norm=d_q_norm[i], k_norm=d_k_norm[i]))
            w_out = w_out_odd[i].astype(BF16)

        x_all, *proj, k_cache, v_cache, w1, w3, w2 = _ffn(x, mod, 0, ln_g[l, 0], ln_b[l, 0], ffn_w, lay,
                                                         in_proj=in_proj, next_weights=ffn_stacks + (l, 1))
        ffn_w = (w1, w3, w2)
        prompt = dict(batch=batch_p, seq_len=len_p, row0=0)
        sample = dict(batch=batch_s, seq_len=len_s, row0=lay.n_p)
        if l % 2 == 0:
            aq, akv, bq, bk, bv, bo, bg = proj
            new["a_k"], new["a_v"] = _cache_layout(k_cache), _cache_layout(v_cache)
            scan = (bq, bk, bv, bg, bo, b_gate_bias[i], b_norm_g[i])
            ya_p = _attention(aq, akv, None, a_sink[i], **prompt, tq=Q_BLOCK, q_blocks=len_p // Q_BLOCK, banded=False)
            yb_p, c_new, n_new, m_new = _mlstm(*scan, None, **prompt, emit_state=True)
            new["b_C"] = c_new[:, None]
            new["b_n"] = n_new.reshape(batch_p, 1, 2, B_HEADS, B_DIM)
            new["b_m"] = m_new[:, :, 0].reshape(batch_p, 1, 2, B_HEADS)
            ya_s = _attention(aq, akv, _prep_ctx_kv(cache_a_k[:, i], cache_a_v[:, i]), a_sink[i], **sample,
                              tq=Q_BLOCK, q_blocks=4, banded=True)
            yb_s, = _mlstm(*scan, (state_b_C[:, i], state_b_n[:, i], state_b_m[:, i]), **sample, emit_state=False)
        else:
            cq, ff, fb, cv, cg, dq, dkv = proj
            new["d_k"], new["d_v"] = _cache_layout(k_cache), _cache_layout(v_cache)
            scan = (cq, ff, fb, cv, cg, c_lb_logits, c_norm_g[i], l)
            ya_p, s_new = _hgrn(*scan, None, **prompt, emit_state=True)
            new["c_S"] = s_new[:, None]
            yb_p = _attention(dq, dkv, None, None, **prompt, tq=Q_BLOCK, q_blocks=len_p // Q_BLOCK, banded=False)
            ya_s, = _hgrn(*scan, state_c_S[:, i], **sample, emit_state=False)
            yb_s = _attention(dq, dkv, _prep_ctx_kv(cache_d_k[:, i], cache_d_v[:, i]), None, **sample,
                              tq=Q_BLOCK, q_blocks=2, banded=False)
        last = l == DEPTH - 1
        x = _ffn([x_all], mod, 2, ln_g[l, 2], ln_b[l, 2], ffn_w, lay,
                 mixer=(ya_p, yb_p, ya_s, yb_s, w_out, ln_g[l, 1], ln_b[l, 1]), split_out=last,
                 next_weights=None if last else ffn_stacks + (l + 1, 0))
        if not last:
            *x, w1, w3, w2 = x
            ffn_w = (w1, w3, w2)

    y_prompt, y_sample = x
    return (y_prompt.reshape(batch_p, len_p, d), y_sample.reshape(batch_s, len_s, d), new["a_k"], new["a_v"],
            new["b_C"], new["b_n"], new["b_m"], new["c_S"], new["d_k"], new["d_v"])
```

```python
import functools

import jax
import jax.numpy as jnp
import numpy as np
from jax import lax
from jax.experimental import pallas as pl
from jax.experimental.pallas import tpu as pltpu

F32 = jnp.float32
BF16 = jnp.bfloat16

DEPTH = 2
GRID_W = 64
HEAD_DIM = 64
A_HEADS = 8
A_KV = 2
WINDOW = 128
B_HEADS = 4
B_DIM = 128
C_HEADS = 4
C_DK = 128
C_DV = 128
D_HEADS = 8
D_KV = 2
Q_BLOCK = 128
ROPE_THETA = 10000.0
ALPHA = (2 * DEPTH) ** 0.25
NEG_INF = -1e30
LOG2_E = 1.4426950408889634
QK_SCALE = HEAD_DIM ** -0.5 * LOG2_E

MOD_ROWS = 8
FF_CHUNK = 256
FFN_TILE = 512
WEIGHT_CAST_STEPS = 16
MLSTM_T = 64
HGRN_T = 64
SUB = 8
SCAN_UNROLL = 8
MOD_TILE = 2304
ATT_KEY_TILE = 512
VMEM_LIMIT = 58 * 1024 * 1024


def _cparams(sem):
    return pltpu.CompilerParams(dimension_semantics=sem, vmem_limit_bytes=VMEM_LIMIT)


def _dot(a, b):
    return jnp.dot(a, b, preferred_element_type=F32)


def _dot_nt(a, b):
    return lax.dot_general(a, b, (((1,), (1,)), ((), ())), preferred_element_type=F32)


def _dot_tn(a, b):
    return lax.dot_general(a, b, (((0,), (0,)), ((), ())), preferred_element_type=F32)


def _split3(x):
    hi = x.astype(BF16)
    r1 = x - hi.astype(F32)
    mid = r1.astype(BF16)
    lo = (r1 - mid.astype(F32)).astype(BF16)
    return hi, mid, lo


def _dot_sel_rhs(x, sel):
    hi = x.astype(BF16)
    lo = (x - hi.astype(F32)).astype(BF16)
    r = _dot(jnp.concatenate([hi, lo], 0), sel)
    return r[:x.shape[0]] + r[x.shape[0]:]


def _run_interleaved(gens):
    live = list(gens)
    while live:
        nxt = []
        for g in live:
            try:
                next(g)
                nxt.append(g)
            except StopIteration:
                pass
        live = nxt


def _run_staggered(gens):
    pending, live = list(gens), []
    while pending or live:
        if pending:
            live.insert(0, pending.pop(0))
        nxt = []
        for g in live:
            try:
                next(g)
                nxt.append(g)
            except StopIteration:
                pass
        live = nxt


def _silu(x):
    return x * jax.nn.sigmoid(x)


def _layernorm(z, g, b):
    mu = jnp.mean(z, -1, keepdims=True)
    zc = z - mu
    var = jnp.mean(zc * zc, -1, keepdims=True)
    return zc * lax.rsqrt(var + 1e-5) * g + b


def _rms_lastdim(x, g):
    return x * lax.rsqrt(jnp.mean(x * x, -1, keepdims=True) + 1e-6) * g


class _Layout:
    def __init__(self, batch_p, len_p, batch_s, len_s):
        self.tm = FFN_TILE
        assert self.tm % len_p == 0 and len_s % self.tm == 0 and (batch_p * len_p) % self.tm == 0
        self.len_p, self.len_s = len_p, len_s
        self.n_p, self.n_s = batch_p * len_p, batch_s * len_s
        self.n = self.n_p + self.n_s
        self.steps_p = self.n_p // self.tm
        self.steps = self.n // self.tm

    def all_rows(self, i):
        return (i, 0)

    def prompt_rows(self, i):
        return (jnp.minimum(i, self.steps_p - 1), 0)

    def sample_rows(self, i):
        return (jnp.maximum(i - self.steps_p, 0), 0)

    def mod_row(self, i):
        req = jnp.maximum(i - self.steps_p, 0) // (self.len_s // self.tm)
        return (jnp.where(i < self.steps_p, 0, 1 + req), 0, 0)

    def rope_rows(self, i):
        return (jnp.maximum(i - self.steps_p, 0) % (self.len_s // self.tm), 0)


def _mod_kernel(c_ref, w_ref, b_ref, o_ref):
    s = _silu(c_ref[...]).astype(BF16)
    o_ref[0] = _dot(s, w_ref[0].astype(BF16)) + b_ref[0]


def _modulation(cvec, ada_w, ada_b):
    depth, d, n = ada_w.shape
    tn = MOD_TILE
    return pl.pallas_call(
        _mod_kernel,
        grid=(depth, n // tn),
        in_specs=[pl.BlockSpec((MOD_ROWS, d), lambda l, j: (0, 0)),
                  pl.BlockSpec((1, d, tn), lambda l, j: (l, 0, j)),
                  pl.BlockSpec((1, 1, tn), lambda l, j: (l, 0, j))],
        out_specs=pl.BlockSpec((1, MOD_ROWS, tn), lambda l, j: (l, 0, j)),
        out_shape=jax.ShapeDtypeStruct((depth, MOD_ROWS, n), F32),
        compiler_params=_cparams(("parallel", "parallel")),
        name="modulation",
    )(cvec, ada_w, ada_b.reshape(depth, 1, n))


def _ffn_kernel(*refs, j, nf, steps_p, n_x, n_out, mixer, in_proj, cast_next):
    it = iter(refs)
    x_refs = [next(it) for _ in range(n_x)]
    mod_ref = next(it)
    if mixer:
        ya_p, yb_p, ya_s, yb_s, wo_ref, g1_ref, b1_ref = (next(it) for _ in range(7))
    w1_ref, w3_ref, w2_ref, g_ref, b_ref = (next(it) for _ in range(5))
    proj_ins = [next(it) for _ in range(in_proj[1])] if in_proj else []
    cast_ins = [next(it) for _ in range(3)] if cast_next else []
    o_refs = [next(it) for _ in range(n_out)]
    proj_outs = [next(it) for _ in range(in_proj[2])] if in_proj else []
    cache_refs = (next(it), next(it)) if in_proj else None
    cast_outs = [next(it) for _ in range(3)] if cast_next else []
    h_ref, acc_ref, res_ref = next(it), next(it), next(it)

    if cast_next:
        @pl.when(pl.program_id(0) < WEIGHT_CAST_STEPS)
        def _():
            for src, dst in zip(cast_ins, cast_outs):
                dst[...] = src[...].astype(BF16)

    is_prompt = pl.program_id(0) < steps_p
    pick = lambda p_ref, s_ref: jnp.where(is_prompt, p_ref[...], s_ref[...])
    m = mod_ref[0]
    shift, scale, gate = m[3 * j:3 * j + 1], m[3 * j + 1:3 * j + 2], m[3 * j + 2:3 * j + 3]
    x = x_refs[0][...] if n_x == 1 else pick(*x_refs)
    if mixer:
        ya, yb = pick(ya_p, ya_s), pick(yb_p, yb_s)
        half = ya.shape[1]
        y = _dot(ya, wo_ref[:half, :]) + _dot(yb, wo_ref[half:, :])
        x = _layernorm(ALPHA * x + m[5:6] * y, g1_ref[...], b1_ref[...])
    res_ref[...] = x
    h_ref[...] = (x * (1.0 + scale) + shift).astype(BF16)
    for f in range(nf):
        cols = slice(f * FF_CHUNK, (f + 1) * FF_CHUNK)
        h = h_ref[...]
        u = (_silu(_dot(h, w1_ref[0, 0, :, cols])) * _dot(h, w3_ref[0, 0, :, cols])).astype(BF16)
        y = _dot(u, w2_ref[0, 0, cols, :])
        if f == 0:
            acc_ref[...] = y
        else:
            acc_ref[...] += y
    out = _layernorm(ALPHA * res_ref[...] + 0.5 * gate * acc_ref[...], g_ref[...], b_ref[...])
    if n_out == 1:
        o_refs[0][...] = out
    else:
        res_ref[...] = out

        @pl.when(is_prompt)
        def _():
            o_refs[0][...] = res_ref[...]

        @pl.when(jnp.logical_not(is_prompt))
        def _():
            o_refs[1][...] = res_ref[...]

    if in_proj:
        @pl.when(is_prompt)
        def _():
            in_proj[0](o_refs[0][...], m, proj_ins, proj_outs, rope=False, cache_refs=cache_refs)

        @pl.when(jnp.logical_not(is_prompt))
        def _():
            in_proj[0](o_refs[0][...], m, proj_ins, proj_outs, rope=True, cache_refs=None)


def _ffn(xs, mod, j, g, b, weights, lay, mixer=None, in_proj=None, split_out=False, next_weights=None):
    w1, w3, w2 = weights
    d = xs[0].shape[1]
    nf = w1.shape[-1] // FF_CHUNK
    tm = lay.tm
    spec = lambda width, index: pl.BlockSpec((tm, width), index)
    vec = pl.BlockSpec((1, d), lambda i: (0, 0))
    whole = lambda a: pl.BlockSpec(a.shape, lambda i: (0, 0, 0, 0), pipeline_mode=pl.Buffered(1))
    pair = (lay.prompt_rows, lay.sample_rows)
    in_specs = [spec(d, lay.all_rows)] if len(xs) == 1 else [spec(d, index) for index in pair]
    in_specs.append(pl.BlockSpec((1, 9, d), lay.mod_row))
    args = list(xs) + [mod]
    if mixer is not None:
        ya_p, yb_p, ya_s, yb_s, w_out, g1, b1 = mixer
        in_specs += [spec(ya_p.shape[1], pair[0]), spec(yb_p.shape[1], pair[0]),
                     spec(ya_s.shape[1], pair[1]), spec(yb_s.shape[1], pair[1]),
                     pl.BlockSpec(w_out.shape, lambda i: (0, 0), pipeline_mode=pl.Buffered(1)), vec, vec]
        args += [ya_p, yb_p, ya_s, yb_s, w_out, g1.reshape(1, d), b1.reshape(1, d)]
    in_specs += [whole(w1), whole(w3), whole(w2), vec, vec]
    args += [w1, w3, w2, g.reshape(1, d), b.reshape(1, d)]
    if split_out:
        out_specs = [spec(d, index) for index in pair]
        out_shape = [jax.ShapeDtypeStruct((lay.n_p, d), F32), jax.ShapeDtypeStruct((lay.n_s, d), F32)]
    else:
        out_specs, out_shape = [spec(d, lay.all_rows)], [jax.ShapeDtypeStruct((lay.n, d), F32)]
    n_out, proj = len(out_specs), None
    if in_proj is not None:
        body, io, kw = in_proj
        p_specs, p_args, (p_out_specs, p_out_shape) = io(lay, **kw)
        c_specs, c_shape = _cache_outputs(lay)
        in_specs, args = in_specs + p_specs, args + p_args
        out_specs, out_shape = out_specs + p_out_specs + c_specs, out_shape + p_out_shape + c_shape
        proj = (body, len(p_args), len(p_out_specs))
    if next_weights is not None:
        *stacks, layer, which = next_weights
        slab_index = lambda i: jnp.minimum(i, WEIGHT_CAST_STEPS - 1)
        for a in stacks:
            block = (1, 1, a.shape[2] // WEIGHT_CAST_STEPS, a.shape[3])
            in_specs.append(pl.BlockSpec(block, lambda i: (layer, which, slab_index(i), 0)))
            out_specs.append(pl.BlockSpec(block, lambda i: (0, 0, slab_index(i), 0)))
            out_shape.append(jax.ShapeDtypeStruct((1, 1) + a.shape[2:], BF16))
        args += stacks
    return pl.pallas_call(
        functools.partial(_ffn_kernel, j=j, nf=nf, steps_p=lay.steps_p, n_x=len(xs), n_out=n_out,
                          mixer=mixer is not None, in_proj=proj, cast_next=next_weights is not None),
        grid=(lay.steps,),
        in_specs=in_specs,
        out_specs=out_specs,
        out_shape=out_shape,
        scratch_shapes=[pltpu.VMEM((tm, d), BF16), pltpu.VMEM((tm, d), F32), pltpu.VMEM((tm, d), F32)],
        compiler_params=_cparams(("arbitrary",)),
        name="ffn_sublayer",
    )(*args)


def _rope_tables(length):
    t = np.arange(length)
    nf = HEAD_DIM // 4
    inv = ROPE_THETA ** (-np.arange(nf, dtype=np.float64) / nf)
    ang_r = (t // GRID_W)[:, None] * inv[None]
    ang_c = (t % GRID_W)[:, None] * inv[None]
    cr, sr, cc, sc = np.cos(ang_r), np.sin(ang_r), np.cos(ang_c), np.sin(ang_c)
    z = np.zeros_like(cr)
    cos = np.concatenate([cr, cr, cc, cc], 1)
    sin_up = np.concatenate([-sr, z, -sc, z], 1)
    sin_dn = np.concatenate([z, sr, z, sc], 1)
    two = lambda a: jnp.asarray(np.concatenate([a, a], 1), F32)
    return two(cos), two(sin_up), two(sin_dn)


def _rope128(x, cos, sin_up, sin_dn):
    nf = HEAD_DIM // 4
    return x * cos + pltpu.roll(x, 128 - nf, 1) * sin_up + pltpu.roll(x, nf, 1) * sin_dn


def _rope(x, cos, sin_up, sin_dn):
    parts = [_rope128(x[:, c:c + 128], cos, sin_up, sin_dn) for c in range(0, x.shape[1], 128)]
    return parts[0] if len(parts) == 1 else jnp.concatenate(parts, 1)


EVEN_COLS = (512, 256, 512, 512, 512, 528)


KV_WIDTH = 128 + A_KV * 256


def _store_cache(ck_ref, cv_ref, kv):
    seq_len = ck_ref.shape[2]
    for s in range(ck_ref.shape[0]):
        rows = slice(s * seq_len, (s + 1) * seq_len)
        ck_ref[s] = kv[rows, :128].T
        cv_ref[s] = kv[rows, 128:].T


def _in_proj_outputs(outs, lay):
    specs = [pl.BlockSpec((lay.tm, width), lay.all_rows) for width, _ in outs]
    shapes = [jax.ShapeDtypeStruct((lay.n, width), dtype) for width, dtype in outs]
    return specs, shapes


def _cache_outputs(lay):
    per_step = lay.tm // lay.len_p
    index = lambda i: lay.prompt_rows(i) + (0,)
    specs = [pl.BlockSpec((per_step, 128, lay.len_p), index)] * 2
    shapes = [jax.ShapeDtypeStruct((lay.n_p // lay.len_p, 128, lay.len_p), F32)] * 2
    return specs, shapes


def _cache_layout(c):
    nb, _, sl = c.shape
    return jnp.transpose(c.reshape(nb, 1, A_KV, HEAD_DIM, sl), (0, 1, 4, 2, 3))


def _kv_with_ones(kv):
    ones = jnp.ones((kv.shape[0], 128), kv.dtype)
    v0, v1 = kv[:, 128:192], kv[:, 192:256]
    return jnp.concatenate([kv[:, :128], v0, v0, ones, v1, v1, ones], 1)


def _even_in_body(x, m, ins, outs, rope, cache_refs):
    w_ref, *tabs = ins
    aq_ref, akv_ref, bq_ref, bk_ref, bv_ref, bo_ref, bg_ref = outs
    cache = cache_refs is not None

    h = (x * (1.0 + m[4:5]) + m[3:4]).astype(BF16)
    offs = [0]
    for c in EVEN_COLS:
        offs.append(offs[-1] + c)
    proj = lambda k: _dot(h, w_ref[:, offs[k]:offs[k + 1]])

    aq = proj(0)
    akv = proj(1)
    if cache:
        _store_cache(*cache_refs, akv)
    if rope:
        cos, s_up, s_dn = (t[...] for t in tabs)
        aq = _rope(aq, cos, s_up, s_dn)
        akv = jnp.concatenate([_rope(akv[:, :128], cos, s_up, s_dn), akv[:, 128:]], 1)
    aq_ref[...] = (aq * QK_SCALE).astype(BF16)
    akv_ref[...] = _kv_with_ones(akv).astype(BF16)
    bq_ref[...] = proj(2).astype(BF16)
    bk_ref[...] = (proj(3) * (B_DIM ** -0.5)).astype(BF16)
    bv_ref[...] = proj(4).astype(BF16)
    tail = proj(5)
    n_gates = 4 * B_HEADS
    bo_ref[...] = tail[:, n_gates:]
    bg_ref[...] = tail[:, :128]


def _rope_inputs(lay):
    return [pl.BlockSpec((lay.tm, 128), lay.rope_rows)] * 3, list(_rope_tables(lay.len_s))


def _even_in_io(lay, w):
    specs, tabs = _rope_inputs(lay)
    in_specs = [pl.BlockSpec(w.shape, lambda i: (0, 0), pipeline_mode=pl.Buffered(1))] + specs
    outs = [(512, BF16), (KV_WIDTH, BF16), (512, BF16), (512, BF16), (512, BF16), (512, F32), (128, F32)]
    return in_specs, [w] + tabs, _in_proj_outputs(outs, lay)


ODD_COLS = (512, 512, 512, 512, 512, 512, 256)


def _head_rms(x, seg_ref, g):
    x2 = x * x
    hi = x2.astype(BF16)
    lo = (x2 - hi.astype(F32)).astype(BF16)
    w = seg_ref.shape[0]
    ms = [_dot(hi[:, c:c + w], seg_ref[...]) + _dot(lo[:, c:c + w], seg_ref[...]) for c in range(0, x.shape[1], w)]
    ms = ms[0] if len(ms) == 1 else jnp.concatenate(ms, 1)
    return x * lax.rsqrt(ms + 1e-6) * g


def _odd_in_body(x, m, ins, outs, rope, cache_refs):
    w_ref, segq_ref, segk_ref, qn_ref, kn_ref, *tabs = ins
    q_ref, ff_ref, fb_ref, v_ref, cg_ref, dq_ref, dkv_ref = outs
    cache = cache_refs is not None

    h = (x * (1.0 + m[4:5]) + m[3:4]).astype(BF16)
    offs = [0]
    for c in ODD_COLS:
        offs.append(offs[-1] + c)
    proj = lambda k: _dot(h, w_ref[:, offs[k]:offs[k + 1]])

    q_ref[...] = _silu(proj(0)).astype(BF16)
    ff_ref[...] = proj(1)
    fb_ref[...] = proj(2)
    v_ref[...] = proj(3).astype(BF16)
    cg_ref[...] = proj(4)
    dq = _head_rms(proj(5), segq_ref, qn_ref[...])
    dkv = proj(6)
    dk = _head_rms(dkv[:, :128], segk_ref, kn_ref[...])
    dv = dkv[:, 128:]
    if cache:
        _store_cache(*cache_refs, jnp.concatenate([dk, dv], 1))
    if rope:
        cos, s_up, s_dn = (t[...] for t in tabs)
        dq = _rope(dq, cos, s_up, s_dn)
        dk = _rope(dk, cos, s_up, s_dn)
    dq_ref[...] = (dq * QK_SCALE).astype(BF16)
    dkv_ref[...] = _kv_with_ones(jnp.concatenate([dk, dv], 1)).astype(BF16)


def _segment_mean_matrix(width):
    r = jnp.arange(width) // HEAD_DIM
    return jnp.where(r[:, None] == r[None, :], 1.0 / HEAD_DIM, 0.0).astype(BF16)


def _odd_in_io(lay, w, q_norm, k_norm):
    const = lambda a: pl.BlockSpec(a.shape, lambda i: (0, 0))
    segq, segk = _segment_mean_matrix(256), _segment_mean_matrix(128)
    qn = jnp.tile(q_norm, D_HEADS).reshape(1, 512)
    kn = jnp.tile(k_norm, D_KV).reshape(1, 128)
    specs, tabs = _rope_inputs(lay)
    in_specs = [pl.BlockSpec(w.shape, lambda i: (0, 0), pipeline_mode=pl.Buffered(1)),
                const(segq), const(segk), const(qn), const(kn)] + specs
    outs = [(512, BF16), (512, F32), (512, F32), (512, BF16), (512, F32), (512, BF16), (KV_WIDTH, BF16)]
    return in_specs, [w, segq, segk, qn, kn] + tabs, _in_proj_outputs(outs, lay)


def _attn_kernel(*refs, seq_len, tq, q_blocks, n_ctx, banded, has_sink):
    it = iter(refs)
    q_ref, kv_ref = next(it), next(it)
    ckv_ref = next(it) if n_ctx else None
    sink_ref = next(it) if has_sink else None
    o_ref = next(it)
    groups = A_HEADS // A_KV
    rows = groups * tq
    low_half = lax.broadcasted_iota(jnp.int32, (tq, 2 * HEAD_DIM), 1) < HEAD_DIM

    def run_kv_head(blk, kh):
        r0 = blk * tq
        j = pl.program_id(1) * q_blocks + blk
        kcol = slice(kh * HEAD_DIM, (kh + 1) * HEAD_DIM)
        vcol = slice(128 + kh * 256, 128 + (kh + 1) * 256)
        qs = jnp.concatenate([q_ref[r0:r0 + tq, (kh * groups + g) * HEAD_DIM:(kh * groups + g + 1) * HEAD_DIM]
                              for g in range(groups)], axis=0)
        tiles = []
        if banded:
            span = tq + 2 * WINDOW
            start = pl.multiple_of(jnp.clip(j * tq - WINDOW, 0, seq_len - span), WINDOW)
            qpos = j * tq + (lax.broadcasted_iota(jnp.int32, (rows, span), 0) & (tq - 1))
            kpos = start + lax.broadcasted_iota(jnp.int32, (rows, span), 1)
            band = jnp.abs(kpos - qpos) <= WINDOW
            tiles.append((kv_ref[pl.ds(start, span), kcol], kv_ref[pl.ds(start, span), vcol], band))
        else:
            tk = min(ATT_KEY_TILE, seq_len)
            for t in range(seq_len // tk):
                tiles.append((kv_ref[t * tk:(t + 1) * tk, kcol], kv_ref[t * tk:(t + 1) * tk, vcol], None))
        scores = [_dot_nt(qs, k_t) for k_t, _, _ in tiles]
        if n_ctx:
            tiles.append((None, ckv_ref[0, kh, HEAD_DIM:, :], None))
            scores.append(_dot(qs, ckv_ref[0, kh, :HEAD_DIM, :]))
        yield
        scores = [s if mask is None else jnp.where(mask, s, NEG_INF) for s, (_, _, mask) in zip(scores, tiles)]
        m = functools.reduce(jnp.maximum, [jnp.max(s, -1, keepdims=True) for s in scores])
        if has_sink:
            sink = jnp.concatenate([jnp.full((tq, 1), sink_ref[kh * groups + g] * LOG2_E, F32)
                                    for g in range(groups)], 0)
            m = jnp.maximum(m, sink)
        pv = functools.reduce(lambda a, b: a + b,
                              [(_dot_nt if k_t is None else _dot)(jnp.exp2(s - m).astype(BF16), v_t)
                               for s, (k_t, v_t, _) in zip(scores, tiles)])
        yield
        total = pv[:, 2 * HEAD_DIM:]
        if has_sink:
            total = total + jnp.exp2(sink - m)
        o = pv[:, :2 * HEAD_DIM] / total
        for g in range(0, groups, 2):
            pair = jnp.where(low_half, o[g * tq:(g + 1) * tq], o[(g + 1) * tq:(g + 2) * tq])
            c0 = (kh * groups + g) * HEAD_DIM
            o_ref[r0:r0 + tq, c0:c0 + 2 * HEAD_DIM] = pair.astype(BF16)

    run = _run_staggered if seq_len + n_ctx > 2 * ATT_KEY_TILE else _run_interleaved
    run([run_kv_head(blk, kh) for blk in range(q_blocks) for kh in range(A_KV)])


def _attention(q, kv, ctx_kv, sink, batch, seq_len, row0, tq, q_blocks, banded):
    n_ctx = 0 if ctx_kv is None else ctx_kv.shape[-1]
    per_seq = seq_len // (tq * q_blocks)
    q0, kv0 = row0 // (tq * q_blocks), row0 // seq_len
    in_specs = [pl.BlockSpec((q_blocks * tq, 512), lambda b, j: (q0 + b * per_seq + j, 0)),
                pl.BlockSpec((seq_len, KV_WIDTH), lambda b, j: (kv0 + b, 0))]
    args = [q, kv]
    if n_ctx:
        in_specs.append(pl.BlockSpec((1,) + ctx_kv.shape[1:], lambda b, j: (b, 0, 0, 0)))
        args.append(ctx_kv)
    if sink is not None:
        in_specs.append(pl.BlockSpec(memory_space=pltpu.SMEM))
        args.append(sink.reshape(-1).astype(F32))
    return pl.pallas_call(
        functools.partial(_attn_kernel, seq_len=seq_len, tq=tq, q_blocks=q_blocks, n_ctx=n_ctx, banded=banded,
                          has_sink=sink is not None),
        grid=(batch, per_seq),
        in_specs=in_specs,
        out_specs=pl.BlockSpec((q_blocks * tq, 512), lambda b, j: (b * per_seq + j, 0)),
        out_shape=jax.ShapeDtypeStruct((batch * seq_len, 512), BF16),
        compiler_params=_cparams(("parallel", "arbitrary")),
        name="gqa_attention",
    )(*args)


def _log_sigmoid(x):
    return jnp.minimum(x, 0.0) - jnp.log1p(jnp.exp(-jnp.abs(x)))


def _mlstm_kernel(*refs, seq_len, has_init, emit_state):
    it = iter(refs)
    q_ref, k_ref, v_ref, g_ref, bo_ref, gb_ref, ng_ref = (next(it) for _ in range(7))
    c0_ref, n0_ref, m0_ref = (next(it), next(it), next(it)) if has_init else (None, None, None)
    y_ref = next(it)
    co_ref, no_ref, mo_ref = (next(it), next(it), next(it)) if emit_state else (None, None, None)
    c_scr, n_scr, m_scr, hf_scr, hb_scr, ig_buf, bc_buf, ut_buf = (next(it) for _ in range(8))

    t_len = MLSTM_T
    n_chunks = seq_len // t_len
    n_streams = 2 * B_HEADS

    for s in range(n_streams):
        if has_init:
            c_scr[s] = c0_ref[0, s // B_HEADS, s % B_HEADS]
        else:
            c_scr[s] = jnp.zeros((B_DIM, B_DIM), F32)
    if has_init:
        n_scr[...] = n0_ref[0]
        m_scr[...] = m0_ref[0]
    else:
        n_scr[...] = jnp.zeros_like(n_scr)
        m_scr[...] = jnp.zeros_like(m_scr)

    ri = lax.broadcasted_iota(jnp.int32, (t_len, t_len), 0)
    ci = lax.broadcasted_iota(jnp.int32, (t_len, t_len), 1)
    lower = ci <= ri
    upper = ci >= ri
    tri_f = jnp.where(lower, 1.0, 0.0).astype(BF16)
    tri_b = jnp.where(upper, 1.0, 0.0).astype(BF16)
    gate_bias = gb_ref[...]
    sel_row = lax.broadcasted_iota(jnp.int32, (128, B_HEADS * B_DIM), 0)
    sel_head = lax.broadcasted_iota(jnp.int32, (128, B_HEADS * B_DIM), 1) // B_DIM
    gate_select = []
    for direction in range(2):
        i_base = 2 * direction * B_HEADS
        gate_select.append((jnp.where(sel_row == i_base + sel_head, 1.0, 0.0).astype(BF16),
                            jnp.where(sel_row == i_base + B_HEADS + sel_head, 1.0, 0.0).astype(BF16)))

    def chunk_rows(c, direction):
        chunk = c if direction == 0 else n_chunks - 1 - c
        return pl.ds(pl.multiple_of(chunk * t_len, t_len), t_len)

    def prepare_gates(c, direction):
        slot = c % 2
        rows = chunk_rows(jnp.minimum(c, n_chunks - 1), direction)
        tri = tri_f if direction == 0 else tri_b
        sel_i, sel_f = gate_select[direction]
        gates = g_ref[rows, :] + gate_bias
        hi, mid, lo = _split3(_log_sigmoid(gates))
        b3 = _dot(tri, jnp.concatenate([hi, mid, lo], 1))
        ig_buf[slot, direction] = _dot_sel_rhs(gates, sel_i)
        yield
        bc_all = b3[:, :128] + b3[:, 128:256] + b3[:, 256:]
        bc_rep = _dot_sel_rhs(bc_all, sel_f)
        ut_buf[slot, direction] = (gates - pltpu.roll(bc_all, 128 - B_HEADS, 1)).T
        yield
        bc_buf[slot, direction] = bc_rep

    def run_direction(c, direction):
        rows = chunk_rows(c, direction)
        slot = c % 2
        causal = lower if direction == 0 else upper
        last = t_len - 1 if direction == 0 else 0
        h_out = hf_scr if direction == 0 else hb_scr
        ig_rep, bc_rep, u_t = ig_buf[slot, direction], bc_buf[slot, direction], ut_buf[slot, direction]
        heads = []
        for hd in range(B_HEADS):
            s = direction * B_HEADS + hd
            col = slice(hd * B_DIM, (hd + 1) * B_DIM)
            qc, kc, vc = q_ref[rows, col], k_ref[rows, col], v_ref[rows, col]
            c_prev = c_scr[s]
            n_prev = n_scr[s:s + 1, :]
            qk = _dot_nt(qc, kc)
            q_state = _dot(qc, c_prev.astype(BF16))
            qn = _dot_nt(qc, jnp.broadcast_to(n_prev, (B_DIM, B_DIM)).astype(BF16))
            heads.append((s, col, qc, kc, vc, c_prev, n_prev, qk, q_state, qn))
        yield
        staged = []
        for (s, col, qc, kc, vc, c_prev, n_prev, qk, q_state, qn) in heads:
            i_col = 2 * direction * B_HEADS + (s % B_HEADS)
            ig, bc = ig_rep[:, col], bc_rep[:, col]
            dmat = jnp.where(causal, bc[:, :t_len] + u_t[i_col:i_col + 1, :], -jnp.inf)
            m_prev = m_scr[s:s + 1, :]
            inter = bc + m_prev
            m_t = jnp.maximum(inter, jnp.max(dmat, -1, keepdims=True))
            w = (jnp.exp(dmat - m_t[:, :t_len]) * qk).astype(BF16)
            wv = _dot(w, jnp.concatenate([vc, jnp.ones_like(vc)], 1))
            b_tot = bc[last:last + 1, :]
            g = b_tot - bc + ig
            m_new = jnp.maximum(b_tot + m_prev, jnp.max(g, 0, keepdims=True))
            kw = kc.astype(F32) * jnp.exp(g - m_new)
            upd = _dot_tn(kw.astype(BF16), vc)
            decay = jnp.exp(b_tot + m_prev - m_new)
            staged.append((s, col, inter, m_t, c_prev, n_prev, q_state, qn, wv, m_new, kw, upd, decay))
        yield
        for (s, col, inter, m_t, c_prev, n_prev, q_state, qn, wv, m_new, kw, upd, decay) in staged:
            a = jnp.exp(inter - m_t)
            num = a * q_state + wv[:, :B_DIM]
            den = a * qn + wv[:, B_DIM:]
            h_out[rows, col] = num / jnp.maximum(jnp.abs(den), jnp.exp(-m_t))
            c_scr[s] = decay * c_prev + upd
            n_scr[s:s + 1, :] = decay * n_prev + jnp.sum(kw, 0, keepdims=True)
            m_scr[s:s + 1, :] = m_new

    def chained(c, direction):
        yield from prepare_gates(c, direction)
        yield
        yield from run_direction(c, direction)

    def step(c, carry):
        _run_interleaved([chained(c, 0), chained(c, 1)])
        return carry

    lax.fori_loop(0, n_chunks, step, 0, unroll=min(SCAN_UNROLL, n_chunks))

    blk = min(256, seq_len)
    ng = ng_ref[...]

    def finish(i, carry):
        rows = pl.ds(pl.multiple_of(i * blk, blk), blk)
        hsum = hf_scr[rows, :] + hb_scr[rows, :]
        gate = jax.nn.sigmoid(bo_ref[rows, :])
        parts = [_rms_lastdim(hsum[:, hd * B_DIM:(hd + 1) * B_DIM], ng[:, hd * B_DIM:(hd + 1) * B_DIM])
                 for hd in range(B_HEADS)]
        y_ref[rows, :] = (gate * jnp.concatenate(parts, 1)).astype(BF16)
        return carry

    lax.fori_loop(0, seq_len // blk, finish, 0)

    if emit_state:
        for s in range(n_streams):
            co_ref[0, s // B_HEADS, s % B_HEADS] = c_scr[s]
        no_ref[0] = n_scr[...]
        mo_ref[0] = m_scr[...]


def _mlstm(q, k, v, gates, bo, gate_bias, norm_g, init, batch, seq_len, row0, emit_state):
    n = batch * seq_len
    seq0 = row0 // seq_len
    in_row = lambda width: pl.BlockSpec((seq_len, width), lambda b: (seq0 + b, 0))
    row = lambda width: pl.BlockSpec((seq_len, width), lambda b: (b, 0))
    const = lambda a: pl.BlockSpec(a.shape, lambda b: (0,) * a.ndim)
    gb = jnp.zeros((1, 128), F32).at[0, :4 * B_HEADS].set(gate_bias.reshape(-1))
    ng = norm_g.reshape(1, B_HEADS * B_DIM)
    in_specs = [in_row(512), in_row(512), in_row(512), in_row(128), in_row(512), const(gb), const(ng)]
    args = [q, k, v, gates, bo, gb, ng]
    n_streams = 2 * B_HEADS
    if init is not None:
        c0, n0, m0 = init
        in_specs += [pl.BlockSpec((1, 2, B_HEADS, B_DIM, B_DIM), lambda b: (b, 0, 0, 0, 0)),
                     pl.BlockSpec((1, n_streams, B_DIM), lambda b: (b, 0, 0)),
                     pl.BlockSpec((1, n_streams, B_DIM), lambda b: (b, 0, 0))]
        args += [c0, n0.reshape(batch, n_streams, B_DIM),
                 jnp.broadcast_to(m0.reshape(batch, n_streams, 1), (batch, n_streams, B_DIM))]
    out_specs = [row(512)]
    out_shape = [jax.ShapeDtypeStruct((n, 512), BF16)]
    if emit_state:
        out_specs += [pl.BlockSpec((1, 2, B_HEADS, B_DIM, B_DIM), lambda b: (b, 0, 0, 0, 0)),
                      pl.BlockSpec((1, n_streams, B_DIM), lambda b: (b, 0, 0)),
                      pl.BlockSpec((1, n_streams, B_DIM), lambda b: (b, 0, 0))]
        out_shape += [jax.ShapeDtypeStruct((batch, 2, B_HEADS, B_DIM, B_DIM), F32),
                      jax.ShapeDtypeStruct((batch, n_streams, B_DIM), F32),
                      jax.ShapeDtypeStruct((batch, n_streams, B_DIM), F32)]
    return pl.pallas_call(
        functools.partial(_mlstm_kernel, seq_len=seq_len, has_init=init is not None, emit_state=emit_state),
        grid=(batch,),
        in_specs=in_specs,
        out_specs=out_specs,
        out_shape=out_shape,
        scratch_shapes=[pltpu.VMEM((n_streams, B_DIM, B_DIM), F32),
                        pltpu.VMEM((n_streams, B_DIM), F32),
                        pltpu.VMEM((n_streams, B_DIM), F32),
                        pltpu.VMEM((seq_len, 512), F32),
                        pltpu.VMEM((seq_len, 512), F32),
                        pltpu.VMEM((2, 2, MLSTM_T, B_HEADS * B_DIM), F32),
                        pltpu.VMEM((2, 2, MLSTM_T, B_HEADS * B_DIM), F32),
                        pltpu.VMEM((2, 2, 128, MLSTM_T), F32)],
        compiler_params=_cparams(("parallel",)),
        name="mlstm_scan",
    )(*args)


def _hgrn_kernel(*refs, seq_len, layer, has_init, emit_state):
    it = iter(refs)
    q_ref, ff_ref, fb_ref, v_ref, cg_ref, lbl_ref, ng_ref = (next(it) for _ in range(7))
    s0_ref = next(it) if has_init else None
    y_ref = next(it)
    so_ref = next(it) if emit_state else None
    st_scr, of_scr, ob_scr, ak_scr, vf_scr = (next(it) for _ in range(5))

    t_len = HGRN_T
    n_sub = t_len // SUB
    n_chunks = seq_len // t_len
    n_streams = 2 * C_HEADS

    logits = lbl_ref[...]
    e = jnp.exp(logits - jnp.max(logits, 0, keepdims=True))
    sm = e / jnp.sum(e, 0, keepdims=True)
    lb = jnp.sum(sm[0:layer + 1], 0, keepdims=True) - sm[0:1]

    for s in range(n_streams):
        if has_init:
            st_scr[s] = s0_ref[0, s // C_HEADS, s % C_HEADS].T
        else:
            st_scr[s] = jnp.zeros((C_DV, C_DK), F32)

    ri = lax.broadcasted_iota(jnp.int32, (t_len, t_len), 0)
    ci = lax.broadcasted_iota(jnp.int32, (t_len, t_len), 1)
    tri_f = jnp.where(ci <= ri, 1.0, 0.0).astype(BF16)
    tri_b = jnp.where(ci >= ri, 1.0, 0.0).astype(BF16)
    sub_row = lax.broadcasted_iota(jnp.int32, (SUB, C_DK), 0)
    ones_dk = jnp.ones((C_DK, C_DK), BF16)

    def run_stream(c, direction, hd):
        chunk = c if direction == 0 else n_chunks - 1 - c
        rows = pl.ds(pl.multiple_of(chunk * t_len, t_len), t_len)
        f_ref = ff_ref if direction == 0 else fb_ref
        tri = tri_f if direction == 0 else tri_b
        last = t_len - 1 if direction == 0 else 0
        o_out = of_scr if direction == 0 else ob_scr
        s = direction * C_HEADS + hd
        col = slice(hd * C_DK, (hd + 1) * C_DK)
        lbh = lb[:, col]
        f = lbh + (1.0 - lbh) * jax.nn.sigmoid(f_ref[rows, col])
        log_k = jnp.log2(1.0 - f)
        hi, mid, lo = _split3(jnp.log2(f))
        a3 = _dot(tri, jnp.concatenate([hi, mid, lo], 1))
        yield
        a_cum = a3[:, :C_DK] + a3[:, C_DK:2 * C_DK] + a3[:, 2 * C_DK:]
        a_key = a_cum - log_k
        ak_scr[s] = a_key
        a_tot = a_cum[last:last + 1, :]
        qf = q_ref[rows, col].astype(F32)
        vc = v_ref[rows, col]
        vf_scr[s] = vc.astype(F32)
        st = st_scr[s]
        inter = _dot_nt((qf * jnp.exp2(a_cum)).astype(BF16), st.astype(BF16))
        upd = _dot_tn(vc, jnp.exp2(a_tot - a_key).astype(BF16))
        ps = []
        for blk in range(n_sub):
            b0 = blk * SUB
            a_i, q_i = a_cum[b0:b0 + SUB], qf[b0:b0 + SUB]
            for j in range(SUB):
                seen = (sub_row >= j) if direction == 0 else (sub_row <= j)
                key_row = ak_scr[s, pl.ds(b0 + j, SUB, stride=0), :]
                ps.append(jnp.where(seen, jnp.exp2(a_i - key_row), 0.0) * q_i)
        att = _dot(jnp.concatenate(ps, 0).astype(BF16), ones_dk)
        off = []
        half = SUB
        while half < t_len:
            for g0 in range(0, t_len, 2 * half):
                first, second = slice(g0, g0 + half), slice(g0 + half, g0 + 2 * half)
                if direction == 0:
                    qr, kr, a_ref = second, first, a_cum[g0 + half - 1:g0 + half]
                else:
                    qr, kr, a_ref = first, second, a_cum[g0 + half:g0 + half + 1]
                qt = (qf[qr] * jnp.exp2(a_cum[qr] - a_ref)).astype(BF16)
                kt = jnp.exp2(a_ref - a_key[kr]).astype(BF16)
                off.append((qr, kr, _dot_nt(qt, kt)))
            half *= 2
        yield
        st_scr[s] = jnp.exp2(a_tot) * st + upd
        blocks = []
        for blk in range(n_sub):
            b0 = blk * SUB
            o_i = inter[b0:b0 + SUB]
            for j in range(SUB):
                r = (blk * SUB + j) * SUB
                o_i = o_i + att[r:r + SUB] * vf_scr[s, pl.ds(b0 + j, SUB, stride=0), :]
            blocks.append(o_i)
        off = [(qr, _dot(att_off.astype(BF16), vc[kr])) for qr, kr, att_off in off]
        yield
        for qr, o_off in off:
            for k, blk in enumerate(range(qr.start // SUB, qr.stop // SUB)):
                blocks[blk] = blocks[blk] + o_off[k * SUB:(k + 1) * SUB]
        o_out[rows, col] = jnp.concatenate(blocks, 0)

    def step(c, carry):
        _run_interleaved([run_stream(c, direction, hd) for direction in range(2) for hd in range(C_HEADS)])
        return carry

    lax.fori_loop(0, n_chunks, step, 0, unroll=min(SCAN_UNROLL, n_chunks))

    blk_rows = min(256, seq_len)
    ng = ng_ref[...]

    def finish(i, carry):
        rows = pl.ds(pl.multiple_of(i * blk_rows, blk_rows), blk_rows)
        osum = of_scr[rows, :] + ob_scr[rows, :]
        parts = [_rms_lastdim(osum[:, hd * C_DV:(hd + 1) * C_DV], ng[:, hd * C_DV:(hd + 1) * C_DV])
                 for hd in range(C_HEADS)]
        y_ref[rows, :] = (jnp.concatenate(parts, 1) * _silu(cg_ref[rows, :])).astype(BF16)
        return carry

    lax.fori_loop(0, seq_len // blk_rows, finish, 0)

    if emit_state:
        for s in range(n_streams):
            so_ref[0, s // C_HEADS, s % C_HEADS] = st_scr[s].T


def _hgrn(q, ff, fb, v, cg, lb_logits, norm_g, layer, init, batch, seq_len, row0, emit_state):
    n = batch * seq_len
    seq0 = row0 // seq_len
    in_row = lambda width: pl.BlockSpec((seq_len, width), lambda b: (seq0 + b, 0))
    row = lambda width: pl.BlockSpec((seq_len, width), lambda b: (b, 0))
    const = lambda a: pl.BlockSpec(a.shape, lambda b: (0,) * a.ndim)
    ng = norm_g.reshape(1, C_HEADS * C_DV)
    state_spec = pl.BlockSpec((1, 2, C_HEADS, C_DK, C_DV), lambda b: (b, 0, 0, 0, 0))
    in_specs = [in_row(512)] * 5 + [const(lb_logits), const(ng)]
    args = [q, ff, fb, v, cg, lb_logits, ng]
    if init is not None:
        in_specs.append(state_spec)
        args.append(init)
    out_specs = [row(512)]
    out_shape = [jax.ShapeDtypeStruct((n, 512), BF16)]
    if emit_state:
        out_specs.append(state_spec)
        out_shape.append(jax.ShapeDtypeStruct((batch, 2, C_HEADS, C_DK, C_DV), F32))
    return pl.pallas_call(
        functools.partial(_hgrn_kernel, seq_len=seq_len, layer=layer, has_init=init is not None,
                          emit_state=emit_state),
        grid=(batch,),
        in_specs=in_specs,
        out_specs=out_specs,
        out_shape=out_shape,
        scratch_shapes=[pltpu.VMEM((2 * C_HEADS, C_DV, C_DK), F32),
                        pltpu.VMEM((seq_len, 512), F32),
                        pltpu.VMEM((seq_len, 512), F32),
                        pltpu.VMEM((2 * C_HEADS, HGRN_T, C_DK), F32),
                        pltpu.VMEM((2 * C_HEADS, HGRN_T, C_DV), F32)],
        compiler_params=_cparams(("parallel",)),
        name="hgrn2_scan",
    )(*args)


def _prep_ctx_kv(k_ctx, v_ctx):
    k_t = jnp.transpose(k_ctx, (0, 2, 3, 1)).astype(BF16)
    v_t = jnp.transpose(v_ctx, (0, 2, 3, 1)).astype(BF16)
    ones = jnp.ones(k_t.shape[:2] + (128, k_t.shape[3]), BF16)
    return jnp.concatenate([k_t, v_t, v_t, ones], 2)


def kernel(x_prompt, x_sample, c, cache_a_k, cache_a_v, state_b_C, state_b_n, state_b_m, state_c_S, cache_d_k, cache_d_v, c_ctx, ada_w, ada_b, ln_g, ln_b, ffn_w1, ffn_w3, ffn_w2, w_in_even, w_out_even, a_sink, b_gate_bias, b_norm_g, w_in_odd, w_out_odd, c_lb_logits, c_norm_g, d_q_norm, d_k_norm):
    batch_p, len_p, d = x_prompt.shape
    batch_s, len_s, _ = x_sample.shape

    cvec = jnp.concatenate([c_ctx[None], c, jnp.zeros((MOD_ROWS - 1 - batch_s, d), F32)], 0)
    mod_all = _modulation(cvec, ada_w, ada_b)

    lay = _Layout(batch_p, len_p, batch_s, len_s)
    new = {}
    ffn_stacks = (ffn_w1, ffn_w3, ffn_w2)
    ffn_w = tuple(a[0:1, 0:1].astype(BF16) for a in ffn_stacks)
    x = [x_prompt.reshape(lay.n_p, d), x_sample.reshape(lay.n_s, d)]

    for l in range(DEPTH):
        mod = mod_all[l].reshape(MOD_ROWS, 9, d)
        i = l // 2
        if l % 2 == 0:
            in_proj = (_even_in_body, _even_in_io, dict(w=w_in_even[i].astype(BF16)))
            w_out = w_out_even[i].astype(BF16)
        else:
            in_proj = (_odd_in_body, _odd_in_io,
                       dict(w=w_in_odd[i].astype(BF16), q_norm=d_q_norm[i], k_norm=d_k_norm[i]))
            w_out = w_out_odd[i].astype(BF16)

        x_all, *proj, k_cache, v_cache, w1, w3, w2 = _ffn(x, mod, 0, ln_g[l, 0], ln_b[l, 0], ffn_w, lay,
                                                         in_proj=in_proj, next_weights=ffn_stacks + (l, 1))
        ffn_w = (w1, w3, w2)
        prompt = dict(batch=batch_p, seq_len=len_p, row0=0)
        sample = dict(batch=batch_s, seq_len=len_s, row0=lay.n_p)
        if l % 2 == 0:
            aq, akv, bq, bk, bv, bo, bg = proj
            new["a_k"], new["a_v"] = _cache_layout(k_cache), _cache_layout(v_cache)
            scan = (bq, bk, bv, bg, bo, b_gate_bias[i], b_norm_g[i])
            ya_p = _attention(aq, akv, None, a_sink[i], **prompt, tq=Q_BLOCK, q_blocks=len_p // Q_BLOCK, banded=False)
            yb_p, c_new, n_new, m_new = _mlstm(*scan, None, **prompt, emit_state=True)
            new["b_C"] = c_new[:, None]
            new["b_n"] = n_new.reshape(batch_p, 1, 2, B_HEADS, B_DIM)
            new["b_m"] = m_new[:, :, 0].reshape(batch_p, 1, 2, B_HEADS)
            ya_s = _attention(aq, akv, _prep_ctx_kv(cache_a_k[:, i], cache_a_v[:, i]), a_sink[i], **sample,
                              tq=Q_BLOCK, q_blocks=4, banded=True)
            yb_s, = _mlstm(*scan, (state_b_C[:, i], state_b_n[:, i], state_b_m[:, i]), **sample, emit_state=False)
        else:
            cq, ff, fb, cv, cg, dq, dkv = proj
            new["d_k"], new["d_v"] = _cache_layout(k_cache), _cache_layout(v_cache)
            scan = (cq, ff, fb, cv, cg, c_lb_logits, c_norm_g[i], l)
            ya_p, s_new = _hgrn(*scan, None, **prompt, emit_state=True)
            new["c_S"] = s_new[:, None]
            yb_p = _attention(dq, dkv, None, None, **prompt, tq=Q_BLOCK, q_blocks=len_p // Q_BLOCK, banded=False)
            ya_s, = _hgrn(*scan, state_c_S[:, i], **sample, emit_state=False)
            yb_s = _attention(dq, dkv, _prep_ctx_kv(cache_d_k[:, i], cache_d_v[:, i]), None, **sample,
                              tq=Q_BLOCK, q_blocks=2, banded=False)
        last = l == DEPTH - 1
        x = _ffn([x_all], mod, 2, ln_g[l, 2], ln_b[l, 2], ffn_w, lay,
                 mixer=(ya_p, yb_p, ya_s, yb_s, w_out, ln_g[l, 1], ln_b[l, 1]), split_out=last,
                 next_weights=None if last else ffn_stacks + (l + 1, 0))
        if not last:
            *x, w1, w3, w2 = x
            ffn_w = (w1, w3, w2)

    y_prompt, y_sample = x
    return (y_prompt.reshape(batch_p, len_p, d), y_sample.reshape(batch_s, len_s, d), new["a_k"], new["a_v"],
            new["b_C"], new["b_n"], new["b_m"], new["c_S"], new["d_k"], new["d_v"])
```
